```python
import math
import jax, jax.numpy as jnp
from jax import lax
import numpy as np

D_MODEL = 1024
BATCH = 8
SEQ = 2048
DEPTH = 2
DEC_BATCH = 128
DEC_SEQ = 8
PAST_LEN = 16384
PAGE_SIZE = 128

F32 = jnp.float32
MIX = D_MODEL // 2
N_EVEN = (DEPTH + 1) // 2
N_ODD = DEPTH // 2
CHUNK = 32
NORM_EPS = 1e-5
A_HD = 64
A_H = MIX // A_HD
A_LORA_W = 64
A_LORA_A = 64
A_GN_EPS = 64e-5
A_SHIFT_W = 3 * MIX + A_LORA_W + A_LORA_A
B_HD = 128
B_H = MIX // B_HD
IN_EVEN = A_SHIFT_W + MIX + 4 * MIX
C_H = 4
C_VD = MIX // C_H
C_KD = C_VD // 2
C_KW = C_H * C_KD
C_LORA = 16
C_GATE_NORM = 16.0
C_IN = 2 * C_KW + MIX + C_LORA + MIX
D_H = 4
D_HD = MIX // D_H
D_CONV = 4
D_IN = 3 * MIX + 2 * D_H + MIX
IN_ODD = C_IN + D_IN

kernel_name = "rwkv7_hgrn2_gla_mlstm_hybrid_step"


def _rms(x, g, eps=NORM_EPS):
    xf = x.astype(F32)
    return xf * lax.rsqrt(jnp.mean(xf * xf, -1, keepdims=True) + eps) * g.astype(F32)


def _head_rms(o, g):
    return _rms(o, g.reshape(o.shape[-2:]))


def _split(t, sizes):
    return jnp.split(t, np.cumsum(sizes)[:-1].tolist(), axis=-1)


def _chunk(T):
    return math.gcd(T, CHUNK)


def _chunked_gla(q, k, v, log_g, s0):
    Bn, T, H, K = q.shape
    V = v.shape[-1]
    L = _chunk(T)
    N = T // L
    q, k, log_g = (t.reshape(Bn, N, L, H, K) for t in (q, k, log_g))
    v = v.reshape(Bn, N, L, H, V)
    b = jnp.cumsum(log_g, axis=2)
    b_last = b[:, :, -1]
    q_dec = q * jnp.exp(b)
    k_inv = k * jnp.exp(-b)
    k_end = k * jnp.exp(b_last[:, :, None] - b)
    causal = jnp.tril(jnp.ones((L, L), bool))
    att = jnp.where(causal, jnp.einsum('bnlhk,bnmhk->bnhlm', q_dec, k_inv), 0.0)
    o = jnp.einsum('bnhlm,bnmhv->bnlhv', att, v)
    kv = jnp.einsum('bnlhk,bnlhv->bnhkv', k_end, v)

    def step(s, inp):
        dec, kv_c = inp
        return dec[..., None] * s + kv_c, s

    s_fin, s_in = lax.scan(step, s0, (jnp.moveaxis(jnp.exp(b_last), 1, 0), jnp.moveaxis(kv, 1, 0)))
    s_in = jnp.moveaxis(s_in, 0, 1)
    o = o + jnp.einsum('bnlhk,bnhkv->bnlhv', q_dec, s_in)
    return o.reshape(Bn, T, H, V), s_fin


def _chunked_mlstm(q, k, v, log_i, log_f, c0, n0, m0):
    Bn, T, H, d = q.shape
    L = _chunk(T)
    N = T // L
    q, k, v = (t.reshape(Bn, N, L, H, d) for t in (q, k, v))
    li = jnp.swapaxes(log_i.reshape(Bn, N, L, H), 2, 3)
    lf = jnp.swapaxes(log_f.reshape(Bn, N, L, H), 2, 3)
    a = jnp.cumsum(lf, -1)
    a_last = a[..., -1]
    causal = jnp.tril(jnp.ones((L, L), bool))
    dlog = jnp.where(causal, a[..., :, None] - a[..., None, :] + li[..., None, :], -jnp.inf)
    m_intra = jnp.max(dlog, -1)
    s_intra = jnp.exp(dlog - m_intra[..., None]) * jnp.einsum('bnlhd,bnmhd->bnhlm', q, k)
    num_intra = jnp.einsum('bnhlm,bnmhd->bnlhd', s_intra, v)
    den_intra = jnp.sum(s_intra, -1)
    e_end = a_last[..., None] - a + li
    m_loc = jnp.max(e_end, -1)
    w_end = jnp.exp(e_end - m_loc[..., None])
    c_loc = jnp.einsum('bnhl,bnlhv,bnlhk->bnhvk', w_end, v, k)
    n_loc = jnp.einsum('bnhl,bnlhk->bnhk', w_end, k)

    def step(carry, inp):
        c, n, m = carry
        al, ml, cl, nl = inp
        m_new = jnp.maximum(al + m, ml)
        fo = jnp.exp(al + m - m_new)
        fl = jnp.exp(ml - m_new)
        new = (fo[..., None, None] * c + fl[..., None, None] * cl, fo[..., None] * n + fl[..., None] * nl, m_new)
        return new, (c, n, m)

    mv = lambda t: jnp.moveaxis(t, 1, 0)
    (c_f, n_f, m_f), (c_in, n_in, m_in) = lax.scan(step, (c0, n0, m0), (mv(a_last), mv(m_loc), mv(c_loc), mv(n_loc)))
    c_in, n_in, m_in = mv(c_in), mv(n_in), mv(m_in)
    g = a + m_in[..., None]
    m_t = jnp.maximum(g, m_intra)
    f_inter = jnp.exp(g - m_t)
    f_intra = jnp.exp(m_intra - m_t)
    tb = lambda t: jnp.swapaxes(t, 2, 3)[..., None]
    num = num_intra * tb(f_intra) + jnp.einsum('bnhvk,bnlhk->bnlhv', c_in, q) * tb(f_inter)
    den = f_intra * den_intra + f_inter * jnp.einsum('bnhk,bnlhk->bnhl', n_in, q)
    h = num / tb(jnp.maximum(jnp.abs(den), jnp.exp(-m_t)))
    return h.reshape(Bn, T, H, d), c_f, n_f, m_f


def _rwkv7(p, gate, prev, s0, mu, w0, w2, a0, a2, k_k, k_a, r_k, ln_w, ln_b):
    Bn, T, _ = p.shape
    pf = p.astype(F32)
    shifted = jnp.concatenate([prev.astype(F32)[:, None], pf[:, :-1]], axis=1)
    xm = pf + (shifted - pf) * mu
    r, k, v, w_lo, a_lo = _split(xm, (MIX, MIX, MIX, A_LORA_W, A_LORA_A))
    w = -jax.nn.softplus(-(w0 + jnp.tanh(w_lo) @ w2)) - 0.5
    decay = jnp.exp(-jnp.exp(w))
    a = jax.nn.sigmoid(a0 + a_lo @ a2)
    heads = lambda t: t.reshape(Bn, T, A_H, A_HD)
    kk = heads(k * k_k)
    kk = kk * lax.rsqrt(jnp.maximum(jnp.sum(kk * kk, -1, keepdims=True), 1e-12))
    k = k * (1.0 + (a - 1.0) * k_a)
    r, k, v, decay, a = (heads(t) for t in (r, k, v, decay, a))

    def step(s, inp):
        r_t, w_t, k_t, v_t, kk_t, b_t = inp
        sa = jnp.einsum('bhvk,bhk->bhv', s, -kk_t)
        s = s * w_t[:, :, None, :] + sa[..., None] * b_t[:, :, None, :] + v_t[..., None] * k_t[:, :, None, :]
        return s, jnp.einsum('bhvk,bhk->bhv', s, r_t)

    xs = tuple(jnp.moveaxis(t, 1, 0) for t in (r, decay, k, v, kk, kk * a))
    s_fin, o = lax.scan(step, s0, xs)
    o = jnp.moveaxis(o, 0, 1)
    mean = jnp.mean(o, -1, keepdims=True)
    var = jnp.mean(jnp.square(o - mean), -1, keepdims=True)
    o = (o - mean) * lax.rsqrt(var + A_GN_EPS) * ln_w.reshape(A_H, A_HD) + ln_b.reshape(A_H, A_HD)
    o = o + jnp.sum(r * k * r_k.reshape(A_H, A_HD), -1, keepdims=True) * v
    o = o.reshape(Bn, T, MIX) * jax.nn.silu(gate.astype(F32))
    return o, p[:, -1], s_fin


def _even_layer(h, e, shift, s_rwkv, s_hgrn, w_in, w_out, a_mu, a_w0, a_w2, a_a0, a_a2, a_kk, a_ka, a_rk,
                a_ln_w, a_ln_b, b_lb, b_norm):
    Bn, T, _ = h.shape
    p = h @ w_in[e]
    pa, ga, bq, bf, bi, gb = _split(p, (A_SHIFT_W, MIX, MIX, MIX, MIX, MIX))
    oa, shift_new, s_rwkv_new = _rwkv7(pa, ga, shift, s_rwkv.astype(F32), a_mu[e], a_w0[e], a_w2[e], a_a0[e],
                                       a_a2[e], a_kk[e], a_ka[e], a_rk[e], a_ln_w[e], a_ln_b[e])
    lb = jnp.cumsum(jax.nn.softmax(b_lb.astype(F32), axis=0), axis=0)[e]
    g = lb + (1.0 - lb) * jax.nn.sigmoid(bf.astype(F32))
    hb = lambda t: t.reshape(Bn, T, B_H, B_HD)
    q = jax.nn.silu(bq.astype(F32)) * (B_HD ** -0.5)
    ob, s_hgrn_new = _chunked_gla(hb(q), hb(1.0 - g), hb(bi.astype(F32)), hb(jnp.log(g)), s_hgrn.astype(F32))
    ob = _head_rms(ob, b_norm[e]).reshape(Bn, T, MIX) * jax.nn.silu(gb.astype(F32))
    y = jnp.concatenate([oa, ob], -1).astype(h.dtype) @ w_out[e]
    return y, (shift_new, s_rwkv_new, s_hgrn_new)


def _odd_layer(h, o, s_gla, conv_buf, mc, mn, mm, w_in, w_out, c_g2, c_g2b, c_norm, d_conv_w, d_conv_b,
               d_ib, d_fb, d_norm):
    Bn, T, _ = h.shape
    p = h @ w_in[o]
    cq, ck, cv, cg, gc, dqk, dv, di, df, gd = _split(
        p, (C_KW, C_KW, MIX, C_LORA, MIX, 2 * MIX, MIX, D_H, D_H, MIX))
    hc = lambda t, d: t.astype(F32).reshape(Bn, T, C_H, d)
    log_g = jax.nn.log_sigmoid(cg.astype(F32) @ c_g2[o] + c_g2b[o]) / C_GATE_NORM
    oc, s_gla_new = _chunked_gla(hc(cq, C_KD) * (C_KD ** -0.5), hc(ck, C_KD), hc(cv, C_VD),
                                 log_g.reshape(Bn, T, C_H, C_KD), s_gla.astype(F32))
    oc = _head_rms(oc, c_norm[o]).reshape(Bn, T, MIX) * jax.nn.silu(gc.astype(F32))
    padded = jnp.concatenate([conv_buf.astype(dqk.dtype), dqk], axis=1)
    conv = d_conv_b[o] + sum(padded[:, j:j + T] * d_conv_w[o, j] for j in range(D_CONV))
    conv = jax.nn.silu(conv.astype(F32))
    dq, dk = _split(conv, (MIX, MIX))
    hd = lambda t: t.reshape(Bn, T, D_H, D_HD)
    log_i = di.astype(F32) + d_ib[o]
    log_f = jax.nn.log_sigmoid(df.astype(F32) + d_fb[o])
    od, mc_new, mn_new, mm_new = _chunked_mlstm(hd(dq), hd(dk) * (D_HD ** -0.5), hd(dv.astype(F32)), log_i, log_f,
                                                mc.astype(F32), mn.astype(F32), mm.astype(F32))
    od = _head_rms(od, d_norm[o]).reshape(Bn, T, MIX) * jax.nn.silu(gd.astype(F32))
    y = jnp.concatenate([oc, od], -1).astype(h.dtype) @ w_out[o]
    return y, (s_gla_new, padded[:, -(D_CONV - 1):], mc_new, mn_new, mm_new)


def setup_inputs(seed: int = 0) -> dict:
    key = jax.random.key(seed)
    ks = iter(jax.random.split(key, 48))
    nrm = lambda shape, s=1.0: s * jax.random.normal(next(ks), shape, F32)
    D = D_MODEL
    return {
        "x_prompt": nrm((BATCH, SEQ, D)),
        "x_sample": nrm((DEC_BATCH, DEC_SEQ, D)),
        "state_shift_a": nrm((N_EVEN, DEC_BATCH, A_SHIFT_W)),
        "state_rwkv": nrm((N_EVEN, DEC_BATCH, A_H, A_HD, A_HD), 0.3),
        "state_hgrn": nrm((N_EVEN, DEC_BATCH, B_H, B_HD, B_HD), 0.3),
        "state_gla": nrm((N_ODD, DEC_BATCH, C_H, C_KD, C_VD), 0.3),
        "state_conv_d": nrm((N_ODD, DEC_BATCH, D_CONV - 1, 2 * MIX)),
        "state_mlstm_c": nrm((N_ODD, DEC_BATCH, D_H, D_HD, D_HD), 0.3),
        "state_mlstm_n": nrm((N_ODD, DEC_BATCH, D_H, D_HD), 0.3),
        "state_mlstm_m": nrm((N_ODD, DEC_BATCH, D_H)),
        "norm_g": 1.0 + nrm((DEPTH, D), 0.01),
        "w_in_even": nrm((N_EVEN, D, IN_EVEN), D ** -0.5),
        "w_out_even": nrm((N_EVEN, 2 * MIX, D), (2 * MIX) ** -0.5),
        "a_mu": jax.random.uniform(next(ks), (N_EVEN, A_SHIFT_W), F32),
        "a_w0": nrm((N_EVEN, MIX), 0.5),
        "a_w2": nrm((N_EVEN, A_LORA_W, MIX), 0.1),
        "a_a0": nrm((N_EVEN, MIX), 0.5),
        "a_a2": nrm((N_EVEN, A_LORA_A, MIX), 0.1),
        "a_kk": 0.85 + nrm((N_EVEN, MIX), 0.05),
        "a_ka": 1.0 + nrm((N_EVEN, MIX), 0.05),
        "a_rk": nrm((N_EVEN, MIX), 0.1),
        "a_ln_w": 1.0 + nrm((N_EVEN, MIX), 0.01),
        "a_ln_b": nrm((N_EVEN, MIX), 0.01),
        "b_lb": nrm((N_EVEN + 1, MIX), 0.1),
        "b_norm": 1.0 + nrm((N_EVEN, MIX), 0.01),
        "w_in_odd": nrm((N_ODD, D, IN_ODD), D ** -0.5),
        "w_out_odd": nrm((N_ODD, 2 * MIX, D), (2 * MIX) ** -0.5),
        "c_g2": nrm((N_ODD, C_LORA, C_KW), C_LORA ** -0.5),
        "c_g2b": nrm((N_ODD, C_KW), 0.1),
        "c_norm": 1.0 + nrm((N_ODD, MIX), 0.01),
        "d_conv_w": nrm((N_ODD, D_CONV, 2 * MIX), 0.5),
        "d_conv_b": nrm((N_ODD, 2 * MIX), 0.01),
        "d_ib": nrm((N_ODD, D_H), 0.1),
        "d_fb": jnp.linspace(3.0, 6.0, D_H, dtype=F32)[None] + nrm((N_ODD, D_H), 0.1),
        "d_norm": 1.0 + nrm((N_ODD, MIX), 0.01),
        "final_norm": 1.0 + nrm((D,), 0.01),
    }


def reference(x_prompt, x_sample, state_shift_a, state_rwkv, state_hgrn, state_gla, state_conv_d, state_mlstm_c,
              state_mlstm_n, state_mlstm_m, norm_g, w_in_even, w_out_even, a_mu, a_w0, a_w2, a_a0, a_a2, a_kk,
              a_ka, a_rk, a_ln_w, a_ln_b, b_lb, b_norm, w_in_odd, w_out_odd, c_g2, c_g2b, c_norm, d_conv_w,
              d_conv_b, d_ib, d_fb, d_norm, final_norm):
    even_p = (w_in_even, w_out_even, a_mu, a_w0, a_w2, a_a0, a_a2, a_kk, a_ka, a_rk, a_ln_w, a_ln_b, b_lb, b_norm)
    odd_p = (w_in_odd, w_out_odd, c_g2, c_g2b, c_norm, d_conv_w, d_conv_b, d_ib, d_fb, d_norm)

    def run(x, shift, s_rwkv, s_hgrn, s_gla, conv, mc, mn, mm):
        st_e, st_o = [], []
        for layer in range(DEPTH):
            h = _rms(x, norm_g[layer]).astype(x.dtype)
            if layer % 2 == 0:
                e = layer // 2
                y, st = _even_layer(h, e, shift[e], s_rwkv[e], s_hgrn[e], *even_p)
                st_e.append(st)
            else:
                o = layer // 2
                y, st = _odd_layer(h, o, s_gla[o], conv[o], mc[o], mn[o], mm[o], *odd_p)
                st_o.append(st)
            x = x + y.astype(x.dtype)
        y = _rms(x, final_norm).astype(x.dtype)
        se = lambda j: jnp.stack([s[j] for s in st_e])
        so = lambda j: jnp.stack([s[j] for s in st_o])
        return y, se(0), se(1), se(2), so(0), so(1), so(2), so(3), so(4)

    bp = x_prompt.shape[0]
    z = lambda *s: jnp.zeros(s, x_prompt.dtype)
    (y_prompt, p_shift, p_rwkv, p_hgrn, p_gla, p_conv, p_mc, p_mn, p_mm) = run(
        x_prompt, z(N_EVEN, bp, A_SHIFT_W), z(N_EVEN, bp, A_H, A_HD, A_HD), z(N_EVEN, bp, B_H, B_HD, B_HD),
        z(N_ODD, bp, C_H, C_KD, C_VD), z(N_ODD, bp, D_CONV - 1, 2 * MIX), z(N_ODD, bp, D_H, D_HD, D_HD),
        z(N_ODD, bp, D_H, D_HD), z(N_ODD, bp, D_H))
    (y_sample, s_shift, s_rwkv, s_hgrn, s_gla, s_conv, s_mc, s_mn, s_mm) = run(
        x_sample, state_shift_a, state_rwkv, state_hgrn, state_gla, state_conv_d, state_mlstm_c,
        state_mlstm_n, state_mlstm_m)
    return (y_prompt, y_sample, p_shift, p_rwkv, p_hgrn, p_gla, p_conv, p_mc, p_mn, p_mm,
            s_shift, s_rwkv, s_hgrn, s_gla, s_conv, s_mc, s_mn, s_mm)
```

```python
import functools
import math

import jax
import jax.numpy as jnp
import numpy as np
from jax import lax
from jax.experimental import pallas as pl
from jax.experimental.pallas import tpu as pltpu

F32 = jnp.float32
BF16 = jnp.bfloat16
HI = lax.Precision.HIGHEST

D_MODEL = 1024
MIX = D_MODEL // 2
NORM_EPS = 1e-5
A_HD = 64
A_H = MIX // A_HD
A_LORA = 64
A_GN_EPS = 64e-5
A_SHIFT_W = 3 * MIX + 2 * A_LORA
B_HD = 128
B_H = MIX // B_HD
IN_EVEN = A_SHIFT_W + 5 * MIX
C_H = 4
C_VD = MIX // C_H
C_KD = C_VD // 2
C_KW = C_H * C_KD
C_LORA = 16
C_GATE_NORM = 16.0
D_H = 4
D_HD = MIX // D_H
D_CONV = 4
LANE = 128
SLAB = LANE
SLAB_I = C_LORA
SLAB_F = C_LORA + D_H
IN_ODD_PAD = 2 * C_KW + 2 * MIX + 2 * MIX + 2 * MIX + SLAB
VMEM_LIMIT = 48 * 1024 * 1024


def _dot(a, b, prec=None):
    return jnp.dot(a, b, preferred_element_type=F32, precision=prec)


def _dot_nt(a, b, prec=None):
    return lax.dot_general(a, b, (((1,), (1,)), ((), ())), preferred_element_type=F32, precision=prec)


def _dot_tn(a, b, prec=None):
    return lax.dot_general(a, b, (((0,), (0,)), ((), ())), preferred_element_type=F32, precision=prec)


def _iota(shape, dim):
    return lax.broadcasted_iota(jnp.int32, shape, dim)


def _silu(x):
    return x * jax.nn.sigmoid(x)


def _rms(x, g):
    return x * lax.rsqrt(jnp.mean(x * x, -1, keepdims=True) + NORM_EPS) * g


def _tile_rows(x, n):
    return jnp.concatenate([x] * n, axis=0)


def _stack_heads(x, n_heads, width):
    return jnp.concatenate([x[:, h * width:(h + 1) * width] for h in range(n_heads)], axis=0)


def _unstack_heads(x, n_heads, rows):
    return jnp.concatenate([x[h * rows:(h + 1) * rows] for h in range(n_heads)], axis=1)


def _head_masks(n_heads, rows, width):
    hl = n_heads * rows
    rh = _iota((hl, n_heads * width), 0) // rows
    lh = _iota((hl, n_heads * width), 1) // width
    wide = (rh == lh).astype(F32)
    ri, ci = _iota((hl, hl), 0), _iota((hl, hl), 1)
    same = (ri // rows) == (ci // rows)
    incl = same & (ci <= ri)
    strict = same & (ci < ri)
    return wide, same, incl, strict


def _tril(n):
    return (_iota((n, n), 1) <= _iota((n, n), 0)).astype(F32)


def _seg_sum(x, seg):
    bd = ((_iota((LANE, LANE), 0) // seg) == (_iota((LANE, LANE), 1) // seg)).astype(F32)
    parts = [_dot(x[:, j:j + LANE], bd, HI) for j in range(0, x.shape[1], LANE)]
    return jnp.concatenate(parts, axis=1)


def _head_rms(x, g, width):
    parts = []
    for j in range(0, x.shape[1], width):
        xs = x[:, j:j + width]
        parts.append(xs * lax.rsqrt(jnp.mean(xs * xs, -1, keepdims=True) + NORM_EPS))
    return jnp.concatenate(parts, axis=1) * g


def _inproj_kernel(x_ref, g_ref, w_ref, p_ref):
    h = _rms(x_ref[...], g_ref[...])
    p_ref[...] = _dot(h.astype(BF16), w_ref[...])


def _mid_kernel(o_ref, x_ref, wo_ref, g_ref, wi_ref, x1_ref, p_ref):
    x1 = x_ref[...] + _dot(o_ref[...].astype(BF16), wo_ref[...])
    x1_ref[...] = x1
    p_ref[...] = _dot(_rms(x1, g_ref[...]).astype(BF16), wi_ref[...])


def _final_kernel(o_ref, x_ref, wo_ref, g_ref, y_ref):
    x2 = x_ref[...] + _dot(o_ref[...].astype(BF16), wo_ref[...])
    y_ref[...] = _rms(x2, g_ref[...])


def _row_tile(m):
    return math.gcd(m, 512)


def _full(shape):
    return pl.BlockSpec(shape, lambda i: (0,) * len(shape))


def _dense_params():
    return pltpu.CompilerParams(dimension_semantics=("arbitrary",), vmem_limit_bytes=VMEM_LIMIT)


def _inproj(x, g, w):
    m, d = x.shape
    n = w.shape[1]
    tm = _row_tile(m)
    return pl.pallas_call(
        _inproj_kernel, grid=(m // tm,),
        in_specs=[pl.BlockSpec((tm, d), lambda i: (i, 0)), _full((1, d)), _full((d, n))],
        out_specs=pl.BlockSpec((tm, n), lambda i: (i, 0)),
        out_shape=jax.ShapeDtypeStruct((m, n), F32),
        compiler_params=_dense_params(), name="inproj")(x, g, w)


def _mid(o, x, wo, g, wi):
    m, d = x.shape
    n = wi.shape[1]
    tm = _row_tile(m)
    return pl.pallas_call(
        _mid_kernel, grid=(m // tm,),
        in_specs=[pl.BlockSpec((tm, d), lambda i: (i, 0)), pl.BlockSpec((tm, d), lambda i: (i, 0)),
                  _full((d, d)), _full((1, d)), _full((d, n))],
        out_specs=[pl.BlockSpec((tm, d), lambda i: (i, 0)), pl.BlockSpec((tm, n), lambda i: (i, 0))],
        out_shape=[jax.ShapeDtypeStruct((m, d), F32), jax.ShapeDtypeStruct((m, n), F32)],
        compiler_params=_dense_params(), name="outproj_inproj")(o, x, wo, g, wi)


def _final(o, x, wo, g):
    m, d = x.shape
    tm = _row_tile(m)
    return pl.pallas_call(
        _final_kernel, grid=(m // tm,),
        in_specs=[pl.BlockSpec((tm, d), lambda i: (i, 0)), pl.BlockSpec((tm, d), lambda i: (i, 0)),
                  _full((d, d)), _full((1, d))],
        out_specs=pl.BlockSpec((tm, d), lambda i: (i, 0)),
        out_shape=jax.ShapeDtypeStruct((m, d), F32),
        compiler_params=_dense_params(), name="outproj_final")(o, x, wo, g)


def _rwkv_chunk(r, k, v, kk, bv, lw, s, consts):
    tril, wide, incl, strict, eye = consts
    L = r.shape[0]
    c = _dot(tril, lw, HI)
    cl = c[L - 1:L]
    e_neg = jnp.exp(-c)
    e_end = jnp.exp(cl - c)
    tile = functools.partial(_tile_rows, n=A_H)
    kk_w = tile(kk * jnp.exp(c - lw)) * wide
    r_w = tile(r * jnp.exp(c)) * wide
    k_h = tile(k * e_neg)
    b_h = tile(bv * e_neg)
    ke_w = tile(k * e_end) * wide
    be_w = tile(bv * e_end) * wide
    a_kk = jnp.where(strict, _dot_nt(kk_w, k_h, HI), 0.0)
    a_kb = jnp.where(strict, _dot_nt(kk_w, b_h, HI), 0.0)
    a_rk = jnp.where(incl, _dot_nt(r_w, k_h, HI), 0.0)
    a_rb = jnp.where(incl, _dot_nt(r_w, b_h, HI), 0.0)
    pw = -a_kb
    t_inv = eye + pw
    n = 2
    while n < L:
        pw = _dot(pw, pw, HI)
        t_inv = _dot(t_inv, eye + pw, HI)
        n *= 2
    v_st = _stack_heads(v, A_H, A_HD)
    u = -_dot(t_inv, _dot_nt(kk_w, s, HI) + _dot(a_kk, v_st, HI), HI)
    o_st = _dot_nt(r_w, s, HI) + _dot(a_rk, v_st, HI) + _dot(a_rb, u, HI)
    s_new = s * jnp.exp(cl) + _dot_tn(v_st, ke_w, HI) + _dot_tn(u, be_w, HI)
    return _unstack_heads(o_st, A_H, L), s_new


def _gla_chunk(q, k, logg, v, s, consts, n_heads, kd, vd):
    tril, wide, incl, ones_v = consts
    L = q.shape[0]
    tile = functools.partial(_tile_rows, n=n_heads)
    b = _dot(tril, logg, HI)
    bl = b[L - 1:L]
    q_w = tile(q * jnp.exp(b)) * wide
    att = jnp.where(incl, _dot_nt(q_w, tile(k * jnp.exp(-b)), HI), 0.0)
    v_st = _stack_heads(v, n_heads, vd)
    o_st = _dot(att, v_st, HI) + _dot(q_w, s, HI)
    dcol = jnp.exp(_dot_tn(tile(logg) * wide, ones_v, HI))
    s_new = s * dcol + _dot_tn(tile(k * jnp.exp(bl - b)) * wide, v_st, HI)
    return _unstack_heads(o_st, n_heads, L), s_new


def _mlstm_chunk(q, k, v, li, lf, c, nrow, m, consts):
    wide, same, incl, tril_bd, ones_bd, lane0 = consts
    L = q.shape[0]
    hl = D_H * L
    tile = functools.partial(_tile_rows, n=D_H)
    bc = lambda col: jnp.broadcast_to(col, (hl, LANE))
    to_row = lambda col: _dot_nt(lane0, bc(col), HI)
    a = _dot(tril_bd, bc(lf), HI)[:, :1]
    a_last = _dot(ones_bd, bc(lf), HI)[:, :1]
    dlog = jnp.where(incl, a - to_row(a) + to_row(li), -jnp.inf)
    m_intra = jnp.max(dlog, -1, keepdims=True)
    q_st = _stack_heads(q, D_H, D_HD)
    k_st = _stack_heads(k, D_H, D_HD)
    v_st = _stack_heads(v, D_H, D_HD)
    s_intra = jnp.exp(dlog - m_intra) * _dot_nt(q_st, k_st, HI)
    num_intra = _dot(s_intra, v_st, HI)
    den_intra = jnp.sum(s_intra, -1, keepdims=True)
    e_end = a_last - a + li
    m_loc = jnp.max(jnp.where(same, to_row(e_end), -jnp.inf), -1, keepdims=True)
    w_end = jnp.exp(e_end - m_loc)
    c_loc = _dot_tn(tile(v) * wide * w_end, k_st, HI)
    n_loc = _dot(ones_bd, w_end * k_st, HI)
    g = a + m
    m_t = jnp.maximum(g, m_intra)
    f_inter = jnp.exp(g - m_t)
    f_intra = jnp.exp(m_intra - m_t)
    qc = _dot_nt(q_st, c, HI)
    num_inter = jnp.concatenate([qc[h * L:(h + 1) * L, h * D_HD:(h + 1) * D_HD] for h in range(D_H)], axis=0)
    den_inter = jnp.sum(q_st * nrow, -1, keepdims=True)
    num = num_intra * f_intra + num_inter * f_inter
    den = f_intra * den_intra + f_inter * den_inter
    h_st = num / jnp.maximum(jnp.abs(den), jnp.exp(-m_t))
    m_new = jnp.maximum(a_last + m, m_loc)
    fo = jnp.exp(a_last + m - m_new)
    fl = jnp.exp(m_loc - m_new)
    per_value = lambda col: jnp.concatenate(
        [jnp.broadcast_to(col[h * L:h * L + 1], (D_HD, 1)) for h in range(D_H)], axis=0)
    c_new = per_value(fo) * c + per_value(fl) * c_loc
    n_new = fo * nrow + fl * n_loc
    return _unstack_heads(h_st, D_H, L), c_new, n_new, m_new


(_E_W0, _E_A0, _E_KK, _E_KA, _E_RK, _E_LNW, _E_LNB, _E_BNORM, _E_LB0, _E_LB1) = range(10)


def _even_kernel(p_ref, shift0_ref, srw0_ref, shg0_ref, mu_ref, w2_ref, a2_ref, par_ref,
                 o_ref, shift_ref, srw_ref, shg_ref,
                 prev_sc, srw_sc, shg_sc, r_sc, k_sc, v_sc, kk_sc, bv_sc, lw_sc, oa_sc,
                 q_sc, kf_sc, lg_sc, vi_sc, ob_sc, *, tb, la, lb):
    j = pl.program_id(1)
    nj = pl.num_programs(1)

    @pl.when(j == 0)
    def _():
        prev_sc[...] = jnp.broadcast_to(shift0_ref[...], prev_sc.shape)
        srw_sc[...] = jnp.concatenate([srw0_ref[h] for h in range(A_H)], axis=1)
        shg_sc[...] = shg0_ref[...]

    par = par_ref[...]
    row = lambda i: par[i:i + 1]

    pa = p_ref[:, :A_SHIFT_W]
    rolled = pltpu.roll(pa, 1, axis=0)
    first = _iota(pa.shape, 0) == 0
    shifted = jnp.where(first, jnp.broadcast_to(prev_sc[0:1], pa.shape), rolled)
    prev_sc[...] = jnp.broadcast_to(pa[tb - 1:tb], prev_sc.shape)
    xm = pa + (shifted - pa) * mu_ref[...]
    r = xm[:, :MIX]
    k = xm[:, MIX:2 * MIX]
    v = xm[:, 2 * MIX:3 * MIX]
    lora_in = xm[:, 3 * MIX:]
    w = -jax.nn.softplus(-(row(_E_W0) + _dot(jnp.tanh(lora_in).astype(BF16), w2_ref[...]))) - 0.5
    a = jax.nn.sigmoid(row(_E_A0) + _dot(lora_in.astype(BF16), a2_ref[...]))
    kk = k * row(_E_KK)
    kk = kk * lax.rsqrt(jnp.maximum(_seg_sum(kk * kk, A_HD), 1e-12))
    k = k * (1.0 + (a - 1.0) * row(_E_KA))
    r_sc[...] = r
    k_sc[...] = k
    v_sc[...] = v
    kk_sc[...] = kk
    bv_sc[...] = kk * a
    lw_sc[...] = -jnp.exp(w)

    wide, _, incl, strict = _head_masks(A_H, la, A_HD)
    hl = A_H * la
    eye = (_iota((hl, hl), 0) == _iota((hl, hl), 1)).astype(F32)
    consts_a = (_tril(la), wide, incl, strict, eye)

    def rwkv_body(ci, s):
        sl = pl.ds(pl.multiple_of(ci * la, la), la)
        o, s = _rwkv_chunk(r_sc[sl, :], k_sc[sl, :], v_sc[sl, :], kk_sc[sl, :], bv_sc[sl, :], lw_sc[sl, :],
                           s, consts_a)
        oa_sc[sl, :] = o
        return s

    srw_sc[...] = lax.fori_loop(0, tb // la, rwkv_body, srw_sc[...])

    e0 = row(_E_LB0)
    e1 = row(_E_LB1)
    emax = jnp.maximum(e0, e1)
    e0 = jnp.exp(e0 - emax)
    lower = e0 / (e0 + jnp.exp(e1 - emax))
    base = A_SHIFT_W + MIX
    g = lower + (1.0 - lower) * jax.nn.sigmoid(p_ref[:, base + MIX:base + 2 * MIX])
    q_sc[...] = _silu(p_ref[:, base:base + MIX]) * (B_HD ** -0.5)
    kf_sc[...] = 1.0 - g
    lg_sc[...] = jnp.log(g)
    vi_sc[...] = p_ref[:, base + 2 * MIX:base + 3 * MIX]

    wide_b, _, incl_b, _ = _head_masks(B_H, lb, B_HD)
    consts_b = (_tril(lb), wide_b, incl_b, jnp.ones((B_H * lb, B_HD), F32))

    def hgrn_body(ci, s):
        sl = pl.ds(pl.multiple_of(ci * lb, lb), lb)
        o, s = _gla_chunk(q_sc[sl, :], kf_sc[sl, :], lg_sc[sl, :], vi_sc[sl, :], s, consts_b, B_H, B_HD, B_HD)
        ob_sc[sl, :] = o
        return s

    shg_sc[...] = lax.fori_loop(0, tb // lb, hgrn_body, shg_sc[...])

    o = oa_sc[...]
    mean = _seg_sum(o, A_HD) * (1.0 / A_HD)
    cen = o - mean
    var = _seg_sum(cen * cen, A_HD) * (1.0 / A_HD)
    oa = cen * lax.rsqrt(var + A_GN_EPS) * row(_E_LNW) + row(_E_LNB)
    oa = oa + _seg_sum(r_sc[...] * k_sc[...] * row(_E_RK), A_HD) * v_sc[...]
    oa = oa * _silu(p_ref[:, A_SHIFT_W:A_SHIFT_W + MIX])
    ob = _head_rms(ob_sc[...], row(_E_BNORM), B_HD) * _silu(p_ref[:, base + 3 * MIX:base + 4 * MIX])
    o_ref[...] = jnp.concatenate([oa, ob], axis=1).astype(o_ref.dtype)

    @pl.when(j == nj - 1)
    def _():
        shift_ref[...] = prev_sc[0:1]
        s = srw_sc[...]
        for h in range(A_H):
            srw_ref[h] = s[:, h * A_HD:(h + 1) * A_HD]
        shg_ref[...] = shg_sc[...]


def _even_mixer(p, shift0, srw0, shg0, mu, w2p, a2p, par, tb, la, lb):
    bsz, t, _ = p.shape
    kern = functools.partial(_even_kernel, tb=tb, la=la, lb=lb)
    bmap3 = lambda b, j: (b, 0, 0)
    bmap4 = lambda b, j: (b, 0, 0, 0)
    cmap = lambda b, j: (0, 0)
    blk = lambda: pltpu.VMEM((tb, MIX), F32)
    return pl.pallas_call(
        kern, grid=(bsz, t // tb),
        in_specs=[pl.BlockSpec((None, tb, IN_EVEN), lambda b, j: (b, j, 0)),
                  pl.BlockSpec((None, 1, A_SHIFT_W), bmap3),
                  pl.BlockSpec((None, A_H, A_HD, A_HD), bmap4),
                  pl.BlockSpec((None, B_H * B_HD, B_HD), bmap3),
                  pl.BlockSpec((1, A_SHIFT_W), cmap),
                  pl.BlockSpec((2 * A_LORA, MIX), cmap),
                  pl.BlockSpec((2 * A_LORA, MIX), cmap),
                  pl.BlockSpec((16, MIX), cmap)],
        out_specs=[pl.BlockSpec((None, tb, D_MODEL), lambda b, j: (b, j, 0)),
                   pl.BlockSpec((None, 1, A_SHIFT_W), bmap3),
                   pl.BlockSpec((None, A_H, A_HD, A_HD), bmap4),
                   pl.BlockSpec((None, B_H * B_HD, B_HD), bmap3)],
        out_shape=[jax.ShapeDtypeStruct((bsz, t, D_MODEL), F32),
                   jax.ShapeDtypeStruct((bsz, 1, A_SHIFT_W), F32),
                   jax.ShapeDtypeStruct((bsz, A_H, A_HD, A_HD), F32),
                   jax.ShapeDtypeStruct((bsz, B_H * B_HD, B_HD), F32)],
        scratch_shapes=[pltpu.VMEM((8, A_SHIFT_W), F32), pltpu.VMEM((A_HD, MIX), F32),
                        pltpu.VMEM((B_H * B_HD, B_HD), F32)] + [blk() for _ in range(12)],
        compiler_params=pltpu.CompilerParams(dimension_semantics=("arbitrary", "arbitrary"),
                                             vmem_limit_bytes=VMEM_LIMIT),
        name="mixer_rwkv_hgrn")(p, shift0, srw0, shg0, mu, w2p, a2p, par)


_O_CQ = 0
_O_CK = _O_CQ + C_KW
_O_CV = _O_CK + C_KW
_O_GC = _O_CV + MIX
_O_DQK = _O_GC + MIX
_O_DV = _O_DQK + 2 * MIX
_O_GD = _O_DV + MIX
_O_SLAB = _O_GD + MIX
(_P_CNORM, _P_DNORM, _P_G2B) = range(3)


def _odd_kernel(p_ref, sgl0_ref, conv0_ref, mc0_ref, mn0_ref, mm0_ref, g2_ref, cw_ref, cb_ref, sb_ref, par_ref,
                o_ref, sgl_ref, conv_ref, mc_ref, mn_ref, mm_ref,
                prev_sc, sgl_sc, mc_sc, mn_sc, mm_sc, cq_sc, ck_sc, lg_sc, oc_sc,
                dq_sc, dk_sc, li_sc, lf_sc, od_sc, *, tb, lc):
    j = pl.program_id(1)
    nj = pl.num_programs(1)
    hl = D_H * lc
    head_rows = lambda x: jnp.concatenate(
        [jnp.broadcast_to(x[h:h + 1], (lc, x.shape[1])) for h in range(D_H)], axis=0)

    @pl.when(j == 0)
    def _():
        prev_sc[...] = conv0_ref[...]
        sgl_sc[...] = sgl0_ref[...]
        mc_sc[...] = mc0_ref[...]
        mn_sc[...] = head_rows(mn0_ref[...])
        mm_sc[...] = head_rows(mm0_ref[...])[:, :1]

    par = par_ref[...]
    row = lambda i: par[i:i + 1]

    slab = p_ref[:, _O_SLAB:_O_SLAB + SLAB]
    pre = _dot(slab.astype(BF16), g2_ref[...]) + row(_P_G2B)[:, :C_KW]
    lg_sc[...] = jax.nn.log_sigmoid(pre) * (1.0 / C_GATE_NORM)
    cq_sc[...] = p_ref[:, _O_CQ:_O_CQ + C_KW] * (C_KD ** -0.5)
    ck_sc[...] = p_ref[:, _O_CK:_O_CK + C_KW]

    wide_c, _, incl_c, _ = _head_masks(C_H, lc, C_KD)
    consts_c = (_tril(lc), wide_c, incl_c, jnp.ones((C_H * lc, C_VD), F32))

    def gla_body(ci, s):
        sl = pl.ds(pl.multiple_of(ci * lc, lc), lc)
        o, s = _gla_chunk(cq_sc[sl, :], ck_sc[sl, :], lg_sc[sl, :], p_ref[sl, _O_CV:_O_CV + MIX], s,
                          consts_c, C_H, C_KD, C_VD)
        oc_sc[sl, :] = o
        return s

    sgl_sc[...] = lax.fori_loop(0, tb // lc, gla_body, sgl_sc[...])

    x = p_ref[:, _O_DQK:_O_DQK + 2 * MIX]
    prev = prev_sc[...]
    cw = cw_ref[...]
    conv = cb_ref[...] + x * cw[D_CONV - 1:D_CONV]
    head_row = _iota((8, 2 * MIX), 0)
    for s in range(1, D_CONV):
        xr = pltpu.roll(x, s, axis=0)
        top = jnp.where(head_row < s, pltpu.roll(prev, s, axis=0), xr[:8])
        xs = top if tb == 8 else jnp.concatenate([top, xr[8:]], axis=0)
        conv = conv + xs * cw[D_CONV - 1 - s:D_CONV - s]
    prev_sc[...] = x[tb - 8:]
    conv = _silu(conv)
    dq_sc[...] = conv[:, :MIX]
    dk_sc[...] = conv[:, MIX:] * (D_HD ** -0.5)
    gates = slab + sb_ref[...]
    lane = _iota((hl, SLAB), 1)
    head = _iota((hl, SLAB), 0) // lc
    pick = lambda z, first: jnp.sum(jnp.where(lane == head + first, _tile_rows(z, D_H), 0.0), -1, keepdims=True)

    wide_d, same_d, incl_d, _ = _head_masks(D_H, lc, D_HD)
    lane0 = (_iota((hl, LANE), 1) == 0).astype(F32)
    consts_d = (wide_d, same_d, incl_d, incl_d.astype(F32), same_d.astype(F32), lane0)

    def mlstm_body(ci, carry):
        c, n, m = carry
        sl = pl.ds(pl.multiple_of(ci * lc, lc), lc)
        gsl = gates_sc_read(sl)
        li = pick(gsl, SLAB_I)
        lf = pick(jax.nn.log_sigmoid(gsl), SLAB_F)
        o, c, n, m = _mlstm_chunk(dq_sc[sl, :], dk_sc[sl, :], p_ref[sl, _O_DV:_O_DV + MIX], li, lf, c, n, m,
                                  consts_d)
        od_sc[sl, :] = o
        return c, n, m

    li_sc[...] = gates
    gates_sc_read = lambda sl: li_sc[sl, :]
    c, n, m = lax.fori_loop(0, tb // lc, mlstm_body, (mc_sc[...], mn_sc[...], mm_sc[...]))
    mc_sc[...] = c
    mn_sc[...] = n
    mm_sc[...] = m

    oc = _head_rms(oc_sc[...], row(_P_CNORM), C_VD) * _silu(p_ref[:, _O_GC:_O_GC + MIX])
    od = _head_rms(od_sc[...], row(_P_DNORM), D_HD) * _silu(p_ref[:, _O_GD:_O_GD + MIX])
    o_ref[...] = jnp.concatenate([oc, od], axis=1).astype(o_ref.dtype)

    @pl.when(j == nj - 1)
    def _():
        sgl_ref[...] = sgl_sc[...]
        conv_ref[...] = prev_sc[...]
        mc_ref[...] = mc_sc[...]
        mn_ref[...] = jnp.concatenate([mn_sc[h * lc:h * lc + 1] for h in range(D_H)], axis=0)
        mm_ref[...] = jnp.concatenate(
            [jnp.broadcast_to(mm_sc[h * lc:h * lc + 1], (1, LANE)) for h in range(D_H)], axis=0)


def _odd_mixer(p, sgl0, conv0, mc0, mn0, mm0, g2p, cw, cb, sb, par, tb, lc):
    bsz, t, _ = p.shape
    hl = D_H * lc
    kern = functools.partial(_odd_kernel, tb=tb, lc=lc)
    bmap3 = lambda b, j: (b, 0, 0)
    cmap = lambda b, j: (0, 0)
    blk = lambda w: pltpu.VMEM((tb, w), F32)
    return pl.pallas_call(
        kern, grid=(bsz, t // tb),
        in_specs=[pl.BlockSpec((None, tb, IN_ODD_PAD), lambda b, j: (b, j, 0)),
                  pl.BlockSpec((None, C_KW, C_VD), bmap3),
                  pl.BlockSpec((None, 8, 2 * MIX), bmap3),
                  pl.BlockSpec((None, MIX, D_HD), bmap3),
                  pl.BlockSpec((None, D_H, D_HD), bmap3),
                  pl.BlockSpec((None, D_H, LANE), bmap3),
                  pl.BlockSpec((SLAB, C_KW), cmap),
                  pl.BlockSpec((D_CONV, 2 * MIX), cmap),
                  pl.BlockSpec((1, 2 * MIX), cmap),
                  pl.BlockSpec((1, SLAB), cmap),
                  pl.BlockSpec((8, MIX), cmap)],
        out_specs=[pl.BlockSpec((None, tb, D_MODEL), lambda b, j: (b, j, 0)),
                   pl.BlockSpec((None, C_KW, C_VD), bmap3),
                   pl.BlockSpec((None, 8, 2 * MIX), bmap3),
                   pl.BlockSpec((None, MIX, D_HD), bmap3),
                   pl.BlockSpec((None, D_H, D_HD), bmap3),
                   pl.BlockSpec((None, D_H, LANE), bmap3)],
        out_shape=[jax.ShapeDtypeStruct((bsz, t, D_MODEL), F32),
                   jax.ShapeDtypeStruct((bsz, C_KW, C_VD), F32),
                   jax.ShapeDtypeStruct((bsz, 8, 2 * MIX), F32),
                   jax.ShapeDtypeStruct((bsz, MIX, D_HD), F32),
                   jax.ShapeDtypeStruct((bsz, D_H, D_HD), F32),
                   jax.ShapeDtypeStruct((bsz, D_H, LANE), F32)],
        scratch_shapes=[pltpu.VMEM((8, 2 * MIX), F32), pltpu.VMEM((C_KW, C_VD), F32),
                        pltpu.VMEM((MIX, D_HD), F32), pltpu.VMEM((hl, D_HD), F32), pltpu.VMEM((hl, 1), F32),
                        blk(C_KW), blk(C_KW), blk(C_KW), blk(MIX),
                        blk(MIX), blk(MIX), blk(SLAB), blk(SLAB), blk(MIX)],
        compiler_params=pltpu.CompilerParams(dimension_semantics=("arbitrary", "arbitrary"),
                                             vmem_limit_bytes=VMEM_LIMIT),
        name="mixer_gla_mlstm")(p, sgl0, conv0, mc0, mn0, mm0, g2p, cw, cb, sb, par)


def _time_block(t):
    return math.gcd(t, 256)


def _run(x, shift, s_rwkv, s_hgrn, s_gla, conv, mc, mn, mm, wts):
    bsz, t, d = x.shape
    tb = _time_block(t)
    la = math.gcd(t, 16)
    lbc = math.gcd(t, 32)
    x2 = x.reshape(bsz * t, d)
    p0 = _inproj(x2, wts["g0"], wts["w_in0"])
    o0, shift_n, srw_n, shg_n = _even_mixer(
        p0.reshape(bsz, t, IN_EVEN), shift[0][:, None], s_rwkv[0], s_hgrn[0].reshape(bsz, B_H * B_HD, B_HD),
        wts["mu"], wts["w2p"], wts["a2p"], wts["par_e"], tb, la, lbc)
    x1, p1 = _mid(o0.reshape(bsz * t, d), x2, wts["w_out0"], wts["g1"], wts["w_in1"])
    conv8 = jnp.pad(conv[0], ((0, 0), (8 - (D_CONV - 1), 0), (0, 0)))
    mm_l = jnp.broadcast_to(mm[0][:, :, None], (bsz, D_H, LANE))
    o1, sgl_n, conv_n, mc_n, mn_n, mm_n = _odd_mixer(
        p1.reshape(bsz, t, IN_ODD_PAD), s_gla[0].reshape(bsz, C_KW, C_VD), conv8,
        mc[0].reshape(bsz, MIX, D_HD), mn[0], mm_l,
        wts["g2p"], wts["cw"], wts["cb"], wts["sb"], wts["par_o"], tb, lbc)
    y = _final(o1.reshape(bsz * t, d), x1, wts["w_out1"], wts["gf"])
    return (y.reshape(bsz, t, d), shift_n.reshape(1, bsz, A_SHIFT_W), srw_n[None],
            shg_n.reshape(1, bsz, B_H, B_HD, B_HD), sgl_n.reshape(1, bsz, C_H, C_KD, C_VD),
            conv_n[None, :, 8 - (D_CONV - 1):], mc_n.reshape(1, bsz, D_H, D_HD, D_HD), mn_n[None],
            mm_n[None, :, :, 0])


def _odd_column_order():
    sizes = (C_KW, C_KW, MIX, C_LORA, MIX, 2 * MIX, MIX, D_H, D_H, MIX)
    off = np.cumsum((0,) + sizes)
    cq, ck, cv, cg, gc, dqk, dv, di, df, gd = (np.arange(off[i], off[i + 1]) for i in range(len(sizes)))
    return np.concatenate([cq, ck, cv, gc, dqk, dv, gd, cg, di, df])


def kernel(x_prompt, x_sample, state_shift_a, state_rwkv, state_hgrn, state_gla, state_conv_d, state_mlstm_c,
           state_mlstm_n, state_mlstm_m, norm_g, w_in_even, w_out_even, a_mu, a_w0, a_w2, a_a0, a_a2, a_kk,
           a_ka, a_rk, a_ln_w, a_ln_b, b_lb, b_norm, w_in_odd, w_out_odd, c_g2, c_g2b, c_norm, d_conv_w,
           d_conv_b, d_ib, d_fb, d_norm, final_norm):
    assert w_in_even.shape[0] == 1 and w_in_odd.shape[0] == 1 and b_lb.shape[0] == 2
    zpad = lambda a, rows_before, rows_total: jnp.pad(a, ((rows_before, rows_total - rows_before - a.shape[0]), (0, 0)))
    par_e = jnp.concatenate([a_w0, a_a0, a_kk, a_ka, a_rk, a_ln_w, a_ln_b, b_norm, b_lb[0:1], b_lb[1:2]], axis=0)
    g2b = jnp.pad(c_g2b, ((0, 0), (0, MIX - C_KW)))
    par_o = jnp.concatenate([c_norm, d_norm, g2b], axis=0)
    w_in1 = jnp.pad(w_in_odd[0][:, _odd_column_order()], ((0, 0), (0, SLAB - C_LORA - 2 * D_H)))
    sb = jnp.pad(jnp.concatenate([d_ib, d_fb], axis=1), ((0, 0), (SLAB_I, SLAB - SLAB_I - 2 * D_H)))
    wts = {
        "g0": norm_g[0:1], "g1": norm_g[1:2], "gf": final_norm[None],
        "w_in0": w_in_even[0].astype(BF16), "w_out0": w_out_even[0].astype(BF16),
        "w_in1": w_in1.astype(BF16), "w_out1": w_out_odd[0].astype(BF16),
        "mu": a_mu, "w2p": zpad(a_w2[0], 0, 2 * A_LORA).astype(BF16),
        "a2p": zpad(a_a2[0], A_LORA, 2 * A_LORA).astype(BF16),
        "par_e": zpad(par_e, 0, 16), "par_o": zpad(par_o, 0, 8),
        "g2p": zpad(c_g2[0], 0, SLAB).astype(BF16), "cw": d_conv_w[0], "cb": d_conv_b, "sb": sb,
    }
    bp = x_prompt.shape[0]
    z = lambda *s: jnp.zeros(s, x_prompt.dtype)
    prompt = _run(x_prompt, z(1, bp, A_SHIFT_W), z(1, bp, A_H, A_HD, A_HD), z(1, bp, B_H, B_HD, B_HD),
                  z(1, bp, C_H, C_KD, C_VD), z(1, bp, D_CONV - 1, 2 * MIX), z(1, bp, D_H, D_HD, D_HD),
                  z(1, bp, D_H, D_HD), z(1, bp, D_H), wts)
    sample = _run(x_sample, state_shift_a, state_rwkv, state_hgrn, state_gla, state_conv_d, state_mlstm_c,
                  state_mlstm_n, state_mlstm_m, wts)
    return (prompt[0], sample[0]) + prompt[1:] + sample[1:]
```

```python
import functools
import math

import jax
import jax.numpy as jnp
import numpy as np
from jax import lax
from jax.experimental import pallas as pl
from jax.experimental.pallas import tpu as pltpu

F32 = jnp.float32
BF16 = jnp.bfloat16

D_MODEL = 1024
MIX = D_MODEL // 2
NORM_EPS = 1e-5
A_HD = 64
A_H = MIX // A_HD
A_LORA = 64
A_GN_EPS = 64e-5
A_SHIFT_W = 3 * MIX + 2 * A_LORA
B_HD = 128
B_H = MIX // B_HD
IN_EVEN = A_SHIFT_W + 5 * MIX
C_H = 4
C_VD = MIX // C_H
C_KD = C_VD // 2
C_KW = C_H * C_KD
C_LORA = 16
C_GATE_NORM = 16.0
D_H = 4
D_HD = MIX // D_H
D_CONV = 4
LANE = 128
SLAB = LANE
SLAB_I = C_LORA
SLAB_F = C_LORA + D_H
IN_ODD_PAD = 2 * C_KW + 2 * MIX + 2 * MIX + 2 * MIX + SLAB
VMEM_LIMIT = 48 * 1024 * 1024


_DIMS = {"nn": ((1,), (0,)), "nt": ((1,), (1,)), "tn": ((0,), (0,))}
M_EXACT_R = "rx"
M_EXACT_L = "lx"
M_G = "bf"
M_TINV = "x3"
M_ST = "bf"
M_ATT = "bf"


def _split2(x):
    hi = x.astype(BF16)
    return hi, (x - hi.astype(F32)).astype(BF16)


def _split3(x):
    hi = x.astype(BF16)
    r1 = x - hi.astype(F32)
    mid = r1.astype(BF16)
    return hi, mid, (r1 - mid.astype(F32)).astype(BF16)


def _mm(a, b, form, mode):
    dn = (_DIMS[form], ((), ()))
    d = lambda x, y: lax.dot_general(x, y, dn, preferred_element_type=F32)
    if mode == "bf":
        return d(a.astype(BF16), b.astype(BF16))
    if mode == "x3":
        ah, al = _split2(a)
        bh, bl = _split2(b)
        return d(ah, bh) + (d(ah, bl) + d(al, bh))
    if mode == "lx":
        bb = b.astype(BF16)
        h, m, l = _split3(a)
        return d(h, bb) + (d(m, bb) + d(l, bb))
    assert mode == "rx"
    ab = a.astype(BF16)
    h, m, l = _split3(b)
    return d(ab, h) + (d(ab, m) + d(ab, l))


def _dot(a, b):
    return jnp.dot(a, b, preferred_element_type=F32)


def _iota(shape, dim):
    return lax.broadcasted_iota(jnp.int32, shape, dim)


def _silu(x):
    return x * jax.nn.sigmoid(x)


def _rms(x, g):
    return x * lax.rsqrt(jnp.mean(x * x, -1, keepdims=True) + NORM_EPS) * g


def _tile_rows(x, n):
    return jnp.concatenate([x] * n, axis=0)


def _stack_heads(x, n_heads, width):
    return jnp.concatenate([x[:, h * width:(h + 1) * width] for h in range(n_heads)], axis=0)


def _unstack_heads(x, n_heads, rows):
    return jnp.concatenate([x[h * rows:(h + 1) * rows] for h in range(n_heads)], axis=1)


def _head_masks(n_heads, rows, width):
    hl = n_heads * rows
    rh = _iota((hl, n_heads * width), 0) // rows
    lh = _iota((hl, n_heads * width), 1) // width
    wide = (rh == lh).astype(F32)
    ri, ci = _iota((hl, hl), 0), _iota((hl, hl), 1)
    same = (ri // rows) == (ci // rows)
    incl = same & (ci <= ri)
    strict = same & (ci < ri)
    return wide, same, incl, strict


def _chunk_mats(n, chunk):
    ri, ci = _iota((n, n), 0), _iota((n, n), 1)
    same = (ri // chunk) == (ci // chunk)
    return (same & (ci <= ri)).astype(BF16), same.astype(BF16)


def _seg_sum(x, seg):
    bd = ((_iota((LANE, LANE), 0) // seg) == (_iota((LANE, LANE), 1) // seg)).astype(BF16)
    parts = [_mm(x[:, j:j + LANE], bd, "nn", M_EXACT_L) for j in range(0, x.shape[1], LANE)]
    return jnp.concatenate(parts, axis=1)


def _head_rms(x, g, width):
    parts = []
    for j in range(0, x.shape[1], width):
        xs = x[:, j:j + width]
        parts.append(xs * lax.rsqrt(jnp.mean(xs * xs, -1, keepdims=True) + NORM_EPS))
    return jnp.concatenate(parts, axis=1) * g


def _inproj_kernel(x_ref, g_ref, w_ref, p_ref):
    h = _rms(x_ref[...], g_ref[...])
    p_ref[...] = _dot(h.astype(BF16), w_ref[...])


def _mid_kernel(o_ref, x_ref, wo_ref, g_ref, wi_ref, x1_ref, p_ref):
    x1 = x_ref[...] + _dot(o_ref[...].astype(BF16), wo_ref[...])
    x1_ref[...] = x1
    p_ref[...] = _dot(_rms(x1, g_ref[...]).astype(BF16), wi_ref[...])


def _final_kernel(o_ref, x_ref, wo_ref, g_ref, y_ref):
    x2 = x_ref[...] + _dot(o_ref[...].astype(BF16), wo_ref[...])
    y_ref[...] = _rms(x2, g_ref[...])


def _row_tile(m):
    return math.gcd(m, 512)


def _full(shape):
    return pl.BlockSpec(shape, lambda i: (0,) * len(shape))


def _dense_params():
    return pltpu.CompilerParams(dimension_semantics=("arbitrary",), vmem_limit_bytes=VMEM_LIMIT)


def _inproj(x, g, w):
    m, d = x.shape
    n = w.shape[1]
    tm = _row_tile(m)
    return pl.pallas_call(
        _inproj_kernel, grid=(m // tm,),
        in_specs=[pl.BlockSpec((tm, d), lambda i: (i, 0)), _full((1, d)), _full((d, n))],
        out_specs=pl.BlockSpec((tm, n), lambda i: (i, 0)),
        out_shape=jax.ShapeDtypeStruct((m, n), F32),
        compiler_params=_dense_params(), name="inproj")(x, g, w)


def _mid(o, x, wo, g, wi):
    m, d = x.shape
    n = wi.shape[1]
    tm = _row_tile(m)
    return pl.pallas_call(
        _mid_kernel, grid=(m // tm,),
        in_specs=[pl.BlockSpec((tm, d), lambda i: (i, 0)), pl.BlockSpec((tm, d), lambda i: (i, 0)),
                  _full((d, d)), _full((1, d)), _full((d, n))],
        out_specs=[pl.BlockSpec((tm, d), lambda i: (i, 0)), pl.BlockSpec((tm, n), lambda i: (i, 0))],
        out_shape=[jax.ShapeDtypeStruct((m, d), F32), jax.ShapeDtypeStruct((m, n), F32)],
        compiler_params=_dense_params(), name="outproj_inproj")(o, x, wo, g, wi)


def _final(o, x, wo, g):
    m, d = x.shape
    tm = _row_tile(m)
    return pl.pallas_call(
        _final_kernel, grid=(m // tm,),
        in_specs=[pl.BlockSpec((tm, d), lambda i: (i, 0)), pl.BlockSpec((tm, d), lambda i: (i, 0)),
                  _full((d, d)), _full((1, d))],
        out_specs=pl.BlockSpec((tm, d), lambda i: (i, 0)),
        out_shape=jax.ShapeDtypeStruct((m, d), F32),
        compiler_params=_dense_params(), name="outproj_final")(o, x, wo, g)


def _rwkv_chunk(kkt, rt, kh, bh, ke, be, v, ecl, s, consts):
    wide, incl, strict, eye = consts
    L = kkt.shape[0]
    hl = A_H * L
    tile = functools.partial(_tile_rows, n=A_H)
    x_w = jnp.concatenate([tile(kkt) * wide, tile(rt) * wide], axis=0)
    g = _mm(x_w, jnp.concatenate([tile(kh), tile(bh)], axis=0), "nt", M_G)
    a_kk = jnp.where(strict, g[:hl, :hl], 0.0)
    a_kb = jnp.where(strict, g[:hl, hl:], 0.0)
    a_r = jnp.concatenate([jnp.where(incl, g[hl:, :hl], 0.0), jnp.where(incl, g[hl:, hl:], 0.0)], axis=1)
    pw = -a_kb
    t_inv = eye + pw
    n = 2
    while n < L:
        pw = _mm(pw, pw, "nn", M_TINV)
        t_inv = _mm(t_inv, eye + pw, "nn", M_TINV)
        n *= 2
    v_st = _stack_heads(v, A_H, A_HD)
    xs = _mm(x_w, s, "nt", M_ST)
    u = -_mm(t_inv, xs[:hl] + _mm(a_kk, v_st, "nn", M_ST), "nn", M_ST)
    vu = jnp.concatenate([v_st, u], axis=0)
    o_st = xs[hl:] + _mm(a_r, vu, "nn", M_ST)
    e_w = jnp.concatenate([tile(ke) * wide, tile(be) * wide], axis=0)
    s_new = s * ecl + _mm(vu, e_w, "tn", M_ST)
    return _unstack_heads(o_st, A_H, L), s_new


def _gla_chunk(qd, ki, kend, bl, v, s, consts, n_heads, vd):
    wide, incl = consts
    L = qd.shape[0]
    tile = functools.partial(_tile_rows, n=n_heads)
    q_w = tile(qd) * wide
    att = jnp.where(incl, _mm(q_w, tile(ki), "nt", M_ATT), 0.0)
    v_st = _stack_heads(v, n_heads, vd)
    o_st = _mm(att, v_st, "nn", M_ATT) + _mm(q_w, s, "nn", M_ATT)
    dcol = jnp.exp(jnp.broadcast_to(bl, (vd, bl.shape[1])).T)
    s_new = s * dcol + _mm(tile(kend) * wide, v_st, "tn", M_ATT)
    return _unstack_heads(o_st, n_heads, L), s_new


def _mlstm_chunk(q, k, v, li, a, a_last, c, nrow, m, consts):
    wide, same, incl, ones_bd = consts
    L = q.shape[0]
    hl = D_H * L
    tile = functools.partial(_tile_rows, n=D_H)
    to_row = lambda col: jnp.broadcast_to(col, (hl, LANE)).T[:hl]
    dlog = jnp.where(incl, a - to_row(a) + to_row(li), -jnp.inf)
    m_intra = jnp.max(dlog, -1, keepdims=True)
    q_st = _stack_heads(q, D_H, D_HD)
    k_st = _stack_heads(k, D_H, D_HD)
    v_st = _stack_heads(v, D_H, D_HD)
    s_intra = jnp.exp(dlog - m_intra) * _mm(q_st, k_st, "nt", M_ATT)
    num_intra = _mm(s_intra, v_st, "nn", M_ATT)
    den_intra = jnp.sum(s_intra, -1, keepdims=True)
    e_end = a_last - a + li
    m_loc = jnp.max(jnp.where(same, to_row(e_end), -jnp.inf), -1, keepdims=True)
    w_end = jnp.exp(e_end - m_loc)
    c_loc = _mm(tile(v) * wide * w_end, k_st, "tn", M_ATT)
    n_loc = _mm(ones_bd, w_end * k_st, "nn", M_ATT)
    g = a + m
    m_t = jnp.maximum(g, m_intra)
    f_inter = jnp.exp(g - m_t)
    f_intra = jnp.exp(m_intra - m_t)
    qc = _mm(q_st, c, "nt", M_ATT)
    num_inter = jnp.concatenate([qc[h * L:(h + 1) * L, h * D_HD:(h + 1) * D_HD] for h in range(D_H)], axis=0)
    den_inter = jnp.sum(q_st * nrow, -1, keepdims=True)
    num = num_intra * f_intra + num_inter * f_inter
    den = f_intra * den_intra + f_inter * den_inter
    h_st = num / jnp.maximum(jnp.abs(den), jnp.exp(-m_t))
    m_new = jnp.maximum(a_last + m, m_loc)
    fo = jnp.exp(a_last + m - m_new)
    fl = jnp.exp(m_loc - m_new)
    per_value = lambda col: jnp.concatenate(
        [jnp.broadcast_to(col[h * L:h * L + 1], (D_HD, 1)) for h in range(D_H)], axis=0)
    c_new = per_value(fo) * c + per_value(fl) * c_loc
    n_new = fo * nrow + fl * n_loc
    return _unstack_heads(h_st, D_H, L), c_new, n_new, m_new


(_E_W0, _E_A0, _E_KK, _E_KA, _E_RK, _E_LNW, _E_LNB, _E_BNORM, _E_LB0, _E_LB1) = range(10)


def _even_kernel(p_ref, shift0_ref, srw0_ref, shg0_ref, mu_ref, w2_ref, a2_ref, par_ref,
                 o_ref, shift_ref, srw_ref, shg_ref,
                 prev_sc, srw_sc, shg_sc, kkt_sc, rt_sc, kh_sc, bh_sc, ke_sc, be_sc, v_sc, ecl_sc, bonus_sc, oa_sc,
                 qd_sc, ki_sc, kend_sc, bl_sc, ob_sc, *, tb, la, lb):
    j = pl.program_id(1)
    nj = pl.num_programs(1)

    @pl.when(j == 0)
    def _():
        prev_sc[...] = jnp.broadcast_to(shift0_ref[...], prev_sc.shape)
        srw_sc[...] = jnp.concatenate([srw0_ref[h] for h in range(A_H)], axis=1)
        shg_sc[...] = shg0_ref[...]

    par = par_ref[...]
    row = lambda i: par[i:i + 1]

    pa = p_ref[:, :A_SHIFT_W]
    rolled = pltpu.roll(pa, 1, axis=0)
    first = _iota(pa.shape, 0) == 0
    shifted = jnp.where(first, jnp.broadcast_to(prev_sc[0:1], pa.shape), rolled)
    prev_sc[...] = jnp.broadcast_to(pa[tb - 1:tb], prev_sc.shape)
    xm = pa + (shifted - pa) * mu_ref[...]
    r = xm[:, :MIX]
    k = xm[:, MIX:2 * MIX]
    v = xm[:, 2 * MIX:3 * MIX]
    lora_in = xm[:, 3 * MIX:]
    w = -jax.nn.softplus(-(row(_E_W0) + _dot(jnp.tanh(lora_in).astype(BF16), w2_ref[...]))) - 0.5
    a = jax.nn.sigmoid(row(_E_A0) + _dot(lora_in.astype(BF16), a2_ref[...]))
    kk = k * row(_E_KK)
    kk = kk * lax.rsqrt(jnp.maximum(_seg_sum(kk * kk, A_HD), 1e-12))
    k = k * (1.0 + (a - 1.0) * row(_E_KA))
    bv = kk * a
    lw = -jnp.exp(w)
    cum_a, ones_a = _chunk_mats(tb, la)
    c = _mm(cum_a, lw, "nn", M_EXACT_R)
    cl = _mm(ones_a, lw, "nn", M_EXACT_R)
    e_neg = jnp.exp(-c)
    e_end = jnp.exp(cl - c)
    kkt_sc[...] = kk * jnp.exp(c - lw)
    rt_sc[...] = r * jnp.exp(c)
    kh_sc[...] = k * e_neg
    bh_sc[...] = bv * e_neg
    ke_sc[...] = k * e_end
    be_sc[...] = bv * e_end
    v_sc[...] = v
    ecl_sc[...] = jnp.exp(cl)
    bonus_sc[...] = _seg_sum(r * k * row(_E_RK), A_HD) * v

    wide, _, incl, strict = _head_masks(A_H, la, A_HD)
    hl = A_H * la
    eye = (_iota((hl, hl), 0) == _iota((hl, hl), 1)).astype(F32)
    consts_a = (wide, incl, strict, eye)

    def rwkv_body(ci, s):
        sl = pl.ds(pl.multiple_of(ci * la, la), la)
        o, s = _rwkv_chunk(kkt_sc[sl, :], rt_sc[sl, :], kh_sc[sl, :], bh_sc[sl, :], ke_sc[sl, :], be_sc[sl, :],
                           v_sc[sl, :], ecl_sc[pl.ds(ci * la, 1), :], s, consts_a)
        oa_sc[sl, :] = o
        return s

    srw_sc[...] = lax.fori_loop(0, tb // la, rwkv_body, srw_sc[...])

    e0 = row(_E_LB0)
    e1 = row(_E_LB1)
    emax = jnp.maximum(e0, e1)
    e0 = jnp.exp(e0 - emax)
    lower = e0 / (e0 + jnp.exp(e1 - emax))
    base = A_SHIFT_W + MIX
    g = lower + (1.0 - lower) * jax.nn.sigmoid(p_ref[:, base + MIX:base + 2 * MIX])
    logg = jnp.log(g)
    cum_b, ones_b = _chunk_mats(tb, lb)
    gb = _mm(cum_b, logg, "nn", M_EXACT_R)
    gl = _mm(ones_b, logg, "nn", M_EXACT_R)
    qd_sc[...] = _silu(p_ref[:, base:base + MIX]) * (B_HD ** -0.5) * jnp.exp(gb)
    ki_sc[...] = (1.0 - g) * jnp.exp(-gb)
    kend_sc[...] = (1.0 - g) * jnp.exp(gl - gb)
    bl_sc[...] = gl

    wide_b, _, incl_b, _ = _head_masks(B_H, lb, B_HD)
    consts_b = (wide_b, incl_b)

    def hgrn_body(ci, s):
        sl = pl.ds(pl.multiple_of(ci * lb, lb), lb)
        o, s = _gla_chunk(qd_sc[sl, :], ki_sc[sl, :], kend_sc[sl, :], bl_sc[pl.ds(ci * lb, 1), :],
                          p_ref[sl, base + 2 * MIX:base + 3 * MIX], s, consts_b, B_H, B_HD)
        ob_sc[sl, :] = o
        return s

    shg_sc[...] = lax.fori_loop(0, tb // lb, hgrn_body, shg_sc[...])

    o = oa_sc[...]
    mean = _seg_sum(o, A_HD) * (1.0 / A_HD)
    cen = o - mean
    var = _seg_sum(cen * cen, A_HD) * (1.0 / A_HD)
    oa = cen * lax.rsqrt(var + A_GN_EPS) * row(_E_LNW) + row(_E_LNB)
    oa = oa + bonus_sc[...]
    oa = oa * _silu(p_ref[:, A_SHIFT_W:A_SHIFT_W + MIX])
    ob = _head_rms(ob_sc[...], row(_E_BNORM), B_HD) * _silu(p_ref[:, base + 3 * MIX:base + 4 * MIX])
    o_ref[...] = jnp.concatenate([oa, ob], axis=1).astype(o_ref.dtype)

    @pl.when(j == nj - 1)
    def _():
        shift_ref[...] = prev_sc[0:1]
        s = srw_sc[...]
        for h in range(A_H):
            srw_ref[h] = s[:, h * A_HD:(h + 1) * A_HD]
        shg_ref[...] = shg_sc[...]


def _even_mixer(p, shift0, srw0, shg0, mu, w2p, a2p, par, tb, la, lb):
    bsz, t, _ = p.shape
    kern = functools.partial(_even_kernel, tb=tb, la=la, lb=lb)
    bmap3 = lambda b, j: (b, 0, 0)
    bmap4 = lambda b, j: (b, 0, 0, 0)
    cmap = lambda b, j: (0, 0)
    blk = lambda: pltpu.VMEM((tb, MIX), F32)
    return pl.pallas_call(
        kern, grid=(bsz, t // tb),
        in_specs=[pl.BlockSpec((None, tb, IN_EVEN), lambda b, j: (b, j, 0)),
                  pl.BlockSpec((None, 1, A_SHIFT_W), bmap3),
                  pl.BlockSpec((None, A_H, A_HD, A_HD), bmap4),
                  pl.BlockSpec((None, B_H * B_HD, B_HD), bmap3),
                  pl.BlockSpec((1, A_SHIFT_W), cmap),
                  pl.BlockSpec((2 * A_LORA, MIX), cmap),
                  pl.BlockSpec((2 * A_LORA, MIX), cmap),
                  pl.BlockSpec((16, MIX), cmap)],
        out_specs=[pl.BlockSpec((None, tb, D_MODEL), lambda b, j: (b, j, 0)),
                   pl.BlockSpec((None, 1, A_SHIFT_W), bmap3),
                   pl.BlockSpec((None, A_H, A_HD, A_HD), bmap4),
                   pl.BlockSpec((None, B_H * B_HD, B_HD), bmap3)],
        out_shape=[jax.ShapeDtypeStruct((bsz, t, D_MODEL), F32),
                   jax.ShapeDtypeStruct((bsz, 1, A_SHIFT_W), F32),
                   jax.ShapeDtypeStruct((bsz, A_H, A_HD, A_HD), F32),
                   jax.ShapeDtypeStruct((bsz, B_H * B_HD, B_HD), F32)],
        scratch_shapes=[pltpu.VMEM((8, A_SHIFT_W), F32), pltpu.VMEM((A_HD, MIX), F32),
                        pltpu.VMEM((B_H * B_HD, B_HD), F32)] + [blk() for _ in range(15)],
        compiler_params=pltpu.CompilerParams(dimension_semantics=("arbitrary", "arbitrary"),
                                             vmem_limit_bytes=VMEM_LIMIT),
        name="mixer_rwkv_hgrn")(p, shift0, srw0, shg0, mu, w2p, a2p, par)


_O_CQ = 0
_O_CK = _O_CQ + C_KW
_O_CV = _O_CK + C_KW
_O_GC = _O_CV + MIX
_O_DQK = _O_GC + MIX
_O_DV = _O_DQK + 2 * MIX
_O_GD = _O_DV + MIX
_O_SLAB = _O_GD + MIX
(_P_CNORM, _P_DNORM, _P_G2B) = range(3)


def _odd_kernel(p_ref, sgl0_ref, conv0_ref, mc0_ref, mn0_ref, mm0_ref, g2_ref, cw_ref, cb_ref, sb_ref, par_ref,
                o_ref, sgl_ref, conv_ref, mc_ref, mn_ref, mm_ref,
                prev_sc, sgl_sc, mc_sc, mn_sc, mm_sc, cq_sc, ck_sc, ce_sc, cl_sc, oc_sc,
                dq_sc, dk_sc, li_sc, fa_sc, fl_sc, od_sc, *, tb, lc):
    j = pl.program_id(1)
    nj = pl.num_programs(1)
    hl = D_H * lc
    head_rows = lambda x: jnp.concatenate(
        [jnp.broadcast_to(x[h:h + 1], (lc, x.shape[1])) for h in range(D_H)], axis=0)

    @pl.when(j == 0)
    def _():
        prev_sc[...] = conv0_ref[...]
        sgl_sc[...] = sgl0_ref[...]
        mc_sc[...] = mc0_ref[...]
        mn_sc[...] = head_rows(mn0_ref[...])
        mm_sc[...] = head_rows(mm0_ref[...])[:, :1]

    par = par_ref[...]
    row = lambda i: par[i:i + 1]

    cum_c, ones_c = _chunk_mats(tb, lc)
    slab = p_ref[:, _O_SLAB:_O_SLAB + SLAB]
    pre = _dot(slab.astype(BF16), g2_ref[...]) + row(_P_G2B)[:, :C_KW]
    logg = jax.nn.log_sigmoid(pre) * (1.0 / C_GATE_NORM)
    gb = _mm(cum_c, logg, "nn", M_EXACT_R)
    gl = _mm(ones_c, logg, "nn", M_EXACT_R)
    ck = p_ref[:, _O_CK:_O_CK + C_KW]
    cq_sc[...] = p_ref[:, _O_CQ:_O_CQ + C_KW] * (C_KD ** -0.5) * jnp.exp(gb)
    ck_sc[...] = ck * jnp.exp(-gb)
    ce_sc[...] = ck * jnp.exp(gl - gb)
    cl_sc[...] = gl

    wide_c, _, incl_c, _ = _head_masks(C_H, lc, C_KD)
    consts_c = (wide_c, incl_c)

    def gla_body(ci, s):
        sl = pl.ds(pl.multiple_of(ci * lc, lc), lc)
        o, s = _gla_chunk(cq_sc[sl, :], ck_sc[sl, :], ce_sc[sl, :], cl_sc[pl.ds(ci * lc, 1), :],
                          p_ref[sl, _O_CV:_O_CV + MIX], s, consts_c, C_H, C_VD)
        oc_sc[sl, :] = o
        return s

    sgl_sc[...] = lax.fori_loop(0, tb // lc, gla_body, sgl_sc[...])

    x = p_ref[:, _O_DQK:_O_DQK + 2 * MIX]
    prev = prev_sc[...]
    cw = cw_ref[...]
    conv = cb_ref[...] + x * cw[D_CONV - 1:D_CONV]
    head_row = _iota((8, 2 * MIX), 0)
    for s in range(1, D_CONV):
        xr = pltpu.roll(x, s, axis=0)
        top = jnp.where(head_row < s, pltpu.roll(prev, s, axis=0), xr[:8])
        xs = top if tb == 8 else jnp.concatenate([top, xr[8:]], axis=0)
        conv = conv + xs * cw[D_CONV - 1 - s:D_CONV - s]
    prev_sc[...] = x[tb - 8:]
    conv = _silu(conv)
    dq_sc[...] = conv[:, :MIX]
    dk_sc[...] = conv[:, MIX:] * (D_HD ** -0.5)
    gates = slab + sb_ref[...]
    lf = pltpu.roll(jax.nn.log_sigmoid(gates), SLAB - (SLAB_F - SLAB_I), axis=1)
    li_sc[...] = gates
    fa_sc[...] = _mm(cum_c, lf, "nn", M_EXACT_R)
    fl_sc[...] = _mm(ones_c, lf, "nn", M_EXACT_R)
    lane = _iota((hl, SLAB), 1)
    head = _iota((hl, SLAB), 0) // lc
    pick = lambda z: jnp.sum(jnp.where(lane == head + SLAB_I, _tile_rows(z, D_H), 0.0), -1, keepdims=True)

    wide_d, same_d, incl_d, _ = _head_masks(D_H, lc, D_HD)
    consts_d = (wide_d, same_d, incl_d, same_d.astype(BF16))

    def mlstm_body(ci, carry):
        c, n, m = carry
        sl = pl.ds(pl.multiple_of(ci * lc, lc), lc)
        o, c, n, m = _mlstm_chunk(dq_sc[sl, :], dk_sc[sl, :], p_ref[sl, _O_DV:_O_DV + MIX],
                                  pick(li_sc[sl, :]), pick(fa_sc[sl, :]), pick(fl_sc[sl, :]), c, n, m, consts_d)
        od_sc[sl, :] = o
        return c, n, m

    c, n, m = lax.fori_loop(0, tb // lc, mlstm_body, (mc_sc[...], mn_sc[...], mm_sc[...]))
    mc_sc[...] = c
    mn_sc[...] = n
    mm_sc[...] = m

    oc = _head_rms(oc_sc[...], row(_P_CNORM), C_VD) * _silu(p_ref[:, _O_GC:_O_GC + MIX])
    od = _head_rms(od_sc[...], row(_P_DNORM), D_HD) * _silu(p_ref[:, _O_GD:_O_GD + MIX])
    o_ref[...] = jnp.concatenate([oc, od], axis=1).astype(o_ref.dtype)

    @pl.when(j == nj - 1)
    def _():
        sgl_ref[...] = sgl_sc[...]
        conv_ref[...] = prev_sc[...]
        mc_ref[...] = mc_sc[...]
        mn_ref[...] = jnp.concatenate([mn_sc[h * lc:h * lc + 1] for h in range(D_H)], axis=0)
        mm_ref[...] = jnp.concatenate(
            [jnp.broadcast_to(mm_sc[h * lc:h * lc + 1], (1, LANE)) for h in range(D_H)], axis=0)


def _odd_mixer(p, sgl0, conv0, mc0, mn0, mm0, g2p, cw, cb, sb, par, tb, lc):
    bsz, t, _ = p.shape
    hl = D_H * lc
    kern = functools.partial(_odd_kernel, tb=tb, lc=lc)
    bmap3 = lambda b, j: (b, 0, 0)
    cmap = lambda b, j: (0, 0)
    blk = lambda w: pltpu.VMEM((tb, w), F32)
    return pl.pallas_call(
        kern, grid=(bsz, t // tb),
        in_specs=[pl.BlockSpec((None, tb, IN_ODD_PAD), lambda b, j: (b, j, 0)),
                  pl.BlockSpec((None, C_KW, C_VD), bmap3),
                  pl.BlockSpec((None, 8, 2 * MIX), bmap3),
                  pl.BlockSpec((None, MIX, D_HD), bmap3),
                  pl.BlockSpec((None, D_H, D_HD), bmap3),
                  pl.BlockSpec((None, D_H, LANE), bmap3),
                  pl.BlockSpec((SLAB, C_KW), cmap),
                  pl.BlockSpec((D_CONV, 2 * MIX), cmap),
                  pl.BlockSpec((1, 2 * MIX), cmap),
                  pl.BlockSpec((1, SLAB), cmap),
                  pl.BlockSpec((8, MIX), cmap)],
        out_specs=[pl.BlockSpec((None, tb, D_MODEL), lambda b, j: (b, j, 0)),
                   pl.BlockSpec((None, C_KW, C_VD), bmap3),
                   pl.BlockSpec((None, 8, 2 * MIX), bmap3),
                   pl.BlockSpec((None, MIX, D_HD), bmap3),
                   pl.BlockSpec((None, D_H, D_HD), bmap3),
                   pl.BlockSpec((None, D_H, LANE), bmap3)],
        out_shape=[jax.ShapeDtypeStruct((bsz, t, D_MODEL), F32),
                   jax.ShapeDtypeStruct((bsz, C_KW, C_VD), F32),
                   jax.ShapeDtypeStruct((bsz, 8, 2 * MIX), F32),
                   jax.ShapeDtypeStruct((bsz, MIX, D_HD), F32),
                   jax.ShapeDtypeStruct((bsz, D_H, D_HD), F32),
                   jax.ShapeDtypeStruct((bsz, D_H, LANE), F32)],
        scratch_shapes=[pltpu.VMEM((8, 2 * MIX), F32), pltpu.VMEM((C_KW, C_VD), F32),
                        pltpu.VMEM((MIX, D_HD), F32), pltpu.VMEM((hl, D_HD), F32), pltpu.VMEM((hl, 1), F32),
                        blk(C_KW), blk(C_KW), blk(C_KW), blk(C_KW), blk(MIX),
                        blk(MIX), blk(MIX), blk(SLAB), blk(SLAB), blk(SLAB), blk(MIX)],
        compiler_params=pltpu.CompilerParams(dimension_semantics=("arbitrary", "arbitrary"),
                                             vmem_limit_bytes=VMEM_LIMIT),
        name="mixer_gla_mlstm")(p, sgl0, conv0, mc0, mn0, mm0, g2p, cw, cb, sb, par)


def _time_block(t):
    return math.gcd(t, 256)


def _run(x, shift, s_rwkv, s_hgrn, s_gla, conv, mc, mn, mm, wts):
    bsz, t, d = x.shape
    tb = _time_block(t)
    la = math.gcd(t, 16)
    lbc = math.gcd(t, 32)
    x2 = x.reshape(bsz * t, d)
    p0 = _inproj(x2, wts["g0"], wts["w_in0"])
    o0, shift_n, srw_n, shg_n = _even_mixer(
        p0.reshape(bsz, t, IN_EVEN), shift[0][:, None], s_rwkv[0], s_hgrn[0].reshape(bsz, B_H * B_HD, B_HD),
        wts["mu"], wts["w2p"], wts["a2p"], wts["par_e"], tb, la, lbc)
    x1, p1 = _mid(o0.reshape(bsz * t, d), x2, wts["w_out0"], wts["g1"], wts["w_in1"])
    conv8 = jnp.pad(conv[0], ((0, 0), (8 - (D_CONV - 1), 0), (0, 0)))
    mm_l = jnp.broadcast_to(mm[0][:, :, None], (bsz, D_H, LANE))
    o1, sgl_n, conv_n, mc_n, mn_n, mm_n = _odd_mixer(
        p1.reshape(bsz, t, IN_ODD_PAD), s_gla[0].reshape(bsz, C_KW, C_VD), conv8,
        mc[0].reshape(bsz, MIX, D_HD), mn[0], mm_l,
        wts["g2p"], wts["cw"], wts["cb"], wts["sb"], wts["par_o"], tb, lbc)
    y = _final(o1.reshape(bsz * t, d), x1, wts["w_out1"], wts["gf"])
    return (y.reshape(bsz, t, d), shift_n.reshape(1, bsz, A_SHIFT_W), srw_n[None],
            shg_n.reshape(1, bsz, B_H, B_HD, B_HD), sgl_n.reshape(1, bsz, C_H, C_KD, C_VD),
            conv_n[None, :, 8 - (D_CONV - 1):], mc_n.reshape(1, bsz, D_H, D_HD, D_HD), mn_n[None],
            mm_n[None, :, :, 0])


def _odd_column_order():
    sizes = (C_KW, C_KW, MIX, C_LORA, MIX, 2 * MIX, MIX, D_H, D_H, MIX)
    off = np.cumsum((0,) + sizes)
    cq, ck, cv, cg, gc, dqk, dv, di, df, gd = (np.arange(off[i], off[i + 1]) for i in range(len(sizes)))
    return np.concatenate([cq, ck, cv, gc, dqk, dv, gd, cg, di, df])


def kernel(x_prompt, x_sample, state_shift_a, state_rwkv, state_hgrn, state_gla, state_conv_d, state_mlstm_c,
           state_mlstm_n, state_mlstm_m, norm_g, w_in_even, w_out_even, a_mu, a_w0, a_w2, a_a0, a_a2, a_kk,
           a_ka, a_rk, a_ln_w, a_ln_b, b_lb, b_norm, w_in_odd, w_out_odd, c_g2, c_g2b, c_norm, d_conv_w,
           d_conv_b, d_ib, d_fb, d_norm, final_norm):
    assert w_in_even.shape[0] == 1 and w_in_odd.shape[0] == 1 and b_lb.shape[0] == 2
    zpad = lambda a, rows_before, rows_total: jnp.pad(a, ((rows_before, rows_total - rows_before - a.shape[0]), (0, 0)))
    par_e = jnp.concatenate([a_w0, a_a0, a_kk, a_ka, a_rk, a_ln_w, a_ln_b, b_norm, b_lb[0:1], b_lb[1:2]], axis=0)
    g2b = jnp.pad(c_g2b, ((0, 0), (0, MIX - C_KW)))
    par_o = jnp.concatenate([c_norm, d_norm, g2b], axis=0)
    w_in1 = jnp.pad(w_in_odd[0][:, _odd_column_order()], ((0, 0), (0, SLAB - C_LORA - 2 * D_H)))
    sb = jnp.pad(jnp.concatenate([d_ib, d_fb], axis=1), ((0, 0), (SLAB_I, SLAB - SLAB_I - 2 * D_H)))
    wts = {
        "g0": norm_g[0:1], "g1": norm_g[1:2], "gf": final_norm[None],
        "w_in0": w_in_even[0].astype(BF16), "w_out0": w_out_even[0].astype(BF16),
        "w_in1": w_in1.astype(BF16), "w_out1": w_out_odd[0].astype(BF16),
        "mu": a_mu, "w2p": zpad(a_w2[0], 0, 2 * A_LORA).astype(BF16),
        "a2p": zpad(a_a2[0], A_LORA, 2 * A_LORA).astype(BF16),
        "par_e": zpad(par_e, 0, 16), "par_o": zpad(par_o, 0, 8),
        "g2p": zpad(c_g2[0], 0, SLAB).astype(BF16), "cw": d_conv_w[0], "cb": d_conv_b, "sb": sb,
    }
    bp = x_prompt.shape[0]
    z = lambda *s: jnp.zeros(s, x_prompt.dtype)
    prompt = _run(x_prompt, z(1, bp, A_SHIFT_W), z(1, bp, A_H, A_HD, A_HD), z(1, bp, B_H, B_HD, B_HD),
                  z(1, bp, C_H, C_KD, C_VD), z(1, bp, D_CONV - 1, 2 * MIX), z(1, bp, D_H, D_HD, D_HD),
                  z(1, bp, D_H, D_HD), z(1, bp, D_H), wts)
    sample = _run(x_sample, state_shift_a, state_rwkv, state_hgrn, state_gla, state_conv_d, state_mlstm_c,
                  state_mlstm_n, state_mlstm_m, wts)
    return (prompt[0], sample[0]) + prompt[1:] + sample[1:]
```

```python
import functools
import math

import jax
import jax.numpy as jnp
import numpy as np
from jax import lax
from jax.experimental import pallas as pl
from jax.experimental.pallas import tpu as pltpu

F32 = jnp.float32
BF16 = jnp.bfloat16

D_MODEL = 1024
MIX = D_MODEL // 2
NORM_EPS = 1e-5
A_HD = 64
A_H = MIX // A_HD
A_LORA = 64
A_GN_EPS = 64e-5
A_SHIFT_W = 3 * MIX + 2 * A_LORA
B_HD = 128
B_H = MIX // B_HD
IN_EVEN = A_SHIFT_W + 5 * MIX
C_H = 4
C_VD = MIX // C_H
C_KD = C_VD // 2
C_KW = C_H * C_KD
C_LORA = 16
C_GATE_NORM = 16.0
D_H = 4
D_HD = MIX // D_H
D_CONV = 4
LANE = 128
SLAB = LANE
SLAB_I = C_LORA
SLAB_F = C_LORA + D_H
IN_ODD_PAD = 2 * C_KW + 2 * MIX + 2 * MIX + 2 * MIX + SLAB
VMEM_LIMIT = 48 * 1024 * 1024


_DIMS = {"nn": ((1,), (0,)), "nt": ((1,), (1,)), "tn": ((0,), (0,))}
M_EXACT_R = "rx"
M_EXACT_L = "lx"
M_G = "bf"
M_TINV = "bf"
M_ST = "bf"
M_ATT = "bf"


def _split2(x):
    hi = x.astype(BF16)
    return hi, (x - hi.astype(F32)).astype(BF16)


def _split3(x):
    hi = x.astype(BF16)
    r1 = x - hi.astype(F32)
    mid = r1.astype(BF16)
    return hi, mid, (r1 - mid.astype(F32)).astype(BF16)


def _mm(a, b, form, mode):
    dn = (_DIMS[form], ((), ()))
    d = lambda x, y: lax.dot_general(x, y, dn, preferred_element_type=F32)
    if mode == "bf":
        return d(a.astype(BF16), b.astype(BF16))
    if mode == "x3":
        ah, al = _split2(a)
        bh, bl = _split2(b)
        return d(ah, bh) + (d(ah, bl) + d(al, bh))
    if mode == "lx":
        bb = b.astype(BF16)
        h, m, l = _split3(a)
        return d(h, bb) + (d(m, bb) + d(l, bb))
    assert mode == "rx"
    ab = a.astype(BF16)
    h, m, l = _split3(b)
    return d(ab, h) + (d(ab, m) + d(ab, l))


def _dot(a, b):
    return jnp.dot(a, b, preferred_element_type=F32)


def _iota(shape, dim):
    return lax.broadcasted_iota(jnp.int32, shape, dim)


def _silu(x):
    return x * jax.nn.sigmoid(x)


def _rms(x, g):
    return x * lax.rsqrt(jnp.mean(x * x, -1, keepdims=True) + NORM_EPS) * g


def _unroll(trips):
    return 2 if trips % 2 == 0 else 1


def _tile_rows(x, n):
    return jnp.concatenate([x] * n, axis=0)


def _stack_heads(x, n_heads, width):
    return jnp.concatenate([x[:, h * width:(h + 1) * width] for h in range(n_heads)], axis=0)


def _unstack_heads(x, n_heads, rows):
    return jnp.concatenate([x[h * rows:(h + 1) * rows] for h in range(n_heads)], axis=1)


def _head_masks(n_heads, rows, width):
    hl = n_heads * rows
    rh = _iota((hl, n_heads * width), 0) // rows
    lh = _iota((hl, n_heads * width), 1) // width
    wide = (rh == lh).astype(F32)
    ri, ci = _iota((hl, hl), 0), _iota((hl, hl), 1)
    same = (ri // rows) == (ci // rows)
    incl = same & (ci <= ri)
    strict = same & (ci < ri)
    return wide, same, incl, strict


def _chunk_mats(n, chunk):
    ri, ci = _iota((n, n), 0), _iota((n, n), 1)
    same = (ri // chunk) == (ci // chunk)
    return (same & (ci <= ri)).astype(BF16), same.astype(BF16)


def _seg_sum(x, seg):
    bd = ((_iota((LANE, LANE), 0) // seg) == (_iota((LANE, LANE), 1) // seg)).astype(BF16)
    parts = [_mm(x[:, j:j + LANE], bd, "nn", M_EXACT_L) for j in range(0, x.shape[1], LANE)]
    return jnp.concatenate(parts, axis=1)


def _head_rms(x, g, width):
    parts = []
    for j in range(0, x.shape[1], width):
        xs = x[:, j:j + width]
        parts.append(xs * lax.rsqrt(jnp.mean(xs * xs, -1, keepdims=True) + NORM_EPS))
    return jnp.concatenate(parts, axis=1) * g


def _inproj_kernel(x_ref, g_ref, w_ref, p_ref):
    h = _rms(x_ref[...], g_ref[...])
    p_ref[...] = _dot(h.astype(BF16), w_ref[...])


def _mid_kernel(o_ref, x_ref, wo_ref, g_ref, wi_ref, x1_ref, p_ref):
    x1 = x_ref[...] + _dot(o_ref[...].astype(BF16), wo_ref[...])
    x1_ref[...] = x1
    p_ref[...] = _dot(_rms(x1, g_ref[...]).astype(BF16), wi_ref[...])


def _final_kernel(o_ref, x_ref, wo_ref, g_ref, y_ref):
    x2 = x_ref[...] + _dot(o_ref[...].astype(BF16), wo_ref[...])
    y_ref[...] = _rms(x2, g_ref[...])


def _row_tile(m):
    return math.gcd(m, 512)


def _full(shape):
    return pl.BlockSpec(shape, lambda i: (0,) * len(shape))


def _dense_params():
    return pltpu.CompilerParams(dimension_semantics=("arbitrary",), vmem_limit_bytes=VMEM_LIMIT)


def _inproj(x, g, w):
    m, d = x.shape
    n = w.shape[1]
    tm = _row_tile(m)
    return pl.pallas_call(
        _inproj_kernel, grid=(m // tm,),
        in_specs=[pl.BlockSpec((tm, d), lambda i: (i, 0)), _full((1, d)), _full((d, n))],
        out_specs=pl.BlockSpec((tm, n), lambda i: (i, 0)),
        out_shape=jax.ShapeDtypeStruct((m, n), F32),
        compiler_params=_dense_params(), name="inproj")(x, g, w)


def _mid(o, x, wo, g, wi):
    m, d = x.shape
    n = wi.shape[1]
    tm = _row_tile(m)
    return pl.pallas_call(
        _mid_kernel, grid=(m // tm,),
        in_specs=[pl.BlockSpec((tm, d), lambda i: (i, 0)), pl.BlockSpec((tm, d), lambda i: (i, 0)),
                  _full((d, d)), _full((1, d)), _full((d, n))],
        out_specs=[pl.BlockSpec((tm, d), lambda i: (i, 0)), pl.BlockSpec((tm, n), lambda i: (i, 0))],
        out_shape=[jax.ShapeDtypeStruct((m, d), F32), jax.ShapeDtypeStruct((m, n), F32)],
        compiler_params=_dense_params(), name="outproj_inproj")(o, x, wo, g, wi)


def _final(o, x, wo, g):
    m, d = x.shape
    tm = _row_tile(m)
    return pl.pallas_call(
        _final_kernel, grid=(m // tm,),
        in_specs=[pl.BlockSpec((tm, d), lambda i: (i, 0)), pl.BlockSpec((tm, d), lambda i: (i, 0)),
                  _full((d, d)), _full((1, d))],
        out_specs=pl.BlockSpec((tm, d), lambda i: (i, 0)),
        out_shape=jax.ShapeDtypeStruct((m, d), F32),
        compiler_params=_dense_params(), name="outproj_final")(o, x, wo, g)


def _interleave(*stage_generators):
    live = list(stage_generators)
    while live:
        for gen in list(live):
            try:
                next(gen)
            except StopIteration:
                live.remove(gen)


def _rwkv_factor_stages(chunks, consts, out):
    wide, incl, strict, eye, halves = consts
    L = chunks[0][0].shape[0]
    hl = A_H * L
    n = range(len(chunks))
    tile = functools.partial(_tile_rows, n=A_H)
    kk_w = [tile(c[0]) * wide for c in chunks]
    r_w = [tile(c[1]) * wide for c in chunks]
    g = [_mm(jnp.concatenate([kk_w[i], r_w[i]], axis=0),
             jnp.concatenate([tile(chunks[i][2]), tile(chunks[i][3])], axis=0), "nt", M_G) for i in n]
    yield
    a_kk = [jnp.where(strict, x[:hl, :hl], 0.0) for x in g]
    a_kb = [jnp.where(strict, x[:hl, hl:], 0.0) for x in g]
    a_rk = [jnp.where(incl, x[hl:, :hl], 0.0) for x in g]
    a_rb = [jnp.where(incl, x[hl:, hl:], 0.0) for x in g]
    t = [eye - jnp.where(halves[0], x, 0.0) for x in a_kb]
    v_st = [_stack_heads(c[6], A_H, A_HD) for c in chunks]
    av = [_mm(a_kk[i], v_st[i], "nn", M_ST) for i in n]
    o2 = [_mm(a_rk[i], v_st[i], "nn", M_ST) for i in n]
    for half in halves[1:]:
        ta = [_mm(t[i], jnp.where(half, a_kb[i], 0.0), "nn", M_TINV) for i in n]
        yield
        t = [t[i] - _mm(ta[i], t[i], "nn", M_TINV) for i in n]
        yield
    kv = [_mm(v_st[i], tile(chunks[i][4]) * wide, "tn", M_ST) for i in n]
    m1 = [_mm(t[i], kk_w[i], "nn", M_ST) for i in n]
    m2 = [_mm(t[i], av[i], "nn", M_ST) for i in n]
    yield
    out["factors"] = [
        (jnp.concatenate([-m1[i], r_w[i]], axis=0).astype(BF16), -m2[i], o2[i], a_rb[i].astype(BF16), kv[i],
         (tile(chunks[i][5]) * wide).astype(BF16)) for i in n]


def _rwkv_state_stages(n_chunks, load_factors, store_o, s, out):
    for c in range(n_chunks):
        xm, m2n, o2, a_rb, kv, be_w, ecl = load_factors(c)
        hl = xm.shape[0] // 2
        xs = _mm(xm, s, "nt", M_ST)
        yield
        u = xs[:hl] + m2n
        s = s * ecl + kv + _mm(u, be_w, "tn", M_ST)
        o_st = xs[hl:] + o2 + _mm(a_rb, u, "nn", M_ST)
        yield
        store_o(c, _unstack_heads(o_st, A_H, hl // A_H))
    out["rwkv"] = s


def _gla_stages(n_chunks, load_chunk, store_o, s, consts, n_heads, vd, out, key):
    wide, incl = consts
    tile = functools.partial(_tile_rows, n=n_heads)
    for c in range(n_chunks):
        qd, ki, kend, bl, v = load_chunk(c)
        q_w = tile(qd) * wide
        v_st = _stack_heads(v, n_heads, vd)
        att = jnp.where(incl, _mm(q_w, tile(ki), "nt", M_ATT), 0.0)
        kv = _mm(tile(kend) * wide, v_st, "tn", M_ATT)
        yield
        o_st = _mm(att, v_st, "nn", M_ATT) + _mm(q_w, s, "nn", M_ATT)
        dcol = jnp.exp(jnp.broadcast_to(bl, (vd, bl.shape[1])).T)
        s = s * dcol + kv
        yield
        store_o(c, _unstack_heads(o_st, n_heads, qd.shape[0]))
    out[key] = s


def _mlstm_stages(n_chunks, load_chunk, store_o, state, consts, out):
    wide, same, incl, ones_bd = consts
    c, nrow, m = state
    tile = functools.partial(_tile_rows, n=D_H)
    for ch in range(n_chunks):
        q, k, v, li, a, a_last = load_chunk(ch)
        L = q.shape[0]
        hl = D_H * L
        to_row = lambda col: jnp.broadcast_to(col, (hl, LANE)).T[:hl]
        q_st = _stack_heads(q, D_H, D_HD)
        k_st = _stack_heads(k, D_H, D_HD)
        v_st = _stack_heads(v, D_H, D_HD)
        qk = _mm(q_st, k_st, "nt", M_ATT)
        dlog = jnp.where(incl, a - to_row(a) + to_row(li), -jnp.inf)
        m_intra = jnp.max(dlog, -1, keepdims=True)
        e_end = a_last - a + li
        m_loc = jnp.max(jnp.where(same, to_row(e_end), -jnp.inf), -1, keepdims=True)
        yield
        s_intra = jnp.exp(dlog - m_intra) * qk
        num_intra = _mm(s_intra, v_st, "nn", M_ATT)
        den_intra = jnp.sum(s_intra, -1, keepdims=True)
        w_end = jnp.exp(e_end - m_loc)
        c_loc = _mm(tile(v) * wide * w_end, k_st, "tn", M_ATT)
        n_loc = _mm(ones_bd, w_end * k_st, "nn", M_ATT)
        yield
        g = a + m
        m_t = jnp.maximum(g, m_intra)
        f_inter = jnp.exp(g - m_t)
        f_intra = jnp.exp(m_intra - m_t)
        qc = _mm(q_st, c, "nt", M_ATT)
        num_inter = jnp.concatenate(
            [qc[h * L:(h + 1) * L, h * D_HD:(h + 1) * D_HD] for h in range(D_H)], axis=0)
        den_inter = jnp.sum(q_st * nrow, -1, keepdims=True)
        num = num_intra * f_intra + num_inter * f_inter
        den = f_intra * den_intra + f_inter * den_inter
        h_st = num / jnp.maximum(jnp.abs(den), jnp.exp(-m_t))
        m_new = jnp.maximum(a_last + m, m_loc)
        fo = jnp.exp(a_last + m - m_new)
        fl = jnp.exp(m_loc - m_new)
        per_value = lambda col: jnp.concatenate(
            [jnp.broadcast_to(col[h * L:h * L + 1], (D_HD, 1)) for h in range(D_H)], axis=0)
        c = per_value(fo) * c + per_value(fl) * c_loc
        nrow = fo * nrow + fl * n_loc
        m = m_new
        yield
        store_o(ch, _unstack_heads(h_st, D_H, L))
    out["mlstm"] = (c, nrow, m)


(_E_W0, _E_A0, _E_KK, _E_KA, _E_RK, _E_LNW, _E_LNB, _E_BNORM, _E_LB0, _E_LB1) = range(10)


def _even_kernel(p_ref, shift0_ref, srw0_ref, shg0_ref, mu_ref, w2_ref, a2_ref, par_ref,
                 o_ref, shift_ref, srw_ref, shg_ref,
                 prev_sc, srw_sc, shg_sc, kkt_sc, rt_sc, kh_sc, bh_sc, ke_sc, be_sc, v_sc, ecl_sc, bonus_sc, oa_sc,
                 qd_sc, ki_sc, kend_sc, bl_sc, ob_sc, xm_sc, m2_sc, o2_sc, arb_sc, kv_sc, bew_sc, *, tb, la, lb):
    j = pl.program_id(1)
    nj = pl.num_programs(1)

    @pl.when(j == 0)
    def _():
        prev_sc[...] = jnp.broadcast_to(shift0_ref[...], prev_sc.shape)
        srw_sc[...] = jnp.concatenate([srw0_ref[h] for h in range(A_H)], axis=1)
        shg_sc[...] = shg0_ref[...]

    par = par_ref[...]
    row = lambda i: par[i:i + 1]

    pa = p_ref[:, :A_SHIFT_W]
    rolled = pltpu.roll(pa, 1, axis=0)
    first = _iota(pa.shape, 0) == 0
    shifted = jnp.where(first, jnp.broadcast_to(prev_sc[0:1], pa.shape), rolled)
    prev_sc[...] = jnp.broadcast_to(pa[tb - 1:tb], prev_sc.shape)
    xm = pa + (shifted - pa) * mu_ref[...]
    r = xm[:, :MIX]
    k = xm[:, MIX:2 * MIX]
    v = xm[:, 2 * MIX:3 * MIX]
    lora_in = xm[:, 3 * MIX:]
    w = -jax.nn.softplus(-(row(_E_W0) + _dot(jnp.tanh(lora_in).astype(BF16), w2_ref[...]))) - 0.5
    a = jax.nn.sigmoid(row(_E_A0) + _dot(lora_in.astype(BF16), a2_ref[...]))
    kk = k * row(_E_KK)
    kk = kk * lax.rsqrt(jnp.maximum(_seg_sum(kk * kk, A_HD), 1e-12))
    k = k * (1.0 + (a - 1.0) * row(_E_KA))
    bv = kk * a
    lw = -jnp.exp(w)
    cum_a, ones_a = _chunk_mats(tb, la)
    c = _mm(cum_a, lw, "nn", M_EXACT_R)
    cl = _mm(ones_a, lw, "nn", M_EXACT_R)
    e_neg = jnp.exp(-c)
    e_end = jnp.exp(cl - c)
    kkt_sc[...] = kk * jnp.exp(c - lw)
    rt_sc[...] = r * jnp.exp(c)
    kh_sc[...] = k * e_neg
    bh_sc[...] = bv * e_neg
    ke_sc[...] = k * e_end
    be_sc[...] = bv * e_end
    v_sc[...] = v
    ecl_sc[...] = jnp.exp(cl)
    bonus_sc[...] = _seg_sum(r * k * row(_E_RK), A_HD) * v

    wide, _, incl, strict = _head_masks(A_H, la, A_HD)
    hl = A_H * la
    ri, ci_ = _iota((hl, hl), 0), _iota((hl, hl), 1)
    eye = (ri == ci_).astype(F32)
    halves = []
    m = 1
    while m < la:
        halves.append(((ri // (2 * m)) == (ci_ // (2 * m))) & ((ri // m) != (ci_ // m)))
        m *= 2
    consts_a = (wide, incl, strict, eye, halves)

    e0 = row(_E_LB0)
    e1 = row(_E_LB1)
    emax = jnp.maximum(e0, e1)
    e0 = jnp.exp(e0 - emax)
    lower = e0 / (e0 + jnp.exp(e1 - emax))
    base = A_SHIFT_W + MIX
    g = lower + (1.0 - lower) * jax.nn.sigmoid(p_ref[:, base + MIX:base + 2 * MIX])
    logg = jnp.log(g)
    cum_b, ones_b = _chunk_mats(tb, lb)
    gb = _mm(cum_b, logg, "nn", M_EXACT_R)
    gl = _mm(ones_b, logg, "nn", M_EXACT_R)
    qd_sc[...] = _silu(p_ref[:, base:base + MIX]) * (B_HD ** -0.5) * jnp.exp(gb)
    ki_sc[...] = (1.0 - g) * jnp.exp(-gb)
    kend_sc[...] = (1.0 - g) * jnp.exp(gl - gb)
    bl_sc[...] = gl

    wide_b, _, incl_b, _ = _head_masks(B_H, lb, B_HD)
    consts_b = (wide_b, incl_b)

    group = math.gcd(tb // la, 4)
    ngroups = tb // (group * la)
    hper = (group * la) // lb

    def factor_stages(gi, slot, out):
        sls = [pl.ds(pl.multiple_of((gi * group + u) * la, la), la) for u in range(group)]
        yield from _rwkv_factor_stages(
            [(kkt_sc[sl, :], rt_sc[sl, :], kh_sc[sl, :], bh_sc[sl, :], ke_sc[sl, :], be_sc[sl, :], v_sc[sl, :])
             for sl in sls], consts_a, out)
        for u, (xm, m2n, o2, a_rb, kv, be_w) in enumerate(out["factors"]):
            xm_sc[slot + u] = xm
            m2_sc[slot + u] = m2n
            o2_sc[slot + u] = o2
            arb_sc[slot + u] = a_rb
            kv_sc[slot + u] = kv
            bew_sc[slot + u] = be_w

    def chunk_body(gi, carry):
        s_a, s_b = carry
        slot = (gi % 2) * group
        out = {}

        def load_factors(u):
            i = slot + u
            return (xm_sc[i], m2_sc[i], o2_sc[i], arb_sc[i], kv_sc[i], bew_sc[i],
                    ecl_sc[pl.ds((gi * group + u) * la, 1), :])

        def store_oa(u, o):
            oa_sc[pl.ds(pl.multiple_of((gi * group + u) * la, la), la), :] = o

        def load_b(u):
            ci = gi * hper + u
            sl = pl.ds(pl.multiple_of(ci * lb, lb), lb)
            return (qd_sc[sl, :], ki_sc[sl, :], kend_sc[sl, :], bl_sc[pl.ds(ci * lb, 1), :],
                    p_ref[sl, base + 2 * MIX:base + 3 * MIX])

        def store_ob(u, o):
            ob_sc[pl.ds(pl.multiple_of((gi * hper + u) * lb, lb), lb), :] = o

        stages = [_rwkv_state_stages(group, load_factors, store_oa, s_a, out),
                  _gla_stages(hper, load_b, store_ob, s_b, consts_b, B_H, B_HD, out, "hgrn")]
        if ngroups > 1:
            stages.insert(1, factor_stages(jnp.minimum(gi + 1, ngroups - 1), group - slot, {}))
        _interleave(*stages)
        return out["rwkv"], out["hgrn"]

    _interleave(factor_stages(0, 0, {}))
    s_a, s_b = lax.fori_loop(0, ngroups, chunk_body, (srw_sc[...], shg_sc[...]))
    srw_sc[...] = s_a
    shg_sc[...] = s_b

    o = oa_sc[...]
    mean = _seg_sum(o, A_HD) * (1.0 / A_HD)
    cen = o - mean
    var = _seg_sum(cen * cen, A_HD) * (1.0 / A_HD)
    oa = cen * lax.rsqrt(var + A_GN_EPS) * row(_E_LNW) + row(_E_LNB)
    oa = oa + bonus_sc[...]
    oa = oa * _silu(p_ref[:, A_SHIFT_W:A_SHIFT_W + MIX])
    ob = _head_rms(ob_sc[...], row(_E_BNORM), B_HD) * _silu(p_ref[:, base + 3 * MIX:base + 4 * MIX])
    o_ref[...] = jnp.concatenate([oa, ob], axis=1).astype(o_ref.dtype)

    @pl.when(j == nj - 1)
    def _():
        shift_ref[...] = prev_sc[0:1]
        s = srw_sc[...]
        for h in range(A_H):
            srw_ref[h] = s[:, h * A_HD:(h + 1) * A_HD]
        shg_ref[...] = shg_sc[...]


def _even_mixer(p, shift0, srw0, shg0, mu, w2p, a2p, par, tb, la, lb):
    bsz, t, _ = p.shape
    kern = functools.partial(_even_kernel, tb=tb, la=la, lb=lb)
    nch, hl = 2 * math.gcd(tb // la, 4), A_H * la
    bmap3 = lambda b, j: (b, 0, 0)
    bmap4 = lambda b, j: (b, 0, 0, 0)
    cmap = lambda b, j: (0, 0)
    blk = lambda: pltpu.VMEM((tb, MIX), F32)
    return pl.pallas_call(
        kern, grid=(bsz, t // tb),
        in_specs=[pl.BlockSpec((None, tb, IN_EVEN), lambda b, j: (b, j, 0)),
                  pl.BlockSpec((None, 1, A_SHIFT_W), bmap3),
                  pl.BlockSpec((None, A_H, A_HD, A_HD), bmap4),
                  pl.BlockSpec((None, B_H * B_HD, B_HD), bmap3),
                  pl.BlockSpec((1, A_SHIFT_W), cmap),
                  pl.BlockSpec((2 * A_LORA, MIX), cmap),
                  pl.BlockSpec((2 * A_LORA, MIX), cmap),
                  pl.BlockSpec((16, MIX), cmap)],
        out_specs=[pl.BlockSpec((None, tb, D_MODEL), lambda b, j: (b, j, 0)),
                   pl.BlockSpec((None, 1, A_SHIFT_W), bmap3),
                   pl.BlockSpec((None, A_H, A_HD, A_HD), bmap4),
                   pl.BlockSpec((None, B_H * B_HD, B_HD), bmap3)],
        out_shape=[jax.ShapeDtypeStruct((bsz, t, D_MODEL), F32),
                   jax.ShapeDtypeStruct((bsz, 1, A_SHIFT_W), F32),
                   jax.ShapeDtypeStruct((bsz, A_H, A_HD, A_HD), F32),
                   jax.ShapeDtypeStruct((bsz, B_H * B_HD, B_HD), F32)],
        scratch_shapes=[pltpu.VMEM((8, A_SHIFT_W), F32), pltpu.VMEM((A_HD, MIX), F32),
                        pltpu.VMEM((B_H * B_HD, B_HD), F32)] + [blk() for _ in range(15)] + [
                            pltpu.VMEM((nch, 2 * hl, MIX), BF16), pltpu.VMEM((nch, hl, A_HD), F32),
                            pltpu.VMEM((nch, hl, A_HD), F32), pltpu.VMEM((nch, hl, hl), BF16),
                            pltpu.VMEM((nch, A_HD, MIX), F32), pltpu.VMEM((nch, hl, MIX), BF16)],
        compiler_params=pltpu.CompilerParams(dimension_semantics=("arbitrary", "arbitrary"),
                                             vmem_limit_bytes=VMEM_LIMIT),
        name="mixer_rwkv_hgrn")(p, shift0, srw0, shg0, mu, w2p, a2p, par)


_O_CQ = 0
_O_CK = _O_CQ + C_KW
_O_CV = _O_CK + C_KW
_O_GC = _O_CV + MIX
_O_DQK = _O_GC + MIX
_O_DV = _O_DQK + 2 * MIX
_O_GD = _O_DV + MIX
_O_SLAB = _O_GD + MIX
(_P_CNORM, _P_DNORM, _P_G2B) = range(3)


def _odd_kernel(p_ref, sgl0_ref, conv0_ref, mc0_ref, mn0_ref, mm0_ref, g2_ref, cw_ref, cb_ref, sb_ref, par_ref,
                o_ref, sgl_ref, conv_ref, mc_ref, mn_ref, mm_ref,
                prev_sc, sgl_sc, mc_sc, mn_sc, mm_sc, cq_sc, ck_sc, ce_sc, cl_sc, oc_sc,
                dq_sc, dk_sc, li_sc, fa_sc, fl_sc, od_sc, *, tb, lc):
    j = pl.program_id(1)
    nj = pl.num_programs(1)
    hl = D_H * lc
    head_rows = lambda x: jnp.concatenate(
        [jnp.broadcast_to(x[h:h + 1], (lc, x.shape[1])) for h in range(D_H)], axis=0)

    @pl.when(j == 0)
    def _():
        prev_sc[...] = conv0_ref[...]
        sgl_sc[...] = sgl0_ref[...]
        mc_sc[...] = mc0_ref[...]
        mn_sc[...] = head_rows(mn0_ref[...])
        mm_sc[...] = head_rows(mm0_ref[...])[:, :1]

    par = par_ref[...]
    row = lambda i: par[i:i + 1]

    cum_c, ones_c = _chunk_mats(tb, lc)
    slab = p_ref[:, _O_SLAB:_O_SLAB + SLAB]
    pre = _dot(slab.astype(BF16), g2_ref[...]) + row(_P_G2B)[:, :C_KW]
    logg = jax.nn.log_sigmoid(pre) * (1.0 / C_GATE_NORM)
    gb = _mm(cum_c, logg, "nn", M_EXACT_R)
    gl = _mm(ones_c, logg, "nn", M_EXACT_R)
    ck = p_ref[:, _O_CK:_O_CK + C_KW]
    cq_sc[...] = p_ref[:, _O_CQ:_O_CQ + C_KW] * (C_KD ** -0.5) * jnp.exp(gb)
    ck_sc[...] = ck * jnp.exp(-gb)
    ce_sc[...] = ck * jnp.exp(gl - gb)
    cl_sc[...] = gl

    wide_c, _, incl_c, _ = _head_masks(C_H, lc, C_KD)
    consts_c = (wide_c, incl_c)

    x = p_ref[:, _O_DQK:_O_DQK + 2 * MIX]
    prev = prev_sc[...]
    cw = cw_ref[...]
    conv = cb_ref[...] + x * cw[D_CONV - 1:D_CONV]
    head_row = _iota((8, 2 * MIX), 0)
    for s in range(1, D_CONV):
        xr = pltpu.roll(x, s, axis=0)
        top = jnp.where(head_row < s, pltpu.roll(prev, s, axis=0), xr[:8])
        xs = top if tb == 8 else jnp.concatenate([top, xr[8:]], axis=0)
        conv = conv + xs * cw[D_CONV - 1 - s:D_CONV - s]
    prev_sc[...] = x[tb - 8:]
    conv = _silu(conv)
    dq_sc[...] = conv[:, :MIX]
    dk_sc[...] = conv[:, MIX:] * (D_HD ** -0.5)
    gates = slab + sb_ref[...]
    lf = pltpu.roll(jax.nn.log_sigmoid(gates), SLAB - (SLAB_F - SLAB_I), axis=1)
    li_sc[...] = gates
    fa_sc[...] = _mm(cum_c, lf, "nn", M_EXACT_R)
    fl_sc[...] = _mm(ones_c, lf, "nn", M_EXACT_R)
    lane = _iota((hl, SLAB), 1)
    head = _iota((hl, SLAB), 0) // lc
    pick = lambda z: jnp.sum(jnp.where(lane == head + SLAB_I, _tile_rows(z, D_H), 0.0), -1, keepdims=True)

    wide_d, same_d, incl_d, _ = _head_masks(D_H, lc, D_HD)
    consts_d = (wide_d, same_d, incl_d, same_d.astype(BF16))

    per_trip = _unroll(tb // lc)

    def chunk_body(gi, carry):
        s_c, state_d = carry
        out = {}
        rows = lambda u: pl.ds(pl.multiple_of((gi * per_trip + u) * lc, lc), lc)

        def load_c(u):
            sl = rows(u)
            return (cq_sc[sl, :], ck_sc[sl, :], ce_sc[sl, :], cl_sc[pl.ds((gi * per_trip + u) * lc, 1), :],
                    p_ref[sl, _O_CV:_O_CV + MIX])

        def store_c(u, o):
            oc_sc[rows(u), :] = o

        def load_d(u):
            sl = rows(u)
            return (dq_sc[sl, :], dk_sc[sl, :], p_ref[sl, _O_DV:_O_DV + MIX],
                    pick(li_sc[sl, :]), pick(fa_sc[sl, :]), pick(fl_sc[sl, :]))

        def store_d(u, o):
            od_sc[rows(u), :] = o

        _interleave(_mlstm_stages(per_trip, load_d, store_d, state_d, consts_d, out),
                    _gla_stages(per_trip, load_c, store_c, s_c, consts_c, C_H, C_VD, out, "gla"))
        return out["gla"], out["mlstm"]

    s_c, (c, n, m) = lax.fori_loop(0, tb // (lc * per_trip), chunk_body,
                                   (sgl_sc[...], (mc_sc[...], mn_sc[...], mm_sc[...])))
    sgl_sc[...] = s_c
    mc_sc[...] = c
    mn_sc[...] = n
    mm_sc[...] = m

    oc = _head_rms(oc_sc[...], row(_P_CNORM), C_VD) * _silu(p_ref[:, _O_GC:_O_GC + MIX])
    od = _head_rms(od_sc[...], row(_P_DNORM), D_HD) * _silu(p_ref[:, _O_GD:_O_GD + MIX])
    o_ref[...] = jnp.concatenate([oc, od], axis=1).astype(o_ref.dtype)

    @pl.when(j == nj - 1)
    def _():
        sgl_ref[...] = sgl_sc[...]
        conv_ref[...] = prev_sc[...]
        mc_ref[...] = mc_sc[...]
        mn_ref[...] = jnp.concatenate([mn_sc[h * lc:h * lc + 1] for h in range(D_H)], axis=0)
        mm_ref[...] = jnp.concatenate(
            [jnp.broadcast_to(mm_sc[h * lc:h * lc + 1], (1, LANE)) for h in range(D_H)], axis=0)


def _odd_mixer(p, sgl0, conv0, mc0, mn0, mm0, g2p, cw, cb, sb, par, tb, lc):
    bsz, t, _ = p.shape
    hl = D_H * lc
    kern = functools.partial(_odd_kernel, tb=tb, lc=lc)
    bmap3 = lambda b, j: (b, 0, 0)
    cmap = lambda b, j: (0, 0)
    blk = lambda w: pltpu.VMEM((tb, w), F32)
    return pl.pallas_call(
        kern, grid=(bsz, t // tb),
        in_specs=[pl.BlockSpec((None, tb, IN_ODD_PAD), lambda b, j: (b, j, 0)),
                  pl.BlockSpec((None, C_KW, C_VD), bmap3),
                  pl.BlockSpec((None, 8, 2 * MIX), bmap3),
                  pl.BlockSpec((None, MIX, D_HD), bmap3),
                  pl.BlockSpec((None, D_H, D_HD), bmap3),
                  pl.BlockSpec((None, D_H, LANE), bmap3),
                  pl.BlockSpec((SLAB, C_KW), cmap),
                  pl.BlockSpec((D_CONV, 2 * MIX), cmap),
                  pl.BlockSpec((1, 2 * MIX), cmap),
                  pl.BlockSpec((1, SLAB), cmap),
                  pl.BlockSpec((8, MIX), cmap)],
        out_specs=[pl.BlockSpec((None, tb, D_MODEL), lambda b, j: (b, j, 0)),
                   pl.BlockSpec((None, C_KW, C_VD), bmap3),
                   pl.BlockSpec((None, 8, 2 * MIX), bmap3),
                   pl.BlockSpec((None, MIX, D_HD), bmap3),
                   pl.BlockSpec((None, D_H, D_HD), bmap3),
                   pl.BlockSpec((None, D_H, LANE), bmap3)],
        out_shape=[jax.ShapeDtypeStruct((bsz, t, D_MODEL), F32),
                   jax.ShapeDtypeStruct((bsz, C_KW, C_VD), F32),
                   jax.ShapeDtypeStruct((bsz, 8, 2 * MIX), F32),
                   jax.ShapeDtypeStruct((bsz, MIX, D_HD), F32),
                   jax.ShapeDtypeStruct((bsz, D_H, D_HD), F32),
                   jax.ShapeDtypeStruct((bsz, D_H, LANE), F32)],
        scratch_shapes=[pltpu.VMEM((8, 2 * MIX), F32), pltpu.VMEM((C_KW, C_VD), F32),
                        pltpu.VMEM((MIX, D_HD), F32), pltpu.VMEM((hl, D_HD), F32), pltpu.VMEM((hl, 1), F32),
                        blk(C_KW), blk(C_KW), blk(C_KW), blk(C_KW), blk(MIX),
                        blk(MIX), blk(MIX), blk(SLAB), blk(SLAB), blk(SLAB), blk(MIX)],
        compiler_params=pltpu.CompilerParams(dimension_semantics=("arbitrary", "arbitrary"),
                                             vmem_limit_bytes=VMEM_LIMIT),
        name="mixer_gla_mlstm")(p, sgl0, conv0, mc0, mn0, mm0, g2p, cw, cb, sb, par)


def _time_block(t):
    return math.gcd(t, 256)


def _run(x, shift, s_rwkv, s_hgrn, s_gla, conv, mc, mn, mm, wts):
    bsz, t, d = x.shape
    tb = _time_block(t)
    la = math.gcd(t, 16)
    lbc = math.gcd(t, 32)
    x2 = x.reshape(bsz * t, d)
    p0 = _inproj(x2, wts["g0"], wts["w_in0"])
    o0, shift_n, srw_n, shg_n = _even_mixer(
        p0.reshape(bsz, t, IN_EVEN), shift[0][:, None], s_rwkv[0], s_hgrn[0].reshape(bsz, B_H * B_HD, B_HD),
        wts["mu"], wts["w2p"], wts["a2p"], wts["par_e"], tb, la, lbc)
    x1, p1 = _mid(o0.reshape(bsz * t, d), x2, wts["w_out0"], wts["g1"], wts["w_in1"])
    conv8 = jnp.pad(conv[0], ((0, 0), (8 - (D_CONV - 1), 0), (0, 0)))
    mm_l = jnp.broadcast_to(mm[0][:, :, None], (bsz, D_H, LANE))
    o1, sgl_n, conv_n, mc_n, mn_n, mm_n = _odd_mixer(
        p1.reshape(bsz, t, IN_ODD_PAD), s_gla[0].reshape(bsz, C_KW, C_VD), conv8,
        mc[0].reshape(bsz, MIX, D_HD), mn[0], mm_l,
        wts["g2p"], wts["cw"], wts["cb"], wts["sb"], wts["par_o"], tb, lbc)
    y = _final(o1.reshape(bsz * t, d), x1, wts["w_out1"], wts["gf"])
    return (y.reshape(bsz, t, d), shift_n.reshape(1, bsz, A_SHIFT_W), srw_n[None],
            shg_n.reshape(1, bsz, B_H, B_HD, B_HD), sgl_n.reshape(1, bsz, C_H, C_KD, C_VD),
            conv_n[None, :, 8 - (D_CONV - 1):], mc_n.reshape(1, bsz, D_H, D_HD, D_HD), mn_n[None],
            mm_n[None, :, :, 0])


def _odd_column_order():
    sizes = (C_KW, C_KW, MIX, C_LORA, MIX, 2 * MIX, MIX, D_H, D_H, MIX)
    off = np.cumsum((0,) + sizes)
    cq, ck, cv, cg, gc, dqk, dv, di, df, gd = (np.arange(off[i], off[i + 1]) for i in range(len(sizes)))
    return np.concatenate([cq, ck, cv, gc, dqk, dv, gd, cg, di, df])


def kernel(x_prompt, x_sample, state_shift_a, state_rwkv, state_hgrn, state_gla, state_conv_d, state_mlstm_c,
           state_mlstm_n, state_mlstm_m, norm_g, w_in_even, w_out_even, a_mu, a_w0, a_w2, a_a0, a_a2, a_kk,
           a_ka, a_rk, a_ln_w, a_ln_b, b_lb, b_norm, w_in_odd, w_out_odd, c_g2, c_g2b, c_norm, d_conv_w,
           d_conv_b, d_ib, d_fb, d_norm, final_norm):
    assert w_in_even.shape[0] == 1 and w_in_odd.shape[0] == 1 and b_lb.shape[0] == 2
    zpad = lambda a, rows_before, rows_total: jnp.pad(a, ((rows_before, rows_total - rows_before - a.shape[0]), (0, 0)))
    par_e = jnp.concatenate([a_w0, a_a0, a_kk, a_ka, a_rk, a_ln_w, a_ln_b, b_norm, b_lb[0:1], b_lb[1:2]], axis=0)
    g2b = jnp.pad(c_g2b, ((0, 0), (0, MIX - C_KW)))
    par_o = jnp.concatenate([c_norm, d_norm, g2b], axis=0)
    w_in1 = jnp.pad(w_in_odd[0][:, _odd_column_order()], ((0, 0), (0, SLAB - C_LORA - 2 * D_H)))
    sb = jnp.pad(jnp.concatenate([d_ib, d_fb], axis=1), ((0, 0), (SLAB_I, SLAB - SLAB_I - 2 * D_H)))
    wts = {
        "g0": norm_g[0:1], "g1": norm_g[1:2], "gf": final_norm[None],
        "w_in0": w_in_even[0].astype(BF16), "w_out0": w_out_even[0].astype(BF16),
        "w_in1": w_in1.astype(BF16), "w_out1": w_out_odd[0].astype(BF16),
        "mu": a_mu, "w2p": zpad(a_w2[0], 0, 2 * A_LORA).astype(BF16),
        "a2p": zpad(a_a2[0], A_LORA, 2 * A_LORA).astype(BF16),
        "par_e": zpad(par_e, 0, 16), "par_o": zpad(par_o, 0, 8),
        "g2p": zpad(c_g2[0], 0, SLAB).astype(BF16), "cw": d_conv_w[0], "cb": d_conv_b, "sb": sb,
    }
    bp = x_prompt.shape[0]
    z = lambda *s: jnp.zeros(s, x_prompt.dtype)
    prompt = _run(x_prompt, z(1, bp, A_SHIFT_W), z(1, bp, A_H, A_HD, A_HD), z(1, bp, B_H, B_HD, B_HD),
                  z(1, bp, C_H, C_KD, C_VD), z(1, bp, D_CONV - 1, 2 * MIX), z(1, bp, D_H, D_HD, D_HD),
                  z(1, bp, D_H, D_HD), z(1, bp, D_H), wts)
    sample = _run(x_sample, state_shift_a, state_rwkv, state_hgrn, state_gla, state_conv_d, state_mlstm_c,
                  state_mlstm_n, state_mlstm_m, wts)
    return (prompt[0], sample[0]) + prompt[1:] + sample[1:]
```

```python
import functools
import math

import jax
import jax.numpy as jnp
import numpy as np
from jax import lax
from jax.experimental import pallas as pl
from jax.experimental.pallas import tpu as pltpu

F32 = jnp.float32
BF16 = jnp.bfloat16

D_MODEL = 1024
MIX = D_MODEL // 2
NORM_EPS = 1e-5
A_HD = 64
A_H = MIX // A_HD
A_LORA = 64
A_GN_EPS = 64e-5
A_SHIFT_W = 3 * MIX + 2 * A_LORA
B_HD = 128
B_H = MIX // B_HD
IN_EVEN = A_SHIFT_W + 5 * MIX
C_H = 4
C_VD = MIX // C_H
C_KD = C_VD // 2
C_KW = C_H * C_KD
C_LORA = 16
C_GATE_NORM = 16.0
D_H = 4
D_HD = MIX // D_H
D_CONV = 4
LANE = 128
SLAB = LANE
SLAB_I = C_LORA
SLAB_F = C_LORA + D_H
IN_ODD_PAD = 2 * C_KW + 2 * MIX + 2 * MIX + 2 * MIX + SLAB
VMEM_LIMIT = 48 * 1024 * 1024


_DIMS = {"nn": ((1,), (0,)), "nt": ((1,), (1,)), "tn": ((0,), (0,))}
M_EXACT_R = "rx"
M_EXACT_L = "lx"
M_G = "bf"
M_TINV = "bf"
M_ST = "bf"
M_ATT = "bf"


def _split2(x):
    hi = x.astype(BF16)
    return hi, (x - hi.astype(F32)).astype(BF16)


def _mm(a, b, form, mode):
    dn = (_DIMS[form], ((), ()))
    d = lambda x, y: lax.dot_general(x, y, dn, preferred_element_type=F32)
    if mode == "bf":
        return d(a.astype(BF16), b.astype(BF16))
    if mode == "x3":
        ah, al = _split2(a)
        bh, bl = _split2(b)
        return d(ah, bh) + (d(ah, bl) + d(al, bh))
    if mode == "lx":
        bb = b.astype(BF16)
        h, l = _split2(a)
        return d(h, bb) + d(l, bb)
    assert mode == "rx"
    ab = a.astype(BF16)
    h, l = _split2(b)
    return d(ab, h) + d(ab, l)


def _dot(a, b):
    return jnp.dot(a, b, preferred_element_type=F32)


def _iota(shape, dim):
    return lax.broadcasted_iota(jnp.int32, shape, dim)


def _sigmoid(x):
    return 0.5 + 0.5 * jnp.tanh(0.5 * x)


def _log_sigmoid(x):
    return jnp.minimum(x, 0.0) - jnp.log(1.0 + jnp.exp(-jnp.abs(x)))


def _silu(x):
    return x * _sigmoid(x)


def _rms(x, g):
    return x * lax.rsqrt(jnp.mean(x * x, -1, keepdims=True) + NORM_EPS) * g


def _unroll(trips):
    return 2 if trips % 2 == 0 else 1


def _tile_rows(x, n):
    return jnp.concatenate([x] * n, axis=0)


def _stack_heads(x, n_heads, width):
    return jnp.concatenate([x[:, h * width:(h + 1) * width] for h in range(n_heads)], axis=0)


def _unstack_heads(x, n_heads, rows):
    return jnp.concatenate([x[h * rows:(h + 1) * rows] for h in range(n_heads)], axis=1)


def _head_masks(n_heads, rows, width):
    hl = n_heads * rows
    rh = _iota((hl, n_heads * width), 0) // rows
    lh = _iota((hl, n_heads * width), 1) // width
    wide = (rh == lh).astype(F32)
    ri, ci = _iota((hl, hl), 0), _iota((hl, hl), 1)
    same = (ri // rows) == (ci // rows)
    incl = same & (ci <= ri)
    strict = same & (ci < ri)
    return wide, same, incl, strict


def _chunk_mats(n, chunk):
    ri, ci = _iota((n, n), 0), _iota((n, n), 1)
    same = (ri // chunk) == (ci // chunk)
    return (same & (ci <= ri)).astype(BF16), same.astype(BF16)


def _seg_sum(x, seg):
    bd = ((_iota((LANE, LANE), 0) // seg) == (_iota((LANE, LANE), 1) // seg)).astype(BF16)
    parts = [_mm(x[:, j:j + LANE], bd, "nn", M_EXACT_L) for j in range(0, x.shape[1], LANE)]
    return jnp.concatenate(parts, axis=1)


def _head_rms(x, g, width):
    parts = []
    for j in range(0, x.shape[1], width):
        xs = x[:, j:j + width]
        parts.append(xs * lax.rsqrt(jnp.mean(xs * xs, -1, keepdims=True) + NORM_EPS))
    return jnp.concatenate(parts, axis=1) * g


def _inproj_kernel(x_ref, g_ref, w_ref, p_ref):
    h = _rms(x_ref[...], g_ref[...])
    p_ref[...] = _dot(h.astype(BF16), w_ref[...])


def _mid_kernel(o_ref, x_ref, wo_ref, g_ref, wi_ref, x1_ref, p_ref):
    x1 = x_ref[...] + _dot(o_ref[...].astype(BF16), wo_ref[...])
    x1_ref[...] = x1
    p_ref[...] = _dot(_rms(x1, g_ref[...]).astype(BF16), wi_ref[...])


def _final_kernel(o_ref, x_ref, wo_ref, g_ref, y_ref):
    x2 = x_ref[...] + _dot(o_ref[...].astype(BF16), wo_ref[...])
    y_ref[...] = _rms(x2, g_ref[...])


def _row_tile(m):
    return math.gcd(m, 512)


def _full(shape):
    return pl.BlockSpec(shape, lambda i: (0,) * len(shape))


def _dense_params():
    return pltpu.CompilerParams(dimension_semantics=("arbitrary",), vmem_limit_bytes=VMEM_LIMIT)


def _inproj(x, g, w):
    m, d = x.shape
    n = w.shape[1]
    tm = _row_tile(m)
    return pl.pallas_call(
        _inproj_kernel, grid=(m // tm,),
        in_specs=[pl.BlockSpec((tm, d), lambda i: (i, 0)), _full((1, d)), _full((d, n))],
        out_specs=pl.BlockSpec((tm, n), lambda i: (i, 0)),
        out_shape=jax.ShapeDtypeStruct((m, n), F32),
        compiler_params=_dense_params(), name="inproj")(x, g, w)


def _mid(o, x, wo, g, wi):
    m, d = x.shape
    n = wi.shape[1]
    tm = _row_tile(m)
    return pl.pallas_call(
        _mid_kernel, grid=(m // tm,),
        in_specs=[pl.BlockSpec((tm, d), lambda i: (i, 0)), pl.BlockSpec((tm, d), lambda i: (i, 0)),
                  _full((d, d)), _full((1, d)), _full((d, n))],
        out_specs=[pl.BlockSpec((tm, d), lambda i: (i, 0)), pl.BlockSpec((tm, n), lambda i: (i, 0))],
        out_shape=[jax.ShapeDtypeStruct((m, d), F32), jax.ShapeDtypeStruct((m, n), F32)],
        compiler_params=_dense_params(), name="outproj_inproj")(o, x, wo, g, wi)


def _final(o, x, wo, g):
    m, d = x.shape
    tm = _row_tile(m)
    return pl.pallas_call(
        _final_kernel, grid=(m // tm,),
        in_specs=[pl.BlockSpec((tm, d), lambda i: (i, 0)), pl.BlockSpec((tm, d), lambda i: (i, 0)),
                  _full((d, d)), _full((1, d))],
        out_specs=pl.BlockSpec((tm, d), lambda i: (i, 0)),
        out_shape=jax.ShapeDtypeStruct((m, d), F32),
        compiler_params=_dense_params(), name="outproj_final")(o, x, wo, g)


def _interleave(*stage_generators):
    live = list(stage_generators)
    while live:
        for gen in list(live):
            try:
                next(gen)
            except StopIteration:
                live.remove(gen)


def _rwkv_factor_stages(chunks, consts, out):
    wide, incl, strict, eye, halves = consts
    L = chunks[0][0].shape[0]
    hl = A_H * L
    n = range(len(chunks))
    tile = functools.partial(_tile_rows, n=A_H)
    kk_w = [tile(c[0]) * wide for c in chunks]
    r_w = [tile(c[1]) * wide for c in chunks]
    g = [_mm(jnp.concatenate([kk_w[i], r_w[i]], axis=0),
             jnp.concatenate([tile(chunks[i][2]), tile(chunks[i][3])], axis=0), "nt", M_G) for i in n]
    yield
    a_kk = [jnp.where(strict, x[:hl, :hl], 0.0) for x in g]
    a_kb = [jnp.where(strict, x[:hl, hl:], 0.0) for x in g]
    a_rk = [jnp.where(incl, x[hl:, :hl], 0.0) for x in g]
    a_rb = [jnp.where(incl, x[hl:, hl:], 0.0) for x in g]
    t = [eye - jnp.where(halves[0], x, 0.0) for x in a_kb]
    v_st = [_stack_heads(c[6], A_H, A_HD) for c in chunks]
    av = [_mm(a_kk[i], v_st[i], "nn", M_ST) for i in n]
    o2 = [_mm(a_rk[i], v_st[i], "nn", M_ST) for i in n]
    for half in halves[1:]:
        ta = [_mm(t[i], jnp.where(half, a_kb[i], 0.0), "nn", M_TINV) for i in n]
        yield
        t = [t[i] - _mm(ta[i], t[i], "nn", M_TINV) for i in n]
        yield
    kv = [_mm(v_st[i], tile(chunks[i][4]) * wide, "tn", M_ST) for i in n]
    m1 = [_mm(t[i], kk_w[i], "nn", M_ST) for i in n]
    m2 = [_mm(t[i], av[i], "nn", M_ST) for i in n]
    yield
    out["factors"] = [
        (jnp.concatenate([-m1[i], r_w[i]], axis=0).astype(BF16), -m2[i], o2[i], a_rb[i].astype(BF16), kv[i],
         (tile(chunks[i][5]) * wide).astype(BF16)) for i in n]


def _rwkv_state_stages(n_chunks, load_factors, store_o, s, out):
    for c in range(n_chunks):
        xm, m2n, o2, a_rb, kv, be_w, ecl = load_factors(c)
        hl = xm.shape[0] // 2
        xs = _mm(xm, s, "nt", M_ST)
        yield
        u = xs[:hl] + m2n
        s = s * ecl + kv + _mm(u, be_w, "tn", M_ST)
        o_st = xs[hl:] + o2 + _mm(a_rb, u, "nn", M_ST)
        yield
        store_o(c, _unstack_heads(o_st, A_H, hl // A_H))
    out["rwkv"] = s


def _gla_stages(n_chunks, load_chunk, store_o, s, consts, n_heads, vd, out, key):
    wide, incl = consts
    tile = functools.partial(_tile_rows, n=n_heads)
    for c in range(n_chunks):
        qd, ki, kend, bl, v = load_chunk(c)
        q_w = tile(qd) * wide
        v_st = _stack_heads(v, n_heads, vd)
        att = jnp.where(incl, _mm(q_w, tile(ki), "nt", M_ATT), 0.0)
        kv = _mm(tile(kend) * wide, v_st, "tn", M_ATT)
        yield
        o_st = _mm(att, v_st, "nn", M_ATT) + _mm(q_w, s, "nn", M_ATT)
        dcol = jnp.exp(jnp.broadcast_to(bl, (vd, bl.shape[1])).T)
        s = s * dcol + kv
        yield
        store_o(c, _unstack_heads(o_st, n_heads, qd.shape[0]))
    out[key] = s


def _mlstm_stages(n_chunks, load_chunk, store_o, state, consts, out):
    wide, same, incl, ones_bd = consts
    c, nrow, m = state
    tile = functools.partial(_tile_rows, n=D_H)
    for ch in range(n_chunks):
        q, k, v, li, a, a_last = load_chunk(ch)
        L = q.shape[0]
        hl = D_H * L
        to_row = lambda col: jnp.broadcast_to(col, (hl, LANE)).T[:hl]
        q_st = _stack_heads(q, D_H, D_HD)
        k_st = _stack_heads(k, D_H, D_HD)
        v_st = _stack_heads(v, D_H, D_HD)
        qk = _mm(q_st, k_st, "nt", M_ATT)
        dlog = jnp.where(incl, a - to_row(a) + to_row(li), -jnp.inf)
        m_intra = jnp.max(dlog, -1, keepdims=True)
        e_end = a_last - a + li
        m_loc = jnp.max(jnp.where(same, to_row(e_end), -jnp.inf), -1, keepdims=True)
        yield
        s_intra = jnp.exp(dlog - m_intra) * qk
        num_intra = _mm(s_intra, v_st, "nn", M_ATT)
        den_intra = jnp.sum(s_intra, -1, keepdims=True)
        w_end = jnp.exp(e_end - m_loc)
        c_loc = _mm(tile(v) * wide * w_end, k_st, "tn", M_ATT)
        n_loc = _mm(ones_bd, w_end * k_st, "nn", M_ATT)
        yield
        g = a + m
        m_t = jnp.maximum(g, m_intra)
        f_inter = jnp.exp(g - m_t)
        f_intra = jnp.exp(m_intra - m_t)
        qc = _mm(q_st, c, "nt", M_ATT)
        num_inter = jnp.concatenate(
            [qc[h * L:(h + 1) * L, h * D_HD:(h + 1) * D_HD] for h in range(D_H)], axis=0)
        den_inter = jnp.sum(q_st * nrow, -1, keepdims=True)
        num = num_intra * f_intra + num_inter * f_inter
        den = f_intra * den_intra + f_inter * den_inter
        h_st = num / jnp.maximum(jnp.abs(den), jnp.exp(-m_t))
        m_new = jnp.maximum(a_last + m, m_loc)
        fo = jnp.exp(a_last + m - m_new)
        fl = jnp.exp(m_loc - m_new)
        per_value = lambda col: jnp.concatenate(
            [jnp.broadcast_to(col[h * L:h * L + 1], (D_HD, 1)) for h in range(D_H)], axis=0)
        c = per_value(fo) * c + per_value(fl) * c_loc
        nrow = fo * nrow + fl * n_loc
        m = m_new
        yield
        store_o(ch, _unstack_heads(h_st, D_H, L))
    out["mlstm"] = (c, nrow, m)


(_E_W0, _E_A0, _E_KK, _E_KA, _E_RK, _E_LNW, _E_LNB, _E_BNORM, _E_LB0, _E_LB1) = range(10)


def _even_kernel(p_ref, shift0_ref, srw0_ref, shg0_ref, mu_ref, w2_ref, a2_ref, par_ref,
                 o_ref, shift_ref, srw_ref, shg_ref,
                 prev_sc, srw_sc, shg_sc, kkt_sc, rt_sc, kh_sc, bh_sc, ke_sc, be_sc, v_sc, epos_sc, bonus_sc, oa_sc,
                 qd_sc, ki_sc, kend_sc, bl_sc, ob_sc, xm_sc, m2_sc, o2_sc, arb_sc, kv_sc, bew_sc, *, tb, la, lb):
    j = pl.program_id(1)
    nj = pl.num_programs(1)

    @pl.when(j == 0)
    def _():
        prev_sc[...] = jnp.broadcast_to(shift0_ref[...], prev_sc.shape)
        srw_sc[...] = jnp.concatenate([srw0_ref[h] for h in range(A_H)], axis=1)
        shg_sc[...] = shg0_ref[...]

    par = par_ref[...]
    row = lambda i: par[i:i + 1]

    pa = p_ref[:, :A_SHIFT_W]
    rolled = pltpu.roll(pa, 1, axis=0)
    first = _iota(pa.shape, 0) == 0
    shifted = jnp.where(first, jnp.broadcast_to(prev_sc[0:1], pa.shape), rolled)
    prev_sc[...] = jnp.broadcast_to(pa[tb - 1:tb], prev_sc.shape)
    xm = pa + (shifted - pa) * mu_ref[...]
    r = xm[:, :MIX]
    k = xm[:, MIX:2 * MIX]
    v = xm[:, 2 * MIX:3 * MIX]
    lora_in = xm[:, 3 * MIX:]
    lw = _sigmoid(row(_E_W0) + _dot(jnp.tanh(lora_in).astype(BF16), w2_ref[...])) * (-math.exp(-0.5))
    a = _sigmoid(row(_E_A0) + _dot(lora_in.astype(BF16), a2_ref[...]))
    kk = k * row(_E_KK)
    kk = kk * lax.rsqrt(jnp.maximum(_seg_sum(kk * kk, A_HD), 1e-12))
    k = k * (1.0 + (a - 1.0) * row(_E_KA))
    bv = kk * a
    cum_a, ones_a = _chunk_mats(tb, la)
    c = _mm(cum_a, lw, "nn", M_EXACT_R)
    cl = _mm(ones_a, lw, "nn", M_EXACT_R)
    e_pos = jnp.exp(c)
    e_neg = jnp.exp(-c)
    e_end = jnp.exp(cl - c)
    kkt_sc[...] = kk * jnp.exp(c - lw)
    rt_sc[...] = r * e_pos
    kh_sc[...] = k * e_neg
    bh_sc[...] = bv * e_neg
    ke_sc[...] = k * e_end
    be_sc[...] = bv * e_end
    v_sc[...] = v
    epos_sc[...] = e_pos
    bonus_sc[...] = _seg_sum(r * k * row(_E_RK), A_HD) * v

    wide, _, incl, strict = _head_masks(A_H, la, A_HD)
    hl = A_H * la
    ri, ci_ = _iota((hl, hl), 0), _iota((hl, hl), 1)
    eye = (ri == ci_).astype(F32)
    halves = []
    m = 1
    while m < la:
        halves.append(((ri // (2 * m)) == (ci_ // (2 * m))) & ((ri // m) != (ci_ // m)))
        m *= 2
    consts_a = (wide, incl, strict, eye, halves)

    e0 = row(_E_LB0)
    e1 = row(_E_LB1)
    emax = jnp.maximum(e0, e1)
    e0 = jnp.exp(e0 - emax)
    lower = e0 / (e0 + jnp.exp(e1 - emax))
    base = A_SHIFT_W + MIX
    g = lower + (1.0 - lower) * jax.nn.sigmoid(p_ref[:, base + MIX:base + 2 * MIX])
    logg = jnp.log(g)
    cum_b, ones_b = _chunk_mats(tb, lb)
    gb = _mm(cum_b, logg, "nn", M_EXACT_R)
    gl = _mm(ones_b, logg, "nn", M_EXACT_R)
    qd_sc[...] = _silu(p_ref[:, base:base + MIX]) * (B_HD ** -0.5) * jnp.exp(gb)
    ki_sc[...] = (1.0 - g) * jnp.exp(-gb)
    kend_sc[...] = (1.0 - g) * jnp.exp(gl - gb)
    bl_sc[...] = gl

    wide_b, _, incl_b, _ = _head_masks(B_H, lb, B_HD)
    consts_b = (wide_b, incl_b)

    group = math.gcd(tb // la, 4)
    ngroups = tb // (group * la)
    hper = (group * la) // lb

    def factor_stages(gi, slot, out):
        sls = [pl.ds(pl.multiple_of((gi * group + u) * la, la), la) for u in range(group)]
        yield from _rwkv_factor_stages(
            [(kkt_sc[sl, :], rt_sc[sl, :], kh_sc[sl, :], bh_sc[sl, :], ke_sc[sl, :], be_sc[sl, :], v_sc[sl, :])
             for sl in sls], consts_a, out)
        for u, (xm, m2n, o2, a_rb, kv, be_w) in enumerate(out["factors"]):
            xm_sc[slot + u] = xm
            m2_sc[slot + u] = m2n
            o2_sc[slot + u] = o2
            arb_sc[slot + u] = a_rb
            kv_sc[slot + u] = kv
            bew_sc[slot + u] = be_w

    def chunk_body(gi, carry):
        s_a, s_b = carry
        slot = (gi % 2) * group
        out = {}

        def load_factors(u):
            i = slot + u
            return (xm_sc[i], m2_sc[i], o2_sc[i], arb_sc[i], kv_sc[i], bew_sc[i],
                    epos_sc[pl.ds((gi * group + u) * la + la - 1, 1), :])

        def store_oa(u, o):
            oa_sc[pl.ds(pl.multiple_of((gi * group + u) * la, la), la), :] = o

        def load_b(u):
            ci = gi * hper + u
            sl = pl.ds(pl.multiple_of(ci * lb, lb), lb)
            return (qd_sc[sl, :], ki_sc[sl, :], kend_sc[sl, :], bl_sc[pl.ds(ci * lb, 1), :],
                    p_ref[sl, base + 2 * MIX:base + 3 * MIX])

        def store_ob(u, o):
            ob_sc[pl.ds(pl.multiple_of((gi * hper + u) * lb, lb), lb), :] = o

        stages = [_rwkv_state_stages(group, load_factors, store_oa, s_a, out),
                  _gla_stages(hper, load_b, store_ob, s_b, consts_b, B_H, B_HD, out, "hgrn")]
        if ngroups > 1:
            stages.insert(1, factor_stages(jnp.minimum(gi + 1, ngroups - 1), group - slot, {}))
        _interleave(*stages)
        return out["rwkv"], out["hgrn"]

    _interleave(factor_stages(0, 0, {}))
    s_a, s_b = lax.fori_loop(0, ngroups, chunk_body, (srw_sc[...], shg_sc[...]))
    srw_sc[...] = s_a
    shg_sc[...] = s_b

    o = oa_sc[...]
    mean = _seg_sum(o, A_HD) * (1.0 / A_HD)
    cen = o - mean
    var = _seg_sum(cen * cen, A_HD) * (1.0 / A_HD)
    oa = cen * lax.rsqrt(var + A_GN_EPS) * row(_E_LNW) + row(_E_LNB)
    oa = oa + bonus_sc[...]
    oa = oa * _silu(p_ref[:, A_SHIFT_W:A_SHIFT_W + MIX])
    ob = _head_rms(ob_sc[...], row(_E_BNORM), B_HD) * _silu(p_ref[:, base + 3 * MIX:base + 4 * MIX])
    o_ref[...] = jnp.concatenate([oa, ob], axis=1).astype(o_ref.dtype)

    @pl.when(j == nj - 1)
    def _():
        shift_ref[...] = prev_sc[0:1]
        s = srw_sc[...]
        for h in range(A_H):
            srw_ref[h] = s[:, h * A_HD:(h + 1) * A_HD]
        shg_ref[...] = shg_sc[...]


def _even_mixer(p, shift0, srw0, shg0, mu, w2p, a2p, par, tb, la, lb):
    bsz, t, _ = p.shape
    kern = functools.partial(_even_kernel, tb=tb, la=la, lb=lb)
    nch, hl = 2 * math.gcd(tb // la, 4), A_H * la
    bmap3 = lambda b, j: (b, 0, 0)
    bmap4 = lambda b, j: (b, 0, 0, 0)
    cmap = lambda b, j: (0, 0)
    blk = lambda: pltpu.VMEM((tb, MIX), F32)
    return pl.pallas_call(
        kern, grid=(bsz, t // tb),
        in_specs=[pl.BlockSpec((None, tb, IN_EVEN), lambda b, j: (b, j, 0)),
                  pl.BlockSpec((None, 1, A_SHIFT_W), bmap3),
                  pl.BlockSpec((None, A_H, A_HD, A_HD), bmap4),
                  pl.BlockSpec((None, B_H * B_HD, B_HD), bmap3),
                  pl.BlockSpec((1, A_SHIFT_W), cmap),
                  pl.BlockSpec((2 * A_LORA, MIX), cmap),
                  pl.BlockSpec((2 * A_LORA, MIX), cmap),
                  pl.BlockSpec((16, MIX), cmap)],
        out_specs=[pl.BlockSpec((None, tb, D_MODEL), lambda b, j: (b, j, 0)),
                   pl.BlockSpec((None, 1, A_SHIFT_W), bmap3),
                   pl.BlockSpec((None, A_H, A_HD, A_HD), bmap4),
                   pl.BlockSpec((None, B_H * B_HD, B_HD), bmap3)],
        out_shape=[jax.ShapeDtypeStruct((bsz, t, D_MODEL), F32),
                   jax.ShapeDtypeStruct((bsz, 1, A_SHIFT_W), F32),
                   jax.ShapeDtypeStruct((bsz, A_H, A_HD, A_HD), F32),
                   jax.ShapeDtypeStruct((bsz, B_H * B_HD, B_HD), F32)],
        scratch_shapes=[pltpu.VMEM((8, A_SHIFT_W), F32), pltpu.VMEM((A_HD, MIX), F32),
                        pltpu.VMEM((B_H * B_HD, B_HD), F32)] + [blk() for _ in range(15)] + [
                            pltpu.VMEM((nch, 2 * hl, MIX), BF16), pltpu.VMEM((nch, hl, A_HD), F32),
                            pltpu.VMEM((nch, hl, A_HD), F32), pltpu.VMEM((nch, hl, hl), BF16),
                            pltpu.VMEM((nch, A_HD, MIX), F32), pltpu.VMEM((nch, hl, MIX), BF16)],
        compiler_params=pltpu.CompilerParams(dimension_semantics=("arbitrary", "arbitrary"),
                                             vmem_limit_bytes=VMEM_LIMIT),
        name="mixer_rwkv_hgrn")(p, shift0, srw0, shg0, mu, w2p, a2p, par)


_O_CQ = 0
_O_CK = _O_CQ + C_KW
_O_CV = _O_CK + C_KW
_O_GC = _O_CV + MIX
_O_DQK = _O_GC + MIX
_O_DV = _O_DQK + 2 * MIX
_O_GD = _O_DV + MIX
_O_SLAB = _O_GD + MIX
(_P_CNORM, _P_DNORM, _P_G2B) = range(3)


def _odd_kernel(p_ref, sgl0_ref, conv0_ref, mc0_ref, mn0_ref, mm0_ref, g2_ref, cw_ref, cb_ref, sb_ref, par_ref,
                o_ref, sgl_ref, conv_ref, mc_ref, mn_ref, mm_ref,
                prev_sc, sgl_sc, mc_sc, mn_sc, mm_sc, cq_sc, ck_sc, ce_sc, cl_sc, oc_sc,
                dq_sc, dk_sc, li_sc, fa_sc, fl_sc, od_sc, *, tb, lc):
    j = pl.program_id(1)
    nj = pl.num_programs(1)
    hl = D_H * lc
    head_rows = lambda x: jnp.concatenate(
        [jnp.broadcast_to(x[h:h + 1], (lc, x.shape[1])) for h in range(D_H)], axis=0)

    @pl.when(j == 0)
    def _():
        prev_sc[...] = conv0_ref[...]
        sgl_sc[...] = sgl0_ref[...]
        mc_sc[...] = mc0_ref[...]
        mn_sc[...] = head_rows(mn0_ref[...])
        mm_sc[...] = head_rows(mm0_ref[...])[:, :1]

    par = par_ref[...]
    row = lambda i: par[i:i + 1]

    cum_c, ones_c = _chunk_mats(tb, lc)
    slab = p_ref[:, _O_SLAB:_O_SLAB + SLAB]
    pre = _dot(slab.astype(BF16), g2_ref[...]) + row(_P_G2B)[:, :C_KW]
    logg = _log_sigmoid(pre) * (1.0 / C_GATE_NORM)
    gb = _mm(cum_c, logg, "nn", M_EXACT_R)
    gl = _mm(ones_c, logg, "nn", M_EXACT_R)
    ck = p_ref[:, _O_CK:_O_CK + C_KW]
    cq_sc[...] = p_ref[:, _O_CQ:_O_CQ + C_KW] * (C_KD ** -0.5) * jnp.exp(gb)
    ck_sc[...] = ck * jnp.exp(-gb)
    ce_sc[...] = ck * jnp.exp(gl - gb)
    cl_sc[...] = gl

    wide_c, _, incl_c, _ = _head_masks(C_H, lc, C_KD)
    consts_c = (wide_c, incl_c)

    x = p_ref[:, _O_DQK:_O_DQK + 2 * MIX]
    prev = prev_sc[...]
    cw = cw_ref[...]
    conv = cb_ref[...] + x * cw[D_CONV - 1:D_CONV]
    head_row = _iota((8, 2 * MIX), 0)
    for s in range(1, D_CONV):
        xr = pltpu.roll(x, s, axis=0)
        top = jnp.where(head_row < s, pltpu.roll(prev, s, axis=0), xr[:8])
        xs = top if tb == 8 else jnp.concatenate([top, xr[8:]], axis=0)
        conv = conv + xs * cw[D_CONV - 1 - s:D_CONV - s]
    prev_sc[...] = x[tb - 8:]
    conv = _silu(conv)
    dq_sc[...] = conv[:, :MIX]
    dk_sc[...] = conv[:, MIX:] * (D_HD ** -0.5)
    gates = slab + sb_ref[...]
    lf = pltpu.roll(_log_sigmoid(gates), SLAB - (SLAB_F - SLAB_I), axis=1)
    li_sc[...] = gates
    fa_sc[...] = _mm(cum_c, lf, "nn", M_EXACT_R)
    fl_sc[...] = _mm(ones_c, lf, "nn", M_EXACT_R)
    lane = _iota((hl, SLAB), 1)
    head = _iota((hl, SLAB), 0) // lc
    pick = lambda z: jnp.sum(jnp.where(lane == head + SLAB_I, _tile_rows(z, D_H), 0.0), -1, keepdims=True)

    wide_d, same_d, incl_d, _ = _head_masks(D_H, lc, D_HD)
    consts_d = (wide_d, same_d, incl_d, same_d.astype(BF16))

    per_trip = _unroll(tb // lc)

    def chunk_body(gi, carry):
        s_c, state_d = carry
        out = {}
        rows = lambda u: pl.ds(pl.multiple_of((gi * per_trip + u) * lc, lc), lc)

        def load_c(u):
            sl = rows(u)
            return (cq_sc[sl, :], ck_sc[sl, :], ce_sc[sl, :], cl_sc[pl.ds((gi * per_trip + u) * lc, 1), :],
                    p_ref[sl, _O_CV:_O_CV + MIX])

        def store_c(u, o):
            oc_sc[rows(u), :] = o

        def load_d(u):
            sl = rows(u)
            return (dq_sc[sl, :], dk_sc[sl, :], p_ref[sl, _O_DV:_O_DV + MIX],
                    pick(li_sc[sl, :]), pick(fa_sc[sl, :]), pick(fl_sc[sl, :]))

        def store_d(u, o):
            od_sc[rows(u), :] = o

        _interleave(_mlstm_stages(per_trip, load_d, store_d, state_d, consts_d, out),
                    _gla_stages(per_trip, load_c, store_c, s_c, consts_c, C_H, C_VD, out, "gla"))
        return out["gla"], out["mlstm"]

    s_c, (c, n, m) = lax.fori_loop(0, tb // (lc * per_trip), chunk_body,
                                   (sgl_sc[...], (mc_sc[...], mn_sc[...], mm_sc[...])))
    sgl_sc[...] = s_c
    mc_sc[...] = c
    mn_sc[...] = n
    mm_sc[...] = m

    oc = _head_rms(oc_sc[...], row(_P_CNORM), C_VD) * _silu(p_ref[:, _O_GC:_O_GC + MIX])
    od = _head_rms(od_sc[...], row(_P_DNORM), D_HD) * _silu(p_ref[:, _O_GD:_O_GD + MIX])
    o_ref[...] = jnp.concatenate([oc, od], axis=1).astype(o_ref.dtype)

    @pl.when(j == nj - 1)
    def _():
        sgl_ref[...] = sgl_sc[...]
        conv_ref[...] = prev_sc[...]
        mc_ref[...] = mc_sc[...]
        mn_ref[...] = jnp.concatenate([mn_sc[h * lc:h * lc + 1] for h in range(D_H)], axis=0)
        mm_ref[...] = jnp.concatenate(
            [jnp.broadcast_to(mm_sc[h * lc:h * lc + 1], (1, LANE)) for h in range(D_H)], axis=0)


def _odd_mixer(p, sgl0, conv0, mc0, mn0, mm0, g2p, cw, cb, sb, par, tb, lc):
    bsz, t, _ = p.shape
    hl = D_H * lc
    kern = functools.partial(_odd_kernel, tb=tb, lc=lc)
    bmap3 = lambda b, j: (b, 0, 0)
    cmap = lambda b, j: (0, 0)
    blk = lambda w: pltpu.VMEM((tb, w), F32)
    return pl.pallas_call(
        kern, grid=(bsz, t // tb),
        in_specs=[pl.BlockSpec((None, tb, IN_ODD_PAD), lambda b, j: (b, j, 0)),
                  pl.BlockSpec((None, C_KW, C_VD), bmap3),
                  pl.BlockSpec((None, 8, 2 * MIX), bmap3),
                  pl.BlockSpec((None, MIX, D_HD), bmap3),
                  pl.BlockSpec((None, D_H, D_HD), bmap3),
                  pl.BlockSpec((None, D_H, LANE), bmap3),
                  pl.BlockSpec((SLAB, C_KW), cmap),
                  pl.BlockSpec((D_CONV, 2 * MIX), cmap),
                  pl.BlockSpec((1, 2 * MIX), cmap),
                  pl.BlockSpec((1, SLAB), cmap),
                  pl.BlockSpec((8, MIX), cmap)],
        out_specs=[pl.BlockSpec((None, tb, D_MODEL), lambda b, j: (b, j, 0)),
                   pl.BlockSpec((None, C_KW, C_VD), bmap3),
                   pl.BlockSpec((None, 8, 2 * MIX), bmap3),
                   pl.BlockSpec((None, MIX, D_HD), bmap3),
                   pl.BlockSpec((None, D_H, D_HD), bmap3),
                   pl.BlockSpec((None, D_H, LANE), bmap3)],
        out_shape=[jax.ShapeDtypeStruct((bsz, t, D_MODEL), F32),
                   jax.ShapeDtypeStruct((bsz, C_KW, C_VD), F32),
                   jax.ShapeDtypeStruct((bsz, 8, 2 * MIX), F32),
                   jax.ShapeDtypeStruct((bsz, MIX, D_HD), F32),
                   jax.ShapeDtypeStruct((bsz, D_H, D_HD), F32),
                   jax.ShapeDtypeStruct((bsz, D_H, LANE), F32)],
        scratch_shapes=[pltpu.VMEM((8, 2 * MIX), F32), pltpu.VMEM((C_KW, C_VD), F32),
                        pltpu.VMEM((MIX, D_HD), F32), pltpu.VMEM((hl, D_HD), F32), pltpu.VMEM((hl, 1), F32),
                        blk(C_KW), blk(C_KW), blk(C_KW), blk(C_KW), blk(MIX),
                        blk(MIX), blk(MIX), blk(SLAB), blk(SLAB), blk(SLAB), blk(MIX)],
        compiler_params=pltpu.CompilerParams(dimension_semantics=("arbitrary", "arbitrary"),
                                             vmem_limit_bytes=VMEM_LIMIT),
        name="mixer_gla_mlstm")(p, sgl0, conv0, mc0, mn0, mm0, g2p, cw, cb, sb, par)


def _time_block(t):
    return math.gcd(t, 256)


def _run(x, shift, s_rwkv, s_hgrn, s_gla, conv, mc, mn, mm, wts):
    bsz, t, d = x.shape
    tb = _time_block(t)
    la = math.gcd(t, 16)
    lbc = math.gcd(t, 32)
    x2 = x.reshape(bsz * t, d)
    p0 = _inproj(x2, wts["g0"], wts["w_in0"])
    o0, shift_n, srw_n, shg_n = _even_mixer(
        p0.reshape(bsz, t, IN_EVEN), shift[0][:, None], s_rwkv[0], s_hgrn[0].reshape(bsz, B_H * B_HD, B_HD),
        wts["mu"], wts["w2p"], wts["a2p"], wts["par_e"], tb, la, lbc)
    x1, p1 = _mid(o0.reshape(bsz * t, d), x2, wts["w_out0"], wts["g1"], wts["w_in1"])
    conv8 = jnp.pad(conv[0], ((0, 0), (8 - (D_CONV - 1), 0), (0, 0)))
    mm_l = jnp.broadcast_to(mm[0][:, :, None], (bsz, D_H, LANE))
    o1, sgl_n, conv_n, mc_n, mn_n, mm_n = _odd_mixer(
        p1.reshape(bsz, t, IN_ODD_PAD), s_gla[0].reshape(bsz, C_KW, C_VD), conv8,
        mc[0].reshape(bsz, MIX, D_HD), mn[0], mm_l,
        wts["g2p"], wts["cw"], wts["cb"], wts["sb"], wts["par_o"], tb, lbc)
    y = _final(o1.reshape(bsz * t, d), x1, wts["w_out1"], wts["gf"])
    return (y.reshape(bsz, t, d), shift_n.reshape(1, bsz, A_SHIFT_W), srw_n[None],
            shg_n.reshape(1, bsz, B_H, B_HD, B_HD), sgl_n.reshape(1, bsz, C_H, C_KD, C_VD),
            conv_n[None, :, 8 - (D_CONV - 1):], mc_n.reshape(1, bsz, D_H, D_HD, D_HD), mn_n[None],
            mm_n[None, :, :, 0])


def _odd_column_order():
    sizes = (C_KW, C_KW, MIX, C_LORA, MIX, 2 * MIX, MIX, D_H, D_H, MIX)
    off = np.cumsum((0,) + sizes)
    cq, ck, cv, cg, gc, dqk, dv, di, df, gd = (np.arange(off[i], off[i + 1]) for i in range(len(sizes)))
    return np.concatenate([cq, ck, cv, gc, dqk, dv, gd, cg, di, df])


def kernel(x_prompt, x_sample, state_shift_a, state_rwkv, state_hgrn, state_gla, state_conv_d, state_mlstm_c,
           state_mlstm_n, state_mlstm_m, norm_g, w_in_even, w_out_even, a_mu, a_w0, a_w2, a_a0, a_a2, a_kk,
           a_ka, a_rk, a_ln_w, a_ln_b, b_lb, b_norm, w_in_odd, w_out_odd, c_g2, c_g2b, c_norm, d_conv_w,
           d_conv_b, d_ib, d_fb, d_norm, final_norm):
    assert w_in_even.shape[0] == 1 and w_in_odd.shape[0] == 1 and b_lb.shape[0] == 2
    zpad = lambda a, rows_before, rows_total: jnp.pad(a, ((rows_before, rows_total - rows_before - a.shape[0]), (0, 0)))
    par_e = jnp.concatenate([a_w0, a_a0, a_kk, a_ka, a_rk, a_ln_w, a_ln_b, b_norm, b_lb[0:1], b_lb[1:2]], axis=0)
    g2b = jnp.pad(c_g2b, ((0, 0), (0, MIX - C_KW)))
    par_o = jnp.concatenate([c_norm, d_norm, g2b], axis=0)
    w_in1 = jnp.pad(w_in_odd[0][:, _odd_column_order()], ((0, 0), (0, SLAB - C_LORA - 2 * D_H)))
    sb = jnp.pad(jnp.concatenate([d_ib, d_fb], axis=1), ((0, 0), (SLAB_I, SLAB - SLAB_I - 2 * D_H)))
    wts = {
        "g0": norm_g[0:1], "g1": norm_g[1:2], "gf": final_norm[None],
        "w_in0": w_in_even[0].astype(BF16), "w_out0": w_out_even[0].astype(BF16),
        "w_in1": w_in1.astype(BF16), "w_out1": w_out_odd[0].astype(BF16),
        "mu": a_mu, "w2p": zpad(a_w2[0], 0, 2 * A_LORA).astype(BF16),
        "a2p": zpad(a_a2[0], A_LORA, 2 * A_LORA).astype(BF16),
        "par_e": zpad(par_e, 0, 16), "par_o": zpad(par_o, 0, 8),
        "g2p": zpad(c_g2[0], 0, SLAB).astype(BF16), "cw": d_conv_w[0], "cb": d_conv_b, "sb": sb,
    }
    bp = x_prompt.shape[0]
    z = lambda *s: jnp.zeros(s, x_prompt.dtype)
    prompt = _run(x_prompt, z(1, bp, A_SHIFT_W), z(1, bp, A_H, A_HD, A_HD), z(1, bp, B_H, B_HD, B_HD),
                  z(1, bp, C_H, C_KD, C_VD), z(1, bp, D_CONV - 1, 2 * MIX), z(1, bp, D_H, D_HD, D_HD),
                  z(1, bp, D_H, D_HD), z(1, bp, D_H), wts)
    sample = _run(x_sample, state_shift_a, state_rwkv, state_hgrn, state_gla, state_conv_d, state_mlstm_c,
                  state_mlstm_n, state_mlstm_m, wts)
    return (prompt[0], sample[0]) + prompt[1:] + sample[1:]
```

```python
import functools
import math

import jax
import jax.numpy as jnp
import numpy as np
from jax import lax
from jax.experimental import pallas as pl
from jax.experimental.pallas import tpu as pltpu

F32 = jnp.float32
BF16 = jnp.bfloat16

D_MODEL = 1024
MIX = D_MODEL // 2
NORM_EPS = 1e-5
A_HD = 64
A_H = MIX // A_HD
A_LORA = 64
A_GN_EPS = 64e-5
A_SHIFT_W = 3 * MIX + 2 * A_LORA
B_HD = 128
B_H = MIX // B_HD
IN_EVEN = A_SHIFT_W + 5 * MIX
C_H = 4
C_VD = MIX // C_H
C_KD = C_VD // 2
C_KW = C_H * C_KD
C_LORA = 16
C_GATE_NORM = 16.0
D_H = 4
D_HD = MIX // D_H
D_CONV = 4
LANE = 128
SLAB = LANE
SLAB_I = C_LORA
SLAB_F = C_LORA + D_H
IN_ODD_PAD = 2 * C_KW + 2 * MIX + 2 * MIX + 2 * MIX + SLAB
VMEM_LIMIT = 48 * 1024 * 1024


_DIMS = {"nn": ((1,), (0,)), "nt": ((1,), (1,)), "tn": ((0,), (0,))}
M_EXACT_R = "rx"
M_EXACT_L = "lx"
M_G = "bf"
M_TINV = "bf"
M_ST = "bf"
M_ATT = "bf"


def _split2(x):
    hi = x.astype(BF16)
    return hi, (x - hi.astype(F32)).astype(BF16)


def _mm(a, b, form, mode):
    dn = (_DIMS[form], ((), ()))
    d = lambda x, y: lax.dot_general(x, y, dn, preferred_element_type=F32)
    if mode == "bf":
        return d(a.astype(BF16), b.astype(BF16))
    if mode == "x3":
        ah, al = _split2(a)
        bh, bl = _split2(b)
        return d(ah, bh) + (d(ah, bl) + d(al, bh))
    if mode == "lx":
        bb = b.astype(BF16)
        h, l = _split2(a)
        return d(h, bb) + d(l, bb)
    assert mode == "rx"
    ab = a.astype(BF16)
    h, l = _split2(b)
    return d(ab, h) + d(ab, l)


def _dot(a, b):
    return jnp.dot(a, b, preferred_element_type=F32)


def _iota(shape, dim):
    return lax.broadcasted_iota(jnp.int32, shape, dim)


def _sigmoid(x):
    return 0.5 + 0.5 * jnp.tanh(0.5 * x)


def _log_sigmoid(x):
    return jnp.minimum(x, 0.0) - jnp.log(1.0 + jnp.exp(-jnp.abs(x)))


def _silu(x):
    return x * _sigmoid(x)


def _rms(x, g):
    return x * lax.rsqrt(jnp.mean(x * x, -1, keepdims=True) + NORM_EPS) * g


def _unroll(trips):
    return 2 if trips % 2 == 0 else 1


def _tile_rows(x, n):
    return jnp.concatenate([x] * n, axis=0)


def _stack_heads(x, n_heads, width):
    return jnp.concatenate([x[:, h * width:(h + 1) * width] for h in range(n_heads)], axis=0)


def _unstack_heads(x, n_heads, rows):
    return jnp.concatenate([x[h * rows:(h + 1) * rows] for h in range(n_heads)], axis=1)


def _head_masks(n_heads, rows, width):
    hl = n_heads * rows
    rh = _iota((hl, n_heads * width), 0) // rows
    lh = _iota((hl, n_heads * width), 1) // width
    wide = (rh == lh).astype(F32)
    ri, ci = _iota((hl, hl), 0), _iota((hl, hl), 1)
    same = (ri // rows) == (ci // rows)
    incl = same & (ci <= ri)
    strict = same & (ci < ri)
    return wide, same, incl, strict


def _chunk_mats(n, chunk):
    ri, ci = _iota((n, n), 0), _iota((n, n), 1)
    same = (ri // chunk) == (ci // chunk)
    return (same & (ci <= ri)).astype(BF16), same.astype(BF16)


def _seg_sum(x, seg):
    bd = ((_iota((LANE, LANE), 0) // seg) == (_iota((LANE, LANE), 1) // seg)).astype(BF16)
    parts = [_mm(x[:, j:j + LANE], bd, "nn", M_EXACT_L) for j in range(0, x.shape[1], LANE)]
    return jnp.concatenate(parts, axis=1)


def _head_rms(x, g, width):
    parts = []
    for j in range(0, x.shape[1], width):
        xs = x[:, j:j + width]
        parts.append(xs * lax.rsqrt(jnp.mean(xs * xs, -1, keepdims=True) + NORM_EPS))
    return jnp.concatenate(parts, axis=1) * g


def _inproj_kernel(x_ref, g_ref, w_ref, p_ref):
    h = _rms(x_ref[...], g_ref[...])
    p_ref[...] = _dot(h.astype(BF16), w_ref[...])


def _mid_kernel(o_ref, x_ref, wo_ref, g_ref, wi_ref, x1_ref, p_ref):
    x1 = x_ref[...] + _dot(o_ref[...].astype(BF16), wo_ref[...])
    x1_ref[...] = x1
    p_ref[...] = _dot(_rms(x1, g_ref[...]).astype(BF16), wi_ref[...])


def _final_kernel(o_ref, x_ref, wo_ref, g_ref, y_ref):
    x2 = x_ref[...] + _dot(o_ref[...].astype(BF16), wo_ref[...])
    y_ref[...] = _rms(x2, g_ref[...])


def _row_tile(m):
    return math.gcd(m, 512)


def _full(shape):
    return pl.BlockSpec(shape, lambda i: (0,) * len(shape))


def _dense_params():
    return pltpu.CompilerParams(dimension_semantics=("arbitrary",), vmem_limit_bytes=VMEM_LIMIT)


def _inproj(x, g, w):
    m, d = x.shape
    n = w.shape[1]
    tm = _row_tile(m)
    return pl.pallas_call(
        _inproj_kernel, grid=(m // tm,),
        in_specs=[pl.BlockSpec((tm, d), lambda i: (i, 0)), _full((1, d)), _full((d, n))],
        out_specs=pl.BlockSpec((tm, n), lambda i: (i, 0)),
        out_shape=jax.ShapeDtypeStruct((m, n), F32),
        compiler_params=_dense_params(), name="inproj")(x, g, w)


def _mid(o, x, wo, g, wi):
    m, d = x.shape
    n = wi.shape[1]
    tm = _row_tile(m)
    return pl.pallas_call(
        _mid_kernel, grid=(m // tm,),
        in_specs=[pl.BlockSpec((tm, d), lambda i: (i, 0)), pl.BlockSpec((tm, d), lambda i: (i, 0)),
                  _full((d, d)), _full((1, d)), _full((d, n))],
        out_specs=[pl.BlockSpec((tm, d), lambda i: (i, 0)), pl.BlockSpec((tm, n), lambda i: (i, 0))],
        out_shape=[jax.ShapeDtypeStruct((m, d), F32), jax.ShapeDtypeStruct((m, n), F32)],
        compiler_params=_dense_params(), name="outproj_inproj")(o, x, wo, g, wi)


def _final(o, x, wo, g):
    m, d = x.shape
    tm = _row_tile(m)
    return pl.pallas_call(
        _final_kernel, grid=(m // tm,),
        in_specs=[pl.BlockSpec((tm, d), lambda i: (i, 0)), pl.BlockSpec((tm, d), lambda i: (i, 0)),
                  _full((d, d)), _full((1, d))],
        out_specs=pl.BlockSpec((tm, d), lambda i: (i, 0)),
        out_shape=jax.ShapeDtypeStruct((m, d), F32),
        compiler_params=_dense_params(), name="outproj_final")(o, x, wo, g)


def _interleave(*stage_generators):
    live = list(stage_generators)
    while live:
        for gen in list(live):
            try:
                next(gen)
            except StopIteration:
                live.remove(gen)


def _rwkv_factor_stages(chunks, consts, out):
    wide, incl, strict, eye, halves = consts
    L = chunks[0][0].shape[0]
    hl = A_H * L
    n = range(len(chunks))
    tile = functools.partial(_tile_rows, n=A_H)
    kk_w = [tile(c[0]) * wide for c in chunks]
    r_w = [tile(c[1]) * wide for c in chunks]
    g = [_mm(jnp.concatenate([kk_w[i], r_w[i]], axis=0),
             jnp.concatenate([tile(chunks[i][2]), tile(chunks[i][3])], axis=0), "nt", M_G) for i in n]
    yield
    a_kk = [jnp.where(strict, x[:hl, :hl], 0.0) for x in g]
    a_kb = [jnp.where(strict, x[:hl, hl:], 0.0) for x in g]
    a_rk = [jnp.where(incl, x[hl:, :hl], 0.0) for x in g]
    a_rb = [jnp.where(incl, x[hl:, hl:], 0.0) for x in g]
    t = [eye - jnp.where(halves[0], x, 0.0) for x in a_kb]
    v_st = [_stack_heads(c[6], A_H, A_HD) for c in chunks]
    av = [_mm(a_kk[i], v_st[i], "nn", M_ST) for i in n]
    o2 = [_mm(a_rk[i], v_st[i], "nn", M_ST) for i in n]
    for half in halves[1:]:
        ta = [_mm(t[i], jnp.where(half, a_kb[i], 0.0), "nn", M_TINV) for i in n]
        yield
        t = [t[i] - _mm(ta[i], t[i], "nn", M_TINV) for i in n]
        yield
    kv = [_mm(v_st[i], tile(chunks[i][4]) * wide, "tn", M_ST) for i in n]
    m1 = [_mm(t[i], kk_w[i], "nn", M_ST) for i in n]
    m2 = [_mm(t[i], av[i], "nn", M_ST) for i in n]
    yield
    out["factors"] = [
        (jnp.concatenate([-m1[i], r_w[i]], axis=0).astype(BF16), -m2[i], o2[i], a_rb[i].astype(BF16), kv[i],
         (tile(chunks[i][5]) * wide).astype(BF16)) for i in n]


def _rwkv_state_stages(n_chunks, load_factors, store_o, s, out):
    for c in range(n_chunks):
        xm, m2n, o2, a_rb, kv, be_w, ecl = load_factors(c)
        hl = xm.shape[0] // 2
        xs = _mm(xm, s, "nt", M_ST)
        yield
        u = xs[:hl] + m2n
        s = s * ecl + kv + _mm(u, be_w, "tn", M_ST)
        o_st = xs[hl:] + o2 + _mm(a_rb, u, "nn", M_ST)
        yield
        store_o(c, _unstack_heads(o_st, A_H, hl // A_H))
    out["rwkv"] = s


def _gla_stages(n_chunks, load_chunk, store_o, s, consts, n_heads, vd, out, key):
    wide, incl = consts
    tile = functools.partial(_tile_rows, n=n_heads)
    for c in range(n_chunks):
        qd, ki, kend, bl, v = load_chunk(c)
        q_w = tile(qd) * wide
        v_st = _stack_heads(v, n_heads, vd)
        att = jnp.where(incl, _mm(q_w, tile(ki), "nt", M_ATT), 0.0)
        kv = _mm(tile(kend) * wide, v_st, "tn", M_ATT)
        yield
        o_st = _mm(att, v_st, "nn", M_ATT) + _mm(q_w, s, "nn", M_ATT)
        dcol = jnp.exp(jnp.broadcast_to(bl, (vd, bl.shape[1])).T)
        s = s * dcol + kv
        yield
        store_o(c, _unstack_heads(o_st, n_heads, qd.shape[0]))
    out[key] = s


def _mlstm_stages(n_chunks, load_chunk, store_o, state, consts, out):
    wide, same, incl, ones_bd = consts
    c, nrow, m = state
    tile = functools.partial(_tile_rows, n=D_H)
    for ch in range(n_chunks):
        q, k, v, li, a, a_last = load_chunk(ch)
        L = q.shape[0]
        hl = D_H * L
        to_row = lambda col: jnp.broadcast_to(col, (hl, LANE)).T[:hl]
        q_st = _stack_heads(q, D_H, D_HD)
        k_st = _stack_heads(k, D_H, D_HD)
        v_st = _stack_heads(v, D_H, D_HD)
        qk = _mm(q_st, k_st, "nt", M_ATT)
        dlog = jnp.where(incl, a - to_row(a) + to_row(li), -jnp.inf)
        m_intra = jnp.max(dlog, -1, keepdims=True)
        e_end = a_last - a + li
        m_loc = jnp.max(jnp.where(same, to_row(e_end), -jnp.inf), -1, keepdims=True)
        yield
        s_intra = jnp.exp(dlog - m_intra) * qk
        num_intra = _mm(s_intra, v_st, "nn", M_ATT)
        den_intra = jnp.sum(s_intra, -1, keepdims=True)
        w_end = jnp.exp(e_end - m_loc)
        c_loc = _mm(tile(v) * wide * w_end, k_st, "tn", M_ATT)
        n_loc = _mm(ones_bd, w_end * k_st, "nn", M_ATT)
        yield
        g = a + m
        m_t = jnp.maximum(g, m_intra)
        f_inter = jnp.exp(g - m_t)
        f_intra = jnp.exp(m_intra - m_t)
        qc = _mm(q_st, c, "nt", M_ATT)
        num_inter = jnp.concatenate(
            [qc[h * L:(h + 1) * L, h * D_HD:(h + 1) * D_HD] for h in range(D_H)], axis=0)
        den_inter = jnp.sum(q_st * nrow, -1, keepdims=True)
        num = num_intra * f_intra + num_inter * f_inter
        den = f_intra * den_intra + f_inter * den_inter
        h_st = num / jnp.maximum(jnp.abs(den), jnp.exp(-m_t))
        m_new = jnp.maximum(a_last + m, m_loc)
        fo = jnp.exp(a_last + m - m_new)
        fl = jnp.exp(m_loc - m_new)
        per_value = lambda col: jnp.concatenate(
            [jnp.broadcast_to(col[h * L:h * L + 1], (D_HD, 1)) for h in range(D_H)], axis=0)
        c = per_value(fo) * c + per_value(fl) * c_loc
        nrow = fo * nrow + fl * n_loc
        m = m_new
        yield
        store_o(ch, _unstack_heads(h_st, D_H, L))
    out["mlstm"] = (c, nrow, m)


(_E_W0, _E_A0, _E_KK, _E_KA, _E_RK, _E_LNW, _E_LNB, _E_BNORM, _E_LB0, _E_LB1) = range(10)


def _even_kernel(p_ref, shift0_ref, srw0_ref, shg0_ref, mu_ref, w2_ref, a2_ref, par_ref,
                 o_ref, shift_ref, srw_ref, shg_ref,
                 prev_sc, srw_sc, shg_sc, kkt_sc, rt_sc, kh_sc, bh_sc, ke_sc, be_sc, v_sc, epos_sc, bonus_sc, oa_sc,
                 qd_sc, ki_sc, kend_sc, bl_sc, ob_sc, xm_sc, m2_sc, o2_sc, arb_sc, kv_sc, bew_sc, *, nb, tb, la, lb):
    j = pl.program_id(1)
    nj = pl.num_programs(1)
    seqs = range(nb)
    cols = lambda lo, hi: jnp.concatenate([p_ref[b, :, lo:hi] for b in seqs], axis=0)

    @pl.when(j == 0)
    def _():
        for b in seqs:
            prev_sc[b] = jnp.broadcast_to(shift0_ref[b], prev_sc.shape[1:])
            srw_sc[b] = jnp.concatenate([srw0_ref[b, h] for h in range(A_H)], axis=1)
            shg_sc[b] = shg0_ref[b]

    par = par_ref[...]
    row = lambda i: par[i:i + 1]

    pa = cols(0, A_SHIFT_W)
    rolled = pltpu.roll(pa, 1, axis=0)
    first = _iota(pa.shape, 0) % tb == 0
    before = jnp.concatenate([jnp.broadcast_to(prev_sc[b, 0:1], (tb, A_SHIFT_W)) for b in seqs], axis=0)
    shifted = jnp.where(first, before, rolled)
    for b in seqs:
        prev_sc[b] = jnp.broadcast_to(pa[(b + 1) * tb - 1:(b + 1) * tb], prev_sc.shape[1:])
    xm = pa + (shifted - pa) * mu_ref[...]
    r = xm[:, :MIX]
    k = xm[:, MIX:2 * MIX]
    v = xm[:, 2 * MIX:3 * MIX]
    lora_in = xm[:, 3 * MIX:]
    lw = _sigmoid(row(_E_W0) + _dot(jnp.tanh(lora_in).astype(BF16), w2_ref[...])) * (-math.exp(-0.5))
    a = _sigmoid(row(_E_A0) + _dot(lora_in.astype(BF16), a2_ref[...]))
    kk = k * row(_E_KK)
    kk = kk * lax.rsqrt(jnp.maximum(_seg_sum(kk * kk, A_HD), 1e-12))
    k = k * (1.0 + (a - 1.0) * row(_E_KA))
    bv = kk * a
    cum_a, ones_a = _chunk_mats(nb * tb, la)
    c = _mm(cum_a, lw, "nn", M_EXACT_R)
    cl = _mm(ones_a, lw, "nn", M_EXACT_R)
    e_pos = jnp.exp(c)
    e_neg = jnp.exp(-c)
    e_end = jnp.exp(cl - c)
    kkt_sc[...] = kk * jnp.exp(c - lw)
    rt_sc[...] = r * e_pos
    kh_sc[...] = k * e_neg
    bh_sc[...] = bv * e_neg
    ke_sc[...] = k * e_end
    be_sc[...] = bv * e_end
    v_sc[...] = v
    epos_sc[...] = e_pos
    bonus_sc[...] = _seg_sum(r * k * row(_E_RK), A_HD) * v

    wide, _, incl, strict = _head_masks(A_H, la, A_HD)
    hl = A_H * la
    ri, ci_ = _iota((hl, hl), 0), _iota((hl, hl), 1)
    eye = (ri == ci_).astype(F32)
    halves = []
    m = 1
    while m < la:
        halves.append(((ri // (2 * m)) == (ci_ // (2 * m))) & ((ri // m) != (ci_ // m)))
        m *= 2
    consts_a = (wide, incl, strict, eye, halves)

    e0 = row(_E_LB0)
    e1 = row(_E_LB1)
    emax = jnp.maximum(e0, e1)
    e0 = jnp.exp(e0 - emax)
    lower = e0 / (e0 + jnp.exp(e1 - emax))
    base = A_SHIFT_W + MIX
    g = lower + (1.0 - lower) * jax.nn.sigmoid(cols(base + MIX, base + 2 * MIX))
    logg = jnp.log(g)
    cum_b, ones_b = _chunk_mats(nb * tb, lb)
    gb = _mm(cum_b, logg, "nn", M_EXACT_R)
    gl = _mm(ones_b, logg, "nn", M_EXACT_R)
    qd_sc[...] = _silu(cols(base, base + MIX)) * (B_HD ** -0.5) * jnp.exp(gb)
    ki_sc[...] = (1.0 - g) * jnp.exp(-gb)
    kend_sc[...] = (1.0 - g) * jnp.exp(gl - gb)
    bl_sc[...] = gl

    wide_b, _, incl_b, _ = _head_masks(B_H, lb, B_HD)
    consts_b = (wide_b, incl_b)

    per_seq = tb // la
    group = math.gcd(nb * per_seq, 4)
    ngroups = nb * per_seq // group
    run = min(group, per_seq)
    chains = group // run
    assert per_seq % run == 0 and (run * la) % lb == 0
    hper = (run * la) // lb

    def factor_stages(gi, slot, out):
        sls = [pl.ds(pl.multiple_of((gi * group + u) * la, la), la) for u in range(group)]
        yield from _rwkv_factor_stages(
            [(kkt_sc[sl, :], rt_sc[sl, :], kh_sc[sl, :], bh_sc[sl, :], ke_sc[sl, :], be_sc[sl, :], v_sc[sl, :])
             for sl in sls], consts_a, out)
        for u, (xm, m2n, o2, a_rb, kv, be_w) in enumerate(out["factors"]):
            xm_sc[slot + u] = xm
            m2_sc[slot + u] = m2n
            o2_sc[slot + u] = o2
            arb_sc[slot + u] = a_rb
            kv_sc[slot + u] = kv
            bew_sc[slot + u] = be_w

    def chunk_body(gi, carry):
        slot = (gi % 2) * group
        stages, finish = [], []
        for q in range(chains):
            first = gi * group + q * run
            seq = first // per_seq
            row0 = first * la
            out = {}

            def load_factors(u, q=q, first=first):
                i = slot + q * run + u
                return (xm_sc[i], m2_sc[i], o2_sc[i], arb_sc[i], kv_sc[i], bew_sc[i],
                        epos_sc[pl.ds((first + u) * la + la - 1, 1), :])

            def store_oa(u, o, row0=row0):
                oa_sc[pl.ds(pl.multiple_of(row0 + u * la, la), la), :] = o

            def load_b(u, row0=row0, seq=seq):
                r = row0 + u * lb
                sl = pl.ds(pl.multiple_of(r, lb), lb)
                return (qd_sc[sl, :], ki_sc[sl, :], kend_sc[sl, :], bl_sc[pl.ds(r, 1), :],
                        p_ref[seq, pl.ds(pl.multiple_of(r - seq * tb, lb), lb), base + 2 * MIX:base + 3 * MIX])

            def store_ob(u, o, row0=row0):
                ob_sc[pl.ds(pl.multiple_of(row0 + u * lb, lb), lb), :] = o

            stages.append(_rwkv_state_stages(run, load_factors, store_oa, srw_sc[seq], out))
            stages.append(_gla_stages(hper, load_b, store_ob, shg_sc[seq], consts_b, B_H, B_HD, out, "hgrn"))
            finish.append((seq, out))
        if ngroups > 1:
            stages.insert(1, factor_stages(jnp.minimum(gi + 1, ngroups - 1), group - slot, {}))
        _interleave(*stages)
        for seq, out in finish:
            srw_sc[seq] = out["rwkv"]
            shg_sc[seq] = out["hgrn"]
        return carry

    _interleave(factor_stages(0, 0, {}))
    lax.fori_loop(0, ngroups, chunk_body, 0)

    o = oa_sc[...]
    mean = _seg_sum(o, A_HD) * (1.0 / A_HD)
    cen = o - mean
    var = _seg_sum(cen * cen, A_HD) * (1.0 / A_HD)
    oa = cen * lax.rsqrt(var + A_GN_EPS) * row(_E_LNW) + row(_E_LNB)
    oa = oa + bonus_sc[...]
    oa = oa * _silu(cols(A_SHIFT_W, A_SHIFT_W + MIX))
    ob = _head_rms(ob_sc[...], row(_E_BNORM), B_HD) * _silu(cols(base + 3 * MIX, base + 4 * MIX))
    o = jnp.concatenate([oa, ob], axis=1).astype(o_ref.dtype)
    for b in seqs:
        o_ref[b] = o[b * tb:(b + 1) * tb]

    @pl.when(j == nj - 1)
    def _():
        for b in seqs:
            shift_ref[b] = prev_sc[b, 0:1]
            s = srw_sc[b]
            for h in range(A_H):
                srw_ref[b, h] = s[:, h * A_HD:(h + 1) * A_HD]
            shg_ref[b] = shg_sc[b]


def _even_mixer(p, shift0, srw0, shg0, mu, w2p, a2p, par, nb, tb, la, lb):
    bsz, t, _ = p.shape
    kern = functools.partial(_even_kernel, nb=nb, tb=tb, la=la, lb=lb)
    nch, hl = 2 * math.gcd(nb * tb // la, 4), A_H * la
    bmap3 = lambda b, j: (b, 0, 0)
    bmap4 = lambda b, j: (b, 0, 0, 0)
    cmap = lambda b, j: (0, 0)
    blk = lambda: pltpu.VMEM((nb * tb, MIX), F32)
    return pl.pallas_call(
        kern, grid=(bsz // nb, t // tb),
        in_specs=[pl.BlockSpec((nb, tb, IN_EVEN), lambda b, j: (b, j, 0)),
                  pl.BlockSpec((nb, 1, A_SHIFT_W), bmap3),
                  pl.BlockSpec((nb, A_H, A_HD, A_HD), bmap4),
                  pl.BlockSpec((nb, B_H * B_HD, B_HD), bmap3),
                  pl.BlockSpec((1, A_SHIFT_W), cmap),
                  pl.BlockSpec((2 * A_LORA, MIX), cmap),
                  pl.BlockSpec((2 * A_LORA, MIX), cmap),
                  pl.BlockSpec((16, MIX), cmap)],
        out_specs=[pl.BlockSpec((nb, tb, D_MODEL), lambda b, j: (b, j, 0)),
                   pl.BlockSpec((nb, 1, A_SHIFT_W), bmap3),
                   pl.BlockSpec((nb, A_H, A_HD, A_HD), bmap4),
                   pl.BlockSpec((nb, B_H * B_HD, B_HD), bmap3)],
        out_shape=[jax.ShapeDtypeStruct((bsz, t, D_MODEL), F32),
                   jax.ShapeDtypeStruct((bsz, 1, A_SHIFT_W), F32),
                   jax.ShapeDtypeStruct((bsz, A_H, A_HD, A_HD), F32),
                   jax.ShapeDtypeStruct((bsz, B_H * B_HD, B_HD), F32)],
        scratch_shapes=[pltpu.VMEM((nb, 8, A_SHIFT_W), F32), pltpu.VMEM((nb, A_HD, MIX), F32),
                        pltpu.VMEM((nb, B_H * B_HD, B_HD), F32)] + [blk() for _ in range(15)] + [
                            pltpu.VMEM((nch, 2 * hl, MIX), BF16), pltpu.VMEM((nch, hl, A_HD), F32),
                            pltpu.VMEM((nch, hl, A_HD), F32), pltpu.VMEM((nch, hl, hl), BF16),
                            pltpu.VMEM((nch, A_HD, MIX), F32), pltpu.VMEM((nch, hl, MIX), BF16)],
        compiler_params=pltpu.CompilerParams(dimension_semantics=("arbitrary", "arbitrary"),
                                             vmem_limit_bytes=VMEM_LIMIT),
        name="mixer_rwkv_hgrn")(p, shift0, srw0, shg0, mu, w2p, a2p, par)


_O_CQ = 0
_O_CK = _O_CQ + C_KW
_O_CV = _O_CK + C_KW
_O_GC = _O_CV + MIX
_O_DQK = _O_GC + MIX
_O_DV = _O_DQK + 2 * MIX
_O_GD = _O_DV + MIX
_O_SLAB = _O_GD + MIX
(_P_CNORM, _P_DNORM, _P_G2B) = range(3)


def _odd_kernel(p_ref, sgl0_ref, conv0_ref, mc0_ref, mn0_ref, mm0_ref, g2_ref, cw_ref, cb_ref, sb_ref, par_ref,
                o_ref, sgl_ref, conv_ref, mc_ref, mn_ref, mm_ref,
                prev_sc, sgl_sc, mc_sc, mn_sc, mm_sc, cq_sc, ck_sc, ce_sc, cl_sc, oc_sc,
                dq_sc, dk_sc, li_sc, fa_sc, fl_sc, od_sc, *, nb, tb, lc):
    j = pl.program_id(1)
    nj = pl.num_programs(1)
    hl = D_H * lc
    seqs = range(nb)
    cols = lambda lo, hi: jnp.concatenate([p_ref[b, :, lo:hi] for b in seqs], axis=0)
    head_rows = lambda x: jnp.concatenate(
        [jnp.broadcast_to(x[h:h + 1], (lc, x.shape[1])) for h in range(D_H)], axis=0)

    @pl.when(j == 0)
    def _():
        for b in seqs:
            prev_sc[b] = conv0_ref[b]
            sgl_sc[b] = sgl0_ref[b]
            mc_sc[b] = mc0_ref[b]
            mn_sc[b] = head_rows(mn0_ref[b])
            mm_sc[b] = head_rows(mm0_ref[b])[:, :1]

    par = par_ref[...]
    row = lambda i: par[i:i + 1]

    cum_c, ones_c = _chunk_mats(nb * tb, lc)
    slab = cols(_O_SLAB, _O_SLAB + SLAB)
    pre = _dot(slab.astype(BF16), g2_ref[...]) + row(_P_G2B)[:, :C_KW]
    logg = _log_sigmoid(pre) * (1.0 / C_GATE_NORM)
    gb = _mm(cum_c, logg, "nn", M_EXACT_R)
    gl = _mm(ones_c, logg, "nn", M_EXACT_R)
    ck = cols(_O_CK, _O_CK + C_KW)
    cq_sc[...] = cols(_O_CQ, _O_CQ + C_KW) * (C_KD ** -0.5) * jnp.exp(gb)
    ck_sc[...] = ck * jnp.exp(-gb)
    ce_sc[...] = ck * jnp.exp(gl - gb)
    cl_sc[...] = gl

    wide_c, _, incl_c, _ = _head_masks(C_H, lc, C_KD)
    consts_c = (wide_c, incl_c)

    x = cols(_O_DQK, _O_DQK + 2 * MIX)
    cw = cw_ref[...]
    conv = cb_ref[...] + x * cw[D_CONV - 1:D_CONV]
    head_row = _iota((8, 2 * MIX), 0)
    for s in range(1, D_CONV):
        xr = pltpu.roll(x, s, axis=0)
        pieces = []
        for b in seqs:
            pieces.append(jnp.where(head_row < s, pltpu.roll(prev_sc[b], s, axis=0), xr[b * tb:b * tb + 8]))
            if tb > 8:
                pieces.append(xr[b * tb + 8:(b + 1) * tb])
        xs = pieces[0] if len(pieces) == 1 else jnp.concatenate(pieces, axis=0)
        conv = conv + xs * cw[D_CONV - 1 - s:D_CONV - s]
    for b in seqs:
        prev_sc[b] = x[(b + 1) * tb - 8:(b + 1) * tb]
    conv = _silu(conv)
    dq_sc[...] = conv[:, :MIX]
    dk_sc[...] = conv[:, MIX:] * (D_HD ** -0.5)
    gates = slab + sb_ref[...]
    lf = pltpu.roll(_log_sigmoid(gates), SLAB - (SLAB_F - SLAB_I), axis=1)
    li_sc[...] = gates
    fa_sc[...] = _mm(cum_c, lf, "nn", M_EXACT_R)
    fl_sc[...] = _mm(ones_c, lf, "nn", M_EXACT_R)
    lane = _iota((hl, SLAB), 1)
    head = _iota((hl, SLAB), 0) // lc
    pick = lambda z: jnp.sum(jnp.where(lane == head + SLAB_I, _tile_rows(z, D_H), 0.0), -1, keepdims=True)

    wide_d, same_d, incl_d, _ = _head_masks(D_H, lc, D_HD)
    consts_d = (wide_d, same_d, incl_d, same_d.astype(BF16))

    per_seq = tb // lc
    per_trip = math.gcd(nb * per_seq, 2 if per_seq > 1 else 4)
    run = min(per_trip, per_seq)
    chains = per_trip // run
    assert per_seq % run == 0

    def chunk_body(gi, carry):
        stages, finish = [], []
        for q in range(chains):
            first = gi * per_trip + q * run
            seq = first // per_seq
            out = {}
            rows = lambda u, first=first: pl.ds(pl.multiple_of((first + u) * lc, lc), lc)
            in_seq = lambda u, first=first, seq=seq: pl.ds(pl.multiple_of((first + u) * lc - seq * tb, lc), lc)

            def load_c(u, first=first, seq=seq, rows=rows, in_seq=in_seq):
                sl = rows(u)
                return (cq_sc[sl, :], ck_sc[sl, :], ce_sc[sl, :], cl_sc[pl.ds((first + u) * lc, 1), :],
                        p_ref[seq, in_seq(u), _O_CV:_O_CV + MIX])

            def store_c(u, o, rows=rows):
                oc_sc[rows(u), :] = o

            def load_d(u, seq=seq, rows=rows, in_seq=in_seq):
                sl = rows(u)
                return (dq_sc[sl, :], dk_sc[sl, :], p_ref[seq, in_seq(u), _O_DV:_O_DV + MIX],
                        pick(li_sc[sl, :]), pick(fa_sc[sl, :]), pick(fl_sc[sl, :]))

            def store_d(u, o, rows=rows):
                od_sc[rows(u), :] = o

            stages.append(_mlstm_stages(run, load_d, store_d, (mc_sc[seq], mn_sc[seq], mm_sc[seq]), consts_d, out))
            stages.append(_gla_stages(run, load_c, store_c, sgl_sc[seq], consts_c, C_H, C_VD, out, "gla"))
            finish.append((seq, out))
        _interleave(*stages)
        for seq, out in finish:
            sgl_sc[seq] = out["gla"]
            mc_sc[seq], mn_sc[seq], mm_sc[seq] = out["mlstm"]
        return carry

    lax.fori_loop(0, nb * per_seq // per_trip, chunk_body, 0)

    oc = _head_rms(oc_sc[...], row(_P_CNORM), C_VD) * _silu(cols(_O_GC, _O_GC + MIX))
    od = _head_rms(od_sc[...], row(_P_DNORM), D_HD) * _silu(cols(_O_GD, _O_GD + MIX))
    o = jnp.concatenate([oc, od], axis=1).astype(o_ref.dtype)
    for b in seqs:
        o_ref[b] = o[b * tb:(b + 1) * tb]

    @pl.when(j == nj - 1)
    def _():
        for b in seqs:
            sgl_ref[b] = sgl_sc[b]
            conv_ref[b] = prev_sc[b]
            mc_ref[b] = mc_sc[b]
            mn_ref[b] = jnp.concatenate([mn_sc[b, h * lc:h * lc + 1] for h in range(D_H)], axis=0)
            mm_ref[b] = jnp.concatenate(
                [jnp.broadcast_to(mm_sc[b, h * lc:h * lc + 1], (1, LANE)) for h in range(D_H)], axis=0)


def _odd_mixer(p, sgl0, conv0, mc0, mn0, mm0, g2p, cw, cb, sb, par, nb, tb, lc):
    bsz, t, _ = p.shape
    hl = D_H * lc
    kern = functools.partial(_odd_kernel, nb=nb, tb=tb, lc=lc)
    bmap3 = lambda b, j: (b, 0, 0)
    cmap = lambda b, j: (0, 0)
    blk = lambda w: pltpu.VMEM((nb * tb, w), F32)
    return pl.pallas_call(
        kern, grid=(bsz // nb, t // tb),
        in_specs=[pl.BlockSpec((nb, tb, IN_ODD_PAD), lambda b, j: (b, j, 0)),
                  pl.BlockSpec((nb, C_KW, C_VD), bmap3),
                  pl.BlockSpec((nb, 8, 2 * MIX), bmap3),
                  pl.BlockSpec((nb, MIX, D_HD), bmap3),
                  pl.BlockSpec((nb, D_H, D_HD), bmap3),
                  pl.BlockSpec((nb, D_H, LANE), bmap3),
                  pl.BlockSpec((SLAB, C_KW), cmap),
                  pl.BlockSpec((D_CONV, 2 * MIX), cmap),
                  pl.BlockSpec((1, 2 * MIX), cmap),
                  pl.BlockSpec((1, SLAB), cmap),
                  pl.BlockSpec((8, MIX), cmap)],
        out_specs=[pl.BlockSpec((nb, tb, D_MODEL), lambda b, j: (b, j, 0)),
                   pl.BlockSpec((nb, C_KW, C_VD), bmap3),
                   pl.BlockSpec((nb, 8, 2 * MIX), bmap3),
                   pl.BlockSpec((nb, MIX, D_HD), bmap3),
                   pl.BlockSpec((nb, D_H, D_HD), bmap3),
                   pl.BlockSpec((nb, D_H, LANE), bmap3)],
        out_shape=[jax.ShapeDtypeStruct((bsz, t, D_MODEL), F32),
                   jax.ShapeDtypeStruct((bsz, C_KW, C_VD), F32),
                   jax.ShapeDtypeStruct((bsz, 8, 2 * MIX), F32),
                   jax.ShapeDtypeStruct((bsz, MIX, D_HD), F32),
                   jax.ShapeDtypeStruct((bsz, D_H, D_HD), F32),
                   jax.ShapeDtypeStruct((bsz, D_H, LANE), F32)],
        scratch_shapes=[pltpu.VMEM((nb, 8, 2 * MIX), F32), pltpu.VMEM((nb, C_KW, C_VD), F32),
                        pltpu.VMEM((nb, MIX, D_HD), F32), pltpu.VMEM((nb, hl, D_HD), F32),
                        pltpu.VMEM((nb, hl, 1), F32),
                        blk(C_KW), blk(C_KW), blk(C_KW), blk(C_KW), blk(MIX),
                        blk(MIX), blk(MIX), blk(SLAB), blk(SLAB), blk(SLAB), blk(MIX)],
        compiler_params=pltpu.CompilerParams(dimension_semantics=("arbitrary", "arbitrary"),
                                             vmem_limit_bytes=VMEM_LIMIT),
        name="mixer_gla_mlstm")(p, sgl0, conv0, mc0, mn0, mm0, g2p, cw, cb, sb, par)


MIXER_ROWS = 256
SHORT_SEQS = 8


def _mixer_blocking(bsz, t):
    tb = math.gcd(t, MIXER_ROWS)
    nb = math.gcd(bsz, SHORT_SEQS) if tb == t and t * SHORT_SEQS <= MIXER_ROWS else 1
    return nb, tb


def _run(x, shift, s_rwkv, s_hgrn, s_gla, conv, mc, mn, mm, wts):
    bsz, t, d = x.shape
    nb, tb = _mixer_blocking(bsz, t)
    la = math.gcd(t, 16)
    lbc = math.gcd(t, 32)
    x2 = x.reshape(bsz * t, d)
    p0 = _inproj(x2, wts["g0"], wts["w_in0"])
    o0, shift_n, srw_n, shg_n = _even_mixer(
        p0.reshape(bsz, t, IN_EVEN), shift[0][:, None], s_rwkv[0], s_hgrn[0].reshape(bsz, B_H * B_HD, B_HD),
        wts["mu"], wts["w2p"], wts["a2p"], wts["par_e"], nb, tb, la, lbc)
    x1, p1 = _mid(o0.reshape(bsz * t, d), x2, wts["w_out0"], wts["g1"], wts["w_in1"])
    conv8 = jnp.pad(conv[0], ((0, 0), (8 - (D_CONV - 1), 0), (0, 0)))
    mm_l = jnp.broadcast_to(mm[0][:, :, None], (bsz, D_H, LANE))
    o1, sgl_n, conv_n, mc_n, mn_n, mm_n = _odd_mixer(
        p1.reshape(bsz, t, IN_ODD_PAD), s_gla[0].reshape(bsz, C_KW, C_VD), conv8,
        mc[0].reshape(bsz, MIX, D_HD), mn[0], mm_l,
        wts["g2p"], wts["cw"], wts["cb"], wts["sb"], wts["par_o"], nb, tb, lbc)
    y = _final(o1.reshape(bsz * t, d), x1, wts["w_out1"], wts["gf"])
    return (y.reshape(bsz, t, d), shift_n.reshape(1, bsz, A_SHIFT_W), srw_n[None],
            shg_n.reshape(1, bsz, B_H, B_HD, B_HD), sgl_n.reshape(1, bsz, C_H, C_KD, C_VD),
            conv_n[None, :, 8 - (D_CONV - 1):], mc_n.reshape(1, bsz, D_H, D_HD, D_HD), mn_n[None],
            mm_n[None, :, :, 0])


def _odd_column_order():
    sizes = (C_KW, C_KW, MIX, C_LORA, MIX, 2 * MIX, MIX, D_H, D_H, MIX)
    off = np.cumsum((0,) + sizes)
    cq, ck, cv, cg, gc, dqk, dv, di, df, gd = (np.arange(off[i], off[i + 1]) for i in range(len(sizes)))
    return np.concatenate([cq, ck, cv, gc, dqk, dv, gd, cg, di, df])


def kernel(x_prompt, x_sample, state_shift_a, state_rwkv, state_hgrn, state_gla, state_conv_d, state_mlstm_c,
           state_mlstm_n, state_mlstm_m, norm_g, w_in_even, w_out_even, a_mu, a_w0, a_w2, a_a0, a_a2, a_kk,
           a_ka, a_rk, a_ln_w, a_ln_b, b_lb, b_norm, w_in_odd, w_out_odd, c_g2, c_g2b, c_norm, d_conv_w,
           d_conv_b, d_ib, d_fb, d_norm, final_norm):
    assert w_in_even.shape[0] == 1 and w_in_odd.shape[0] == 1 and b_lb.shape[0] == 2
    zpad = lambda a, rows_before, rows_total: jnp.pad(a, ((rows_before, rows_total - rows_before - a.shape[0]), (0, 0)))
    par_e = jnp.concatenate([a_w0, a_a0, a_kk, a_ka, a_rk, a_ln_w, a_ln_b, b_norm, b_lb[0:1], b_lb[1:2]], axis=0)
    g2b = jnp.pad(c_g2b, ((0, 0), (0, MIX - C_KW)))
    par_o = jnp.concatenate([c_norm, d_norm, g2b], axis=0)
    w_in1 = jnp.pad(w_in_odd[0][:, _odd_column_order()], ((0, 0), (0, SLAB - C_LORA - 2 * D_H)))
    sb = jnp.pad(jnp.concatenate([d_ib, d_fb], axis=1), ((0, 0), (SLAB_I, SLAB - SLAB_I - 2 * D_H)))
    wts = {
        "g0": norm_g[0:1], "g1": norm_g[1:2], "gf": final_norm[None],
        "w_in0": w_in_even[0].astype(BF16), "w_out0": w_out_even[0].astype(BF16),
        "w_in1": w_in1.astype(BF16), "w_out1": w_out_odd[0].astype(BF16),
        "mu": a_mu, "w2p": zpad(a_w2[0], 0, 2 * A_LORA).astype(BF16),
        "a2p": zpad(a_a2[0], A_LORA, 2 * A_LORA).astype(BF16),
        "par_e": zpad(par_e, 0, 16), "par_o": zpad(par_o, 0, 8),
        "g2p": zpad(c_g2[0], 0, SLAB).astype(BF16), "cw": d_conv_w[0], "cb": d_conv_b, "sb": sb,
    }
    bp = x_prompt.shape[0]
    z = lambda *s: jnp.zeros(s, x_prompt.dtype)
    prompt = _run(x_prompt, z(1, bp, A_SHIFT_W), z(1, bp, A_H, A_HD, A_HD), z(1, bp, B_H, B_HD, B_HD),
                  z(1, bp, C_H, C_KD, C_VD), z(1, bp, D_CONV - 1, 2 * MIX), z(1, bp, D_H, D_HD, D_HD),
                  z(1, bp, D_H, D_HD), z(1, bp, D_H), wts)
    sample = _run(x_sample, state_shift_a, state_rwkv, state_hgrn, state_gla, state_conv_d, state_mlstm_c,
                  state_mlstm_n, state_mlstm_m, wts)
    return (prompt[0], sample[0]) + prompt[1:] + sample[1:]
```

```python
import functools
import math

import jax
import jax.numpy as jnp
import numpy as np
from jax import lax
from jax.experimental import pallas as pl
from jax.experimental.pallas import tpu as pltpu

F32 = jnp.float32
BF16 = jnp.bfloat16

D_MODEL = 1024
MIX = D_MODEL // 2
NORM_EPS = 1e-5
A_HD = 64
A_H = MIX // A_HD
A_LORA = 64
A_GN_EPS = 64e-5
A_SHIFT_W = 3 * MIX + 2 * A_LORA
B_HD = 128
B_H = MIX // B_HD
IN_EVEN = A_SHIFT_W + 5 * MIX
C_H = 4
C_VD = MIX // C_H
C_KD = C_VD // 2
C_KW = C_H * C_KD
C_LORA = 16
C_GATE_NORM = 16.0
D_H = 4
D_HD = MIX // D_H
D_CONV = 4
LANE = 128
SLAB = LANE
SLAB_I = C_LORA
SLAB_F = C_LORA + D_H
IN_ODD_PAD = 2 * C_KW + 2 * MIX + 2 * MIX + 2 * MIX + SLAB
VMEM_LIMIT = 48 * 1024 * 1024


_DIMS = {"nn": ((1,), (0,)), "nt": ((1,), (1,)), "tn": ((0,), (0,))}
M_EXACT_R = "rx"
M_EXACT_L = "lx"
M_G = "bf"
M_TINV = "bf"
M_ST = "bf"
M_ATT = "bf"


def _split2(x):
    hi = x.astype(BF16)
    return hi, (x - hi.astype(F32)).astype(BF16)


def _mm(a, b, form, mode):
    dn = (_DIMS[form], ((), ()))
    d = lambda x, y: lax.dot_general(x, y, dn, preferred_element_type=F32)
    if mode == "bf":
        return d(a.astype(BF16), b.astype(BF16))
    if mode == "x3":
        ah, al = _split2(a)
        bh, bl = _split2(b)
        return d(ah, bh) + (d(ah, bl) + d(al, bh))
    if mode == "lx":
        bb = b.astype(BF16)
        h, l = _split2(a)
        return d(h, bb) + d(l, bb)
    assert mode == "rx"
    ab = a.astype(BF16)
    h, l = _split2(b)
    return d(ab, h) + d(ab, l)


def _dot(a, b):
    return jnp.dot(a, b, preferred_element_type=F32)


def _iota(shape, dim):
    return lax.broadcasted_iota(jnp.int32, shape, dim)


def _sigmoid(x):
    return 0.5 + 0.5 * jnp.tanh(0.5 * x)


def _log_sigmoid(x):
    return jnp.minimum(x, 0.0) - jnp.log(1.0 + jnp.exp(-jnp.abs(x)))


def _silu(x):
    return x * _sigmoid(x)


def _rms(x, g):
    return x * lax.rsqrt(jnp.mean(x * x, -1, keepdims=True) + NORM_EPS) * g


def _unroll(trips):
    return 2 if trips % 2 == 0 else 1


def _tile_rows(x, n):
    return jnp.concatenate([x] * n, axis=0)


def _stack_heads(x, n_heads, width):
    return jnp.concatenate([x[:, h * width:(h + 1) * width] for h in range(n_heads)], axis=0)


def _unstack_heads(x, n_heads, rows):
    return jnp.concatenate([x[h * rows:(h + 1) * rows] for h in range(n_heads)], axis=1)


def _head_masks(n_heads, rows, width):
    hl = n_heads * rows
    rh = _iota((hl, n_heads * width), 0) // rows
    lh = _iota((hl, n_heads * width), 1) // width
    wide = (rh == lh).astype(F32)
    ri, ci = _iota((hl, hl), 0), _iota((hl, hl), 1)
    same = (ri // rows) == (ci // rows)
    incl = same & (ci <= ri)
    strict = same & (ci < ri)
    return wide, same, incl, strict


def _chunk_mats(n, chunk):
    ri, ci = _iota((n, n), 0), _iota((n, n), 1)
    same = (ri // chunk) == (ci // chunk)
    return (same & (ci <= ri)).astype(BF16), same.astype(BF16)


def _seg_sum(x, seg):
    bd = ((_iota((LANE, LANE), 0) // seg) == (_iota((LANE, LANE), 1) // seg)).astype(BF16)
    parts = [_mm(x[:, j:j + LANE], bd, "nn", M_EXACT_L) for j in range(0, x.shape[1], LANE)]
    return jnp.concatenate(parts, axis=1)


def _head_rms(x, g, width):
    parts = []
    for j in range(0, x.shape[1], width):
        xs = x[:, j:j + width]
        parts.append(xs * lax.rsqrt(jnp.mean(xs * xs, -1, keepdims=True) + NORM_EPS))
    return jnp.concatenate(parts, axis=1) * g


def _inproj_kernel(x_ref, g_ref, w_ref, p_ref):
    h = _rms(x_ref[...], g_ref[...])
    p_ref[...] = _dot(h.astype(BF16), w_ref[...])


def _mid_kernel(o_ref, x_ref, wo_ref, g_ref, wi_ref, x1_ref, p_ref):
    x1 = x_ref[...] + _dot(o_ref[...], wo_ref[...])
    x1_ref[...] = x1
    p_ref[...] = _dot(_rms(x1, g_ref[...]).astype(BF16), wi_ref[...])


def _final_kernel(o_ref, x_ref, wo_ref, g_ref, y_ref):
    x2 = x_ref[...] + _dot(o_ref[...], wo_ref[...])
    y_ref[...] = _rms(x2, g_ref[...])


def _row_tile(m):
    return math.gcd(m, 512)


def _full(shape):
    return pl.BlockSpec(shape, lambda i: (0,) * len(shape))


def _dense_params():
    return pltpu.CompilerParams(dimension_semantics=("arbitrary",), vmem_limit_bytes=VMEM_LIMIT)


def _inproj(x, g, w):
    m, d = x.shape
    n = w.shape[1]
    tm = _row_tile(m)
    return pl.pallas_call(
        _inproj_kernel, grid=(m // tm,),
        in_specs=[pl.BlockSpec((tm, d), lambda i: (i, 0)), _full((1, d)), _full((d, n))],
        out_specs=pl.BlockSpec((tm, n), lambda i: (i, 0)),
        out_shape=jax.ShapeDtypeStruct((m, n), F32),
        compiler_params=_dense_params(), name="inproj")(x, g, w)


def _mid(o, x, wo, g, wi):
    m, d = x.shape
    n = wi.shape[1]
    tm = _row_tile(m)
    return pl.pallas_call(
        _mid_kernel, grid=(m // tm,),
        in_specs=[pl.BlockSpec((tm, d), lambda i: (i, 0)), pl.BlockSpec((tm, d), lambda i: (i, 0)),
                  _full((d, d)), _full((1, d)), _full((d, n))],
        out_specs=[pl.BlockSpec((tm, d), lambda i: (i, 0)), pl.BlockSpec((tm, n), lambda i: (i, 0))],
        out_shape=[jax.ShapeDtypeStruct((m, d), F32), jax.ShapeDtypeStruct((m, n), F32)],
        compiler_params=_dense_params(), name="outproj_inproj")(o, x, wo, g, wi)


def _final(o, x, wo, g):
    m, d = x.shape
    tm = _row_tile(m)
    return pl.pallas_call(
        _final_kernel, grid=(m // tm,),
        in_specs=[pl.BlockSpec((tm, d), lambda i: (i, 0)), pl.BlockSpec((tm, d), lambda i: (i, 0)),
                  _full((d, d)), _full((1, d))],
        out_specs=pl.BlockSpec((tm, d), lambda i: (i, 0)),
        out_shape=jax.ShapeDtypeStruct((m, d), F32),
        compiler_params=_dense_params(), name="outproj_final")(o, x, wo, g)


def _interleave(*stage_generators):
    live = list(stage_generators)
    while live:
        for gen in list(live):
            try:
                next(gen)
            except StopIteration:
                live.remove(gen)


def _rwkv_factor_stages(chunks, consts, out):
    wide, incl, strict, eye, halves = consts
    L = chunks[0][0].shape[0]
    hl = A_H * L
    n = range(len(chunks))
    tile = functools.partial(_tile_rows, n=A_H)
    kk_w = [tile(c[0]) * wide for c in chunks]
    r_w = [tile(c[1]) * wide for c in chunks]
    stack = lambda x: _stack_heads(x, A_H, A_HD)
    g = [_mm(jnp.concatenate([stack(c[0]), stack(c[1])], axis=0),
             jnp.concatenate([stack(c[2]), stack(c[3])], axis=0), "nt", M_G) for c in chunks]
    yield
    a_kk = [jnp.where(strict, x[:hl, :hl], 0.0) for x in g]
    a_kb = [jnp.where(strict, x[:hl, hl:], 0.0) for x in g]
    a_rk = [jnp.where(incl, x[hl:, :hl], 0.0) for x in g]
    a_rb = [jnp.where(incl, x[hl:, hl:], 0.0) for x in g]
    t = [eye - jnp.where(halves[0], x, 0.0) for x in a_kb]
    v_st = [_stack_heads(c[6], A_H, A_HD) for c in chunks]
    av = [_mm(a_kk[i], v_st[i], "nn", M_ST) for i in n]
    o2 = [_mm(a_rk[i], v_st[i], "nn", M_ST) for i in n]
    for half in halves[1:]:
        ta = [_mm(t[i], jnp.where(half, a_kb[i], 0.0), "nn", M_TINV) for i in n]
        yield
        t = [t[i] - _mm(ta[i], t[i], "nn", M_TINV) for i in n]
        yield
    kv = [_mm(v_st[i], tile(chunks[i][4]) * wide, "tn", M_ST) for i in n]
    m1 = [_mm(t[i], kk_w[i], "nn", M_ST) for i in n]
    m2 = [_mm(t[i], av[i], "nn", M_ST) for i in n]
    yield
    out["factors"] = [
        (jnp.concatenate([-m1[i], r_w[i]], axis=0).astype(BF16), -m2[i], o2[i], a_rb[i].astype(BF16), kv[i],
         (tile(chunks[i][5]) * wide).astype(BF16)) for i in n]


def _rwkv_state_stages(n_chunks, load_factors, store_o, s, out):
    for c in range(n_chunks):
        xm, m2n, o2, a_rb, kv, be_w, ecl = load_factors(c)
        hl = xm.shape[0] // 2
        xs = _mm(xm, s, "nt", M_ST)
        yield
        u = xs[:hl] + m2n
        s = s * ecl + kv + _mm(u, be_w, "tn", M_ST)
        o_st = xs[hl:] + o2 + _mm(a_rb, u, "nn", M_ST)
        yield
        store_o(c, _unstack_heads(o_st, A_H, hl // A_H))
    out["rwkv"] = s


def _gla_stages(n_chunks, load_chunk, store_o, s, consts, n_heads, vd, out, key):
    wide, incl = consts
    tile = functools.partial(_tile_rows, n=n_heads)
    for c in range(n_chunks):
        qd, ki, kend, bl, v = load_chunk(c)
        q_w = tile(qd) * wide
        v_st = _stack_heads(v, n_heads, vd)
        att = jnp.where(incl, _mm(q_w, tile(ki), "nt", M_ATT), 0.0)
        kv = _mm(v_st, tile(kend) * wide, "tn", M_ATT)
        yield
        o_st = _mm(att, v_st, "nn", M_ATT) + _mm(q_w, s, "nt", M_ATT)
        s = s * jnp.exp(bl) + kv
        yield
        store_o(c, _unstack_heads(o_st, n_heads, qd.shape[0]))
    out[key] = s


def _mlstm_stages(n_chunks, load_chunk, store_o, state, consts, out):
    wide, same, incl, ones_bd = consts
    c, nrow, m = state
    tile = functools.partial(_tile_rows, n=D_H)
    for ch in range(n_chunks):
        q, k, v, li, a, a_last = load_chunk(ch)
        L = q.shape[0]
        hl = D_H * L
        to_row = lambda col: jnp.broadcast_to(col, (hl, LANE)).T[:hl]
        q_st = _stack_heads(q, D_H, D_HD)
        k_st = _stack_heads(k, D_H, D_HD)
        v_st = _stack_heads(v, D_H, D_HD)
        qk = _mm(q_st, k_st, "nt", M_ATT)
        dlog = jnp.where(incl, a - to_row(a) + to_row(li), -jnp.inf)
        m_intra = jnp.max(dlog, -1, keepdims=True)
        e_end = a_last - a + li
        m_loc = jnp.max(jnp.where(same, to_row(e_end), -jnp.inf), -1, keepdims=True)
        yield
        s_intra = jnp.exp(dlog - m_intra) * qk
        num_intra = _mm(s_intra, v_st, "nn", M_ATT)
        den_intra = jnp.sum(s_intra, -1, keepdims=True)
        w_end = jnp.exp(e_end - m_loc)
        c_loc = _mm(k_st, tile(v) * wide * w_end, "tn", M_ATT)
        n_loc = _mm(ones_bd, w_end * k_st, "nn", M_ATT)
        yield
        g = a + m
        m_t = jnp.maximum(g, m_intra)
        f_inter = jnp.exp(g - m_t)
        f_intra = jnp.exp(m_intra - m_t)
        qc = _mm(q_st, c, "nn", M_ATT)
        num_inter = jnp.concatenate(
            [qc[h * L:(h + 1) * L, h * D_HD:(h + 1) * D_HD] for h in range(D_H)], axis=0)
        den_inter = jnp.sum(q_st * nrow, -1, keepdims=True)
        num = num_intra * f_intra + num_inter * f_inter
        den = f_intra * den_intra + f_inter * den_inter
        h_st = num / jnp.maximum(jnp.abs(den), jnp.exp(-m_t))
        m_new = jnp.maximum(a_last + m, m_loc)
        fo = jnp.exp(a_last + m - m_new)
        fl = jnp.exp(m_loc - m_new)
        per_value = lambda col: jnp.concatenate(
            [jnp.broadcast_to(col[h * L:h * L + 1], (1, D_HD)) for h in range(D_H)], axis=1)
        c = per_value(fo) * c + per_value(fl) * c_loc
        nrow = fo * nrow + fl * n_loc
        m = m_new
        yield
        store_o(ch, _unstack_heads(h_st, D_H, L))
    out["mlstm"] = (c, nrow, m)


(_E_W0, _E_A0, _E_KK, _E_KA, _E_RK, _E_LNW, _E_LNB, _E_BNORM, _E_LB0, _E_LB1) = range(10)


def _even_kernel(p_ref, shift0_ref, srw0_ref, shg0_ref, mu_ref, w2_ref, a2_ref, par_ref,
                 o_ref, shift_ref, srw_ref, shg_ref,
                 prev_sc, srw_sc, shg_sc, kkt_sc, rt_sc, kh_sc, bh_sc, ke_sc, be_sc, v_sc, epos_sc, bonus_sc, oa_sc,
                 qd_sc, ki_sc, kend_sc, bl_sc, ob_sc, xm_sc, m2_sc, o2_sc, arb_sc, kv_sc, bew_sc, *, nb, tb, la, lb):
    j = pl.program_id(1)
    nj = pl.num_programs(1)
    seqs = range(nb)
    cols = lambda lo, hi: jnp.concatenate([p_ref[b, :, lo:hi] for b in seqs], axis=0)

    @pl.when(j == 0)
    def _():
        for b in seqs:
            prev_sc[b] = jnp.broadcast_to(shift0_ref[b], prev_sc.shape[1:])
            srw_sc[b] = jnp.concatenate([srw0_ref[b, h] for h in range(A_H)], axis=1)
            shg_sc[b] = shg0_ref[b].T

    par = par_ref[...]
    row = lambda i: par[i:i + 1]

    pa = cols(0, A_SHIFT_W)
    rolled = pltpu.roll(pa, 1, axis=0)
    first = _iota(pa.shape, 0) % tb == 0
    before = jnp.concatenate([jnp.broadcast_to(prev_sc[b, 0:1], (tb, A_SHIFT_W)) for b in seqs], axis=0)
    shifted = jnp.where(first, before, rolled)
    for b in seqs:
        prev_sc[b] = jnp.broadcast_to(pa[(b + 1) * tb - 1:(b + 1) * tb], prev_sc.shape[1:])
    xm = pa + (shifted - pa) * mu_ref[...]
    r = xm[:, :MIX]
    k = xm[:, MIX:2 * MIX]
    v = xm[:, 2 * MIX:3 * MIX]
    lora_in = xm[:, 3 * MIX:]
    lw = _sigmoid(row(_E_W0) + _dot(jnp.tanh(lora_in).astype(BF16), w2_ref[...])) * (-math.exp(-0.5))
    a = _sigmoid(row(_E_A0) + _dot(lora_in.astype(BF16), a2_ref[...]))
    kk = k * row(_E_KK)
    kk = kk * lax.rsqrt(jnp.maximum(_seg_sum(kk * kk, A_HD), 1e-12))
    k = k * (1.0 + (a - 1.0) * row(_E_KA))
    bv = kk * a
    cum_a, ones_a = _chunk_mats(nb * tb, la)
    c = _mm(cum_a, lw, "nn", M_EXACT_R)
    cl = _mm(ones_a, lw, "nn", M_EXACT_R)
    e_pos = jnp.exp(c)
    e_neg = jnp.exp(-c)
    e_end = jnp.exp(cl - c)
    kkt_sc[...] = kk * jnp.exp(c - lw)
    rt_sc[...] = r * e_pos
    kh_sc[...] = k * e_neg
    bh_sc[...] = bv * e_neg
    ke_sc[...] = k * e_end
    be_sc[...] = bv * e_end
    v_sc[...] = v
    epos_sc[...] = e_pos
    bonus_sc[...] = _seg_sum(r * k * row(_E_RK), A_HD) * v

    wide, _, incl, strict = _head_masks(A_H, la, A_HD)
    hl = A_H * la
    ri, ci_ = _iota((hl, hl), 0), _iota((hl, hl), 1)
    eye = (ri == ci_).astype(F32)
    halves = []
    m = 1
    while m < la:
        halves.append(((ri // (2 * m)) == (ci_ // (2 * m))) & ((ri // m) != (ci_ // m)))
        m *= 2
    consts_a = (wide, incl, strict, eye, halves)

    e0 = row(_E_LB0)
    e1 = row(_E_LB1)
    emax = jnp.maximum(e0, e1)
    e0 = jnp.exp(e0 - emax)
    lower = e0 / (e0 + jnp.exp(e1 - emax))
    base = A_SHIFT_W + MIX
    g = lower + (1.0 - lower) * jax.nn.sigmoid(cols(base + MIX, base + 2 * MIX))
    logg = jnp.log(g)
    cum_b, ones_b = _chunk_mats(nb * tb, lb)
    gb = _mm(cum_b, logg, "nn", M_EXACT_R)
    gl = _mm(ones_b, logg, "nn", M_EXACT_R)
    qd_sc[...] = _silu(cols(base, base + MIX)) * (B_HD ** -0.5) * jnp.exp(gb)
    ki_sc[...] = (1.0 - g) * jnp.exp(-gb)
    kend_sc[...] = (1.0 - g) * jnp.exp(gl - gb)
    bl_sc[...] = gl

    wide_b, _, incl_b, _ = _head_masks(B_H, lb, B_HD)
    consts_b = (wide_b, incl_b)

    per_seq = tb // la
    group = math.gcd(nb * per_seq, 4)
    ngroups = nb * per_seq // group
    run = min(group, per_seq)
    chains = group // run
    assert per_seq % run == 0 and (run * la) % lb == 0
    hper = (run * la) // lb

    def factor_stages(gi, slot, out):
        sls = [pl.ds(pl.multiple_of((gi * group + u) * la, la), la) for u in range(group)]
        yield from _rwkv_factor_stages(
            [(kkt_sc[sl, :], rt_sc[sl, :], kh_sc[sl, :], bh_sc[sl, :], ke_sc[sl, :], be_sc[sl, :], v_sc[sl, :])
             for sl in sls], consts_a, out)
        for u, (xm, m2n, o2, a_rb, kv, be_w) in enumerate(out["factors"]):
            xm_sc[slot + u] = xm
            m2_sc[slot + u] = m2n
            o2_sc[slot + u] = o2
            arb_sc[slot + u] = a_rb
            kv_sc[slot + u] = kv
            bew_sc[slot + u] = be_w

    def chunk_body(gi, carry):
        slot = (gi % 2) * group
        stages, finish = [], []
        for q in range(chains):
            first = gi * group + q * run
            seq = first // per_seq
            row0 = first * la
            out = {}

            def load_factors(u, q=q, first=first):
                i = slot + q * run + u
                return (xm_sc[i], m2_sc[i], o2_sc[i], arb_sc[i], kv_sc[i], bew_sc[i],
                        epos_sc[pl.ds((first + u) * la + la - 1, 1), :])

            def store_oa(u, o, row0=row0):
                oa_sc[pl.ds(pl.multiple_of(row0 + u * la, la), la), :] = o

            def load_b(u, row0=row0, seq=seq):
                r = row0 + u * lb
                sl = pl.ds(pl.multiple_of(r, lb), lb)
                return (qd_sc[sl, :], ki_sc[sl, :], kend_sc[sl, :], bl_sc[pl.ds(r, 1), :],
                        p_ref[seq, pl.ds(pl.multiple_of(r - seq * tb, lb), lb), base + 2 * MIX:base + 3 * MIX])

            def store_ob(u, o, row0=row0):
                ob_sc[pl.ds(pl.multiple_of(row0 + u * lb, lb), lb), :] = o

            stages.append(_rwkv_state_stages(run, load_factors, store_oa, srw_sc[seq], out))
            stages.append(_gla_stages(hper, load_b, store_ob, shg_sc[seq], consts_b, B_H, B_HD, out, "hgrn"))
            finish.append((seq, out))
        if ngroups > 1:
            stages.insert(1, factor_stages(jnp.minimum(gi + 1, ngroups - 1), group - slot, {}))
        _interleave(*stages)
        for seq, out in finish:
            srw_sc[seq] = out["rwkv"]
            shg_sc[seq] = out["hgrn"]
        return carry

    _interleave(factor_stages(0, 0, {}))
    lax.fori_loop(0, ngroups, chunk_body, 0)

    o = oa_sc[...]
    mean = _seg_sum(o, A_HD) * (1.0 / A_HD)
    cen = o - mean
    var = _seg_sum(cen * cen, A_HD) * (1.0 / A_HD)
    oa = cen * lax.rsqrt(var + A_GN_EPS) * row(_E_LNW) + row(_E_LNB)
    oa = oa + bonus_sc[...]
    oa = oa * _silu(cols(A_SHIFT_W, A_SHIFT_W + MIX))
    ob = _head_rms(ob_sc[...], row(_E_BNORM), B_HD) * _silu(cols(base + 3 * MIX, base + 4 * MIX))
    o_ref[...] = jnp.concatenate([oa, ob], axis=1).astype(o_ref.dtype)

    @pl.when(j == nj - 1)
    def _():
        for b in seqs:
            shift_ref[b] = prev_sc[b, 0:1]
            s = srw_sc[b]
            for h in range(A_H):
                srw_ref[b, h] = s[:, h * A_HD:(h + 1) * A_HD]
            shg_ref[b] = shg_sc[b].T


def _even_mixer(p, shift0, srw0, shg0, mu, w2p, a2p, par, nb, tb, la, lb):
    bsz, t, _ = p.shape
    kern = functools.partial(_even_kernel, nb=nb, tb=tb, la=la, lb=lb)
    nch, hl = 2 * math.gcd(nb * tb // la, 4), A_H * la
    bmap3 = lambda b, j: (b, 0, 0)
    bmap4 = lambda b, j: (b, 0, 0, 0)
    cmap = lambda b, j: (0, 0)
    blk = lambda: pltpu.VMEM((nb * tb, MIX), F32)
    return pl.pallas_call(
        kern, grid=(bsz // nb, t // tb),
        in_specs=[pl.BlockSpec((nb, tb, IN_EVEN), lambda b, j: (b, j, 0)),
                  pl.BlockSpec((nb, 1, A_SHIFT_W), bmap3),
                  pl.BlockSpec((nb, A_H, A_HD, A_HD), bmap4),
                  pl.BlockSpec((nb, B_H * B_HD, B_HD), bmap3),
                  pl.BlockSpec((1, A_SHIFT_W), cmap),
                  pl.BlockSpec((2 * A_LORA, MIX), cmap),
                  pl.BlockSpec((2 * A_LORA, MIX), cmap),
                  pl.BlockSpec((16, MIX), cmap)],
        out_specs=[pl.BlockSpec((nb * tb, D_MODEL), lambda b, j: (b * (t // tb) + j, 0)),
                   pl.BlockSpec((nb, 1, A_SHIFT_W), bmap3),
                   pl.BlockSpec((nb, A_H, A_HD, A_HD), bmap4),
                   pl.BlockSpec((nb, B_H * B_HD, B_HD), bmap3)],
        out_shape=[jax.ShapeDtypeStruct((bsz * t, D_MODEL), BF16),
                   jax.ShapeDtypeStruct((bsz, 1, A_SHIFT_W), F32),
                   jax.ShapeDtypeStruct((bsz, A_H, A_HD, A_HD), F32),
                   jax.ShapeDtypeStruct((bsz, B_H * B_HD, B_HD), F32)],
        scratch_shapes=[pltpu.VMEM((nb, 8, A_SHIFT_W), F32), pltpu.VMEM((nb, A_HD, MIX), F32),
                        pltpu.VMEM((nb, B_HD, B_H * B_HD), F32)] + [blk() for _ in range(15)] + [
                            pltpu.VMEM((nch, 2 * hl, MIX), BF16), pltpu.VMEM((nch, hl, A_HD), F32),
                            pltpu.VMEM((nch, hl, A_HD), F32), pltpu.VMEM((nch, hl, hl), BF16),
                            pltpu.VMEM((nch, A_HD, MIX), F32), pltpu.VMEM((nch, hl, MIX), BF16)],
        compiler_params=pltpu.CompilerParams(dimension_semantics=("arbitrary", "arbitrary"),
                                             vmem_limit_bytes=VMEM_LIMIT),
        name="mixer_rwkv_hgrn")(p, shift0, srw0, shg0, mu, w2p, a2p, par)


_O_CQ = 0
_O_CK = _O_CQ + C_KW
_O_CV = _O_CK + C_KW
_O_GC = _O_CV + MIX
_O_DQK = _O_GC + MIX
_O_DV = _O_DQK + 2 * MIX
_O_GD = _O_DV + MIX
_O_SLAB = _O_GD + MIX
(_P_CNORM, _P_DNORM, _P_G2B) = range(3)


def _odd_kernel(p_ref, sgl0_ref, conv0_ref, mc0_ref, mn0_ref, mm0_ref, g2_ref, cw_ref, cb_ref, sb_ref, par_ref,
                o_ref, sgl_ref, conv_ref, mc_ref, mn_ref, mm_ref,
                prev_sc, sgl_sc, mc_sc, mn_sc, mm_sc, cq_sc, ck_sc, ce_sc, cl_sc, oc_sc,
                dq_sc, dk_sc, li_sc, fa_sc, fl_sc, od_sc, *, nb, tb, lc):
    j = pl.program_id(1)
    nj = pl.num_programs(1)
    hl = D_H * lc
    seqs = range(nb)
    cols = lambda lo, hi: jnp.concatenate([p_ref[b, :, lo:hi] for b in seqs], axis=0)
    head_rows = lambda x: jnp.concatenate(
        [jnp.broadcast_to(x[h:h + 1], (lc, x.shape[1])) for h in range(D_H)], axis=0)

    @pl.when(j == 0)
    def _():
        for b in seqs:
            prev_sc[b] = conv0_ref[b]
            sgl_sc[b] = sgl0_ref[b].T
            mc_sc[b] = mc0_ref[b].T
            mn_sc[b] = head_rows(mn0_ref[b])
            mm_sc[b] = head_rows(mm0_ref[b])[:, :1]

    par = par_ref[...]
    row = lambda i: par[i:i + 1]

    cum_c, ones_c = _chunk_mats(nb * tb, lc)
    slab = cols(_O_SLAB, _O_SLAB + SLAB)
    pre = _dot(slab.astype(BF16), g2_ref[...]) + row(_P_G2B)[:, :C_KW]
    logg = _log_sigmoid(pre) * (1.0 / C_GATE_NORM)
    gb = _mm(cum_c, logg, "nn", M_EXACT_R)
    gl = _mm(ones_c, logg, "nn", M_EXACT_R)
    ck = cols(_O_CK, _O_CK + C_KW)
    cq_sc[...] = cols(_O_CQ, _O_CQ + C_KW) * (C_KD ** -0.5) * jnp.exp(gb)
    ck_sc[...] = ck * jnp.exp(-gb)
    ce_sc[...] = ck * jnp.exp(gl - gb)
    cl_sc[...] = gl

    wide_c, _, incl_c, _ = _head_masks(C_H, lc, C_KD)
    consts_c = (wide_c, incl_c)

    x = cols(_O_DQK, _O_DQK + 2 * MIX)
    cw = cw_ref[...]
    conv = cb_ref[...] + x * cw[D_CONV - 1:D_CONV]
    head_row = _iota((8, 2 * MIX), 0)
    for s in range(1, D_CONV):
        xr = pltpu.roll(x, s, axis=0)
        pieces = []
        for b in seqs:
            pieces.append(jnp.where(head_row < s, pltpu.roll(prev_sc[b], s, axis=0), xr[b * tb:b * tb + 8]))
            if tb > 8:
                pieces.append(xr[b * tb + 8:(b + 1) * tb])
        xs = pieces[0] if len(pieces) == 1 else jnp.concatenate(pieces, axis=0)
        conv = conv + xs * cw[D_CONV - 1 - s:D_CONV - s]
    for b in seqs:
        prev_sc[b] = x[(b + 1) * tb - 8:(b + 1) * tb]
    conv = _silu(conv)
    dq_sc[...] = conv[:, :MIX]
    dk_sc[...] = conv[:, MIX:] * (D_HD ** -0.5)
    gates = slab + sb_ref[...]
    lf = pltpu.roll(_log_sigmoid(gates), SLAB - (SLAB_F - SLAB_I), axis=1)
    li_sc[...] = gates
    fa_sc[...] = _mm(cum_c, lf, "nn", M_EXACT_R)
    fl_sc[...] = _mm(ones_c, lf, "nn", M_EXACT_R)
    lane = _iota((hl, SLAB), 1)
    head = _iota((hl, SLAB), 0) // lc
    pick = lambda z: jnp.sum(jnp.where(lane == head + SLAB_I, _tile_rows(z, D_H), 0.0), -1, keepdims=True)

    wide_d, same_d, incl_d, _ = _head_masks(D_H, lc, D_HD)
    consts_d = (wide_d, same_d, incl_d, same_d.astype(BF16))

    per_seq = tb // lc
    per_trip = math.gcd(nb * per_seq, 2 if per_seq > 1 else 4)
    run = min(per_trip, per_seq)
    chains = per_trip // run
    assert per_seq % run == 0

    def chunk_body(gi, carry):
        stages, finish = [], []
        for q in range(chains):
            first = gi * per_trip + q * run
            seq = first // per_seq
            out = {}
            rows = lambda u, first=first: pl.ds(pl.multiple_of((first + u) * lc, lc), lc)
            in_seq = lambda u, first=first, seq=seq: pl.ds(pl.multiple_of((first + u) * lc - seq * tb, lc), lc)

            def load_c(u, first=first, seq=seq, rows=rows, in_seq=in_seq):
                sl = rows(u)
                return (cq_sc[sl, :], ck_sc[sl, :], ce_sc[sl, :], cl_sc[pl.ds((first + u) * lc, 1), :],
                        p_ref[seq, in_seq(u), _O_CV:_O_CV + MIX])

            def store_c(u, o, rows=rows):
                oc_sc[rows(u), :] = o

            def load_d(u, seq=seq, rows=rows, in_seq=in_seq):
                sl = rows(u)
                return (dq_sc[sl, :], dk_sc[sl, :], p_ref[seq, in_seq(u), _O_DV:_O_DV + MIX],
                        pick(li_sc[sl, :]), pick(fa_sc[sl, :]), pick(fl_sc[sl, :]))

            def store_d(u, o, rows=rows):
                od_sc[rows(u), :] = o

            stages.append(_mlstm_stages(run, load_d, store_d, (mc_sc[seq], mn_sc[seq], mm_sc[seq]), consts_d, out))
            stages.append(_gla_stages(run, load_c, store_c, sgl_sc[seq], consts_c, C_H, C_VD, out, "gla"))
            finish.append((seq, out))
        _interleave(*stages)
        for seq, out in finish:
            sgl_sc[seq] = out["gla"]
            mc_sc[seq], mn_sc[seq], mm_sc[seq] = out["mlstm"]
        return carry

    lax.fori_loop(0, nb * per_seq // per_trip, chunk_body, 0)

    oc = _head_rms(oc_sc[...], row(_P_CNORM), C_VD) * _silu(cols(_O_GC, _O_GC + MIX))
    od = _head_rms(od_sc[...], row(_P_DNORM), D_HD) * _silu(cols(_O_GD, _O_GD + MIX))
    o_ref[...] = jnp.concatenate([oc, od], axis=1).astype(o_ref.dtype)

    @pl.when(j == nj - 1)
    def _():
        for b in seqs:
            sgl_ref[b] = sgl_sc[b].T
            conv_ref[b] = prev_sc[b]
            mc_ref[b] = mc_sc[b].T
            mn_ref[b] = jnp.concatenate([mn_sc[b, h * lc:h * lc + 1] for h in range(D_H)], axis=0)
            mm_ref[b] = jnp.concatenate(
                [jnp.broadcast_to(mm_sc[b, h * lc:h * lc + 1], (1, LANE)) for h in range(D_H)], axis=0)


def _odd_mixer(p, sgl0, conv0, mc0, mn0, mm0, g2p, cw, cb, sb, par, nb, tb, lc):
    bsz, t, _ = p.shape
    hl = D_H * lc
    kern = functools.partial(_odd_kernel, nb=nb, tb=tb, lc=lc)
    bmap3 = lambda b, j: (b, 0, 0)
    cmap = lambda b, j: (0, 0)
    blk = lambda w: pltpu.VMEM((nb * tb, w), F32)
    return pl.pallas_call(
        kern, grid=(bsz // nb, t // tb),
        in_specs=[pl.BlockSpec((nb, tb, IN_ODD_PAD), lambda b, j: (b, j, 0)),
                  pl.BlockSpec((nb, C_KW, C_VD), bmap3),
                  pl.BlockSpec((nb, 8, 2 * MIX), bmap3),
                  pl.BlockSpec((nb, MIX, D_HD), bmap3),
                  pl.BlockSpec((nb, D_H, D_HD), bmap3),
                  pl.BlockSpec((nb, D_H, LANE), bmap3),
                  pl.BlockSpec((SLAB, C_KW), cmap),
                  pl.BlockSpec((D_CONV, 2 * MIX), cmap),
                  pl.BlockSpec((1, 2 * MIX), cmap),
                  pl.BlockSpec((1, SLAB), cmap),
                  pl.BlockSpec((8, MIX), cmap)],
        out_specs=[pl.BlockSpec((nb * tb, D_MODEL), lambda b, j: (b * (t // tb) + j, 0)),
                   pl.BlockSpec((nb, C_KW, C_VD), bmap3),
                   pl.BlockSpec((nb, 8, 2 * MIX), bmap3),
                   pl.BlockSpec((nb, MIX, D_HD), bmap3),
                   pl.BlockSpec((nb, D_H, D_HD), bmap3),
                   pl.BlockSpec((nb, D_H, LANE), bmap3)],
        out_shape=[jax.ShapeDtypeStruct((bsz * t, D_MODEL), BF16),
                   jax.ShapeDtypeStruct((bsz, C_KW, C_VD), F32),
                   jax.ShapeDtypeStruct((bsz, 8, 2 * MIX), F32),
                   jax.ShapeDtypeStruct((bsz, MIX, D_HD), F32),
                   jax.ShapeDtypeStruct((bsz, D_H, D_HD), F32),
                   jax.ShapeDtypeStruct((bsz, D_H, LANE), F32)],
        scratch_shapes=[pltpu.VMEM((nb, 8, 2 * MIX), F32), pltpu.VMEM((nb, C_VD, C_KW), F32),
                        pltpu.VMEM((nb, D_HD, MIX), F32), pltpu.VMEM((nb, hl, D_HD), F32),
                        pltpu.VMEM((nb, hl, 1), F32),
                        blk(C_KW), blk(C_KW), blk(C_KW), blk(C_KW), blk(MIX),
                        blk(MIX), blk(MIX), blk(SLAB), blk(SLAB), blk(SLAB), blk(MIX)],
        compiler_params=pltpu.CompilerParams(dimension_semantics=("arbitrary", "arbitrary"),
                                             vmem_limit_bytes=VMEM_LIMIT),
        name="mixer_gla_mlstm")(p, sgl0, conv0, mc0, mn0, mm0, g2p, cw, cb, sb, par)


MIXER_ROWS = 256
SHORT_SEQS = 8


def _mixer_blocking(bsz, t):
    tb = math.gcd(t, MIXER_ROWS)
    nb = math.gcd(bsz, SHORT_SEQS) if tb == t and t * SHORT_SEQS <= MIXER_ROWS else 1
    return nb, tb


def _run(x, shift, s_rwkv, s_hgrn, s_gla, conv, mc, mn, mm, wts):
    bsz, t, d = x.shape
    nb, tb = _mixer_blocking(bsz, t)
    la = math.gcd(t, 16)
    lbc = math.gcd(t, 32)
    x2 = x.reshape(bsz * t, d)
    p0 = _inproj(x2, wts["g0"], wts["w_in0"])
    o0, shift_n, srw_n, shg_n = _even_mixer(
        p0.reshape(bsz, t, IN_EVEN), shift[0][:, None], s_rwkv[0], s_hgrn[0].reshape(bsz, B_H * B_HD, B_HD),
        wts["mu"], wts["w2p"], wts["a2p"], wts["par_e"], nb, tb, la, lbc)
    x1, p1 = _mid(o0, x2, wts["w_out0"], wts["g1"], wts["w_in1"])
    conv8 = jnp.pad(conv[0], ((0, 0), (8 - (D_CONV - 1), 0), (0, 0)))
    mm_l = jnp.broadcast_to(mm[0][:, :, None], (bsz, D_H, LANE))
    o1, sgl_n, conv_n, mc_n, mn_n, mm_n = _odd_mixer(
        p1.reshape(bsz, t, IN_ODD_PAD), s_gla[0].reshape(bsz, C_KW, C_VD), conv8,
        mc[0].reshape(bsz, MIX, D_HD), mn[0], mm_l,
        wts["g2p"], wts["cw"], wts["cb"], wts["sb"], wts["par_o"], nb, tb, lbc)
    y = _final(o1, x1, wts["w_out1"], wts["gf"])
    return (y.reshape(bsz, t, d), shift_n.reshape(1, bsz, A_SHIFT_W), srw_n[None],
            shg_n.reshape(1, bsz, B_H, B_HD, B_HD), sgl_n.reshape(1, bsz, C_H, C_KD, C_VD),
            conv_n[None, :, 8 - (D_CONV - 1):], mc_n.reshape(1, bsz, D_H, D_HD, D_HD), mn_n[None],
            mm_n[None, :, :, 0])


def _odd_in_projection(w):
    sizes = (C_KW, C_KW, MIX, C_LORA, MIX, 2 * MIX, MIX, D_H, D_H, MIX)
    off = np.cumsum((0,) + sizes)
    cq, ck, cv, cg, gc, dqk, dv, di, df, gd = (w[:, off[i]:off[i + 1]] for i in range(len(sizes)))
    pad = jnp.zeros((w.shape[0], SLAB - C_LORA - 2 * D_H), w.dtype)
    return jnp.concatenate([cq, ck, cv, gc, dqk, dv, gd, cg, di, df, pad], axis=1)


def kernel(x_prompt, x_sample, state_shift_a, state_rwkv, state_hgrn, state_gla, state_conv_d, state_mlstm_c,
           state_mlstm_n, state_mlstm_m, norm_g, w_in_even, w_out_even, a_mu, a_w0, a_w2, a_a0, a_a2, a_kk,
           a_ka, a_rk, a_ln_w, a_ln_b, b_lb, b_norm, w_in_odd, w_out_odd, c_g2, c_g2b, c_norm, d_conv_w,
           d_conv_b, d_ib, d_fb, d_norm, final_norm):
    assert w_in_even.shape[0] == 1 and w_in_odd.shape[0] == 1 and b_lb.shape[0] == 2
    zpad = lambda a, rows_before, rows_total: jnp.pad(a, ((rows_before, rows_total - rows_before - a.shape[0]), (0, 0)))
    par_e = jnp.concatenate([a_w0, a_a0, a_kk, a_ka, a_rk, a_ln_w, a_ln_b, b_norm, b_lb[0:1], b_lb[1:2]], axis=0)
    g2b = jnp.pad(c_g2b, ((0, 0), (0, MIX - C_KW)))
    par_o = jnp.concatenate([c_norm, d_norm, g2b], axis=0)
    w_in1 = _odd_in_projection(w_in_odd[0].astype(BF16))
    sb = jnp.pad(jnp.concatenate([d_ib, d_fb], axis=1), ((0, 0), (SLAB_I, SLAB - SLAB_I - 2 * D_H)))
    wts = {
        "g0": norm_g[0:1], "g1": norm_g[1:2], "gf": final_norm[None],
        "w_in0": w_in_even[0].astype(BF16), "w_out0": w_out_even[0].astype(BF16),
        "w_in1": w_in1, "w_out1": w_out_odd[0].astype(BF16),
        "mu": a_mu, "w2p": zpad(a_w2[0], 0, 2 * A_LORA).astype(BF16),
        "a2p": zpad(a_a2[0], A_LORA, 2 * A_LORA).astype(BF16),
        "par_e": zpad(par_e, 0, 16), "par_o": zpad(par_o, 0, 8),
        "g2p": zpad(c_g2[0], 0, SLAB).astype(BF16), "cw": d_conv_w[0], "cb": d_conv_b, "sb": sb,
    }
    bp = x_prompt.shape[0]
    z = lambda *s: jnp.zeros(s, x_prompt.dtype)
    prompt = _run(x_prompt, z(1, bp, A_SHIFT_W), z(1, bp, A_H, A_HD, A_HD), z(1, bp, B_H, B_HD, B_HD),
                  z(1, bp, C_H, C_KD, C_VD), z(1, bp, D_CONV - 1, 2 * MIX), z(1, bp, D_H, D_HD, D_HD),
                  z(1, bp, D_H, D_HD), z(1, bp, D_H), wts)
    sample = _run(x_sample, state_shift_a, state_rwkv, state_hgrn, state_gla, state_conv_d, state_mlstm_c,
                  state_mlstm_n, state_mlstm_m, wts)
    return (prompt[0], sample[0]) + prompt[1:] + sample[1:]
```

```python
import functools
import math

import jax
import jax.numpy as jnp
import numpy as np
from jax import lax
from jax.experimental import pallas as pl
from jax.experimental.pallas import tpu as pltpu

F32 = jnp.float32
BF16 = jnp.bfloat16

D_MODEL = 1024
MIX = D_MODEL // 2
NORM_EPS = 1e-5
A_HD = 64
A_H = MIX // A_HD
A_LORA = 64
A_GN_EPS = 64e-5
A_SHIFT_W = 3 * MIX + 2 * A_LORA
B_HD = 128
B_H = MIX // B_HD
IN_EVEN = A_SHIFT_W + 5 * MIX
C_H = 4
C_VD = MIX // C_H
C_KD = C_VD // 2
C_KW = C_H * C_KD
C_LORA = 16
C_GATE_NORM = 16.0
D_H = 4
D_HD = MIX // D_H
D_CONV = 4
LANE = 128
SLAB = LANE
SLAB_I = C_LORA
SLAB_F = C_LORA + D_H
IN_ODD_PAD = 2 * C_KW + 2 * MIX + 2 * MIX + 2 * MIX + SLAB
VMEM_LIMIT = 48 * 1024 * 1024


_DIMS = {"nn": ((1,), (0,)), "nt": ((1,), (1,)), "tn": ((0,), (0,))}
M_EXACT_R = "rx"
M_EXACT_L = "lx"
M_G = "bf"
M_TINV = "bf"
M_ST = "bf"
M_ATT = "bf"


def _split2(x):
    hi = x.astype(BF16)
    return hi, (x - hi.astype(F32)).astype(BF16)


def _mm(a, b, form, mode):
    dn = (_DIMS[form], ((), ()))
    d = lambda x, y: lax.dot_general(x, y, dn, preferred_element_type=F32)
    if mode == "bf":
        return d(a.astype(BF16), b.astype(BF16))
    if mode == "x3":
        ah, al = _split2(a)
        bh, bl = _split2(b)
        return d(ah, bh) + (d(ah, bl) + d(al, bh))
    if mode == "lx":
        bb = b.astype(BF16)
        h, l = _split2(a)
        return d(h, bb) + d(l, bb)
    assert mode == "rx"
    ab = a.astype(BF16)
    h, l = _split2(b)
    return d(ab, h) + d(ab, l)


def _dot(a, b):
    return jnp.dot(a, b, preferred_element_type=F32)


def _iota(shape, dim):
    return lax.broadcasted_iota(jnp.int32, shape, dim)


def _sigmoid(x):
    return 0.5 + 0.5 * jnp.tanh(0.5 * x)


def _log_sigmoid(x):
    return jnp.minimum(x, 0.0) - jnp.log(1.0 + jnp.exp(-jnp.abs(x)))


def _silu(x):
    return x * _sigmoid(x)


def _rms(x, g):
    return x * lax.rsqrt(jnp.mean(x * x, -1, keepdims=True) + NORM_EPS) * g


def _rows_at(start_and_alignment, size):
    start, alignment = start_and_alignment
    if not isinstance(start, int):
        start = pl.multiple_of(start, alignment)
    return pl.ds(start, size)


def _unroll(trips):
    return 2 if trips % 2 == 0 else 1


def _tile_rows(x, n):
    return jnp.concatenate([x] * n, axis=0)


def _stack_heads(x, n_heads, width):
    return jnp.concatenate([x[:, h * width:(h + 1) * width] for h in range(n_heads)], axis=0)


def _unstack_heads(x, n_heads, rows):
    return jnp.concatenate([x[h * rows:(h + 1) * rows] for h in range(n_heads)], axis=1)


def _head_masks(n_heads, rows, width):
    hl = n_heads * rows
    rh = _iota((hl, n_heads * width), 0) // rows
    lh = _iota((hl, n_heads * width), 1) // width
    wide = (rh == lh).astype(F32)
    ri, ci = _iota((hl, hl), 0), _iota((hl, hl), 1)
    same = (ri // rows) == (ci // rows)
    incl = same & (ci <= ri)
    strict = same & (ci < ri)
    return wide, same, incl, strict


def _chunk_mats(n, chunk):
    ri, ci = _iota((n, n), 0), _iota((n, n), 1)
    same = (ri // chunk) == (ci // chunk)
    return (same & (ci <= ri)).astype(BF16), same.astype(BF16)


def _seg_sum(x, seg):
    bd = ((_iota((LANE, LANE), 0) // seg) == (_iota((LANE, LANE), 1) // seg)).astype(BF16)
    parts = [_mm(x[:, j:j + LANE], bd, "nn", M_EXACT_L) for j in range(0, x.shape[1], LANE)]
    return jnp.concatenate(parts, axis=1)


def _head_rms(x, g, width):
    parts = []
    for j in range(0, x.shape[1], width):
        xs = x[:, j:j + width]
        parts.append(xs * lax.rsqrt(jnp.mean(xs * xs, -1, keepdims=True) + NORM_EPS))
    return jnp.concatenate(parts, axis=1) * g


def _inproj_kernel(x_ref, g_ref, w_ref, p_ref):
    h = _rms(x_ref[...], g_ref[...])
    p_ref[...] = _dot(h.astype(BF16), w_ref[...])


def _mid_kernel(o_ref, x_ref, wo_ref, g_ref, wi_ref, x1_ref, p_ref):
    x1 = x_ref[...] + _dot(o_ref[...], wo_ref[...])
    x1_ref[...] = x1
    p_ref[...] = _dot(_rms(x1, g_ref[...]).astype(BF16), wi_ref[...])


def _final_kernel(o_ref, x_ref, wo_ref, g_ref, y_ref):
    x2 = x_ref[...] + _dot(o_ref[...], wo_ref[...])
    y_ref[...] = _rms(x2, g_ref[...])


def _row_tile(m):
    return math.gcd(m, 512)


def _full(shape):
    return pl.BlockSpec(shape, lambda i: (0,) * len(shape))


def _dense_params():
    return pltpu.CompilerParams(dimension_semantics=("arbitrary",), vmem_limit_bytes=VMEM_LIMIT)


def _inproj(x, g, w):
    m, d = x.shape
    n = w.shape[1]
    tm = _row_tile(m)
    return pl.pallas_call(
        _inproj_kernel, grid=(m // tm,),
        in_specs=[pl.BlockSpec((tm, d), lambda i: (i, 0)), _full((1, d)), _full((d, n))],
        out_specs=pl.BlockSpec((tm, n), lambda i: (i, 0)),
        out_shape=jax.ShapeDtypeStruct((m, n), F32),
        compiler_params=_dense_params(), name="inproj")(x, g, w)


def _mid(o, x, wo, g, wi):
    m, d = x.shape
    n = wi.shape[1]
    tm = _row_tile(m)
    return pl.pallas_call(
        _mid_kernel, grid=(m // tm,),
        in_specs=[pl.BlockSpec((tm, d), lambda i: (i, 0)), pl.BlockSpec((tm, d), lambda i: (i, 0)),
                  _full((d, d)), _full((1, d)), _full((d, n))],
        out_specs=[pl.BlockSpec((tm, d), lambda i: (i, 0)), pl.BlockSpec((tm, n), lambda i: (i, 0))],
        out_shape=[jax.ShapeDtypeStruct((m, d), F32), jax.ShapeDtypeStruct((m, n), F32)],
        compiler_params=_dense_params(), name="outproj_inproj")(o, x, wo, g, wi)


def _final(o, x, wo, g):
    m, d = x.shape
    tm = _row_tile(m)
    return pl.pallas_call(
        _final_kernel, grid=(m // tm,),
        in_specs=[pl.BlockSpec((tm, d), lambda i: (i, 0)), pl.BlockSpec((tm, d), lambda i: (i, 0)),
                  _full((d, d)), _full((1, d))],
        out_specs=pl.BlockSpec((tm, d), lambda i: (i, 0)),
        out_shape=jax.ShapeDtypeStruct((m, d), F32),
        compiler_params=_dense_params(), name="outproj_final")(o, x, wo, g)


def _interleave(*stage_generators):
    live = list(stage_generators)
    while live:
        for gen in list(live):
            try:
                next(gen)
            except StopIteration:
                live.remove(gen)


def _rwkv_factor_stages(chunks, consts, out):
    wide, incl, strict, eye, halves = consts
    L = chunks[0][0].shape[0]
    hl = A_H * L
    n = range(len(chunks))
    tile = functools.partial(_tile_rows, n=A_H)
    kk_w = [tile(c[0]) * wide for c in chunks]
    r_w = [tile(c[1]) * wide for c in chunks]
    stack = lambda x: _stack_heads(x, A_H, A_HD)
    g = [_mm(jnp.concatenate([stack(c[0]), stack(c[1])], axis=0),
             jnp.concatenate([stack(c[2]), stack(c[3])], axis=0), "nt", M_G) for c in chunks]
    yield
    a_kk = [jnp.where(strict, x[:hl, :hl], 0.0) for x in g]
    a_kb = [jnp.where(strict, x[:hl, hl:], 0.0) for x in g]
    a_rk = [jnp.where(incl, x[hl:, :hl], 0.0) for x in g]
    a_rb = [jnp.where(incl, x[hl:, hl:], 0.0) for x in g]
    t = [eye - jnp.where(halves[0], x, 0.0) for x in a_kb]
    v_st = [_stack_heads(c[6], A_H, A_HD) for c in chunks]
    av = [_mm(a_kk[i], v_st[i], "nn", M_ST) for i in n]
    o2 = [_mm(a_rk[i], v_st[i], "nn", M_ST) for i in n]
    for half in halves[1:]:
        ta = [_mm(t[i], jnp.where(half, a_kb[i], 0.0), "nn", M_TINV) for i in n]
        yield
        t = [t[i] - _mm(ta[i], t[i], "nn", M_TINV) for i in n]
        yield
    kv = [_mm(v_st[i], tile(chunks[i][4]) * wide, "tn", M_ST) for i in n]
    m1 = [_mm(t[i], kk_w[i], "nn", M_ST) for i in n]
    m2 = [_mm(t[i], av[i], "nn", M_ST) for i in n]
    yield
    out["factors"] = [
        (jnp.concatenate([-m1[i], r_w[i]], axis=0).astype(BF16), -m2[i], o2[i], a_rb[i].astype(BF16), kv[i],
         (tile(chunks[i][5]) * wide).astype(BF16)) for i in n]


def _rwkv_state_stages(n_chunks, load_factors, store_o, s, out):
    for c in range(n_chunks):
        xm, m2n, o2, a_rb, kv, be_w, ecl = load_factors(c)
        hl = xm.shape[0] // 2
        xs = _mm(xm, s, "nt", M_ST)
        yield
        u = xs[:hl] + m2n
        s = s * ecl + kv + _mm(u, be_w, "tn", M_ST)
        o_st = xs[hl:] + o2 + _mm(a_rb, u, "nn", M_ST)
        yield
        store_o(c, _unstack_heads(o_st, A_H, hl // A_H))
    out["rwkv"] = s


def _gla_stages(n_chunks, load_chunk, store_o, s, consts, n_heads, vd, out, key):
    wide, incl = consts
    tile = functools.partial(_tile_rows, n=n_heads)
    for c in range(n_chunks):
        qd, ki, kend, bl, v = load_chunk(c)
        q_w = tile(qd) * wide
        v_st = _stack_heads(v, n_heads, vd)
        att = jnp.where(incl, _mm(q_w, tile(ki), "nt", M_ATT), 0.0)
        kv = _mm(v_st, tile(kend) * wide, "tn", M_ATT)
        yield
        o_st = _mm(att, v_st, "nn", M_ATT) + _mm(q_w, s, "nt", M_ATT)
        s = s * jnp.exp(bl) + kv
        yield
        store_o(c, _unstack_heads(o_st, n_heads, qd.shape[0]))
    out[key] = s


def _mlstm_stages(n_chunks, load_chunk, store_o, state, consts, out):
    wide, same, incl, ones_bd = consts
    c, nrow, m = state
    tile = functools.partial(_tile_rows, n=D_H)
    for ch in range(n_chunks):
        q, k, v, li, a, a_last = load_chunk(ch)
        L = q.shape[0]
        hl = D_H * L
        to_row = lambda col: jnp.broadcast_to(col, (hl, LANE)).T[:hl]
        q_st = _stack_heads(q, D_H, D_HD)
        k_st = _stack_heads(k, D_H, D_HD)
        v_st = _stack_heads(v, D_H, D_HD)
        qk = _mm(q_st, k_st, "nt", M_ATT)
        dlog = jnp.where(incl, a - to_row(a) + to_row(li), -jnp.inf)
        m_intra = jnp.max(dlog, -1, keepdims=True)
        e_end = a_last - a + li
        m_loc = jnp.max(jnp.where(same, to_row(e_end), -jnp.inf), -1, keepdims=True)
        yield
        s_intra = jnp.exp(dlog - m_intra) * qk
        num_intra = _mm(s_intra, v_st, "nn", M_ATT)
        den_intra = jnp.sum(s_intra, -1, keepdims=True)
        w_end = jnp.exp(e_end - m_loc)
        c_loc = _mm(k_st, tile(v) * wide * w_end, "tn", M_ATT)
        n_loc = _mm(ones_bd, w_end * k_st, "nn", M_ATT)
        yield
        g = a + m
        m_t = jnp.maximum(g, m_intra)
        f_inter = jnp.exp(g - m_t)
        f_intra = jnp.exp(m_intra - m_t)
        qc = _mm(q_st, c, "nn", M_ATT)
        num_inter = jnp.concatenate(
            [qc[h * L:(h + 1) * L, h * D_HD:(h + 1) * D_HD] for h in range(D_H)], axis=0)
        den_inter = jnp.sum(q_st * nrow, -1, keepdims=True)
        num = num_intra * f_intra + num_inter * f_inter
        den = f_intra * den_intra + f_inter * den_inter
        h_st = num / jnp.maximum(jnp.abs(den), jnp.exp(-m_t))
        m_new = jnp.maximum(a_last + m, m_loc)
        fo = jnp.exp(a_last + m - m_new)
        fl = jnp.exp(m_loc - m_new)
        per_value = lambda col: jnp.concatenate(
            [jnp.broadcast_to(col[h * L:h * L + 1], (1, D_HD)) for h in range(D_H)], axis=1)
        c = per_value(fo) * c + per_value(fl) * c_loc
        nrow = fo * nrow + fl * n_loc
        m = m_new
        yield
        store_o(ch, _unstack_heads(h_st, D_H, L))
    out["mlstm"] = (c, nrow, m)


(_E_W0, _E_A0, _E_KK, _E_KA, _E_RK, _E_LNW, _E_LNB, _E_BNORM, _E_LB0, _E_LB1) = range(10)


def _even_kernel(p_ref, shift0_ref, srw0_ref, shg0_ref, mu_ref, w2_ref, a2_ref, par_ref,
                 o_ref, shift_ref, srw_ref, shg_ref,
                 prev_sc, srw_sc, shg_sc, kkt_sc, rt_sc, kh_sc, bh_sc, ke_sc, be_sc, v_sc, epos_sc, bonus_sc, oa_sc,
                 qd_sc, ki_sc, kend_sc, bl_sc, ob_sc, xm_sc, m2_sc, o2_sc, arb_sc, kv_sc, bew_sc, *, nb, tb, la, lb):
    j = pl.program_id(1)
    nj = pl.num_programs(1)
    seqs = range(nb)
    cols = lambda lo, hi: jnp.concatenate([p_ref[b, :, lo:hi] for b in seqs], axis=0)

    @pl.when(j == 0)
    def _():
        for b in seqs:
            prev_sc[b] = jnp.broadcast_to(shift0_ref[b], prev_sc.shape[1:])
            srw_sc[b] = jnp.concatenate([srw0_ref[b, h] for h in range(A_H)], axis=1)
            shg_sc[b] = shg0_ref[b].T

    par = par_ref[...]
    row = lambda i: par[i:i + 1]

    pa = cols(0, A_SHIFT_W)
    rolled = pltpu.roll(pa, 1, axis=0)
    first = _iota(pa.shape, 0) % tb == 0
    before = jnp.concatenate([jnp.broadcast_to(prev_sc[b, 0:1], (tb, A_SHIFT_W)) for b in seqs], axis=0)
    shifted = jnp.where(first, before, rolled)
    for b in seqs:
        prev_sc[b] = jnp.broadcast_to(pa[(b + 1) * tb - 1:(b + 1) * tb], prev_sc.shape[1:])
    xm = pa + (shifted - pa) * mu_ref[...]
    r = xm[:, :MIX]
    k = xm[:, MIX:2 * MIX]
    v = xm[:, 2 * MIX:3 * MIX]
    lora_in = xm[:, 3 * MIX:]
    lw = _sigmoid(row(_E_W0) + _dot(jnp.tanh(lora_in).astype(BF16), w2_ref[...])) * (-math.exp(-0.5))
    a = _sigmoid(row(_E_A0) + _dot(lora_in.astype(BF16), a2_ref[...]))
    kk = k * row(_E_KK)
    kk = kk * lax.rsqrt(jnp.maximum(_seg_sum(kk * kk, A_HD), 1e-12))
    k = k * (1.0 + (a - 1.0) * row(_E_KA))
    bv = kk * a
    cum_a, ones_a = _chunk_mats(nb * tb, la)
    c = _mm(cum_a, lw, "nn", M_EXACT_R)
    cl = _mm(ones_a, lw, "nn", M_EXACT_R)
    e_pos = jnp.exp(c)
    e_neg = jnp.exp(-c)
    e_end = jnp.exp(cl - c)
    kkt_sc[...] = kk * jnp.exp(c - lw)
    rt_sc[...] = r * e_pos
    kh_sc[...] = k * e_neg
    bh_sc[...] = bv * e_neg
    ke_sc[...] = k * e_end
    be_sc[...] = bv * e_end
    v_sc[...] = v
    epos_sc[...] = e_pos
    bonus_sc[...] = _seg_sum(r * k * row(_E_RK), A_HD) * v

    wide, _, incl, strict = _head_masks(A_H, la, A_HD)
    hl = A_H * la
    ri, ci_ = _iota((hl, hl), 0), _iota((hl, hl), 1)
    eye = (ri == ci_).astype(F32)
    halves = []
    m = 1
    while m < la:
        halves.append(((ri // (2 * m)) == (ci_ // (2 * m))) & ((ri // m) != (ci_ // m)))
        m *= 2
    consts_a = (wide, incl, strict, eye, halves)

    base = A_SHIFT_W + MIX

    def hgrn_token_stages():
        e0 = row(_E_LB0)
        e1 = row(_E_LB1)
        emax = jnp.maximum(e0, e1)
        e0 = jnp.exp(e0 - emax)
        lower = e0 / (e0 + jnp.exp(e1 - emax))
        g = lower + (1.0 - lower) * jax.nn.sigmoid(cols(base + MIX, base + 2 * MIX))
        yield
        logg = jnp.log(g)
        cum_b, ones_b = _chunk_mats(nb * tb, lb)
        yield
        gb = _mm(cum_b, logg, "nn", M_EXACT_R)
        yield
        gl = _mm(ones_b, logg, "nn", M_EXACT_R)
        yield
        qd_sc[...] = _silu(cols(base, base + MIX)) * (B_HD ** -0.5) * jnp.exp(gb)
        yield
        ki_sc[...] = (1.0 - g) * jnp.exp(-gb)
        yield
        kend_sc[...] = (1.0 - g) * jnp.exp(gl - gb)
        bl_sc[...] = gl

    wide_b, _, incl_b, _ = _head_masks(B_H, lb, B_HD)
    consts_b = (wide_b, incl_b)

    per_seq = tb // la
    group = math.gcd(nb * per_seq, 4)
    ngroups = nb * per_seq // group
    run = min(group, per_seq)
    chains = group // run
    assert per_seq % run == 0 and (run * la) % lb == 0
    hper = (run * la) // lb

    def factor_stages(gi, slot, out):
        sls = [_rows_at(((gi * group + u) * la, la), la) for u in range(group)]
        yield from _rwkv_factor_stages(
            [(kkt_sc[sl, :], rt_sc[sl, :], kh_sc[sl, :], bh_sc[sl, :], ke_sc[sl, :], be_sc[sl, :], v_sc[sl, :])
             for sl in sls], consts_a, out)
        for u, (xm, m2n, o2, a_rb, kv, be_w) in enumerate(out["factors"]):
            xm_sc[slot + u] = xm
            m2_sc[slot + u] = m2n
            o2_sc[slot + u] = o2
            arb_sc[slot + u] = a_rb
            kv_sc[slot + u] = kv
            bew_sc[slot + u] = be_w

    def chunk_body(gi, carry, prefetch):
        slot = (gi % 2) * group
        stages, finish = [], []
        for q in range(chains):
            first = gi * group + q * run
            seq = first // per_seq
            row0 = first * la
            out = {}

            def load_factors(u, q=q, first=first):
                i = slot + q * run + u
                return (xm_sc[i], m2_sc[i], o2_sc[i], arb_sc[i], kv_sc[i], bew_sc[i],
                        epos_sc[pl.ds((first + u) * la + la - 1, 1), :])

            def store_oa(u, o, row0=row0):
                oa_sc[_rows_at((row0 + u * la, la), la), :] = o

            def load_b(u, row0=row0, seq=seq):
                r = row0 + u * lb
                sl = _rows_at((r, lb), lb)
                return (qd_sc[sl, :], ki_sc[sl, :], kend_sc[sl, :], bl_sc[pl.ds(r, 1), :],
                        p_ref[seq, _rows_at((r - seq * tb, lb), lb), base + 2 * MIX:base + 3 * MIX])

            def store_ob(u, o, row0=row0):
                ob_sc[_rows_at((row0 + u * lb, lb), lb), :] = o

            stages.append(_rwkv_state_stages(run, load_factors, store_oa, srw_sc[seq], out))
            stages.append(_gla_stages(hper, load_b, store_ob, shg_sc[seq], consts_b, B_H, B_HD, out, "hgrn"))
            finish.append((seq, out))
        if prefetch:
            stages.insert(1, factor_stages(gi + 1, group - slot, {}))
        _interleave(*stages)
        for seq, out in finish:
            srw_sc[seq] = out["rwkv"]
            shg_sc[seq] = out["hgrn"]
        return carry

    _interleave(factor_stages(0, 0, {}), hgrn_token_stages())
    lax.fori_loop(0, ngroups - 1, functools.partial(chunk_body, prefetch=True), 0)
    chunk_body(ngroups - 1, 0, prefetch=False)

    o = oa_sc[...]
    mean = _seg_sum(o, A_HD) * (1.0 / A_HD)
    cen = o - mean
    var = _seg_sum(cen * cen, A_HD) * (1.0 / A_HD)
    oa = cen * lax.rsqrt(var + A_GN_EPS) * row(_E_LNW) + row(_E_LNB)
    oa = oa + bonus_sc[...]
    oa = oa * _silu(cols(A_SHIFT_W, A_SHIFT_W + MIX))
    ob = _head_rms(ob_sc[...], row(_E_BNORM), B_HD) * _silu(cols(base + 3 * MIX, base + 4 * MIX))
    o_ref[...] = jnp.concatenate([oa, ob], axis=1).astype(o_ref.dtype)

    @pl.when(j == nj - 1)
    def _():
        for b in seqs:
            shift_ref[b] = prev_sc[b, 0:1]
            s = srw_sc[b]
            for h in range(A_H):
                srw_ref[b, h] = s[:, h * A_HD:(h + 1) * A_HD]
            shg_ref[b] = shg_sc[b].T


def _even_mixer(p, shift0, srw0, shg0, mu, w2p, a2p, par, nb, tb, la, lb):
    bsz, t, _ = p.shape
    kern = functools.partial(_even_kernel, nb=nb, tb=tb, la=la, lb=lb)
    nch, hl = 2 * math.gcd(nb * tb // la, 4), A_H * la
    bmap3 = lambda b, j: (b, 0, 0)
    bmap4 = lambda b, j: (b, 0, 0, 0)
    cmap = lambda b, j: (0, 0)
    blk = lambda: pltpu.VMEM((nb * tb, MIX), F32)
    return pl.pallas_call(
        kern, grid=(bsz // nb, t // tb),
        in_specs=[pl.BlockSpec((nb, tb, IN_EVEN), lambda b, j: (b, j, 0)),
                  pl.BlockSpec((nb, 1, A_SHIFT_W), bmap3),
                  pl.BlockSpec((nb, A_H, A_HD, A_HD), bmap4),
                  pl.BlockSpec((nb, B_H * B_HD, B_HD), bmap3),
                  pl.BlockSpec((1, A_SHIFT_W), cmap),
                  pl.BlockSpec((2 * A_LORA, MIX), cmap),
                  pl.BlockSpec((2 * A_LORA, MIX), cmap),
                  pl.BlockSpec((16, MIX), cmap)],
        out_specs=[pl.BlockSpec((nb * tb, D_MODEL), lambda b, j: (b * (t // tb) + j, 0)),
                   pl.BlockSpec((nb, 1, A_SHIFT_W), bmap3),
                   pl.BlockSpec((nb, A_H, A_HD, A_HD), bmap4),
                   pl.BlockSpec((nb, B_H * B_HD, B_HD), bmap3)],
        out_shape=[jax.ShapeDtypeStruct((bsz * t, D_MODEL), BF16),
                   jax.ShapeDtypeStruct((bsz, 1, A_SHIFT_W), F32),
                   jax.ShapeDtypeStruct((bsz, A_H, A_HD, A_HD), F32),
                   jax.ShapeDtypeStruct((bsz, B_H * B_HD, B_HD), F32)],
        scratch_shapes=[pltpu.VMEM((nb, 8, A_SHIFT_W), F32), pltpu.VMEM((nb, A_HD, MIX), F32),
                        pltpu.VMEM((nb, B_HD, B_H * B_HD), F32)] + [blk() for _ in range(15)] + [
                            pltpu.VMEM((nch, 2 * hl, MIX), BF16), pltpu.VMEM((nch, hl, A_HD), F32),
                            pltpu.VMEM((nch, hl, A_HD), F32), pltpu.VMEM((nch, hl, hl), BF16),
                            pltpu.VMEM((nch, A_HD, MIX), F32), pltpu.VMEM((nch, hl, MIX), BF16)],
        compiler_params=pltpu.CompilerParams(dimension_semantics=("arbitrary", "arbitrary"),
                                             vmem_limit_bytes=VMEM_LIMIT),
        name="mixer_rwkv_hgrn")(p, shift0, srw0, shg0, mu, w2p, a2p, par)


_O_CQ = 0
_O_CK = _O_CQ + C_KW
_O_CV = _O_CK + C_KW
_O_GC = _O_CV + MIX
_O_DQK = _O_GC + MIX
_O_DV = _O_DQK + 2 * MIX
_O_GD = _O_DV + MIX
_O_SLAB = _O_GD + MIX
(_P_CNORM, _P_DNORM, _P_G2B) = range(3)


def _odd_kernel(p_ref, sgl0_ref, conv0_ref, mc0_ref, mn0_ref, mm0_ref, g2_ref, cw_ref, cb_ref, sb_ref, par_ref,
                o_ref, sgl_ref, conv_ref, mc_ref, mn_ref, mm_ref,
                prev_sc, sgl_sc, mc_sc, mn_sc, mm_sc, cq_sc, ck_sc, ce_sc, cl_sc, oc_sc,
                dq_sc, dk_sc, li_sc, fa_sc, fl_sc, od_sc, *, nb, tb, lc):
    j = pl.program_id(1)
    nj = pl.num_programs(1)
    hl = D_H * lc
    seqs = range(nb)
    cols = lambda lo, hi: jnp.concatenate([p_ref[b, :, lo:hi] for b in seqs], axis=0)
    head_rows = lambda x: jnp.concatenate(
        [jnp.broadcast_to(x[h:h + 1], (lc, x.shape[1])) for h in range(D_H)], axis=0)

    @pl.when(j == 0)
    def _():
        for b in seqs:
            prev_sc[b] = conv0_ref[b]
            sgl_sc[b] = sgl0_ref[b].T
            mc_sc[b] = mc0_ref[b].T
            mn_sc[b] = head_rows(mn0_ref[b])
            mm_sc[b] = head_rows(mm0_ref[b])[:, :1]

    par = par_ref[...]
    row = lambda i: par[i:i + 1]

    cum_c, ones_c = _chunk_mats(nb * tb, lc)
    slab = cols(_O_SLAB, _O_SLAB + SLAB)
    pre = _dot(slab.astype(BF16), g2_ref[...]) + row(_P_G2B)[:, :C_KW]
    logg = _log_sigmoid(pre) * (1.0 / C_GATE_NORM)
    gb = _mm(cum_c, logg, "nn", M_EXACT_R)
    gl = _mm(ones_c, logg, "nn", M_EXACT_R)
    ck = cols(_O_CK, _O_CK + C_KW)
    cq_sc[...] = cols(_O_CQ, _O_CQ + C_KW) * (C_KD ** -0.5) * jnp.exp(gb)
    ck_sc[...] = ck * jnp.exp(-gb)
    ce_sc[...] = ck * jnp.exp(gl - gb)
    cl_sc[...] = gl

    wide_c, _, incl_c, _ = _head_masks(C_H, lc, C_KD)
    consts_c = (wide_c, incl_c)

    x = cols(_O_DQK, _O_DQK + 2 * MIX)
    cw = cw_ref[...]
    conv = cb_ref[...] + x * cw[D_CONV - 1:D_CONV]
    head_row = _iota((8, 2 * MIX), 0)
    for s in range(1, D_CONV):
        xr = pltpu.roll(x, s, axis=0)
        pieces = []
        for b in seqs:
            pieces.append(jnp.where(head_row < s, pltpu.roll(prev_sc[b], s, axis=0), xr[b * tb:b * tb + 8]))
            if tb > 8:
                pieces.append(xr[b * tb + 8:(b + 1) * tb])
        xs = pieces[0] if len(pieces) == 1 else jnp.concatenate(pieces, axis=0)
        conv = conv + xs * cw[D_CONV - 1 - s:D_CONV - s]
    for b in seqs:
        prev_sc[b] = x[(b + 1) * tb - 8:(b + 1) * tb]
    conv = _silu(conv)
    dq_sc[...] = conv[:, :MIX]
    dk_sc[...] = conv[:, MIX:] * (D_HD ** -0.5)
    gates = slab + sb_ref[...]
    lf = pltpu.roll(_log_sigmoid(gates), SLAB - (SLAB_F - SLAB_I), axis=1)
    li_sc[...] = gates
    fa_sc[...] = _mm(cum_c, lf, "nn", M_EXACT_R)
    fl_sc[...] = _mm(ones_c, lf, "nn", M_EXACT_R)
    lane = _iota((hl, SLAB), 1)
    head = _iota((hl, SLAB), 0) // lc
    pick = lambda z: jnp.sum(jnp.where(lane == head + SLAB_I, _tile_rows(z, D_H), 0.0), -1, keepdims=True)

    wide_d, same_d, incl_d, _ = _head_masks(D_H, lc, D_HD)
    consts_d = (wide_d, same_d, incl_d, same_d.astype(BF16))

    per_seq = tb // lc
    per_trip = math.gcd(nb * per_seq, 2 if per_seq > 1 else 4)
    run = min(per_trip, per_seq)
    chains = per_trip // run
    assert per_seq % run == 0

    def chunk_body(gi, carry):
        stages, finish = [], []
        for q in range(chains):
            first = gi * per_trip + q * run
            seq = first // per_seq
            out = {}
            rows = lambda u, first=first: _rows_at(((first + u) * lc, lc), lc)
            in_seq = lambda u, first=first, seq=seq: _rows_at(((first + u) * lc - seq * tb, lc), lc)

            def load_c(u, first=first, seq=seq, rows=rows, in_seq=in_seq):
                sl = rows(u)
                return (cq_sc[sl, :], ck_sc[sl, :], ce_sc[sl, :], cl_sc[pl.ds((first + u) * lc, 1), :],
                        p_ref[seq, in_seq(u), _O_CV:_O_CV + MIX])

            def store_c(u, o, rows=rows):
                oc_sc[rows(u), :] = o

            def load_d(u, seq=seq, rows=rows, in_seq=in_seq):
                sl = rows(u)
                return (dq_sc[sl, :], dk_sc[sl, :], p_ref[seq, in_seq(u), _O_DV:_O_DV + MIX],
                        pick(li_sc[sl, :]), pick(fa_sc[sl, :]), pick(fl_sc[sl, :]))

            def store_d(u, o, rows=rows):
                od_sc[rows(u), :] = o

            stages.append(_mlstm_stages(run, load_d, store_d, (mc_sc[seq], mn_sc[seq], mm_sc[seq]), consts_d, out))
            stages.append(_gla_stages(run, load_c, store_c, sgl_sc[seq], consts_c, C_H, C_VD, out, "gla"))
            finish.append((seq, out))
        _interleave(*stages)
        for seq, out in finish:
            sgl_sc[seq] = out["gla"]
            mc_sc[seq], mn_sc[seq], mm_sc[seq] = out["mlstm"]
        return carry

    lax.fori_loop(0, nb * per_seq // per_trip, chunk_body, 0)

    oc = _head_rms(oc_sc[...], row(_P_CNORM), C_VD) * _silu(cols(_O_GC, _O_GC + MIX))
    od = _head_rms(od_sc[...], row(_P_DNORM), D_HD) * _silu(cols(_O_GD, _O_GD + MIX))
    o_ref[...] = jnp.concatenate([oc, od], axis=1).astype(o_ref.dtype)

    @pl.when(j == nj - 1)
    def _():
        for b in seqs:
            sgl_ref[b] = sgl_sc[b].T
            conv_ref[b] = prev_sc[b]
            mc_ref[b] = mc_sc[b].T
            mn_ref[b] = jnp.concatenate([mn_sc[b, h * lc:h * lc + 1] for h in range(D_H)], axis=0)
            mm_ref[b] = jnp.concatenate(
                [jnp.broadcast_to(mm_sc[b, h * lc:h * lc + 1], (1, LANE)) for h in range(D_H)], axis=0)


def _odd_mixer(p, sgl0, conv0, mc0, mn0, mm0, g2p, cw, cb, sb, par, nb, tb, lc):
    bsz, t, _ = p.shape
    hl = D_H * lc
    kern = functools.partial(_odd_kernel, nb=nb, tb=tb, lc=lc)
    bmap3 = lambda b, j: (b, 0, 0)
    cmap = lambda b, j: (0, 0)
    blk = lambda w: pltpu.VMEM((nb * tb, w), F32)
    return pl.pallas_call(
        kern, grid=(bsz // nb, t // tb),
        in_specs=[pl.BlockSpec((nb, tb, IN_ODD_PAD), lambda b, j: (b, j, 0)),
                  pl.BlockSpec((nb, C_KW, C_VD), bmap3),
                  pl.BlockSpec((nb, 8, 2 * MIX), bmap3),
                  pl.BlockSpec((nb, MIX, D_HD), bmap3),
                  pl.BlockSpec((nb, D_H, D_HD), bmap3),
                  pl.BlockSpec((nb, D_H, LANE), bmap3),
                  pl.BlockSpec((SLAB, C_KW), cmap),
                  pl.BlockSpec((D_CONV, 2 * MIX), cmap),
                  pl.BlockSpec((1, 2 * MIX), cmap),
                  pl.BlockSpec((1, SLAB), cmap),
                  pl.BlockSpec((8, MIX), cmap)],
        out_specs=[pl.BlockSpec((nb * tb, D_MODEL), lambda b, j: (b * (t // tb) + j, 0)),
                   pl.BlockSpec((nb, C_KW, C_VD), bmap3),
                   pl.BlockSpec((nb, 8, 2 * MIX), bmap3),
                   pl.BlockSpec((nb, MIX, D_HD), bmap3),
                   pl.BlockSpec((nb, D_H, D_HD), bmap3),
                   pl.BlockSpec((nb, D_H, LANE), bmap3)],
        out_shape=[jax.ShapeDtypeStruct((bsz * t, D_MODEL), BF16),
                   jax.ShapeDtypeStruct((bsz, C_KW, C_VD), F32),
                   jax.ShapeDtypeStruct((bsz, 8, 2 * MIX), F32),
                   jax.ShapeDtypeStruct((bsz, MIX, D_HD), F32),
                   jax.ShapeDtypeStruct((bsz, D_H, D_HD), F32),
                   jax.ShapeDtypeStruct((bsz, D_H, LANE), F32)],
        scratch_shapes=[pltpu.VMEM((nb, 8, 2 * MIX), F32), pltpu.VMEM((nb, C_VD, C_KW), F32),
                        pltpu.VMEM((nb, D_HD, MIX), F32), pltpu.VMEM((nb, hl, D_HD), F32),
                        pltpu.VMEM((nb, hl, 1), F32),
                        blk(C_KW), blk(C_KW), blk(C_KW), blk(C_KW), blk(MIX),
                        blk(MIX), blk(MIX), blk(SLAB), blk(SLAB), blk(SLAB), blk(MIX)],
        compiler_params=pltpu.CompilerParams(dimension_semantics=("arbitrary", "arbitrary"),
                                             vmem_limit_bytes=VMEM_LIMIT),
        name="mixer_gla_mlstm")(p, sgl0, conv0, mc0, mn0, mm0, g2p, cw, cb, sb, par)


MIXER_ROWS = 256
SHORT_SEQS = 8


def _mixer_blocking(bsz, t):
    tb = math.gcd(t, MIXER_ROWS)
    nb = math.gcd(bsz, SHORT_SEQS) if tb == t and t * SHORT_SEQS <= MIXER_ROWS else 1
    return nb, tb


def _run(x, shift, s_rwkv, s_hgrn, s_gla, conv, mc, mn, mm, wts):
    bsz, t, d = x.shape
    nb, tb = _mixer_blocking(bsz, t)
    la = math.gcd(t, 16)
    lbc = math.gcd(t, 32)
    x2 = x.reshape(bsz * t, d)
    p0 = _inproj(x2, wts["g0"], wts["w_in0"])
    o0, shift_n, srw_n, shg_n = _even_mixer(
        p0.reshape(bsz, t, IN_EVEN), shift[0][:, None], s_rwkv[0], s_hgrn[0].reshape(bsz, B_H * B_HD, B_HD),
        wts["mu"], wts["w2p"], wts["a2p"], wts["par_e"], nb, tb, la, lbc)
    x1, p1 = _mid(o0, x2, wts["w_out0"], wts["g1"], wts["w_in1"])
    conv8 = jnp.pad(conv[0], ((0, 0), (8 - (D_CONV - 1), 0), (0, 0)))
    mm_l = jnp.broadcast_to(mm[0][:, :, None], (bsz, D_H, LANE))
    o1, sgl_n, conv_n, mc_n, mn_n, mm_n = _odd_mixer(
        p1.reshape(bsz, t, IN_ODD_PAD), s_gla[0].reshape(bsz, C_KW, C_VD), conv8,
        mc[0].reshape(bsz, MIX, D_HD), mn[0], mm_l,
        wts["g2p"], wts["cw"], wts["cb"], wts["sb"], wts["par_o"], nb, tb, lbc)
    y = _final(o1, x1, wts["w_out1"], wts["gf"])
    return (y.reshape(bsz, t, d), shift_n.reshape(1, bsz, A_SHIFT_W), srw_n[None],
            shg_n.reshape(1, bsz, B_H, B_HD, B_HD), sgl_n.reshape(1, bsz, C_H, C_KD, C_VD),
            conv_n[None, :, 8 - (D_CONV - 1):], mc_n.reshape(1, bsz, D_H, D_HD, D_HD), mn_n[None],
            mm_n[None, :, :, 0])


def _odd_in_projection(w):
    sizes = (C_KW, C_KW, MIX, C_LORA, MIX, 2 * MIX, MIX, D_H, D_H, MIX)
    off = np.cumsum((0,) + sizes)
    cq, ck, cv, cg, gc, dqk, dv, di, df, gd = (w[:, off[i]:off[i + 1]] for i in range(len(sizes)))
    pad = jnp.zeros((w.shape[0], SLAB - C_LORA - 2 * D_H), w.dtype)
    return jnp.concatenate([cq, ck, cv, gc, dqk, dv, gd, cg, di, df, pad], axis=1)


def kernel(x_prompt, x_sample, state_shift_a, state_rwkv, state_hgrn, state_gla, state_conv_d, state_mlstm_c,
           state_mlstm_n, state_mlstm_m, norm_g, w_in_even, w_out_even, a_mu, a_w0, a_w2, a_a0, a_a2, a_kk,
           a_ka, a_rk, a_ln_w, a_ln_b, b_lb, b_norm, w_in_odd, w_out_odd, c_g2, c_g2b, c_norm, d_conv_w,
           d_conv_b, d_ib, d_fb, d_norm, final_norm):
    assert w_in_even.shape[0] == 1 and w_in_odd.shape[0] == 1 and b_lb.shape[0] == 2
    zpad = lambda a, rows_before, rows_total: jnp.pad(a, ((rows_before, rows_total - rows_before - a.shape[0]), (0, 0)))
    par_e = jnp.concatenate([a_w0, a_a0, a_kk, a_ka, a_rk, a_ln_w, a_ln_b, b_norm, b_lb[0:1], b_lb[1:2]], axis=0)
    g2b = jnp.pad(c_g2b, ((0, 0), (0, MIX - C_KW)))
    par_o = jnp.concatenate([c_norm, d_norm, g2b], axis=0)
    w_in1 = _odd_in_projection(w_in_odd[0].astype(BF16))
    sb = jnp.pad(jnp.concatenate([d_ib, d_fb], axis=1), ((0, 0), (SLAB_I, SLAB - SLAB_I - 2 * D_H)))
    wts = {
        "g0": norm_g[0:1], "g1": norm_g[1:2], "gf": final_norm[None],
        "w_in0": w_in_even[0].astype(BF16), "w_out0": w_out_even[0].astype(BF16),
        "w_in1": w_in1, "w_out1": w_out_odd[0].astype(BF16),
        "mu": a_mu, "w2p": zpad(a_w2[0], 0, 2 * A_LORA).astype(BF16),
        "a2p": zpad(a_a2[0], A_LORA, 2 * A_LORA).astype(BF16),
        "par_e": zpad(par_e, 0, 16), "par_o": zpad(par_o, 0, 8),
        "g2p": zpad(c_g2[0], 0, SLAB).astype(BF16), "cw": d_conv_w[0], "cb": d_conv_b, "sb": sb,
    }
    bp = x_prompt.shape[0]
    z = lambda *s: jnp.zeros(s, x_prompt.dtype)
    prompt = _run(x_prompt, z(1, bp, A_SHIFT_W), z(1, bp, A_H, A_HD, A_HD), z(1, bp, B_H, B_HD, B_HD),
                  z(1, bp, C_H, C_KD, C_VD), z(1, bp, D_CONV - 1, 2 * MIX), z(1, bp, D_H, D_HD, D_HD),
                  z(1, bp, D_H, D_HD), z(1, bp, D_H), wts)
    sample = _run(x_sample, state_shift_a, state_rwkv, state_hgrn, state_gla, state_conv_d, state_mlstm_c,
                  state_mlstm_n, state_mlstm_m, wts)
    return (prompt[0], sample[0]) + prompt[1:] + sample[1:]
```

```python
import functools
import math

import jax
import jax.numpy as jnp
import numpy as np
from jax import lax
from jax.experimental import pallas as pl
from jax.experimental.pallas import tpu as pltpu

F32 = jnp.float32
BF16 = jnp.bfloat16

D_MODEL = 1024
MIX = D_MODEL // 2
NORM_EPS = 1e-5
A_HD = 64
A_H = MIX // A_HD
A_LORA = 64
A_GN_EPS = 64e-5
A_SHIFT_W = 3 * MIX + 2 * A_LORA
B_HD = 128
B_H = MIX // B_HD
IN_EVEN = A_SHIFT_W + 5 * MIX
C_H = 4
C_VD = MIX // C_H
C_KD = C_VD // 2
C_KW = C_H * C_KD
C_LORA = 16
C_GATE_NORM = 16.0
D_H = 4
D_HD = MIX // D_H
D_CONV = 4
LANE = 128
SLAB = LANE
SLAB_I = C_LORA
SLAB_F = C_LORA + D_H
IN_ODD_PAD = 2 * C_KW + 2 * MIX + 2 * MIX + 2 * MIX + SLAB
VMEM_LIMIT = 48 * 1024 * 1024


_DIMS = {"nn": ((1,), (0,)), "nt": ((1,), (1,)), "tn": ((0,), (0,))}
M_EXACT_R = "rx"
M_EXACT_L = "lx"
M_G = "bf"
M_TINV = "bf"
M_ST = "bf"
M_ATT = "bf"


def _split2(x):
    hi = x.astype(BF16)
    return hi, (x - hi.astype(F32)).astype(BF16)


def _mm(a, b, form, mode):
    dn = (_DIMS[form], ((), ()))
    d = lambda x, y: lax.dot_general(x, y, dn, preferred_element_type=F32)
    if mode == "bf":
        return d(a.astype(BF16), b.astype(BF16))
    if mode == "x3":
        ah, al = _split2(a)
        bh, bl = _split2(b)
        return d(ah, bh) + (d(ah, bl) + d(al, bh))
    if mode == "lx":
        bb = b.astype(BF16)
        h, l = _split2(a)
        return d(h, bb) + d(l, bb)
    assert mode == "rx"
    ab = a.astype(BF16)
    h, l = _split2(b)
    return d(ab, h) + d(ab, l)


def _dot(a, b):
    return jnp.dot(a, b, preferred_element_type=F32)


def _iota(shape, dim):
    return lax.broadcasted_iota(jnp.int32, shape, dim)


def _sigmoid(x):
    return 0.5 + 0.5 * jnp.tanh(0.5 * x)


def _log_sigmoid(x):
    return jnp.minimum(x, 0.0) - jnp.log(1.0 + jnp.exp(-jnp.abs(x)))


def _silu(x):
    return x * _sigmoid(x)


def _rms(x, g):
    return x * lax.rsqrt(jnp.mean(x * x, -1, keepdims=True) + NORM_EPS) * g


def _rows_at(start_and_alignment, size):
    start, alignment = start_and_alignment
    if not isinstance(start, int):
        start = pl.multiple_of(start, alignment)
    return pl.ds(start, size)


def _unroll(trips):
    return 2 if trips % 2 == 0 else 1


def _tile_rows(x, n):
    return jnp.concatenate([x] * n, axis=0)


def _stack_heads(x, n_heads, width):
    return jnp.concatenate([x[:, h * width:(h + 1) * width] for h in range(n_heads)], axis=0)


def _unstack_heads(x, n_heads, rows):
    return jnp.concatenate([x[h * rows:(h + 1) * rows] for h in range(n_heads)], axis=1)


def _head_masks(n_heads, rows, width):
    hl = n_heads * rows
    rh = _iota((hl, n_heads * width), 0) // rows
    lh = _iota((hl, n_heads * width), 1) // width
    wide = (rh == lh).astype(F32)
    ri, ci = _iota((hl, hl), 0), _iota((hl, hl), 1)
    same = (ri // rows) == (ci // rows)
    incl = same & (ci <= ri)
    strict = same & (ci < ri)
    return wide, same, incl, strict


def _chunk_mats(n, chunk):
    ri, ci = _iota((n, n), 0), _iota((n, n), 1)
    same = (ri // chunk) == (ci // chunk)
    return (same & (ci <= ri)).astype(BF16), same.astype(BF16)


def _seg_sum(x, seg):
    bd = ((_iota((LANE, LANE), 0) // seg) == (_iota((LANE, LANE), 1) // seg)).astype(BF16)
    parts = [_mm(x[:, j:j + LANE], bd, "nn", M_EXACT_L) for j in range(0, x.shape[1], LANE)]
    return jnp.concatenate(parts, axis=1)


def _head_rms(x, g, width):
    parts = []
    for j in range(0, x.shape[1], width):
        xs = x[:, j:j + width]
        parts.append(xs * lax.rsqrt(jnp.mean(xs * xs, -1, keepdims=True) + NORM_EPS))
    return jnp.concatenate(parts, axis=1) * g


def _inproj_kernel(x_ref, g_ref, w_ref, p_ref):
    h = _rms(x_ref[...], g_ref[...])
    p_ref[...] = _dot(h.astype(BF16), w_ref[...])


def _mid_kernel(o_ref, x_ref, wo_ref, g_ref, wi_ref, x1_ref, p_ref):
    x1 = x_ref[...] + _dot(o_ref[...], wo_ref[...])
    x1_ref[...] = x1
    p_ref[...] = _dot(_rms(x1, g_ref[...]).astype(BF16), wi_ref[...])


def _final_kernel(o_ref, x_ref, wo_ref, g_ref, y_ref):
    x2 = x_ref[...] + _dot(o_ref[...], wo_ref[...])
    y_ref[...] = _rms(x2, g_ref[...])


def _row_tile(m):
    return math.gcd(m, 512)


def _full(shape):
    return pl.BlockSpec(shape, lambda i: (0,) * len(shape))


def _dense_params():
    return pltpu.CompilerParams(dimension_semantics=("arbitrary",), vmem_limit_bytes=VMEM_LIMIT)


def _inproj(x, g, w):
    m, d = x.shape
    n = w.shape[1]
    tm = _row_tile(m)
    return pl.pallas_call(
        _inproj_kernel, grid=(m // tm,),
        in_specs=[pl.BlockSpec((tm, d), lambda i: (i, 0)), _full((1, d)), _full((d, n))],
        out_specs=pl.BlockSpec((tm, n), lambda i: (i, 0)),
        out_shape=jax.ShapeDtypeStruct((m, n), F32),
        compiler_params=_dense_params(), name="inproj")(x, g, w)


def _mid(o, x, wo, g, wi):
    m, d = x.shape
    n = wi.shape[1]
    tm = _row_tile(m)
    return pl.pallas_call(
        _mid_kernel, grid=(m // tm,),
        in_specs=[pl.BlockSpec((tm, d), lambda i: (i, 0)), pl.BlockSpec((tm, d), lambda i: (i, 0)),
                  _full((d, d)), _full((1, d)), _full((d, n))],
        out_specs=[pl.BlockSpec((tm, d), lambda i: (i, 0)), pl.BlockSpec((tm, n), lambda i: (i, 0))],
        out_shape=[jax.ShapeDtypeStruct((m, d), F32), jax.ShapeDtypeStruct((m, n), F32)],
        compiler_params=_dense_params(), name="outproj_inproj")(o, x, wo, g, wi)


def _final(o, x, wo, g):
    m, d = x.shape
    tm = _row_tile(m)
    return pl.pallas_call(
        _final_kernel, grid=(m // tm,),
        in_specs=[pl.BlockSpec((tm, d), lambda i: (i, 0)), pl.BlockSpec((tm, d), lambda i: (i, 0)),
                  _full((d, d)), _full((1, d))],
        out_specs=pl.BlockSpec((tm, d), lambda i: (i, 0)),
        out_shape=jax.ShapeDtypeStruct((m, d), F32),
        compiler_params=_dense_params(), name="outproj_final")(o, x, wo, g)


def _interleave(*stage_generators):
    live = list(stage_generators)
    while live:
        for gen in list(live):
            try:
                next(gen)
            except StopIteration:
                live.remove(gen)


def _rwkv_factor_stages(chunks, consts, out):
    wide, incl, strict, eye, halves = consts
    L = chunks[0][0].shape[0]
    hl = A_H * L
    n = range(len(chunks))
    tile = functools.partial(_tile_rows, n=A_H)
    kk_w = [tile(c[0]) * wide for c in chunks]
    r_w = [tile(c[1]) * wide for c in chunks]
    stack = lambda x: _stack_heads(x, A_H, A_HD)
    g = [_mm(jnp.concatenate([stack(c[0]), stack(c[1])], axis=0),
             jnp.concatenate([stack(c[2]), stack(c[3])], axis=0), "nt", M_G) for c in chunks]
    yield
    a_kk = [jnp.where(strict, x[:hl, :hl], 0.0) for x in g]
    a_kb = [jnp.where(strict, x[:hl, hl:], 0.0) for x in g]
    a_rk = [jnp.where(incl, x[hl:, :hl], 0.0) for x in g]
    a_rb = [jnp.where(incl, x[hl:, hl:], 0.0) for x in g]
    t = [eye - jnp.where(halves[0], x, 0.0) for x in a_kb]
    v_st = [_stack_heads(c[6], A_H, A_HD) for c in chunks]
    av = [_mm(a_kk[i], v_st[i], "nn", M_ST) for i in n]
    o2 = [_mm(a_rk[i], v_st[i], "nn", M_ST) for i in n]
    for half in halves[1:]:
        ta = [_mm(t[i], jnp.where(half, a_kb[i], 0.0), "nn", M_TINV) for i in n]
        yield
        t = [t[i] - _mm(ta[i], t[i], "nn", M_TINV) for i in n]
        yield
    kv = [_mm(v_st[i], tile(chunks[i][4]) * wide, "tn", M_ST) for i in n]
    m1 = [_mm(t[i], kk_w[i], "nn", M_ST) for i in n]
    m2 = [_mm(t[i], av[i], "nn", M_ST) for i in n]
    yield
    out["factors"] = [
        (jnp.concatenate([-m1[i], r_w[i]], axis=0).astype(BF16), -m2[i], o2[i], a_rb[i].astype(BF16), kv[i],
         (tile(chunks[i][5]) * wide).astype(BF16)) for i in n]


def _rwkv_state_stages(n_chunks, load_factors, store_o, s, out):
    for c in range(n_chunks):
        xm, m2n, o2, a_rb, kv, be_w, ecl = load_factors(c)
        hl = xm.shape[0] // 2
        xs = _mm(xm, s, "nt", M_ST)
        yield
        u = xs[:hl] + m2n
        s = s * ecl + kv + _mm(u, be_w, "tn", M_ST)
        o_st = xs[hl:] + o2 + _mm(a_rb, u, "nn", M_ST)
        yield
        store_o(c, _unstack_heads(o_st, A_H, hl // A_H))
    out["rwkv"] = s


def _gla_stages(n_chunks, load_chunk, store_o, s, consts, n_heads, vd, out, key):
    wide, incl = consts
    tile = functools.partial(_tile_rows, n=n_heads)
    for c in range(n_chunks):
        qd, ki, kend, bl, v = load_chunk(c)
        q_w = tile(qd) * wide
        v_st = _stack_heads(v, n_heads, vd)
        att = jnp.where(incl, _mm(q_w, tile(ki), "nt", M_ATT), 0.0)
        kv = _mm(v_st, tile(kend) * wide, "tn", M_ATT)
        yield
        o_st = _mm(att, v_st, "nn", M_ATT) + _mm(q_w, s, "nt", M_ATT)
        s = s * jnp.exp(bl) + kv
        yield
        store_o(c, _unstack_heads(o_st, n_heads, qd.shape[0]))
    out[key] = s


def _mlstm_stages(n_chunks, load_chunk, store_o, state, consts, out):
    wide, same, incl, ones_bd = consts
    c, nrow, m = state
    tile = functools.partial(_tile_rows, n=D_H)
    for ch in range(n_chunks):
        q, k, v, li, a, a_last = load_chunk(ch)
        L = q.shape[0]
        hl = D_H * L
        to_row = lambda col: jnp.broadcast_to(col, (hl, LANE)).T[:hl]
        q_st = _stack_heads(q, D_H, D_HD)
        k_st = _stack_heads(k, D_H, D_HD)
        v_st = _stack_heads(v, D_H, D_HD)
        qk = _mm(q_st, k_st, "nt", M_ATT)
        dlog = jnp.where(incl, a - to_row(a) + to_row(li), -jnp.inf)
        m_intra = jnp.max(dlog, -1, keepdims=True)
        e_end = a_last - a + li
        m_loc = jnp.max(jnp.where(same, to_row(e_end), -jnp.inf), -1, keepdims=True)
        yield
        s_intra = jnp.exp(dlog - m_intra) * qk
        num_intra = _mm(s_intra, v_st, "nn", M_ATT)
        den_intra = jnp.sum(s_intra, -1, keepdims=True)
        w_end = jnp.exp(e_end - m_loc)
        c_loc = _mm(k_st, tile(v) * wide * w_end, "tn", M_ATT)
        n_loc = _mm(ones_bd, w_end * k_st, "nn", M_ATT)
        yield
        g = a + m
        m_t = jnp.maximum(g, m_intra)
        f_inter = jnp.exp(g - m_t)
        f_intra = jnp.exp(m_intra - m_t)
        qc = _mm(q_st, c, "nn", M_ATT)
        num_inter = jnp.concatenate(
            [qc[h * L:(h + 1) * L, h * D_HD:(h + 1) * D_HD] for h in range(D_H)], axis=0)
        den_inter = jnp.sum(q_st * nrow, -1, keepdims=True)
        num = num_intra * f_intra + num_inter * f_inter
        den = f_intra * den_intra + f_inter * den_inter
        h_st = num / jnp.maximum(jnp.abs(den), jnp.exp(-m_t))
        m_new = jnp.maximum(a_last + m, m_loc)
        fo = jnp.exp(a_last + m - m_new)
        fl = jnp.exp(m_loc - m_new)
        per_value = lambda col: jnp.concatenate(
            [jnp.broadcast_to(col[h * L:h * L + 1], (1, D_HD)) for h in range(D_H)], axis=1)
        c = per_value(fo) * c + per_value(fl) * c_loc
        nrow = fo * nrow + fl * n_loc
        m = m_new
        yield
        store_o(ch, _unstack_heads(h_st, D_H, L))
    out["mlstm"] = (c, nrow, m)


(_E_W0, _E_A0, _E_KK, _E_KA, _E_RK, _E_LNW, _E_LNB, _E_BNORM, _E_LB0, _E_LB1) = range(10)


def _even_kernel(p_ref, shift0_ref, srw0_ref, shg0_ref, mu_ref, w2_ref, a2_ref, par_ref,
                 o_ref, shift_ref, srw_ref, shg_ref,
                 prev_sc, srw_sc, shg_sc, kkt_sc, rt_sc, kh_sc, bh_sc, ke_sc, be_sc, v_sc, epos_sc, bonus_sc, oa_sc,
                 qd_sc, ki_sc, kend_sc, bl_sc, ob_sc, xm_sc, m2_sc, o2_sc, arb_sc, kv_sc, bew_sc, *, nb, tb, la, lb):
    j = pl.program_id(1)
    nj = pl.num_programs(1)
    seqs = range(nb)
    cols = lambda lo, hi: jnp.concatenate([p_ref[b, :, lo:hi] for b in seqs], axis=0)

    @pl.when(j == 0)
    def _():
        for b in seqs:
            prev_sc[b] = jnp.broadcast_to(shift0_ref[b], prev_sc.shape[1:])
            srw_sc[b] = jnp.concatenate([srw0_ref[b, h] for h in range(A_H)], axis=1)
            shg_sc[b] = shg0_ref[b].T

    par = par_ref[...]
    row = lambda i: par[i:i + 1]

    pa = cols(0, A_SHIFT_W)
    rolled = pltpu.roll(pa, 1, axis=0)
    first = _iota(pa.shape, 0) % tb == 0
    before = jnp.concatenate([jnp.broadcast_to(prev_sc[b, 0:1], (tb, A_SHIFT_W)) for b in seqs], axis=0)
    shifted = jnp.where(first, before, rolled)
    for b in seqs:
        prev_sc[b] = jnp.broadcast_to(pa[(b + 1) * tb - 1:(b + 1) * tb], prev_sc.shape[1:])
    xm = pa + (shifted - pa) * mu_ref[...]
    r = xm[:, :MIX]
    k = xm[:, MIX:2 * MIX]
    v = xm[:, 2 * MIX:3 * MIX]
    lora_in = xm[:, 3 * MIX:]
    lw = _sigmoid(row(_E_W0) + _dot(jnp.tanh(lora_in).astype(BF16), w2_ref[...])) * (-math.exp(-0.5))
    a = _sigmoid(row(_E_A0) + _dot(lora_in.astype(BF16), a2_ref[...]))
    kk = k * row(_E_KK)
    kk = kk * lax.rsqrt(jnp.maximum(_seg_sum(kk * kk, A_HD), 1e-12))
    k = k * (1.0 + (a - 1.0) * row(_E_KA))
    bv = kk * a
    cum_a, ones_a = _chunk_mats(nb * tb, la)
    c = _mm(cum_a, lw, "nn", M_EXACT_R)
    cl = _mm(ones_a, lw, "nn", M_EXACT_R)
    e_pos = jnp.exp(c)
    e_neg = jnp.exp(-c)
    e_end = jnp.exp(cl - c)
    kkt_sc[...] = kk * jnp.exp(c - lw)
    rt_sc[...] = r * e_pos
    kh_sc[...] = k * e_neg
    bh_sc[...] = bv * e_neg
    ke_sc[...] = k * e_end
    be_sc[...] = bv * e_end
    v_sc[...] = v
    epos_sc[...] = e_pos
    bonus_sc[...] = _seg_sum(r * k * row(_E_RK), A_HD) * v

    wide, _, incl, strict = _head_masks(A_H, la, A_HD)
    hl = A_H * la
    ri, ci_ = _iota((hl, hl), 0), _iota((hl, hl), 1)
    eye = (ri == ci_).astype(F32)
    halves = []
    m = 1
    while m < la:
        halves.append(((ri // (2 * m)) == (ci_ // (2 * m))) & ((ri // m) != (ci_ // m)))
        m *= 2
    consts_a = (wide, incl, strict, eye, halves)

    base = A_SHIFT_W + MIX

    def hgrn_token_stages():
        e0 = row(_E_LB0)
        e1 = row(_E_LB1)
        emax = jnp.maximum(e0, e1)
        e0 = jnp.exp(e0 - emax)
        lower = e0 / (e0 + jnp.exp(e1 - emax))
        g = lower + (1.0 - lower) * jax.nn.sigmoid(cols(base + MIX, base + 2 * MIX))
        yield
        logg = jnp.log(g)
        cum_b, ones_b = _chunk_mats(nb * tb, lb)
        yield
        gb = _mm(cum_b, logg, "nn", M_EXACT_R)
        yield
        gl = _mm(ones_b, logg, "nn", M_EXACT_R)
        yield
        qd_sc[...] = _silu(cols(base, base + MIX)) * (B_HD ** -0.5) * jnp.exp(gb)
        yield
        ki_sc[...] = (1.0 - g) * jnp.exp(-gb)
        yield
        kend_sc[...] = (1.0 - g) * jnp.exp(gl - gb)
        bl_sc[...] = gl

    wide_b, _, incl_b, _ = _head_masks(B_H, lb, B_HD)
    consts_b = (wide_b, incl_b)

    per_seq = tb // la
    group = math.gcd(nb * per_seq, 8)
    ngroups = nb * per_seq // group
    run = min(group, per_seq)
    chains = group // run
    assert per_seq % run == 0 and (run * la) % lb == 0
    hper = (run * la) // lb

    def factor_stages(gi, slot, out):
        sls = [_rows_at(((gi * group + u) * la, la), la) for u in range(group)]
        yield from _rwkv_factor_stages(
            [(kkt_sc[sl, :], rt_sc[sl, :], kh_sc[sl, :], bh_sc[sl, :], ke_sc[sl, :], be_sc[sl, :], v_sc[sl, :])
             for sl in sls], consts_a, out)
        for u, (xm, m2n, o2, a_rb, kv, be_w) in enumerate(out["factors"]):
            xm_sc[slot + u] = xm
            m2_sc[slot + u] = m2n
            o2_sc[slot + u] = o2
            arb_sc[slot + u] = a_rb
            kv_sc[slot + u] = kv
            bew_sc[slot + u] = be_w

    def chunk_body(gi, carry, prefetch):
        slot = (gi % 2) * group
        stages, finish = [], []
        for q in range(chains):
            first = gi * group + q * run
            seq = first // per_seq
            row0 = first * la
            out = {}

            def load_factors(u, q=q, first=first):
                i = slot + q * run + u
                return (xm_sc[i], m2_sc[i], o2_sc[i], arb_sc[i], kv_sc[i], bew_sc[i],
                        epos_sc[pl.ds((first + u) * la + la - 1, 1), :])

            def store_oa(u, o, row0=row0):
                oa_sc[_rows_at((row0 + u * la, la), la), :] = o

            def load_b(u, row0=row0, seq=seq):
                r = row0 + u * lb
                sl = _rows_at((r, lb), lb)
                return (qd_sc[sl, :], ki_sc[sl, :], kend_sc[sl, :], bl_sc[pl.ds(r, 1), :],
                        p_ref[seq, _rows_at((r - seq * tb, lb), lb), base + 2 * MIX:base + 3 * MIX])

            def store_ob(u, o, row0=row0):
                ob_sc[_rows_at((row0 + u * lb, lb), lb), :] = o

            stages.append(_rwkv_state_stages(run, load_factors, store_oa, srw_sc[seq], out))
            stages.append(_gla_stages(hper, load_b, store_ob, shg_sc[seq], consts_b, B_H, B_HD, out, "hgrn"))
            finish.append((seq, out))
        if prefetch:
            stages.insert(1, factor_stages(gi + 1, group - slot, {}))
        _interleave(*stages)
        for seq, out in finish:
            srw_sc[seq] = out["rwkv"]
            shg_sc[seq] = out["hgrn"]
        return carry

    _interleave(factor_stages(0, 0, {}), hgrn_token_stages())
    lax.fori_loop(0, ngroups - 1, functools.partial(chunk_body, prefetch=True), 0)
    chunk_body(ngroups - 1, 0, prefetch=False)

    o = oa_sc[...]
    mean = _seg_sum(o, A_HD) * (1.0 / A_HD)
    cen = o - mean
    var = _seg_sum(cen * cen, A_HD) * (1.0 / A_HD)
    oa = cen * lax.rsqrt(var + A_GN_EPS) * row(_E_LNW) + row(_E_LNB)
    oa = oa + bonus_sc[...]
    oa = oa * _silu(cols(A_SHIFT_W, A_SHIFT_W + MIX))
    ob = _head_rms(ob_sc[...], row(_E_BNORM), B_HD) * _silu(cols(base + 3 * MIX, base + 4 * MIX))
    o_ref[...] = jnp.concatenate([oa, ob], axis=1).astype(o_ref.dtype)

    @pl.when(j == nj - 1)
    def _():
        for b in seqs:
            shift_ref[b] = prev_sc[b, 0:1]
            s = srw_sc[b]
            for h in range(A_H):
                srw_ref[b, h] = s[:, h * A_HD:(h + 1) * A_HD]
            shg_ref[b] = shg_sc[b].T


def _even_mixer(p, shift0, srw0, shg0, mu, w2p, a2p, par, nb, tb, la, lb):
    bsz, t, _ = p.shape
    kern = functools.partial(_even_kernel, nb=nb, tb=tb, la=la, lb=lb)
    nch, hl = 2 * math.gcd(nb * tb // la, 8), A_H * la
    bmap3 = lambda b, j: (b, 0, 0)
    bmap4 = lambda b, j: (b, 0, 0, 0)
    cmap = lambda b, j: (0, 0)
    blk = lambda: pltpu.VMEM((nb * tb, MIX), F32)
    return pl.pallas_call(
        kern, grid=(bsz // nb, t // tb),
        in_specs=[pl.BlockSpec((nb, tb, IN_EVEN), lambda b, j: (b, j, 0)),
                  pl.BlockSpec((nb, 1, A_SHIFT_W), bmap3),
                  pl.BlockSpec((nb, A_H, A_HD, A_HD), bmap4),
                  pl.BlockSpec((nb, B_H * B_HD, B_HD), bmap3),
                  pl.BlockSpec((1, A_SHIFT_W), cmap),
                  pl.BlockSpec((2 * A_LORA, MIX), cmap),
                  pl.BlockSpec((2 * A_LORA, MIX), cmap),
                  pl.BlockSpec((16, MIX), cmap)],
        out_specs=[pl.BlockSpec((nb * tb, D_MODEL), lambda b, j: (b * (t // tb) + j, 0)),
                   pl.BlockSpec((nb, 1, A_SHIFT_W), bmap3),
                   pl.BlockSpec((nb, A_H, A_HD, A_HD), bmap4),
                   pl.BlockSpec((nb, B_H * B_HD, B_HD), bmap3)],
        out_shape=[jax.ShapeDtypeStruct((bsz * t, D_MODEL), BF16),
                   jax.ShapeDtypeStruct((bsz, 1, A_SHIFT_W), F32),
                   jax.ShapeDtypeStruct((bsz, A_H, A_HD, A_HD), F32),
                   jax.ShapeDtypeStruct((bsz, B_H * B_HD, B_HD), F32)],
        scratch_shapes=[pltpu.VMEM((nb, 8, A_SHIFT_W), F32), pltpu.VMEM((nb, A_HD, MIX), F32),
                        pltpu.VMEM((nb, B_HD, B_H * B_HD), F32)] + [blk() for _ in range(15)] + [
                            pltpu.VMEM((nch, 2 * hl, MIX), BF16), pltpu.VMEM((nch, hl, A_HD), F32),
                            pltpu.VMEM((nch, hl, A_HD), F32), pltpu.VMEM((nch, hl, hl), BF16),
                            pltpu.VMEM((nch, A_HD, MIX), F32), pltpu.VMEM((nch, hl, MIX), BF16)],
        compiler_params=pltpu.CompilerParams(dimension_semantics=("arbitrary", "arbitrary"),
                                             vmem_limit_bytes=VMEM_LIMIT),
        name="mixer_rwkv_hgrn")(p, shift0, srw0, shg0, mu, w2p, a2p, par)


_O_CQ = 0
_O_CK = _O_CQ + C_KW
_O_CV = _O_CK + C_KW
_O_GC = _O_CV + MIX
_O_DQK = _O_GC + MIX
_O_DV = _O_DQK + 2 * MIX
_O_GD = _O_DV + MIX
_O_SLAB = _O_GD + MIX
(_P_CNORM, _P_DNORM, _P_G2B) = range(3)


def _odd_kernel(p_ref, sgl0_ref, conv0_ref, mc0_ref, mn0_ref, mm0_ref, g2_ref, cw_ref, cb_ref, sb_ref, par_ref,
                o_ref, sgl_ref, conv_ref, mc_ref, mn_ref, mm_ref,
                prev_sc, sgl_sc, mc_sc, mn_sc, mm_sc, cq_sc, ck_sc, ce_sc, cl_sc, oc_sc,
                dq_sc, dk_sc, li_sc, fa_sc, fl_sc, od_sc, *, nb, tb, lc):
    j = pl.program_id(1)
    nj = pl.num_programs(1)
    hl = D_H * lc
    seqs = range(nb)
    cols = lambda lo, hi: jnp.concatenate([p_ref[b, :, lo:hi] for b in seqs], axis=0)
    head_rows = lambda x: jnp.concatenate(
        [jnp.broadcast_to(x[h:h + 1], (lc, x.shape[1])) for h in range(D_H)], axis=0)

    @pl.when(j == 0)
    def _():
        for b in seqs:
            prev_sc[b] = conv0_ref[b]
            sgl_sc[b] = sgl0_ref[b].T
            mc_sc[b] = mc0_ref[b].T
            mn_sc[b] = head_rows(mn0_ref[b])
            mm_sc[b] = head_rows(mm0_ref[b])[:, :1]

    par = par_ref[...]
    row = lambda i: par[i:i + 1]

    cum_c, ones_c = _chunk_mats(nb * tb, lc)
    slab = cols(_O_SLAB, _O_SLAB + SLAB)
    pre = _dot(slab.astype(BF16), g2_ref[...]) + row(_P_G2B)[:, :C_KW]
    logg = _log_sigmoid(pre) * (1.0 / C_GATE_NORM)
    gb = _mm(cum_c, logg, "nn", M_EXACT_R)
    gl = _mm(ones_c, logg, "nn", M_EXACT_R)
    ck = cols(_O_CK, _O_CK + C_KW)
    cq_sc[...] = cols(_O_CQ, _O_CQ + C_KW) * (C_KD ** -0.5) * jnp.exp(gb)
    ck_sc[...] = ck * jnp.exp(-gb)
    ce_sc[...] = ck * jnp.exp(gl - gb)
    cl_sc[...] = gl

    wide_c, _, incl_c, _ = _head_masks(C_H, lc, C_KD)
    consts_c = (wide_c, incl_c)

    x = cols(_O_DQK, _O_DQK + 2 * MIX)
    cw = cw_ref[...]
    conv = cb_ref[...] + x * cw[D_CONV - 1:D_CONV]
    head_row = _iota((8, 2 * MIX), 0)
    for s in range(1, D_CONV):
        xr = pltpu.roll(x, s, axis=0)
        pieces = []
        for b in seqs:
            pieces.append(jnp.where(head_row < s, pltpu.roll(prev_sc[b], s, axis=0), xr[b * tb:b * tb + 8]))
            if tb > 8:
                pieces.append(xr[b * tb + 8:(b + 1) * tb])
        xs = pieces[0] if len(pieces) == 1 else jnp.concatenate(pieces, axis=0)
        conv = conv + xs * cw[D_CONV - 1 - s:D_CONV - s]
    for b in seqs:
        prev_sc[b] = x[(b + 1) * tb - 8:(b + 1) * tb]
    conv = _silu(conv)
    dq_sc[...] = conv[:, :MIX]
    dk_sc[...] = conv[:, MIX:] * (D_HD ** -0.5)
    gates = slab + sb_ref[...]
    lf = pltpu.roll(_log_sigmoid(gates), SLAB - (SLAB_F - SLAB_I), axis=1)
    li_sc[...] = gates
    fa_sc[...] = _mm(cum_c, lf, "nn", M_EXACT_R)
    fl_sc[...] = _mm(ones_c, lf, "nn", M_EXACT_R)
    lane = _iota((hl, SLAB), 1)
    head = _iota((hl, SLAB), 0) // lc
    pick = lambda z: jnp.sum(jnp.where(lane == head + SLAB_I, _tile_rows(z, D_H), 0.0), -1, keepdims=True)

    wide_d, same_d, incl_d, _ = _head_masks(D_H, lc, D_HD)
    consts_d = (wide_d, same_d, incl_d, same_d.astype(BF16))

    per_seq = tb // lc
    per_trip = math.gcd(nb * per_seq, 2 if per_seq > 1 else 4)
    run = min(per_trip, per_seq)
    chains = per_trip // run
    assert per_seq % run == 0

    def chunk_body(gi, carry):
        stages, finish = [], []
        for q in range(chains):
            first = gi * per_trip + q * run
            seq = first // per_seq
            out = {}
            rows = lambda u, first=first: _rows_at(((first + u) * lc, lc), lc)
            in_seq = lambda u, first=first, seq=seq: _rows_at(((first + u) * lc - seq * tb, lc), lc)

            def load_c(u, first=first, seq=seq, rows=rows, in_seq=in_seq):
                sl = rows(u)
                return (cq_sc[sl, :], ck_sc[sl, :], ce_sc[sl, :], cl_sc[pl.ds((first + u) * lc, 1), :],
                        p_ref[seq, in_seq(u), _O_CV:_O_CV + MIX])

            def store_c(u, o, rows=rows):
                oc_sc[rows(u), :] = o

            def load_d(u, seq=seq, rows=rows, in_seq=in_seq):
                sl = rows(u)
                return (dq_sc[sl, :], dk_sc[sl, :], p_ref[seq, in_seq(u), _O_DV:_O_DV + MIX],
                        pick(li_sc[sl, :]), pick(fa_sc[sl, :]), pick(fl_sc[sl, :]))

            def store_d(u, o, rows=rows):
                od_sc[rows(u), :] = o

            stages.append(_mlstm_stages(run, load_d, store_d, (mc_sc[seq], mn_sc[seq], mm_sc[seq]), consts_d, out))
            stages.append(_gla_stages(run, load_c, store_c, sgl_sc[seq], consts_c, C_H, C_VD, out, "gla"))
            finish.append((seq, out))
        _interleave(*stages)
        for seq, out in finish:
            sgl_sc[seq] = out["gla"]
            mc_sc[seq], mn_sc[seq], mm_sc[seq] = out["mlstm"]
        return carry

    lax.fori_loop(0, nb * per_seq // per_trip, chunk_body, 0)

    oc = _head_rms(oc_sc[...], row(_P_CNORM), C_VD) * _silu(cols(_O_GC, _O_GC + MIX))
    od = _head_rms(od_sc[...], row(_P_DNORM), D_HD) * _silu(cols(_O_GD, _O_GD + MIX))
    o_ref[...] = jnp.concatenate([oc, od], axis=1).astype(o_ref.dtype)

    @pl.when(j == nj - 1)
    def _():
        for b in seqs:
            sgl_ref[b] = sgl_sc[b].T
            conv_ref[b] = prev_sc[b]
            mc_ref[b] = mc_sc[b].T
            mn_ref[b] = jnp.concatenate([mn_sc[b, h * lc:h * lc + 1] for h in range(D_H)], axis=0)
            mm_ref[b] = jnp.concatenate(
                [jnp.broadcast_to(mm_sc[b, h * lc:h * lc + 1], (1, LANE)) for h in range(D_H)], axis=0)


def _odd_mixer(p, sgl0, conv0, mc0, mn0, mm0, g2p, cw, cb, sb, par, nb, tb, lc):
    bsz, t, _ = p.shape
    hl = D_H * lc
    kern = functools.partial(_odd_kernel, nb=nb, tb=tb, lc=lc)
    bmap3 = lambda b, j: (b, 0, 0)
    cmap = lambda b, j: (0, 0)
    blk = lambda w: pltpu.VMEM((nb * tb, w), F32)
    return pl.pallas_call(
        kern, grid=(bsz // nb, t // tb),
        in_specs=[pl.BlockSpec((nb, tb, IN_ODD_PAD), lambda b, j: (b, j, 0)),
                  pl.BlockSpec((nb, C_KW, C_VD), bmap3),
                  pl.BlockSpec((nb, 8, 2 * MIX), bmap3),
                  pl.BlockSpec((nb, MIX, D_HD), bmap3),
                  pl.BlockSpec((nb, D_H, D_HD), bmap3),
                  pl.BlockSpec((nb, D_H, LANE), bmap3),
                  pl.BlockSpec((SLAB, C_KW), cmap),
                  pl.BlockSpec((D_CONV, 2 * MIX), cmap),
                  pl.BlockSpec((1, 2 * MIX), cmap),
                  pl.BlockSpec((1, SLAB), cmap),
                  pl.BlockSpec((8, MIX), cmap)],
        out_specs=[pl.BlockSpec((nb * tb, D_MODEL), lambda b, j: (b * (t // tb) + j, 0)),
                   pl.BlockSpec((nb, C_KW, C_VD), bmap3),
                   pl.BlockSpec((nb, 8, 2 * MIX), bmap3),
                   pl.BlockSpec((nb, MIX, D_HD), bmap3),
                   pl.BlockSpec((nb, D_H, D_HD), bmap3),
                   pl.BlockSpec((nb, D_H, LANE), bmap3)],
        out_shape=[jax.ShapeDtypeStruct((bsz * t, D_MODEL), BF16),
                   jax.ShapeDtypeStruct((bsz, C_KW, C_VD), F32),
                   jax.ShapeDtypeStruct((bsz, 8, 2 * MIX), F32),
                   jax.ShapeDtypeStruct((bsz, MIX, D_HD), F32),
                   jax.ShapeDtypeStruct((bsz, D_H, D_HD), F32),
                   jax.ShapeDtypeStruct((bsz, D_H, LANE), F32)],
        scratch_shapes=[pltpu.VMEM((nb, 8, 2 * MIX), F32), pltpu.VMEM((nb, C_VD, C_KW), F32),
                        pltpu.VMEM((nb, D_HD, MIX), F32), pltpu.VMEM((nb, hl, D_HD), F32),
                        pltpu.VMEM((nb, hl, 1), F32),
                        blk(C_KW), blk(C_KW), blk(C_KW), blk(C_KW), blk(MIX),
                        blk(MIX), blk(MIX), blk(SLAB), blk(SLAB), blk(SLAB), blk(MIX)],
        compiler_params=pltpu.CompilerParams(dimension_semantics=("arbitrary", "arbitrary"),
                                             vmem_limit_bytes=VMEM_LIMIT),
        name="mixer_gla_mlstm")(p, sgl0, conv0, mc0, mn0, mm0, g2p, cw, cb, sb, par)


MIXER_ROWS = 256
SHORT_SEQS = 8


def _mixer_blocking(bsz, t):
    tb = math.gcd(t, MIXER_ROWS)
    nb = math.gcd(bsz, SHORT_SEQS) if tb == t and t * SHORT_SEQS <= MIXER_ROWS else 1
    return nb, tb


def _run(x, shift, s_rwkv, s_hgrn, s_gla, conv, mc, mn, mm, wts):
    bsz, t, d = x.shape
    nb, tb = _mixer_blocking(bsz, t)
    la = math.gcd(t, 16)
    lbc = math.gcd(t, 32)
    x2 = x.reshape(bsz * t, d)
    p0 = _inproj(x2, wts["g0"], wts["w_in0"])
    o0, shift_n, srw_n, shg_n = _even_mixer(
        p0.reshape(bsz, t, IN_EVEN), shift[0][:, None], s_rwkv[0], s_hgrn[0].reshape(bsz, B_H * B_HD, B_HD),
        wts["mu"], wts["w2p"], wts["a2p"], wts["par_e"], nb, tb, la, lbc)
    x1, p1 = _mid(o0, x2, wts["w_out0"], wts["g1"], wts["w_in1"])
    conv8 = jnp.pad(conv[0], ((0, 0), (8 - (D_CONV - 1), 0), (0, 0)))
    mm_l = jnp.broadcast_to(mm[0][:, :, None], (bsz, D_H, LANE))
    o1, sgl_n, conv_n, mc_n, mn_n, mm_n = _odd_mixer(
        p1.reshape(bsz, t, IN_ODD_PAD), s_gla[0].reshape(bsz, C_KW, C_VD), conv8,
        mc[0].reshape(bsz, MIX, D_HD), mn[0], mm_l,
        wts["g2p"], wts["cw"], wts["cb"], wts["sb"], wts["par_o"], nb, tb, lbc)
    y = _final(o1, x1, wts["w_out1"], wts["gf"])
    return (y.reshape(bsz, t, d), shift_n.reshape(1, bsz, A_SHIFT_W), srw_n[None],
            shg_n.reshape(1, bsz, B_H, B_HD, B_HD), sgl_n.reshape(1, bsz, C_H, C_KD, C_VD),
            conv_n[None, :, 8 - (D_CONV - 1):], mc_n.reshape(1, bsz, D_H, D_HD, D_HD), mn_n[None],
            mm_n[None, :, :, 0])


def _odd_in_projection(w):
    sizes = (C_KW, C_KW, MIX, C_LORA, MIX, 2 * MIX, MIX, D_H, D_H, MIX)
    off = np.cumsum((0,) + sizes)
    cq, ck, cv, cg, gc, dqk, dv, di, df, gd = (w[:, off[i]:off[i + 1]] for i in range(len(sizes)))
    pad = jnp.zeros((w.shape[0], SLAB - C_LORA - 2 * D_H), w.dtype)
    return jnp.concatenate([cq, ck, cv, gc, dqk, dv, gd, cg, di, df, pad], axis=1)


def kernel(x_prompt, x_sample, state_shift_a, state_rwkv, state_hgrn, state_gla, state_conv_d, state_mlstm_c,
           state_mlstm_n, state_mlstm_m, norm_g, w_in_even, w_out_even, a_mu, a_w0, a_w2, a_a0, a_a2, a_kk,
           a_ka, a_rk, a_ln_w, a_ln_b, b_lb, b_norm, w_in_odd, w_out_odd, c_g2, c_g2b, c_norm, d_conv_w,
           d_conv_b, d_ib, d_fb, d_norm, final_norm):
    assert w_in_even.shape[0] == 1 and w_in_odd.shape[0] == 1 and b_lb.shape[0] == 2
    zpad = lambda a, rows_before, rows_total: jnp.pad(a, ((rows_before, rows_total - rows_before - a.shape[0]), (0, 0)))
    par_e = jnp.concatenate([a_w0, a_a0, a_kk, a_ka, a_rk, a_ln_w, a_ln_b, b_norm, b_lb[0:1], b_lb[1:2]], axis=0)
    g2b = jnp.pad(c_g2b, ((0, 0), (0, MIX - C_KW)))
    par_o = jnp.concatenate([c_norm, d_norm, g2b], axis=0)
    w_in1 = _odd_in_projection(w_in_odd[0].astype(BF16))
    sb = jnp.pad(jnp.concatenate([d_ib, d_fb], axis=1), ((0, 0), (SLAB_I, SLAB - SLAB_I - 2 * D_H)))
    wts = {
        "g0": norm_g[0:1], "g1": norm_g[1:2], "gf": final_norm[None],
        "w_in0": w_in_even[0].astype(BF16), "w_out0": w_out_even[0].astype(BF16),
        "w_in1": w_in1, "w_out1": w_out_odd[0].astype(BF16),
        "mu": a_mu, "w2p": zpad(a_w2[0], 0, 2 * A_LORA).astype(BF16),
        "a2p": zpad(a_a2[0], A_LORA, 2 * A_LORA).astype(BF16),
        "par_e": zpad(par_e, 0, 16), "par_o": zpad(par_o, 0, 8),
        "g2p": zpad(c_g2[0], 0, SLAB).astype(BF16), "cw": d_conv_w[0], "cb": d_conv_b, "sb": sb,
    }
    bp = x_prompt.shape[0]
    z = lambda *s: jnp.zeros(s, x_prompt.dtype)
    prompt = _run(x_prompt, z(1, bp, A_SHIFT_W), z(1, bp, A_H, A_HD, A_HD), z(1, bp, B_H, B_HD, B_HD),
                  z(1, bp, C_H, C_KD, C_VD), z(1, bp, D_CONV - 1, 2 * MIX), z(1, bp, D_H, D_HD, D_HD),
                  z(1, bp, D_H, D_HD), z(1, bp, D_H), wts)
    sample = _run(x_sample, state_shift_a, state_rwkv, state_hgrn, state_gla, state_conv_d, state_mlstm_c,
                  state_mlstm_n, state_mlstm_m, wts)
    return (prompt[0], sample[0]) + prompt[1:] + sample[1:]
```

```python
import functools
import math

import jax
import jax.numpy as jnp
import numpy as np
from jax import lax
from jax.experimental import pallas as pl
from jax.experimental.pallas import tpu as pltpu

F32 = jnp.float32
BF16 = jnp.bfloat16

D_MODEL = 1024
MIX = D_MODEL // 2
NORM_EPS = 1e-5
A_HD = 64
A_H = MIX // A_HD
A_LORA = 64
A_GN_EPS = 64e-5
A_SHIFT_W = 3 * MIX + 2 * A_LORA
B_HD = 128
B_H = MIX // B_HD
IN_EVEN = A_SHIFT_W + 5 * MIX
C_H = 4
C_VD = MIX // C_H
C_KD = C_VD // 2
C_KW = C_H * C_KD
C_LORA = 16
C_GATE_NORM = 16.0
D_H = 4
D_HD = MIX // D_H
D_CONV = 4
LANE = 128
SLAB = LANE
SLAB_I = C_LORA
SLAB_F = C_LORA + D_H
IN_ODD_PAD = 2 * C_KW + 2 * MIX + 2 * MIX + 2 * MIX + SLAB
VMEM_LIMIT = 48 * 1024 * 1024


_DIMS = {"nn": ((1,), (0,)), "nt": ((1,), (1,)), "tn": ((0,), (0,))}
M_EXACT_R = "rx"
M_EXACT_L = "lx"
M_G = "bf"
M_TINV = "bf"
M_ST = "bf"
M_ATT = "bf"


def _split2(x):
    hi = x.astype(BF16)
    return hi, (x - hi.astype(F32)).astype(BF16)


def _mm(a, b, form, mode):
    dn = (_DIMS[form], ((), ()))
    d = lambda x, y: lax.dot_general(x, y, dn, preferred_element_type=F32)
    if mode == "bf":
        return d(a.astype(BF16), b.astype(BF16))
    if mode == "x3":
        ah, al = _split2(a)
        bh, bl = _split2(b)
        return d(ah, bh) + (d(ah, bl) + d(al, bh))
    if mode == "lx":
        bb = b.astype(BF16)
        h, l = _split2(a)
        return d(h, bb) + d(l, bb)
    assert mode == "rx"
    ab = a.astype(BF16)
    h, l = _split2(b)
    return d(ab, h) + d(ab, l)


def _dot(a, b):
    return jnp.dot(a, b, preferred_element_type=F32)


def _iota(shape, dim):
    return lax.broadcasted_iota(jnp.int32, shape, dim)


def _sigmoid(x):
    return 0.5 + 0.5 * jnp.tanh(0.5 * x)


def _log_sigmoid(x):
    return jnp.minimum(x, 0.0) - jnp.log(1.0 + jnp.exp(-jnp.abs(x)))


def _silu(x):
    return x * _sigmoid(x)


def _rms(x, g):
    return x * lax.rsqrt(jnp.mean(x * x, -1, keepdims=True) + NORM_EPS) * g


def _rows_at(start_and_alignment, size):
    start, alignment = start_and_alignment
    if not isinstance(start, int):
        start = pl.multiple_of(start, alignment)
    return pl.ds(start, size)


def _unroll(trips):
    return 2 if trips % 2 == 0 else 1


def _tile_rows(x, n):
    return jnp.concatenate([x] * n, axis=0)


def _stack_heads(x, n_heads, width):
    return jnp.concatenate([x[:, h * width:(h + 1) * width] for h in range(n_heads)], axis=0)


def _unstack_heads(x, n_heads, rows):
    return jnp.concatenate([x[h * rows:(h + 1) * rows] for h in range(n_heads)], axis=1)


def _head_masks(n_heads, rows, width):
    hl = n_heads * rows
    rh = _iota((hl, n_heads * width), 0) // rows
    lh = _iota((hl, n_heads * width), 1) // width
    wide = (rh == lh).astype(F32)
    ri, ci = _iota((hl, hl), 0), _iota((hl, hl), 1)
    same = (ri // rows) == (ci // rows)
    incl = same & (ci <= ri)
    strict = same & (ci < ri)
    return wide, same, incl, strict


def _chunk_mats(n, chunk):
    ri, ci = _iota((n, n), 0), _iota((n, n), 1)
    same = (ri // chunk) == (ci // chunk)
    return (same & (ci <= ri)).astype(BF16), same.astype(BF16)


def _seg_sum(x, seg):
    bd = ((_iota((LANE, LANE), 0) // seg) == (_iota((LANE, LANE), 1) // seg)).astype(BF16)
    parts = [_mm(x[:, j:j + LANE], bd, "nn", M_EXACT_L) for j in range(0, x.shape[1], LANE)]
    return jnp.concatenate(parts, axis=1)


def _head_rms(x, g, width):
    parts = []
    for j in range(0, x.shape[1], width):
        xs = x[:, j:j + width]
        parts.append(xs * lax.rsqrt(jnp.mean(xs * xs, -1, keepdims=True) + NORM_EPS))
    return jnp.concatenate(parts, axis=1) * g


def _inproj_kernel(x_ref, g_ref, w_ref, p_ref):
    h = _rms(x_ref[...], g_ref[...])
    p_ref[...] = _dot(h.astype(BF16), w_ref[...])


def _mid_kernel(o_ref, x_ref, wo_ref, g_ref, wi_ref, x1_ref, p_ref):
    x1 = x_ref[...] + _dot(o_ref[...], wo_ref[...])
    x1_ref[...] = x1
    p_ref[...] = _dot(_rms(x1, g_ref[...]).astype(BF16), wi_ref[...])


def _final_kernel(o_ref, x_ref, wo_ref, g_ref, y_ref):
    x2 = x_ref[...] + _dot(o_ref[...], wo_ref[...])
    y_ref[...] = _rms(x2, g_ref[...])


def _row_tile(m):
    return math.gcd(m, 512)


def _full(shape):
    return pl.BlockSpec(shape, lambda i: (0,) * len(shape))


def _dense_params():
    return pltpu.CompilerParams(dimension_semantics=("arbitrary",), vmem_limit_bytes=VMEM_LIMIT)


def _inproj(x, g, w):
    m, d = x.shape
    n = w.shape[1]
    tm = _row_tile(m)
    return pl.pallas_call(
        _inproj_kernel, grid=(m // tm,),
        in_specs=[pl.BlockSpec((tm, d), lambda i: (i, 0)), _full((1, d)), _full((d, n))],
        out_specs=pl.BlockSpec((tm, n), lambda i: (i, 0)),
        out_shape=jax.ShapeDtypeStruct((m, n), F32),
        compiler_params=_dense_params(), name="inproj")(x, g, w)


def _mid(o, x, wo, g, wi):
    m, d = x.shape
    n = wi.shape[1]
    tm = _row_tile(m)
    return pl.pallas_call(
        _mid_kernel, grid=(m // tm,),
        in_specs=[pl.BlockSpec((tm, d), lambda i: (i, 0)), pl.BlockSpec((tm, d), lambda i: (i, 0)),
                  _full((d, d)), _full((1, d)), _full((d, n))],
        out_specs=[pl.BlockSpec((tm, d), lambda i: (i, 0)), pl.BlockSpec((tm, n), lambda i: (i, 0))],
        out_shape=[jax.ShapeDtypeStruct((m, d), F32), jax.ShapeDtypeStruct((m, n), F32)],
        compiler_params=_dense_params(), name="outproj_inproj")(o, x, wo, g, wi)


def _final(o, x, wo, g):
    m, d = x.shape
    tm = _row_tile(m)
    return pl.pallas_call(
        _final_kernel, grid=(m // tm,),
        in_specs=[pl.BlockSpec((tm, d), lambda i: (i, 0)), pl.BlockSpec((tm, d), lambda i: (i, 0)),
                  _full((d, d)), _full((1, d))],
        out_specs=pl.BlockSpec((tm, d), lambda i: (i, 0)),
        out_shape=jax.ShapeDtypeStruct((m, d), F32),
        compiler_params=_dense_params(), name="outproj_final")(o, x, wo, g)


def _interleave(*staged):
    live = [[gen, 0, max(n, 1)] for gen, n in staged]
    while live:
        item = min(live, key=lambda it: it[1] / it[2])
        try:
            next(item[0])
            item[1] += 1
        except StopIteration:
            live.remove(item)


def _rwkv_factor_stages(chunks, consts, store):
    wide, incl, strict, eye, halves = consts
    L = chunks[0][0].shape[0]
    hl = A_H * L
    n = range(len(chunks))
    tile = functools.partial(_tile_rows, n=A_H)
    for i in n:
        store(i, "r_w", (tile(chunks[i][1]) * wide).astype(BF16))
        store(i, "be_w", (tile(chunks[i][5]) * wide).astype(BF16))
    stack = lambda x: _stack_heads(x, A_H, A_HD)
    g = [_mm(jnp.concatenate([stack(c[0]), stack(c[1])], axis=0),
             jnp.concatenate([stack(c[2]), stack(c[3])], axis=0), "nt", M_G) for c in chunks]
    yield
    a_kk = [jnp.where(strict, x[:hl, :hl], 0.0) for x in g]
    a_kb = [jnp.where(strict, x[:hl, hl:], 0.0) for x in g]
    a_rk = [jnp.where(incl, x[hl:, :hl], 0.0) for x in g]
    for i in n:
        store(i, "a_rb", jnp.where(incl, g[i][hl:, hl:], 0.0).astype(BF16))
    t = [eye - jnp.where(halves[0], x, 0.0) for x in a_kb]
    v_st = [_stack_heads(c[6], A_H, A_HD) for c in chunks]
    av = [_mm(a_kk[i], v_st[i], "nn", M_ST) for i in n]
    for i in n:
        store(i, "o2", _mm(a_rk[i], v_st[i], "nn", M_ST))
        store(i, "kv", _mm(v_st[i], tile(chunks[i][4]) * wide, "tn", M_ST))
    for half in halves[1:]:
        ta = [_mm(t[i], jnp.where(half, a_kb[i], 0.0), "nn", M_TINV) for i in n]
        yield
        t = [t[i] - _mm(ta[i], t[i], "nn", M_TINV) for i in n]
        yield
    for i in n:
        store(i, "m1n", (-_mm(t[i], tile(chunks[i][0]) * wide, "nn", M_ST)).astype(BF16))
        store(i, "m2n", -_mm(t[i], av[i], "nn", M_ST))
    yield


def _rwkv_state_stages(n_chunks, load_factors, store_o, s, out):
    for c in range(n_chunks):
        xm, m2n, o2, a_rb, kv, be_w, ecl = load_factors(c)
        hl = xm.shape[0] // 2
        xs = _mm(xm, s, "nt", M_ST)
        yield
        u = xs[:hl] + m2n
        s = s * ecl + kv + _mm(u, be_w, "tn", M_ST)
        o_st = xs[hl:] + o2 + _mm(a_rb, u, "nn", M_ST)
        yield
        store_o(c, _unstack_heads(o_st, A_H, hl // A_H))
    out["rwkv"] = s


def _gla_stages(n_chunks, load_chunk, store_o, s, consts, n_heads, vd, out, key):
    wide, incl = consts
    tile = functools.partial(_tile_rows, n=n_heads)
    for c in range(n_chunks):
        qd, ki, kend, bl, v = load_chunk(c)
        q_w = tile(qd) * wide
        v_st = _stack_heads(v, n_heads, vd)
        att = jnp.where(incl, _mm(q_w, tile(ki), "nt", M_ATT), 0.0)
        kv = _mm(v_st, tile(kend) * wide, "tn", M_ATT)
        yield
        o_st = _mm(att, v_st, "nn", M_ATT) + _mm(q_w, s, "nt", M_ATT)
        s = s * jnp.exp(bl) + kv
        yield
        store_o(c, _unstack_heads(o_st, n_heads, qd.shape[0]))
    out[key] = s


def _mlstm_stages(n_chunks, load_chunk, store_o, state, consts, out):
    wide, same, incl, ones_bd = consts
    c, nrow, m = state
    tile = functools.partial(_tile_rows, n=D_H)
    for ch in range(n_chunks):
        q, k, v, li, a, a_last = load_chunk(ch)
        L = q.shape[0]
        hl = D_H * L
        to_row = lambda col: jnp.broadcast_to(col, (hl, LANE)).T[:hl]
        q_st = _stack_heads(q, D_H, D_HD)
        k_st = _stack_heads(k, D_H, D_HD)
        v_st = _stack_heads(v, D_H, D_HD)
        qk = _mm(q_st, k_st, "nt", M_ATT)
        dlog = jnp.where(incl, a - to_row(a) + to_row(li), -jnp.inf)
        m_intra = jnp.max(dlog, -1, keepdims=True)
        e_end = a_last - a + li
        m_loc = jnp.max(jnp.where(same, to_row(e_end), -jnp.inf), -1, keepdims=True)
        yield
        s_intra = jnp.exp(dlog - m_intra) * qk
        num_intra = _mm(s_intra, v_st, "nn", M_ATT)
        den_intra = jnp.sum(s_intra, -1, keepdims=True)
        w_end = jnp.exp(e_end - m_loc)
        c_loc = _mm(k_st, tile(v) * wide * w_end, "tn", M_ATT)
        n_loc = _mm(ones_bd, w_end * k_st, "nn", M_ATT)
        yield
        g = a + m
        m_t = jnp.maximum(g, m_intra)
        f_inter = jnp.exp(g - m_t)
        f_intra = jnp.exp(m_intra - m_t)
        qc = _mm(q_st, c, "nn", M_ATT)
        num_inter = jnp.concatenate(
            [qc[h * L:(h + 1) * L, h * D_HD:(h + 1) * D_HD] for h in range(D_H)], axis=0)
        den_inter = jnp.sum(q_st * nrow, -1, keepdims=True)
        num = num_intra * f_intra + num_inter * f_inter
        den = f_intra * den_intra + f_inter * den_inter
        h_st = num / jnp.maximum(jnp.abs(den), jnp.exp(-m_t))
        m_new = jnp.maximum(a_last + m, m_loc)
        fo = jnp.exp(a_last + m - m_new)
        fl = jnp.exp(m_loc - m_new)
        per_value = lambda col: jnp.concatenate(
            [jnp.broadcast_to(col[h * L:h * L + 1], (1, D_HD)) for h in range(D_H)], axis=1)
        c = per_value(fo) * c + per_value(fl) * c_loc
        nrow = fo * nrow + fl * n_loc
        m = m_new
        yield
        store_o(ch, _unstack_heads(h_st, D_H, L))
    out["mlstm"] = (c, nrow, m)


(_E_W0, _E_A0, _E_KK, _E_KA, _E_RK, _E_LNW, _E_LNB, _E_BNORM, _E_LB0, _E_LB1) = range(10)


def _even_kernel(p_ref, shift0_ref, srw0_ref, shg0_ref, mu_ref, w2_ref, a2_ref, par_ref,
                 o_ref, shift_ref, srw_ref, shg_ref,
                 prev_sc, srw_sc, shg_sc, kkt_sc, rt_sc, kh_sc, bh_sc, ke_sc, be_sc, v_sc, epos_sc, bonus_sc, oa_sc,
                 qd_sc, ki_sc, kend_sc, bl_sc, ob_sc, xm_sc, m2_sc, o2_sc, arb_sc, kv_sc, bew_sc, *, nb, tb, la, lb):
    j = pl.program_id(1)
    nj = pl.num_programs(1)
    seqs = range(nb)
    cols = lambda lo, hi: jnp.concatenate([p_ref[b, :, lo:hi] for b in seqs], axis=0)

    @pl.when(j == 0)
    def _():
        for b in seqs:
            prev_sc[b] = jnp.broadcast_to(shift0_ref[b], prev_sc.shape[1:])
            srw_sc[b] = jnp.concatenate([srw0_ref[b, h] for h in range(A_H)], axis=1)
            shg_sc[b] = shg0_ref[b].T

    par = par_ref[...]
    row = lambda i: par[i:i + 1]

    per_seq = tb // la
    group = math.gcd(nb * per_seq, 8)
    ngroups = nb * per_seq // group
    grows = group * la
    run = min(group, per_seq)
    chains = group // run
    assert per_seq % run == 0 and (run * la) % lb == 0 and (tb % grows == 0 or grows % tb == 0)
    hper = (run * la) // lb
    late_hgrn = nb == 1 and chains == 1 and ngroups > 1

    def rows_of(lo, hi, r0, r1):
        pieces = [p_ref[b, max(r0, b * tb) - b * tb:min(r1, (b + 1) * tb) - b * tb, lo:hi]
                  for b in seqs if max(r0, b * tb) < min(r1, (b + 1) * tb)]
        return pieces[0] if len(pieces) == 1 else jnp.concatenate(pieces, axis=0)

    def rwkv_token_stages(g):
        r0, r1 = g * grows, (g + 1) * grows
        pa = rows_of(0, A_SHIFT_W, r0, r1)
        rolled = pltpu.roll(pa, 1, axis=0)
        starts = range(r0, r1, tb) if grows % tb == 0 else (r0,)
        span = tb if grows % tb == 0 else grows
        first = _iota(pa.shape, 0) % span == 0
        before = [jnp.broadcast_to(prev_sc[s // tb, 0:1], (span, A_SHIFT_W)) for s in starts]
        shifted = jnp.where(first, before[0] if len(before) == 1 else jnp.concatenate(before, axis=0), rolled)
        for s in starts:
            prev_sc[s // tb] = jnp.broadcast_to(pa[s - r0 + span - 1:s - r0 + span], prev_sc.shape[1:])
        xm = pa + (shifted - pa) * mu_ref[...]
        r = xm[:, :MIX]
        k = xm[:, MIX:2 * MIX]
        v = xm[:, 2 * MIX:3 * MIX]
        lora_in = xm[:, 3 * MIX:]
        yield
        lw = _sigmoid(row(_E_W0) + _dot(jnp.tanh(lora_in).astype(BF16), w2_ref[...])) * (-math.exp(-0.5))
        a = _sigmoid(row(_E_A0) + _dot(lora_in.astype(BF16), a2_ref[...]))
        yield
        kk = k * row(_E_KK)
        kk = kk * lax.rsqrt(jnp.maximum(_seg_sum(kk * kk, A_HD), 1e-12))
        yield
        k = k * (1.0 + (a - 1.0) * row(_E_KA))
        bv = kk * a
        cum_a, ones_a = _chunk_mats(grows, la)
        c = _mm(cum_a, lw, "nn", M_EXACT_R)
        yield
        cl = _mm(ones_a, lw, "nn", M_EXACT_R)
        yield
        e_pos = jnp.exp(c)
        e_neg = jnp.exp(-c)
        e_end = jnp.exp(cl - c)
        kkt_sc[r0:r1, :] = kk * jnp.exp(c - lw)
        rt_sc[r0:r1, :] = r * e_pos
        yield
        kh_sc[r0:r1, :] = k * e_neg
        bh_sc[r0:r1, :] = bv * e_neg
        ke_sc[r0:r1, :] = k * e_end
        be_sc[r0:r1, :] = bv * e_end
        yield
        v_sc[r0:r1, :] = v
        epos_sc[r0:r1, :] = e_pos
        bonus_sc[r0:r1, :] = _seg_sum(r * k * row(_E_RK), A_HD) * v

    wide, _, incl, strict = _head_masks(A_H, la, A_HD)
    hl = A_H * la
    ri, ci_ = _iota((hl, hl), 0), _iota((hl, hl), 1)
    eye = (ri == ci_).astype(F32)
    halves = []
    m = 1
    while m < la:
        halves.append(((ri // (2 * m)) == (ci_ // (2 * m))) & ((ri // m) != (ci_ // m)))
        m *= 2
    consts_a = (wide, incl, strict, eye, halves)

    base = A_SHIFT_W + MIX

    def hgrn_token_stages():
        e0 = row(_E_LB0)
        e1 = row(_E_LB1)
        emax = jnp.maximum(e0, e1)
        e0 = jnp.exp(e0 - emax)
        lower = e0 / (e0 + jnp.exp(e1 - emax))
        g = lower + (1.0 - lower) * jax.nn.sigmoid(cols(base + MIX, base + 2 * MIX))
        yield
        logg = jnp.log(g)
        cum_b, ones_b = _chunk_mats(nb * tb, lb)
        yield
        gb = _mm(cum_b, logg, "nn", M_EXACT_R)
        yield
        gl = _mm(ones_b, logg, "nn", M_EXACT_R)
        yield
        qd_sc[...] = _silu(cols(base, base + MIX)) * (B_HD ** -0.5) * jnp.exp(gb)
        yield
        ki_sc[...] = (1.0 - g) * jnp.exp(-gb)
        yield
        kend_sc[...] = (1.0 - g) * jnp.exp(gl - gb)
        bl_sc[...] = gl

    wide_b, _, incl_b, _ = _head_masks(B_H, lb, B_HD)
    consts_b = (wide_b, incl_b)

    def factor_stages(gi, slot):
        sls = [_rows_at(((gi * group + u) * la, la), la) for u in range(group)]
        hl_a = A_H * la

        def store(u, name, value):
            if name == "m1n":
                xm_sc[slot + u, :hl_a, :] = value
            elif name == "r_w":
                xm_sc[slot + u, hl_a:, :] = value
            else:
                {"m2n": m2_sc, "o2": o2_sc, "a_rb": arb_sc, "kv": kv_sc, "be_w": bew_sc}[name][slot + u] = value

        yield from _rwkv_factor_stages(
            [(kkt_sc[sl, :], rt_sc[sl, :], kh_sc[sl, :], bh_sc[sl, :], ke_sc[sl, :], be_sc[sl, :], v_sc[sl, :])
             for sl in sls], consts_a, store)

    def state_stages(gi):
        slot = (gi % 2) * group
        stages, finish = [], []
        for q in range(chains):
            first = gi * group + q * run
            seq = first // per_seq
            row0 = first * la
            out = {}

            def load_factors(u, q=q, first=first):
                i = slot + q * run + u
                return (xm_sc[i], m2_sc[i], o2_sc[i], arb_sc[i], kv_sc[i], bew_sc[i],
                        epos_sc[pl.ds((first + u) * la + la - 1, 1), :])

            def store_oa(u, o, row0=row0):
                oa_sc[_rows_at((row0 + u * la, la), la), :] = o

            def load_b(u, row0=row0, seq=seq):
                r = row0 + u * lb
                sl = _rows_at((r, lb), lb)
                return (qd_sc[sl, :], ki_sc[sl, :], kend_sc[sl, :], bl_sc[pl.ds(r, 1), :],
                        p_ref[seq, _rows_at((r - seq * tb, lb), lb), base + 2 * MIX:base + 3 * MIX])

            def store_ob(u, o, row0=row0):
                ob_sc[_rows_at((row0 + u * lb, lb), lb), :] = o

            stages.append((_rwkv_state_stages(run, load_factors, store_oa, srw_sc[seq], out), 2 * run + 1))
            if late_hgrn and gi == ngroups - 1:
                load_all = functools.partial(load_b, row0=0)
                store_all = functools.partial(store_ob, row0=0)
                stages.append((_gla_stages(hper * ngroups, load_all, store_all, shg_sc[seq], consts_b, B_H, B_HD,
                                           out, "hgrn"), 2 * hper * ngroups + 1))
            elif not late_hgrn:
                stages.append((_gla_stages(hper, load_b, store_ob, shg_sc[seq], consts_b, B_H, B_HD, out, "hgrn"),
                               2 * hper + 1))
            finish.append((seq, out))

        def write_back():
            for seq, out in finish:
                srw_sc[seq] = out["rwkv"]
                if "hgrn" in out:
                    shg_sc[seq] = out["hgrn"]

        return stages, write_back

    def rwkv_output_stages(g):
        r0, r1 = g * grows, (g + 1) * grows
        o = oa_sc[r0:r1, :]
        mean = _seg_sum(o, A_HD) * (1.0 / A_HD)
        cen = o - mean
        yield
        var = _seg_sum(cen * cen, A_HD) * (1.0 / A_HD)
        yield
        oa = cen * lax.rsqrt(var + A_GN_EPS) * row(_E_LNW) + row(_E_LNB)
        oa = oa + bonus_sc[r0:r1, :]
        o_ref[r0:r1, :MIX] = (oa * _silu(rows_of(A_SHIFT_W, A_SHIFT_W + MIX, r0, r1))).astype(o_ref.dtype)

    def hgrn_output(r0, r1):
        ob = _head_rms(ob_sc[r0:r1, :], row(_E_BNORM), B_HD) * _silu(rows_of(base + 3 * MIX, base + 4 * MIX, r0, r1))
        o_ref[r0:r1, MIX:] = ob.astype(o_ref.dtype)

    for phase in range(ngroups + 3):
        stages, after = [], []
        if 0 <= phase - 2 < ngroups:
            chain_stages, write_back = state_stages(phase - 2)
            stages += chain_stages
            after.append(write_back)
        if 0 <= phase - 1 < ngroups:
            stages.append((factor_stages(phase - 1, ((phase - 1) % 2) * group), 2 * len(halves) + 1))
        if phase < ngroups:
            stages.append((rwkv_token_stages(phase), 8))
        if phase == 0:
            stages.append((hgrn_token_stages(), 7))
        if 0 <= phase - 3 < ngroups:
            stages.append((rwkv_output_stages(phase - 3), 3))
        _interleave(*stages)
        for fn in after:
            fn()
        if not late_hgrn and 0 <= phase - 3 < ngroups:
            hgrn_output((phase - 3) * grows, (phase - 2) * grows)
    if late_hgrn:
        hgrn_output(0, nb * tb)

    @pl.when(j == nj - 1)
    def _():
        for b in seqs:
            shift_ref[b] = prev_sc[b, 0:1]
            s = srw_sc[b]
            for h in range(A_H):
                srw_ref[b, h] = s[:, h * A_HD:(h + 1) * A_HD]
            shg_ref[b] = shg_sc[b].T


def _even_mixer(p, shift0, srw0, shg0, mu, w2p, a2p, par, nb, tb, la, lb):
    bsz, t, _ = p.shape
    kern = functools.partial(_even_kernel, nb=nb, tb=tb, la=la, lb=lb)
    nch, hl = 2 * math.gcd(nb * tb // la, 8), A_H * la
    bmap3 = lambda b, j: (b, 0, 0)
    bmap4 = lambda b, j: (b, 0, 0, 0)
    cmap = lambda b, j: (0, 0)
    blk = lambda: pltpu.VMEM((nb * tb, MIX), F32)
    return pl.pallas_call(
        kern, grid=(bsz // nb, t // tb),
        in_specs=[pl.BlockSpec((nb, tb, IN_EVEN), lambda b, j: (b, j, 0)),
                  pl.BlockSpec((nb, 1, A_SHIFT_W), bmap3),
                  pl.BlockSpec((nb, A_H, A_HD, A_HD), bmap4),
                  pl.BlockSpec((nb, B_H * B_HD, B_HD), bmap3),
                  pl.BlockSpec((1, A_SHIFT_W), cmap),
                  pl.BlockSpec((2 * A_LORA, MIX), cmap),
                  pl.BlockSpec((2 * A_LORA, MIX), cmap),
                  pl.BlockSpec((16, MIX), cmap)],
        out_specs=[pl.BlockSpec((nb * tb, D_MODEL), lambda b, j: (b * (t // tb) + j, 0)),
                   pl.BlockSpec((nb, 1, A_SHIFT_W), bmap3),
                   pl.BlockSpec((nb, A_H, A_HD, A_HD), bmap4),
                   pl.BlockSpec((nb, B_H * B_HD, B_HD), bmap3)],
        out_shape=[jax.ShapeDtypeStruct((bsz * t, D_MODEL), BF16),
                   jax.ShapeDtypeStruct((bsz, 1, A_SHIFT_W), F32),
                   jax.ShapeDtypeStruct((bsz, A_H, A_HD, A_HD), F32),
                   jax.ShapeDtypeStruct((bsz, B_H * B_HD, B_HD), F32)],
        scratch_shapes=[pltpu.VMEM((nb, 8, A_SHIFT_W), F32), pltpu.VMEM((nb, A_HD, MIX), F32),
                        pltpu.VMEM((nb, B_HD, B_H * B_HD), F32)] + [blk() for _ in range(15)] + [
                            pltpu.VMEM((nch, 2 * hl, MIX), BF16), pltpu.VMEM((nch, hl, A_HD), F32),
                            pltpu.VMEM((nch, hl, A_HD), F32), pltpu.VMEM((nch, hl, hl), BF16),
                            pltpu.VMEM((nch, A_HD, MIX), F32), pltpu.VMEM((nch, hl, MIX), BF16)],
        compiler_params=pltpu.CompilerParams(dimension_semantics=("arbitrary", "arbitrary"),
                                             vmem_limit_bytes=VMEM_LIMIT),
        name="mixer_rwkv_hgrn")(p, shift0, srw0, shg0, mu, w2p, a2p, par)


_O_CQ = 0
_O_CK = _O_CQ + C_KW
_O_CV = _O_CK + C_KW
_O_GC = _O_CV + MIX
_O_DQK = _O_GC + MIX
_O_DV = _O_DQK + 2 * MIX
_O_GD = _O_DV + MIX
_O_SLAB = _O_GD + MIX
(_P_CNORM, _P_DNORM, _P_G2B) = range(3)


def _odd_kernel(p_ref, sgl0_ref, conv0_ref, mc0_ref, mn0_ref, mm0_ref, g2_ref, cw_ref, cb_ref, sb_ref, par_ref,
                o_ref, sgl_ref, conv_ref, mc_ref, mn_ref, mm_ref,
                prev_sc, sgl_sc, mc_sc, mn_sc, mm_sc, cq_sc, ck_sc, ce_sc, cl_sc, oc_sc,
                dq_sc, dk_sc, li_sc, fa_sc, fl_sc, od_sc, *, nb, tb, lc):
    j = pl.program_id(1)
    nj = pl.num_programs(1)
    hl = D_H * lc
    seqs = range(nb)
    cols = lambda lo, hi: jnp.concatenate([p_ref[b, :, lo:hi] for b in seqs], axis=0)
    head_rows = lambda x: jnp.concatenate(
        [jnp.broadcast_to(x[h:h + 1], (lc, x.shape[1])) for h in range(D_H)], axis=0)

    @pl.when(j == 0)
    def _():
        for b in seqs:
            prev_sc[b] = conv0_ref[b]
            sgl_sc[b] = sgl0_ref[b].T
            mc_sc[b] = mc0_ref[b].T
            mn_sc[b] = head_rows(mn0_ref[b])
            mm_sc[b] = head_rows(mm0_ref[b])[:, :1]

    par = par_ref[...]
    row = lambda i: par[i:i + 1]

    cum_c, ones_c = _chunk_mats(nb * tb, lc)
    slab = cols(_O_SLAB, _O_SLAB + SLAB)
    pre = _dot(slab.astype(BF16), g2_ref[...]) + row(_P_G2B)[:, :C_KW]
    logg = _log_sigmoid(pre) * (1.0 / C_GATE_NORM)
    gb = _mm(cum_c, logg, "nn", M_EXACT_R)
    gl = _mm(ones_c, logg, "nn", M_EXACT_R)
    ck = cols(_O_CK, _O_CK + C_KW)
    cq_sc[...] = cols(_O_CQ, _O_CQ + C_KW) * (C_KD ** -0.5) * jnp.exp(gb)
    ck_sc[...] = ck * jnp.exp(-gb)
    ce_sc[...] = ck * jnp.exp(gl - gb)
    cl_sc[...] = gl

    wide_c, _, incl_c, _ = _head_masks(C_H, lc, C_KD)
    consts_c = (wide_c, incl_c)

    x = cols(_O_DQK, _O_DQK + 2 * MIX)
    cw = cw_ref[...]
    conv = cb_ref[...] + x * cw[D_CONV - 1:D_CONV]
    head_row = _iota((8, 2 * MIX), 0)
    for s in range(1, D_CONV):
        xr = pltpu.roll(x, s, axis=0)
        pieces = []
        for b in seqs:
            pieces.append(jnp.where(head_row < s, pltpu.roll(prev_sc[b], s, axis=0), xr[b * tb:b * tb + 8]))
            if tb > 8:
                pieces.append(xr[b * tb + 8:(b + 1) * tb])
        xs = pieces[0] if len(pieces) == 1 else jnp.concatenate(pieces, axis=0)
        conv = conv + xs * cw[D_CONV - 1 - s:D_CONV - s]
    for b in seqs:
        prev_sc[b] = x[(b + 1) * tb - 8:(b + 1) * tb]
    conv = _silu(conv)
    dq_sc[...] = conv[:, :MIX]
    dk_sc[...] = conv[:, MIX:] * (D_HD ** -0.5)
    gates = slab + sb_ref[...]
    lf = pltpu.roll(_log_sigmoid(gates), SLAB - (SLAB_F - SLAB_I), axis=1)
    li_sc[...] = gates
    fa_sc[...] = _mm(cum_c, lf, "nn", M_EXACT_R)
    fl_sc[...] = _mm(ones_c, lf, "nn", M_EXACT_R)
    lane = _iota((hl, SLAB), 1)
    head = _iota((hl, SLAB), 0) // lc
    pick = lambda z: jnp.sum(jnp.where(lane == head + SLAB_I, _tile_rows(z, D_H), 0.0), -1, keepdims=True)

    wide_d, same_d, incl_d, _ = _head_masks(D_H, lc, D_HD)
    consts_d = (wide_d, same_d, incl_d, same_d.astype(BF16))

    per_seq = tb // lc
    per_trip = math.gcd(nb * per_seq, 2 if per_seq > 1 else 4)
    run = min(per_trip, per_seq)
    chains = per_trip // run
    assert per_seq % run == 0

    def chunk_body(gi, carry):
        stages, finish = [], []
        for q in range(chains):
            first = gi * per_trip + q * run
            seq = first // per_seq
            out = {}
            rows = lambda u, first=first: _rows_at(((first + u) * lc, lc), lc)
            in_seq = lambda u, first=first, seq=seq: _rows_at(((first + u) * lc - seq * tb, lc), lc)

            def load_c(u, first=first, seq=seq, rows=rows, in_seq=in_seq):
                sl = rows(u)
                return (cq_sc[sl, :], ck_sc[sl, :], ce_sc[sl, :], cl_sc[pl.ds((first + u) * lc, 1), :],
                        p_ref[seq, in_seq(u), _O_CV:_O_CV + MIX])

            def store_c(u, o, rows=rows):
                oc_sc[rows(u), :] = o

            def load_d(u, seq=seq, rows=rows, in_seq=in_seq):
                sl = rows(u)
                return (dq_sc[sl, :], dk_sc[sl, :], p_ref[seq, in_seq(u), _O_DV:_O_DV + MIX],
                        pick(li_sc[sl, :]), pick(fa_sc[sl, :]), pick(fl_sc[sl, :]))

            def store_d(u, o, rows=rows):
                od_sc[rows(u), :] = o

            stages.append((_mlstm_stages(run, load_d, store_d, (mc_sc[seq], mn_sc[seq], mm_sc[seq]), consts_d, out),
                           3 * run + 1))
            stages.append((_gla_stages(run, load_c, store_c, sgl_sc[seq], consts_c, C_H, C_VD, out, "gla"),
                           2 * run + 1))
            finish.append((seq, out))
        _interleave(*stages)
        for seq, out in finish:
            sgl_sc[seq] = out["gla"]
            mc_sc[seq], mn_sc[seq], mm_sc[seq] = out["mlstm"]
        return carry

    lax.fori_loop(0, nb * per_seq // per_trip, chunk_body, 0)

    oc = _head_rms(oc_sc[...], row(_P_CNORM), C_VD) * _silu(cols(_O_GC, _O_GC + MIX))
    od = _head_rms(od_sc[...], row(_P_DNORM), D_HD) * _silu(cols(_O_GD, _O_GD + MIX))
    o_ref[...] = jnp.concatenate([oc, od], axis=1).astype(o_ref.dtype)

    @pl.when(j == nj - 1)
    def _():
        for b in seqs:
            sgl_ref[b] = sgl_sc[b].T
            conv_ref[b] = prev_sc[b]
            mc_ref[b] = mc_sc[b].T
            mn_ref[b] = jnp.concatenate([mn_sc[b, h * lc:h * lc + 1] for h in range(D_H)], axis=0)
            mm_ref[b] = jnp.concatenate(
                [jnp.broadcast_to(mm_sc[b, h * lc:h * lc + 1], (1, LANE)) for h in range(D_H)], axis=0)


def _odd_mixer(p, sgl0, conv0, mc0, mn0, mm0, g2p, cw, cb, sb, par, nb, tb, lc):
    bsz, t, _ = p.shape
    hl = D_H * lc
    kern = functools.partial(_odd_kernel, nb=nb, tb=tb, lc=lc)
    bmap3 = lambda b, j: (b, 0, 0)
    cmap = lambda b, j: (0, 0)
    blk = lambda w: pltpu.VMEM((nb * tb, w), F32)
    return pl.pallas_call(
        kern, grid=(bsz // nb, t // tb),
        in_specs=[pl.BlockSpec((nb, tb, IN_ODD_PAD), lambda b, j: (b, j, 0)),
                  pl.BlockSpec((nb, C_KW, C_VD), bmap3),
                  pl.BlockSpec((nb, 8, 2 * MIX), bmap3),
                  pl.BlockSpec((nb, MIX, D_HD), bmap3),
                  pl.BlockSpec((nb, D_H, D_HD), bmap3),
                  pl.BlockSpec((nb, D_H, LANE), bmap3),
                  pl.BlockSpec((SLAB, C_KW), cmap),
                  pl.BlockSpec((D_CONV, 2 * MIX), cmap),
                  pl.BlockSpec((1, 2 * MIX), cmap),
                  pl.BlockSpec((1, SLAB), cmap),
                  pl.BlockSpec((8, MIX), cmap)],
        out_specs=[pl.BlockSpec((nb * tb, D_MODEL), lambda b, j: (b * (t // tb) + j, 0)),
                   pl.BlockSpec((nb, C_KW, C_VD), bmap3),
                   pl.BlockSpec((nb, 8, 2 * MIX), bmap3),
                   pl.BlockSpec((nb, MIX, D_HD), bmap3),
                   pl.BlockSpec((nb, D_H, D_HD), bmap3),
                   pl.BlockSpec((nb, D_H, LANE), bmap3)],
        out_shape=[jax.ShapeDtypeStruct((bsz * t, D_MODEL), BF16),
                   jax.ShapeDtypeStruct((bsz, C_KW, C_VD), F32),
                   jax.ShapeDtypeStruct((bsz, 8, 2 * MIX), F32),
                   jax.ShapeDtypeStruct((bsz, MIX, D_HD), F32),
                   jax.ShapeDtypeStruct((bsz, D_H, D_HD), F32),
                   jax.ShapeDtypeStruct((bsz, D_H, LANE), F32)],
        scratch_shapes=[pltpu.VMEM((nb, 8, 2 * MIX), F32), pltpu.VMEM((nb, C_VD, C_KW), F32),
                        pltpu.VMEM((nb, D_HD, MIX), F32), pltpu.VMEM((nb, hl, D_HD), F32),
                        pltpu.VMEM((nb, hl, 1), F32),
                        blk(C_KW), blk(C_KW), blk(C_KW), blk(C_KW), blk(MIX),
                        blk(MIX), blk(MIX), blk(SLAB), blk(SLAB), blk(SLAB), blk(MIX)],
        compiler_params=pltpu.CompilerParams(dimension_semantics=("arbitrary", "arbitrary"),
                                             vmem_limit_bytes=VMEM_LIMIT),
        name="mixer_gla_mlstm")(p, sgl0, conv0, mc0, mn0, mm0, g2p, cw, cb, sb, par)


MIXER_ROWS = 256
SHORT_SEQS = 8


def _mixer_blocking(bsz, t):
    tb = math.gcd(t, MIXER_ROWS)
    nb = math.gcd(bsz, SHORT_SEQS) if tb == t and t * SHORT_SEQS <= MIXER_ROWS else 1
    return nb, tb


def _run(x, shift, s_rwkv, s_hgrn, s_gla, conv, mc, mn, mm, wts):
    bsz, t, d = x.shape
    nb, tb = _mixer_blocking(bsz, t)
    la = math.gcd(t, 16)
    lbc = math.gcd(t, 32)
    x2 = x.reshape(bsz * t, d)
    p0 = _inproj(x2, wts["g0"], wts["w_in0"])
    o0, shift_n, srw_n, shg_n = _even_mixer(
        p0.reshape(bsz, t, IN_EVEN), shift[0][:, None], s_rwkv[0], s_hgrn[0].reshape(bsz, B_H * B_HD, B_HD),
        wts["mu"], wts["w2p"], wts["a2p"], wts["par_e"], nb, tb, la, lbc)
    x1, p1 = _mid(o0, x2, wts["w_out0"], wts["g1"], wts["w_in1"])
    conv8 = jnp.pad(conv[0], ((0, 0), (8 - (D_CONV - 1), 0), (0, 0)))
    mm_l = jnp.broadcast_to(mm[0][:, :, None], (bsz, D_H, LANE))
    o1, sgl_n, conv_n, mc_n, mn_n, mm_n = _odd_mixer(
        p1.reshape(bsz, t, IN_ODD_PAD), s_gla[0].reshape(bsz, C_KW, C_VD), conv8,
        mc[0].reshape(bsz, MIX, D_HD), mn[0], mm_l,
        wts["g2p"], wts["cw"], wts["cb"], wts["sb"], wts["par_o"], nb, tb, lbc)
    y = _final(o1, x1, wts["w_out1"], wts["gf"])
    return (y.reshape(bsz, t, d), shift_n.reshape(1, bsz, A_SHIFT_W), srw_n[None],
            shg_n.reshape(1, bsz, B_H, B_HD, B_HD), sgl_n.reshape(1, bsz, C_H, C_KD, C_VD),
            conv_n[None, :, 8 - (D_CONV - 1):], mc_n.reshape(1, bsz, D_H, D_HD, D_HD), mn_n[None],
            mm_n[None, :, :, 0])


def _odd_in_projection(w):
    sizes = (C_KW, C_KW, MIX, C_LORA, MIX, 2 * MIX, MIX, D_H, D_H, MIX)
    off = np.cumsum((0,) + sizes)
    cq, ck, cv, cg, gc, dqk, dv, di, df, gd = (w[:, off[i]:off[i + 1]] for i in range(len(sizes)))
    pad = jnp.zeros((w.shape[0], SLAB - C_LORA - 2 * D_H), w.dtype)
    return jnp.concatenate([cq, ck, cv, gc, dqk, dv, gd, cg, di, df, pad], axis=1)


def kernel(x_prompt, x_sample, state_shift_a, state_rwkv, state_hgrn, state_gla, state_conv_d, state_mlstm_c,
           state_mlstm_n, state_mlstm_m, norm_g, w_in_even, w_out_even, a_mu, a_w0, a_w2, a_a0, a_a2, a_kk,
           a_ka, a_rk, a_ln_w, a_ln_b, b_lb, b_norm, w_in_odd, w_out_odd, c_g2, c_g2b, c_norm, d_conv_w,
           d_conv_b, d_ib, d_fb, d_norm, final_norm):
    assert w_in_even.shape[0] == 1 and w_in_odd.shape[0] == 1 and b_lb.shape[0] == 2
    zpad = lambda a, rows_before, rows_total: jnp.pad(a, ((rows_before, rows_total - rows_before - a.shape[0]), (0, 0)))
    par_e = jnp.concatenate([a_w0, a_a0, a_kk, a_ka, a_rk, a_ln_w, a_ln_b, b_norm, b_lb[0:1], b_lb[1:2]], axis=0)
    g2b = jnp.pad(c_g2b, ((0, 0), (0, MIX - C_KW)))
    par_o = jnp.concatenate([c_norm, d_norm, g2b], axis=0)
    w_in1 = _odd_in_projection(w_in_odd[0].astype(BF16))
    sb = jnp.pad(jnp.concatenate([d_ib, d_fb], axis=1), ((0, 0), (SLAB_I, SLAB - SLAB_I - 2 * D_H)))
    wts = {
        "g0": norm_g[0:1], "g1": norm_g[1:2], "gf": final_norm[None],
        "w_in0": w_in_even[0].astype(BF16), "w_out0": w_out_even[0].astype(BF16),
        "w_in1": w_in1, "w_out1": w_out_odd[0].astype(BF16),
        "mu": a_mu, "w2p": zpad(a_w2[0], 0, 2 * A_LORA).astype(BF16),
        "a2p": zpad(a_a2[0], A_LORA, 2 * A_LORA).astype(BF16),
        "par_e": zpad(par_e, 0, 16), "par_o": zpad(par_o, 0, 8),
        "g2p": zpad(c_g2[0], 0, SLAB).astype(BF16), "cw": d_conv_w[0], "cb": d_conv_b, "sb": sb,
    }
    bp = x_prompt.shape[0]
    z = lambda *s: jnp.zeros(s, x_prompt.dtype)
    prompt = _run(x_prompt, z(1, bp, A_SHIFT_W), z(1, bp, A_H, A_HD, A_HD), z(1, bp, B_H, B_HD, B_HD),
                  z(1, bp, C_H, C_KD, C_VD), z(1, bp, D_CONV - 1, 2 * MIX), z(1, bp, D_H, D_HD, D_HD),
                  z(1, bp, D_H, D_HD), z(1, bp, D_H), wts)
    sample = _run(x_sample, state_shift_a, state_rwkv, state_hgrn, state_gla, state_conv_d, state_mlstm_c,
                  state_mlstm_n, state_mlstm_m, wts)
    return (prompt[0], sample[0]) + prompt[1:] + sample[1:]
```

```python
import functools
import math

import jax
import jax.numpy as jnp
import numpy as np
from jax import lax
from jax.experimental import pallas as pl
from jax.experimental.pallas import tpu as pltpu

F32 = jnp.float32
BF16 = jnp.bfloat16

D_MODEL = 1024
MIX = D_MODEL // 2
NORM_EPS = 1e-5
A_HD = 64
A_H = MIX // A_HD
A_LORA = 64
A_GN_EPS = 64e-5
A_SHIFT_W = 3 * MIX + 2 * A_LORA
B_HD = 128
B_H = MIX // B_HD
IN_EVEN = A_SHIFT_W + 5 * MIX
C_H = 4
C_VD = MIX // C_H
C_KD = C_VD // 2
C_KW = C_H * C_KD
C_LORA = 16
C_GATE_NORM = 16.0
D_H = 4
D_HD = MIX // D_H
D_CONV = 4
LANE = 128
SLAB = LANE
SLAB_I = C_LORA
SLAB_F = C_LORA + D_H
IN_ODD_PAD = 2 * C_KW + 2 * MIX + 2 * MIX + 2 * MIX + SLAB
VMEM_LIMIT = 48 * 1024 * 1024


_DIMS = {"nn": ((1,), (0,)), "nt": ((1,), (1,)), "tn": ((0,), (0,))}
M_EXACT_R = "rx"
M_EXACT_L = "lx"
M_G = "bf"
M_TINV = "bf"
M_ST = "bf"
M_ATT = "bf"


def _split2(x):
    hi = x.astype(BF16)
    return hi, (x - hi.astype(F32)).astype(BF16)


def _mm(a, b, form, mode):
    dn = (_DIMS[form], ((), ()))
    d = lambda x, y: lax.dot_general(x, y, dn, preferred_element_type=F32)
    if mode == "bf":
        return d(a.astype(BF16), b.astype(BF16))
    if mode == "x3":
        ah, al = _split2(a)
        bh, bl = _split2(b)
        return d(ah, bh) + (d(ah, bl) + d(al, bh))
    if mode == "lx":
        bb = b.astype(BF16)
        h, l = _split2(a)
        return d(h, bb) + d(l, bb)
    assert mode == "rx"
    ab = a.astype(BF16)
    h, l = _split2(b)
    return d(ab, h) + d(ab, l)


def _dot(a, b):
    return jnp.dot(a, b, preferred_element_type=F32)


def _iota(shape, dim):
    return lax.broadcasted_iota(jnp.int32, shape, dim)


def _sigmoid(x):
    return 0.5 + 0.5 * jnp.tanh(0.5 * x)


def _log_sigmoid(x):
    return jnp.minimum(x, 0.0) - jnp.log(1.0 + jnp.exp(-jnp.abs(x)))


def _silu(x):
    return x * _sigmoid(x)


def _rms(x, g):
    return x * lax.rsqrt(jnp.mean(x * x, -1, keepdims=True) + NORM_EPS) * g


def _rows_at(start_and_alignment, size):
    start, alignment = start_and_alignment
    if not isinstance(start, int):
        start = pl.multiple_of(start, alignment)
    return pl.ds(start, size)


def _unroll(trips):
    return 2 if trips % 2 == 0 else 1


def _tile_rows(x, n):
    return jnp.concatenate([x] * n, axis=0)


def _stack_heads(x, n_heads, width):
    return jnp.concatenate([x[:, h * width:(h + 1) * width] for h in range(n_heads)], axis=0)


def _unstack_heads(x, n_heads, rows):
    return jnp.concatenate([x[h * rows:(h + 1) * rows] for h in range(n_heads)], axis=1)


def _head_masks(n_heads, rows, width):
    hl = n_heads * rows
    rh = _iota((hl, n_heads * width), 0) // rows
    lh = _iota((hl, n_heads * width), 1) // width
    wide = (rh == lh).astype(F32)
    ri, ci = _iota((hl, hl), 0), _iota((hl, hl), 1)
    same = (ri // rows) == (ci // rows)
    incl = same & (ci <= ri)
    strict = same & (ci < ri)
    return wide, same, incl, strict


def _chunk_mats(n, chunk):
    ri, ci = _iota((n, n), 0), _iota((n, n), 1)
    same = (ri // chunk) == (ci // chunk)
    return (same & (ci <= ri)).astype(BF16), same.astype(BF16)


def _seg_sum(x, seg):
    bd = ((_iota((LANE, LANE), 0) // seg) == (_iota((LANE, LANE), 1) // seg)).astype(BF16)
    parts = [_mm(x[:, j:j + LANE], bd, "nn", M_EXACT_L) for j in range(0, x.shape[1], LANE)]
    return jnp.concatenate(parts, axis=1)


def _head_rms(x, g, width):
    parts = []
    for j in range(0, x.shape[1], width):
        xs = x[:, j:j + width]
        parts.append(xs * lax.rsqrt(jnp.mean(xs * xs, -1, keepdims=True) + NORM_EPS))
    return jnp.concatenate(parts, axis=1) * g


def _inproj_kernel(x_ref, g_ref, w_ref, p_ref):
    h = _rms(x_ref[...], g_ref[...])
    p_ref[...] = _dot(h.astype(BF16), w_ref[...])


def _mid_kernel(o_ref, x_ref, wo_ref, g_ref, wi_ref, x1_ref, p_ref):
    x1 = x_ref[...] + _dot(o_ref[...], wo_ref[...])
    x1_ref[...] = x1
    p_ref[...] = _dot(_rms(x1, g_ref[...]).astype(BF16), wi_ref[...])


def _final_kernel(o_ref, x_ref, wo_ref, g_ref, y_ref):
    x2 = x_ref[...] + _dot(o_ref[...], wo_ref[...])
    y_ref[...] = _rms(x2, g_ref[...])


def _row_tile(m):
    return math.gcd(m, 512)


def _full(shape):
    return pl.BlockSpec(shape, lambda i: (0,) * len(shape))


def _dense_params():
    return pltpu.CompilerParams(dimension_semantics=("arbitrary",), vmem_limit_bytes=VMEM_LIMIT)


def _inproj(x, g, w):
    m, d = x.shape
    n = w.shape[1]
    tm = _row_tile(m)
    return pl.pallas_call(
        _inproj_kernel, grid=(m // tm,),
        in_specs=[pl.BlockSpec((tm, d), lambda i: (i, 0)), _full((1, d)), _full((d, n))],
        out_specs=pl.BlockSpec((tm, n), lambda i: (i, 0)),
        out_shape=jax.ShapeDtypeStruct((m, n), F32),
        compiler_params=_dense_params(), name="inproj")(x, g, w)


def _mid(o, x, wo, g, wi):
    m, d = x.shape
    n = wi.shape[1]
    tm = _row_tile(m)
    return pl.pallas_call(
        _mid_kernel, grid=(m // tm,),
        in_specs=[pl.BlockSpec((tm, d), lambda i: (i, 0)), pl.BlockSpec((tm, d), lambda i: (i, 0)),
                  _full((d, d)), _full((1, d)), _full((d, n))],
        out_specs=[pl.BlockSpec((tm, d), lambda i: (i, 0)), pl.BlockSpec((tm, n), lambda i: (i, 0))],
        out_shape=[jax.ShapeDtypeStruct((m, d), F32), jax.ShapeDtypeStruct((m, n), F32)],
        compiler_params=_dense_params(), name="outproj_inproj")(o, x, wo, g, wi)


def _final(o, x, wo, g):
    m, d = x.shape
    tm = _row_tile(m)
    return pl.pallas_call(
        _final_kernel, grid=(m // tm,),
        in_specs=[pl.BlockSpec((tm, d), lambda i: (i, 0)), pl.BlockSpec((tm, d), lambda i: (i, 0)),
                  _full((d, d)), _full((1, d))],
        out_specs=pl.BlockSpec((tm, d), lambda i: (i, 0)),
        out_shape=jax.ShapeDtypeStruct((m, d), F32),
        compiler_params=_dense_params(), name="outproj_final")(o, x, wo, g)


def _interleave(*staged):
    live = [[gen, 0, max(n, 1)] for gen, n in staged]
    while live:
        item = min(live, key=lambda it: it[1] / it[2])
        try:
            next(item[0])
            item[1] += 1
        except StopIteration:
            live.remove(item)


def _rwkv_factor_stages(chunks, consts, store):
    wide, incl, strict, eye, halves = consts
    L = chunks[0][0].shape[0]
    hl = A_H * L
    n = range(len(chunks))
    tile = functools.partial(_tile_rows, n=A_H)
    for i in n:
        store(i, "r_w", (tile(chunks[i][1]) * wide).astype(BF16))
        store(i, "be_w", (tile(chunks[i][5]) * wide).astype(BF16))
    stack = lambda x: _stack_heads(x, A_H, A_HD)
    g = [_mm(jnp.concatenate([stack(c[0]), stack(c[1])], axis=0),
             jnp.concatenate([stack(c[2]), stack(c[3])], axis=0), "nt", M_G) for c in chunks]
    yield
    a_kk = [jnp.where(strict, x[:hl, :hl], 0.0) for x in g]
    a_kb = [jnp.where(strict, x[:hl, hl:], 0.0) for x in g]
    a_rk = [jnp.where(incl, x[hl:, :hl], 0.0) for x in g]
    for i in n:
        store(i, "a_rb", jnp.where(incl, g[i][hl:, hl:], 0.0).astype(BF16))
    t = [eye - jnp.where(halves[0], x, 0.0) for x in a_kb]
    v_st = [_stack_heads(c[6], A_H, A_HD) for c in chunks]
    av = [_mm(a_kk[i], v_st[i], "nn", M_ST) for i in n]
    for i in n:
        store(i, "o2", _mm(a_rk[i], v_st[i], "nn", M_ST))
        store(i, "kv", _mm(v_st[i], tile(chunks[i][4]) * wide, "tn", M_ST))
    for half in halves[1:]:
        ta = [_mm(t[i], jnp.where(half, a_kb[i], 0.0), "nn", M_TINV) for i in n]
        yield
        t = [t[i] - _mm(ta[i], t[i], "nn", M_TINV) for i in n]
        yield
    for i in n:
        store(i, "m1n", (-_mm(t[i], tile(chunks[i][0]) * wide, "nn", M_ST)).astype(BF16))
        store(i, "m2n", -_mm(t[i], av[i], "nn", M_ST))
    yield


def _rwkv_state_stages(n_chunks, load_factors, store_o, s, out):
    for c in range(n_chunks):
        xm, m2n, o2, a_rb, kv, be_w, ecl = load_factors(c)
        hl = xm.shape[0] // 2
        xs = _mm(xm, s, "nt", M_ST)
        yield
        u = xs[:hl] + m2n
        s = s * ecl + kv + _mm(u, be_w, "tn", M_ST)
        o_st = xs[hl:] + o2 + _mm(a_rb, u, "nn", M_ST)
        yield
        store_o(c, _unstack_heads(o_st, A_H, hl // A_H))
    out["rwkv"] = s


def _gla_stages(n_chunks, load_chunk, store_o, s, consts, n_heads, vd, out, key):
    wide, incl = consts
    tile = functools.partial(_tile_rows, n=n_heads)
    for c in range(n_chunks):
        qd, ki, kend, bl, v = load_chunk(c)
        q_w = tile(qd) * wide
        v_st = _stack_heads(v, n_heads, vd)
        att = jnp.where(incl, _mm(q_w, tile(ki), "nt", M_ATT), 0.0)
        kv = _mm(v_st, tile(kend) * wide, "tn", M_ATT)
        yield
        o_st = _mm(att, v_st, "nn", M_ATT) + _mm(q_w, s, "nt", M_ATT)
        s = s * jnp.exp(bl) + kv
        yield
        store_o(c, _unstack_heads(o_st, n_heads, qd.shape[0]))
    out[key] = s


def _mlstm_stages(n_chunks, load_chunk, store_o, state, consts, out):
    wide, same, incl, ones_bd = consts
    c, nrow, m = state
    tile = functools.partial(_tile_rows, n=D_H)
    for ch in range(n_chunks):
        q, k, v, li, a, a_last = load_chunk(ch)
        L = q.shape[0]
        hl = D_H * L
        to_row = lambda col: jnp.broadcast_to(col, (hl, LANE)).T[:hl]
        q_st = _stack_heads(q, D_H, D_HD)
        k_st = _stack_heads(k, D_H, D_HD)
        v_st = _stack_heads(v, D_H, D_HD)
        qk = _mm(q_st, k_st, "nt", M_ATT)
        dlog = jnp.where(incl, a - to_row(a) + to_row(li), -jnp.inf)
        m_intra = jnp.max(dlog, -1, keepdims=True)
        e_end = a_last - a + li
        m_loc = jnp.max(jnp.where(same, to_row(e_end), -jnp.inf), -1, keepdims=True)
        yield
        s_intra = jnp.exp(dlog - m_intra) * qk
        num_intra = _mm(s_intra, v_st, "nn", M_ATT)
        den_intra = jnp.sum(s_intra, -1, keepdims=True)
        w_end = jnp.exp(e_end - m_loc)
        c_loc = _mm(k_st, tile(v) * wide * w_end, "tn", M_ATT)
        n_loc = _mm(ones_bd, w_end * k_st, "nn", M_ATT)
        yield
        g = a + m
        m_t = jnp.maximum(g, m_intra)
        f_inter = jnp.exp(g - m_t)
        f_intra = jnp.exp(m_intra - m_t)
        qc = _mm(q_st, c, "nn", M_ATT)
        num_inter = jnp.concatenate(
            [qc[h * L:(h + 1) * L, h * D_HD:(h + 1) * D_HD] for h in range(D_H)], axis=0)
        den_inter = jnp.sum(q_st * nrow, -1, keepdims=True)
        num = num_intra * f_intra + num_inter * f_inter
        den = f_intra * den_intra + f_inter * den_inter
        h_st = num / jnp.maximum(jnp.abs(den), jnp.exp(-m_t))
        m_new = jnp.maximum(a_last + m, m_loc)
        fo = jnp.exp(a_last + m - m_new)
        fl = jnp.exp(m_loc - m_new)
        per_value = lambda col: jnp.concatenate(
            [jnp.broadcast_to(col[h * L:h * L + 1], (1, D_HD)) for h in range(D_H)], axis=1)
        c = per_value(fo) * c + per_value(fl) * c_loc
        nrow = fo * nrow + fl * n_loc
        m = m_new
        yield
        store_o(ch, _unstack_heads(h_st, D_H, L))
    out["mlstm"] = (c, nrow, m)


(_E_W0, _E_A0, _E_KK, _E_KA, _E_RK, _E_LNW, _E_LNB, _E_BNORM, _E_LB0, _E_LB1) = range(10)


def _even_kernel(p_ref, shift0_ref, srw0_ref, shg0_ref, mu_ref, w2_ref, a2_ref, par_ref,
                 o_ref, shift_ref, srw_ref, shg_ref,
                 prev_sc, srw_sc, shg_sc, kkt_sc, rt_sc, kh_sc, bh_sc, ke_sc, be_sc, v_sc, epos_sc, bonus_sc, oa_sc,
                 qd_sc, ki_sc, kend_sc, bl_sc, ob_sc, xm_sc, m2_sc, o2_sc, arb_sc, kv_sc, bew_sc, *, nb, tb, la, lb):
    j = pl.program_id(1)
    nj = pl.num_programs(1)
    seqs = range(nb)
    cols = lambda lo, hi: jnp.concatenate([p_ref[b, :, lo:hi] for b in seqs], axis=0)

    @pl.when(j == 0)
    def _():
        for b in seqs:
            prev_sc[b] = jnp.broadcast_to(shift0_ref[b], prev_sc.shape[1:])
            srw_sc[b] = jnp.concatenate([srw0_ref[b, h] for h in range(A_H)], axis=1)
            shg_sc[b] = shg0_ref[b].T

    par = par_ref[...]
    row = lambda i: par[i:i + 1]

    per_seq = tb // la
    group = math.gcd(nb * per_seq, 8)
    chains = nb
    run = group // chains
    ngroups = per_seq // run
    prows = run * la
    assert group % chains == 0 and per_seq % run == 0 and prows % lb == 0
    hper = prows // lb
    late_hgrn = ngroups > 1

    def pieces(g):
        if ngroups == 1:
            return [(0, nb * tb)]
        return [(q * tb + g * prows, q * tb + (g + 1) * prows) for q in seqs]

    def rows_of(lo, hi, r0, r1):
        parts = [p_ref[b, max(r0, b * tb) - b * tb:min(r1, (b + 1) * tb) - b * tb, lo:hi]
                 for b in seqs if max(r0, b * tb) < min(r1, (b + 1) * tb)]
        return parts[0] if len(parts) == 1 else jnp.concatenate(parts, axis=0)

    def rwkv_token_stages(r0, r1):
        grows = r1 - r0
        pa = rows_of(0, A_SHIFT_W, r0, r1)
        rolled = pltpu.roll(pa, 1, axis=0)
        starts = range(r0, r1, tb) if grows % tb == 0 else (r0,)
        span = tb if grows % tb == 0 else grows
        first = _iota(pa.shape, 0) % span == 0
        before = [jnp.broadcast_to(prev_sc[s // tb, 0:1], (span, A_SHIFT_W)) for s in starts]
        shifted = jnp.where(first, before[0] if len(before) == 1 else jnp.concatenate(before, axis=0), rolled)
        for s in starts:
            prev_sc[s // tb] = jnp.broadcast_to(pa[s - r0 + span - 1:s - r0 + span], prev_sc.shape[1:])
        xm = pa + (shifted - pa) * mu_ref[...]
        r = xm[:, :MIX]
        k = xm[:, MIX:2 * MIX]
        v = xm[:, 2 * MIX:3 * MIX]
        lora_in = xm[:, 3 * MIX:]
        yield
        lw = _sigmoid(row(_E_W0) + _dot(jnp.tanh(lora_in).astype(BF16), w2_ref[...])) * (-math.exp(-0.5))
        a = _sigmoid(row(_E_A0) + _dot(lora_in.astype(BF16), a2_ref[...]))
        yield
        kk = k * row(_E_KK)
        kk = kk * lax.rsqrt(jnp.maximum(_seg_sum(kk * kk, A_HD), 1e-12))
        yield
        k = k * (1.0 + (a - 1.0) * row(_E_KA))
        bv = kk * a
        cum_a, ones_a = _chunk_mats(grows, la)
        c = _mm(cum_a, lw, "nn", M_EXACT_R)
        yield
        cl = _mm(ones_a, lw, "nn", M_EXACT_R)
        yield
        e_pos = jnp.exp(c)
        e_neg = jnp.exp(-c)
        e_end = jnp.exp(cl - c)
        kkt_sc[r0:r1, :] = kk * jnp.exp(c - lw)
        rt_sc[r0:r1, :] = r * e_pos
        yield
        kh_sc[r0:r1, :] = k * e_neg
        bh_sc[r0:r1, :] = bv * e_neg
        ke_sc[r0:r1, :] = k * e_end
        be_sc[r0:r1, :] = bv * e_end
        yield
        v_sc[r0:r1, :] = v
        epos_sc[r0:r1, :] = e_pos
        bonus_sc[r0:r1, :] = _seg_sum(r * k * row(_E_RK), A_HD) * v

    wide, _, incl, strict = _head_masks(A_H, la, A_HD)
    hl = A_H * la
    ri, ci_ = _iota((hl, hl), 0), _iota((hl, hl), 1)
    eye = (ri == ci_).astype(F32)
    halves = []
    m = 1
    while m < la:
        halves.append(((ri // (2 * m)) == (ci_ // (2 * m))) & ((ri // m) != (ci_ // m)))
        m *= 2
    consts_a = (wide, incl, strict, eye, halves)

    base = A_SHIFT_W + MIX

    def hgrn_token_stages():
        e0 = row(_E_LB0)
        e1 = row(_E_LB1)
        emax = jnp.maximum(e0, e1)
        e0 = jnp.exp(e0 - emax)
        lower = e0 / (e0 + jnp.exp(e1 - emax))
        g = lower + (1.0 - lower) * jax.nn.sigmoid(cols(base + MIX, base + 2 * MIX))
        yield
        logg = jnp.log(g)
        cum_b, ones_b = _chunk_mats(nb * tb, lb)
        yield
        gb = _mm(cum_b, logg, "nn", M_EXACT_R)
        yield
        gl = _mm(ones_b, logg, "nn", M_EXACT_R)
        yield
        qd_sc[...] = _silu(cols(base, base + MIX)) * (B_HD ** -0.5) * jnp.exp(gb)
        yield
        ki_sc[...] = (1.0 - g) * jnp.exp(-gb)
        yield
        kend_sc[...] = (1.0 - g) * jnp.exp(gl - gb)
        bl_sc[...] = gl

    wide_b, _, incl_b, _ = _head_masks(B_H, lb, B_HD)
    consts_b = (wide_b, incl_b)

    def factor_stages(gi, slot):
        sls = [_rows_at(((q * per_seq + gi * run + u) * la, la), la) for q in seqs for u in range(run)]
        hl_a = A_H * la

        def store(u, name, value):
            if name == "m1n":
                xm_sc[slot + u, :hl_a, :] = value
            elif name == "r_w":
                xm_sc[slot + u, hl_a:, :] = value
            else:
                {"m2n": m2_sc, "o2": o2_sc, "a_rb": arb_sc, "kv": kv_sc, "be_w": bew_sc}[name][slot + u] = value

        yield from _rwkv_factor_stages(
            [(kkt_sc[sl, :], rt_sc[sl, :], kh_sc[sl, :], bh_sc[sl, :], ke_sc[sl, :], be_sc[sl, :], v_sc[sl, :])
             for sl in sls], consts_a, store)

    def state_stages(gi):
        slot = (gi % 2) * group
        stages, finish = [], []
        for q in range(chains):
            seq = q
            first = seq * per_seq + gi * run
            row0 = first * la
            out = {}

            def load_factors(u, q=q, first=first):
                i = slot + q * run + u
                return (xm_sc[i], m2_sc[i], o2_sc[i], arb_sc[i], kv_sc[i], bew_sc[i],
                        epos_sc[pl.ds((first + u) * la + la - 1, 1), :])

            def store_oa(u, o, row0=row0):
                oa_sc[_rows_at((row0 + u * la, la), la), :] = o

            def load_b(u, row0=row0, seq=seq):
                r = row0 + u * lb
                sl = _rows_at((r, lb), lb)
                return (qd_sc[sl, :], ki_sc[sl, :], kend_sc[sl, :], bl_sc[pl.ds(r, 1), :],
                        p_ref[seq, _rows_at((r - seq * tb, lb), lb), base + 2 * MIX:base + 3 * MIX])

            def store_ob(u, o, row0=row0):
                ob_sc[_rows_at((row0 + u * lb, lb), lb), :] = o

            stages.append((_rwkv_state_stages(run, load_factors, store_oa, srw_sc[seq], out), 2 * run + 1))
            if late_hgrn and gi == ngroups - 1:
                load_all = functools.partial(load_b, row0=seq * tb)
                store_all = functools.partial(store_ob, row0=seq * tb)
                stages.append((_gla_stages(hper * ngroups, load_all, store_all, shg_sc[seq], consts_b, B_H, B_HD,
                                           out, "hgrn"), 2 * hper * ngroups + 1))
            elif not late_hgrn:
                stages.append((_gla_stages(hper, load_b, store_ob, shg_sc[seq], consts_b, B_H, B_HD, out, "hgrn"),
                               2 * hper + 1))
            finish.append((seq, out))

        def write_back():
            for seq, out in finish:
                srw_sc[seq] = out["rwkv"]
                if "hgrn" in out:
                    shg_sc[seq] = out["hgrn"]

        return stages, write_back

    def rwkv_output_stages(r0, r1):
        o = oa_sc[r0:r1, :]
        mean = _seg_sum(o, A_HD) * (1.0 / A_HD)
        cen = o - mean
        yield
        var = _seg_sum(cen * cen, A_HD) * (1.0 / A_HD)
        yield
        oa = cen * lax.rsqrt(var + A_GN_EPS) * row(_E_LNW) + row(_E_LNB)
        oa = oa + bonus_sc[r0:r1, :]
        put_o(r0, r1, 0, oa * _silu(rows_of(A_SHIFT_W, A_SHIFT_W + MIX, r0, r1)))

    def put_o(r0, r1, lo, value):
        value = value.astype(o_ref.dtype)
        if len(o_ref.shape) == 2:
            o_ref[r0:r1, lo:lo + MIX] = value
        else:
            for b in seqs:
                a, z = max(r0, b * tb), min(r1, (b + 1) * tb)
                if a < z:
                    o_ref[b, a - b * tb:z - b * tb, lo:lo + MIX] = value[a - r0:z - r0]

    def hgrn_output(r0, r1):
        ob = _head_rms(ob_sc[r0:r1, :], row(_E_BNORM), B_HD) * _silu(rows_of(base + 3 * MIX, base + 4 * MIX, r0, r1))
        put_o(r0, r1, MIX, ob)

    for phase in range(ngroups + 3):
        stages, after = [], []
        if 0 <= phase - 2 < ngroups:
            chain_stages, write_back = state_stages(phase - 2)
            stages += chain_stages
            after.append(write_back)
        if 0 <= phase - 1 < ngroups:
            stages.append((factor_stages(phase - 1, ((phase - 1) % 2) * group), 2 * len(halves) + 1))
        if phase < ngroups:
            stages += [(rwkv_token_stages(r0, r1), 8) for r0, r1 in pieces(phase)]
        if phase == 0:
            stages.append((hgrn_token_stages(), 7))
        if 0 <= phase - 3 < ngroups:
            stages += [(rwkv_output_stages(r0, r1), 3) for r0, r1 in pieces(phase - 3)]
        _interleave(*stages)
        for fn in after:
            fn()
    hgrn_output(0, nb * tb)

    @pl.when(j == nj - 1)
    def _():
        for b in seqs:
            shift_ref[b] = prev_sc[b, 0:1]
            s = srw_sc[b]
            for h in range(A_H):
                srw_ref[b, h] = s[:, h * A_HD:(h + 1) * A_HD]
            shg_ref[b] = shg_sc[b].T


def _even_mixer(p, shift0, srw0, shg0, mu, w2p, a2p, par, nb, tb, la, lb):
    bsz, t, _ = p.shape
    kern = functools.partial(_even_kernel, nb=nb, tb=tb, la=la, lb=lb)
    nch, hl = 2 * math.gcd(nb * tb // la, 8), A_H * la
    bmap3 = lambda b, j: (b, 0, 0)
    bmap4 = lambda b, j: (b, 0, 0, 0)
    cmap = lambda b, j: (0, 0)
    blk = lambda: pltpu.VMEM((nb * tb, MIX), F32)
    if nb == 1 or tb == t:
        o_spec = pl.BlockSpec((nb * tb, D_MODEL), lambda b, j: (b * (t // tb) + j, 0))
        o_shape = jax.ShapeDtypeStruct((bsz * t, D_MODEL), BF16)
    else:
        o_spec = pl.BlockSpec((nb, tb, D_MODEL), lambda b, j: (b, j, 0))
        o_shape = jax.ShapeDtypeStruct((bsz, t, D_MODEL), BF16)
    return pl.pallas_call(
        kern, grid=(bsz // nb, t // tb),
        in_specs=[pl.BlockSpec((nb, tb, IN_EVEN), lambda b, j: (b, j, 0)),
                  pl.BlockSpec((nb, 1, A_SHIFT_W), bmap3),
                  pl.BlockSpec((nb, A_H, A_HD, A_HD), bmap4),
                  pl.BlockSpec((nb, B_H * B_HD, B_HD), bmap3),
                  pl.BlockSpec((1, A_SHIFT_W), cmap),
                  pl.BlockSpec((2 * A_LORA, MIX), cmap),
                  pl.BlockSpec((2 * A_LORA, MIX), cmap),
                  pl.BlockSpec((16, MIX), cmap)],
        out_specs=[o_spec,
                   pl.BlockSpec((nb, 1, A_SHIFT_W), bmap3),
                   pl.BlockSpec((nb, A_H, A_HD, A_HD), bmap4),
                   pl.BlockSpec((nb, B_H * B_HD, B_HD), bmap3)],
        out_shape=[o_shape,
                   jax.ShapeDtypeStruct((bsz, 1, A_SHIFT_W), F32),
                   jax.ShapeDtypeStruct((bsz, A_H, A_HD, A_HD), F32),
                   jax.ShapeDtypeStruct((bsz, B_H * B_HD, B_HD), F32)],
        scratch_shapes=[pltpu.VMEM((nb, 8, A_SHIFT_W), F32), pltpu.VMEM((nb, A_HD, MIX), F32),
                        pltpu.VMEM((nb, B_HD, B_H * B_HD), F32)] + [blk() for _ in range(15)] + [
                            pltpu.VMEM((nch, 2 * hl, MIX), BF16), pltpu.VMEM((nch, hl, A_HD), F32),
                            pltpu.VMEM((nch, hl, A_HD), F32), pltpu.VMEM((nch, hl, hl), BF16),
                            pltpu.VMEM((nch, A_HD, MIX), F32), pltpu.VMEM((nch, hl, MIX), BF16)],
        compiler_params=pltpu.CompilerParams(dimension_semantics=("arbitrary", "arbitrary"),
                                             vmem_limit_bytes=VMEM_LIMIT),
        name="mixer_rwkv_hgrn")(p, shift0, srw0, shg0, mu, w2p, a2p, par)


_O_CQ = 0
_O_CK = _O_CQ + C_KW
_O_CV = _O_CK + C_KW
_O_GC = _O_CV + MIX
_O_DQK = _O_GC + MIX
_O_DV = _O_DQK + 2 * MIX
_O_GD = _O_DV + MIX
_O_SLAB = _O_GD + MIX
(_P_CNORM, _P_DNORM, _P_G2B) = range(3)


def _odd_kernel(p_ref, sgl0_ref, conv0_ref, mc0_ref, mn0_ref, mm0_ref, g2_ref, cw_ref, cb_ref, sb_ref, par_ref,
                o_ref, sgl_ref, conv_ref, mc_ref, mn_ref, mm_ref,
                prev_sc, sgl_sc, mc_sc, mn_sc, mm_sc, cq_sc, ck_sc, ce_sc, cl_sc, oc_sc,
                dq_sc, dk_sc, li_sc, fa_sc, fl_sc, od_sc, *, nb, tb, lc):
    j = pl.program_id(1)
    nj = pl.num_programs(1)
    hl = D_H * lc
    seqs = range(nb)
    cols = lambda lo, hi: jnp.concatenate([p_ref[b, :, lo:hi] for b in seqs], axis=0)
    head_rows = lambda x: jnp.concatenate(
        [jnp.broadcast_to(x[h:h + 1], (lc, x.shape[1])) for h in range(D_H)], axis=0)

    @pl.when(j == 0)
    def _():
        for b in seqs:
            prev_sc[b] = conv0_ref[b]
            sgl_sc[b] = sgl0_ref[b].T
            mc_sc[b] = mc0_ref[b].T
            mn_sc[b] = head_rows(mn0_ref[b])
            mm_sc[b] = head_rows(mm0_ref[b])[:, :1]

    par = par_ref[...]
    row = lambda i: par[i:i + 1]

    cum_c, ones_c = _chunk_mats(nb * tb, lc)
    slab = cols(_O_SLAB, _O_SLAB + SLAB)
    pre = _dot(slab.astype(BF16), g2_ref[...]) + row(_P_G2B)[:, :C_KW]
    logg = _log_sigmoid(pre) * (1.0 / C_GATE_NORM)
    gb = _mm(cum_c, logg, "nn", M_EXACT_R)
    gl = _mm(ones_c, logg, "nn", M_EXACT_R)
    ck = cols(_O_CK, _O_CK + C_KW)
    cq_sc[...] = cols(_O_CQ, _O_CQ + C_KW) * (C_KD ** -0.5) * jnp.exp(gb)
    ck_sc[...] = ck * jnp.exp(-gb)
    ce_sc[...] = ck * jnp.exp(gl - gb)
    cl_sc[...] = gl

    wide_c, _, incl_c, _ = _head_masks(C_H, lc, C_KD)
    consts_c = (wide_c, incl_c)

    x = cols(_O_DQK, _O_DQK + 2 * MIX)
    cw = cw_ref[...]
    conv = cb_ref[...] + x * cw[D_CONV - 1:D_CONV]
    head_row = _iota((8, 2 * MIX), 0)
    for s in range(1, D_CONV):
        xr = pltpu.roll(x, s, axis=0)
        pieces = []
        for b in seqs:
            pieces.append(jnp.where(head_row < s, pltpu.roll(prev_sc[b], s, axis=0), xr[b * tb:b * tb + 8]))
            if tb > 8:
                pieces.append(xr[b * tb + 8:(b + 1) * tb])
        xs = pieces[0] if len(pieces) == 1 else jnp.concatenate(pieces, axis=0)
        conv = conv + xs * cw[D_CONV - 1 - s:D_CONV - s]
    for b in seqs:
        prev_sc[b] = x[(b + 1) * tb - 8:(b + 1) * tb]
    conv = _silu(conv)
    dq_sc[...] = conv[:, :MIX]
    dk_sc[...] = conv[:, MIX:] * (D_HD ** -0.5)
    gates = slab + sb_ref[...]
    lf = pltpu.roll(_log_sigmoid(gates), SLAB - (SLAB_F - SLAB_I), axis=1)
    li_sc[...] = gates
    fa_sc[...] = _mm(cum_c, lf, "nn", M_EXACT_R)
    fl_sc[...] = _mm(ones_c, lf, "nn", M_EXACT_R)
    lane = _iota((hl, SLAB), 1)
    head = _iota((hl, SLAB), 0) // lc
    pick = lambda z: jnp.sum(jnp.where(lane == head + SLAB_I, _tile_rows(z, D_H), 0.0), -1, keepdims=True)

    wide_d, same_d, incl_d, _ = _head_masks(D_H, lc, D_HD)
    consts_d = (wide_d, same_d, incl_d, same_d.astype(BF16))

    per_seq = tb // lc
    per_trip = math.gcd(nb * per_seq, 2 if per_seq > 1 else 4)
    run = min(per_trip, per_seq)
    chains = per_trip // run
    assert per_seq % run == 0

    def chunk_body(gi, carry):
        stages, finish = [], []
        for q in range(chains):
            first = gi * per_trip + q * run
            seq = first // per_seq
            out = {}
            rows = lambda u, first=first: _rows_at(((first + u) * lc, lc), lc)
            in_seq = lambda u, first=first, seq=seq: _rows_at(((first + u) * lc - seq * tb, lc), lc)

            def load_c(u, first=first, seq=seq, rows=rows, in_seq=in_seq):
                sl = rows(u)
                return (cq_sc[sl, :], ck_sc[sl, :], ce_sc[sl, :], cl_sc[pl.ds((first + u) * lc, 1), :],
                        p_ref[seq, in_seq(u), _O_CV:_O_CV + MIX])

            def store_c(u, o, rows=rows):
                oc_sc[rows(u), :] = o

            def load_d(u, seq=seq, rows=rows, in_seq=in_seq):
                sl = rows(u)
                return (dq_sc[sl, :], dk_sc[sl, :], p_ref[seq, in_seq(u), _O_DV:_O_DV + MIX],
                        pick(li_sc[sl, :]), pick(fa_sc[sl, :]), pick(fl_sc[sl, :]))

            def store_d(u, o, rows=rows):
                od_sc[rows(u), :] = o

            stages.append((_mlstm_stages(run, load_d, store_d, (mc_sc[seq], mn_sc[seq], mm_sc[seq]), consts_d, out),
                           3 * run + 1))
            stages.append((_gla_stages(run, load_c, store_c, sgl_sc[seq], consts_c, C_H, C_VD, out, "gla"),
                           2 * run + 1))
            finish.append((seq, out))
        _interleave(*stages)
        for seq, out in finish:
            sgl_sc[seq] = out["gla"]
            mc_sc[seq], mn_sc[seq], mm_sc[seq] = out["mlstm"]
        return carry

    lax.fori_loop(0, nb * per_seq // per_trip, chunk_body, 0)

    oc = _head_rms(oc_sc[...], row(_P_CNORM), C_VD) * _silu(cols(_O_GC, _O_GC + MIX))
    od = _head_rms(od_sc[...], row(_P_DNORM), D_HD) * _silu(cols(_O_GD, _O_GD + MIX))
    o_ref[...] = jnp.concatenate([oc, od], axis=1).astype(o_ref.dtype)

    @pl.when(j == nj - 1)
    def _():
        for b in seqs:
            sgl_ref[b] = sgl_sc[b].T
            conv_ref[b] = prev_sc[b]
            mc_ref[b] = mc_sc[b].T
            mn_ref[b] = jnp.concatenate([mn_sc[b, h * lc:h * lc + 1] for h in range(D_H)], axis=0)
            mm_ref[b] = jnp.concatenate(
                [jnp.broadcast_to(mm_sc[b, h * lc:h * lc + 1], (1, LANE)) for h in range(D_H)], axis=0)


def _odd_mixer(p, sgl0, conv0, mc0, mn0, mm0, g2p, cw, cb, sb, par, nb, tb, lc):
    bsz, t, _ = p.shape
    hl = D_H * lc
    kern = functools.partial(_odd_kernel, nb=nb, tb=tb, lc=lc)
    bmap3 = lambda b, j: (b, 0, 0)
    cmap = lambda b, j: (0, 0)
    blk = lambda w: pltpu.VMEM((nb * tb, w), F32)
    return pl.pallas_call(
        kern, grid=(bsz // nb, t // tb),
        in_specs=[pl.BlockSpec((nb, tb, IN_ODD_PAD), lambda b, j: (b, j, 0)),
                  pl.BlockSpec((nb, C_KW, C_VD), bmap3),
                  pl.BlockSpec((nb, 8, 2 * MIX), bmap3),
                  pl.BlockSpec((nb, MIX, D_HD), bmap3),
                  pl.BlockSpec((nb, D_H, D_HD), bmap3),
                  pl.BlockSpec((nb, D_H, LANE), bmap3),
                  pl.BlockSpec((SLAB, C_KW), cmap),
                  pl.BlockSpec((D_CONV, 2 * MIX), cmap),
                  pl.BlockSpec((1, 2 * MIX), cmap),
                  pl.BlockSpec((1, SLAB), cmap),
                  pl.BlockSpec((8, MIX), cmap)],
        out_specs=[pl.BlockSpec((nb * tb, D_MODEL), lambda b, j: (b * (t // tb) + j, 0)),
                   pl.BlockSpec((nb, C_KW, C_VD), bmap3),
                   pl.BlockSpec((nb, 8, 2 * MIX), bmap3),
                   pl.BlockSpec((nb, MIX, D_HD), bmap3),
                   pl.BlockSpec((nb, D_H, D_HD), bmap3),
                   pl.BlockSpec((nb, D_H, LANE), bmap3)],
        out_shape=[jax.ShapeDtypeStruct((bsz * t, D_MODEL), BF16),
                   jax.ShapeDtypeStruct((bsz, C_KW, C_VD), F32),
                   jax.ShapeDtypeStruct((bsz, 8, 2 * MIX), F32),
                   jax.ShapeDtypeStruct((bsz, MIX, D_HD), F32),
                   jax.ShapeDtypeStruct((bsz, D_H, D_HD), F32),
                   jax.ShapeDtypeStruct((bsz, D_H, LANE), F32)],
        scratch_shapes=[pltpu.VMEM((nb, 8, 2 * MIX), F32), pltpu.VMEM((nb, C_VD, C_KW), F32),
                        pltpu.VMEM((nb, D_HD, MIX), F32), pltpu.VMEM((nb, hl, D_HD), F32),
                        pltpu.VMEM((nb, hl, 1), F32),
                        blk(C_KW), blk(C_KW), blk(C_KW), blk(C_KW), blk(MIX),
                        blk(MIX), blk(MIX), blk(SLAB), blk(SLAB), blk(SLAB), blk(MIX)],
        compiler_params=pltpu.CompilerParams(dimension_semantics=("arbitrary", "arbitrary"),
                                             vmem_limit_bytes=VMEM_LIMIT),
        name="mixer_gla_mlstm")(p, sgl0, conv0, mc0, mn0, mm0, g2p, cw, cb, sb, par)


MIXER_ROWS = 256
SHORT_SEQS = 8


def _mixer_blocking(bsz, t):
    tb = math.gcd(t, MIXER_ROWS)
    nb = math.gcd(bsz, SHORT_SEQS) if tb == t and t * SHORT_SEQS <= MIXER_ROWS else 1
    return nb, tb


def _paired_blocking(bsz, nb, tb):
    if nb == 1 and bsz % 2 == 0 and (tb // 2) % 32 == 0:
        return 2, tb // 2
    return nb, tb


def _run(x, shift, s_rwkv, s_hgrn, s_gla, conv, mc, mn, mm, wts):
    bsz, t, d = x.shape
    nb, tb = _mixer_blocking(bsz, t)
    la = math.gcd(t, 16)
    lbc = math.gcd(t, 32)
    x2 = x.reshape(bsz * t, d)
    p0 = _inproj(x2, wts["g0"], wts["w_in0"])
    o0, shift_n, srw_n, shg_n = _even_mixer(
        p0.reshape(bsz, t, IN_EVEN), shift[0][:, None], s_rwkv[0], s_hgrn[0].reshape(bsz, B_H * B_HD, B_HD),
        wts["mu"], wts["w2p"], wts["a2p"], wts["par_e"], *_paired_blocking(bsz, nb, tb), la, lbc)
    x1, p1 = _mid(o0.reshape(bsz * t, d), x2, wts["w_out0"], wts["g1"], wts["w_in1"])
    conv8 = jnp.pad(conv[0], ((0, 0), (8 - (D_CONV - 1), 0), (0, 0)))
    mm_l = jnp.broadcast_to(mm[0][:, :, None], (bsz, D_H, LANE))
    o1, sgl_n, conv_n, mc_n, mn_n, mm_n = _odd_mixer(
        p1.reshape(bsz, t, IN_ODD_PAD), s_gla[0].reshape(bsz, C_KW, C_VD), conv8,
        mc[0].reshape(bsz, MIX, D_HD), mn[0], mm_l,
        wts["g2p"], wts["cw"], wts["cb"], wts["sb"], wts["par_o"], nb, tb, lbc)
    y = _final(o1, x1, wts["w_out1"], wts["gf"])
    return (y.reshape(bsz, t, d), shift_n.reshape(1, bsz, A_SHIFT_W), srw_n[None],
            shg_n.reshape(1, bsz, B_H, B_HD, B_HD), sgl_n.reshape(1, bsz, C_H, C_KD, C_VD),
            conv_n[None, :, 8 - (D_CONV - 1):], mc_n.reshape(1, bsz, D_H, D_HD, D_HD), mn_n[None],
            mm_n[None, :, :, 0])


def _odd_in_projection(w):
    sizes = (C_KW, C_KW, MIX, C_LORA, MIX, 2 * MIX, MIX, D_H, D_H, MIX)
    off = np.cumsum((0,) + sizes)
    cq, ck, cv, cg, gc, dqk, dv, di, df, gd = (w[:, off[i]:off[i + 1]] for i in range(len(sizes)))
    pad = jnp.zeros((w.shape[0], SLAB - C_LORA - 2 * D_H), w.dtype)
    return jnp.concatenate([cq, ck, cv, gc, dqk, dv, gd, cg, di, df, pad], axis=1)


def kernel(x_prompt, x_sample, state_shift_a, state_rwkv, state_hgrn, state_gla, state_conv_d, state_mlstm_c,
           state_mlstm_n, state_mlstm_m, norm_g, w_in_even, w_out_even, a_mu, a_w0, a_w2, a_a0, a_a2, a_kk,
           a_ka, a_rk, a_ln_w, a_ln_b, b_lb, b_norm, w_in_odd, w_out_odd, c_g2, c_g2b, c_norm, d_conv_w,
           d_conv_b, d_ib, d_fb, d_norm, final_norm):
    assert w_in_even.shape[0] == 1 and w_in_odd.shape[0] == 1 and b_lb.shape[0] == 2
    zpad = lambda a, rows_before, rows_total: jnp.pad(a, ((rows_before, rows_total - rows_before - a.shape[0]), (0, 0)))
    par_e = jnp.concatenate([a_w0, a_a0, a_kk, a_ka, a_rk, a_ln_w, a_ln_b, b_norm, b_lb[0:1], b_lb[1:2]], axis=0)
    g2b = jnp.pad(c_g2b, ((0, 0), (0, MIX - C_KW)))
    par_o = jnp.concatenate([c_norm, d_norm, g2b], axis=0)
    w_in1 = _odd_in_projection(w_in_odd[0].astype(BF16))
    sb = jnp.pad(jnp.concatenate([d_ib, d_fb], axis=1), ((0, 0), (SLAB_I, SLAB - SLAB_I - 2 * D_H)))
    wts = {
        "g0": norm_g[0:1], "g1": norm_g[1:2], "gf": final_norm[None],
        "w_in0": w_in_even[0].astype(BF16), "w_out0": w_out_even[0].astype(BF16),
        "w_in1": w_in1, "w_out1": w_out_odd[0].astype(BF16),
        "mu": a_mu, "w2p": zpad(a_w2[0], 0, 2 * A_LORA).astype(BF16),
        "a2p": zpad(a_a2[0], A_LORA, 2 * A_LORA).astype(BF16),
        "par_e": zpad(par_e, 0, 16), "par_o": zpad(par_o, 0, 8),
        "g2p": zpad(c_g2[0], 0, SLAB).astype(BF16), "cw": d_conv_w[0], "cb": d_conv_b, "sb": sb,
    }
    bp = x_prompt.shape[0]
    z = lambda *s: jnp.zeros(s, x_prompt.dtype)
    prompt = _run(x_prompt, z(1, bp, A_SHIFT_W), z(1, bp, A_H, A_HD, A_HD), z(1, bp, B_H, B_HD, B_HD),
                  z(1, bp, C_H, C_KD, C_VD), z(1, bp, D_CONV - 1, 2 * MIX), z(1, bp, D_H, D_HD, D_HD),
                  z(1, bp, D_H, D_HD), z(1, bp, D_H), wts)
    sample = _run(x_sample, state_shift_a, state_rwkv, state_hgrn, state_gla, state_conv_d, state_mlstm_c,
                  state_mlstm_n, state_mlstm_m, wts)
    return (prompt[0], sample[0]) + prompt[1:] + sample[1:]
```

```python
import functools
import math

import jax
import jax.numpy as jnp
import numpy as np
from jax import lax
from jax.experimental import pallas as pl
from jax.experimental.pallas import tpu as pltpu

F32 = jnp.float32
BF16 = jnp.bfloat16

D_MODEL = 1024
MIX = D_MODEL // 2
NORM_EPS = 1e-5
A_HD = 64
A_H = MIX // A_HD
A_LORA = 64
A_GN_EPS = 64e-5
A_SHIFT_W = 3 * MIX + 2 * A_LORA
B_HD = 128
B_H = MIX // B_HD
IN_EVEN = A_SHIFT_W + 5 * MIX
C_H = 4
C_VD = MIX // C_H
C_KD = C_VD // 2
C_KW = C_H * C_KD
C_LORA = 16
C_GATE_NORM = 16.0
D_H = 4
D_HD = MIX // D_H
D_CONV = 4
LANE = 128
SLAB = LANE
SLAB_I = C_LORA
SLAB_F = C_LORA + D_H
IN_ODD_PAD = 2 * C_KW + 2 * MIX + 2 * MIX + 2 * MIX + SLAB
VMEM_LIMIT = 48 * 1024 * 1024


_DIMS = {"nn": ((1,), (0,)), "nt": ((1,), (1,)), "tn": ((0,), (0,))}
M_EXACT_R = "rx"
M_EXACT_L = "lx"
M_G = "bf"
M_TINV = "bf"
M_ST = "bf"
M_ATT = "bf"


def _split2(x):
    hi = x.astype(BF16)
    return hi, (x - hi.astype(F32)).astype(BF16)


def _mm(a, b, form, mode):
    dn = (_DIMS[form], ((), ()))
    d = lambda x, y: lax.dot_general(x, y, dn, preferred_element_type=F32)
    if mode == "bf":
        return d(a.astype(BF16), b.astype(BF16))
    if mode == "x3":
        ah, al = _split2(a)
        bh, bl = _split2(b)
        return d(ah, bh) + (d(ah, bl) + d(al, bh))
    if mode == "lx":
        bb = b.astype(BF16)
        h, l = _split2(a)
        return d(h, bb) + d(l, bb)
    assert mode == "rx"
    ab = a.astype(BF16)
    h, l = _split2(b)
    return d(ab, h) + d(ab, l)


def _dot(a, b):
    return jnp.dot(a, b, preferred_element_type=F32)


def _iota(shape, dim):
    return lax.broadcasted_iota(jnp.int32, shape, dim)


def _sigmoid(x):
    return 0.5 + 0.5 * jnp.tanh(0.5 * x)


def _log_sigmoid(x):
    return jnp.minimum(x, 0.0) - jnp.log(1.0 + jnp.exp(-jnp.abs(x)))


def _silu(x):
    return x * _sigmoid(x)


def _rms(x, g):
    return x * lax.rsqrt(jnp.mean(x * x, -1, keepdims=True) + NORM_EPS) * g


def _rows_at(start_and_alignment, size):
    start, alignment = start_and_alignment
    if not isinstance(start, int):
        start = pl.multiple_of(start, alignment)
    return pl.ds(start, size)


def _unroll(trips):
    return 2 if trips % 2 == 0 else 1


def _tile_rows(x, n):
    return jnp.concatenate([x] * n, axis=0)


def _stack_heads(x, n_heads, width):
    return jnp.concatenate([x[:, h * width:(h + 1) * width] for h in range(n_heads)], axis=0)


def _unstack_heads(x, n_heads, rows):
    return jnp.concatenate([x[h * rows:(h + 1) * rows] for h in range(n_heads)], axis=1)


def _head_masks(n_heads, rows, width):
    hl = n_heads * rows
    rh = _iota((hl, n_heads * width), 0) // rows
    lh = _iota((hl, n_heads * width), 1) // width
    wide = (rh == lh).astype(F32)
    ri, ci = _iota((hl, hl), 0), _iota((hl, hl), 1)
    same = (ri // rows) == (ci // rows)
    incl = same & (ci <= ri)
    strict = same & (ci < ri)
    return wide, same, incl, strict


def _chunk_mats(n, chunk):
    ri, ci = _iota((n, n), 0), _iota((n, n), 1)
    same = (ri // chunk) == (ci // chunk)
    return (same & (ci <= ri)).astype(BF16), same.astype(BF16)


def _seg_sum(x, seg):
    bd = ((_iota((LANE, LANE), 0) // seg) == (_iota((LANE, LANE), 1) // seg)).astype(BF16)
    parts = [_mm(x[:, j:j + LANE], bd, "nn", M_EXACT_L) for j in range(0, x.shape[1], LANE)]
    return jnp.concatenate(parts, axis=1)


def _head_rms(x, g, width):
    parts = []
    for j in range(0, x.shape[1], width):
        xs = x[:, j:j + width]
        parts.append(xs * lax.rsqrt(jnp.mean(xs * xs, -1, keepdims=True) + NORM_EPS))
    return jnp.concatenate(parts, axis=1) * g


def _inproj_kernel(x_ref, g_ref, w_ref, p_ref):
    h = _rms(x_ref[...], g_ref[...])
    p_ref[...] = _dot(h.astype(BF16), w_ref[...])


def _mid_kernel(o_ref, x_ref, wo_ref, g_ref, wi_ref, x1_ref, p_ref):
    x1 = x_ref[...] + _dot(o_ref[...], wo_ref[...])
    x1_ref[...] = x1
    p_ref[...] = _dot(_rms(x1, g_ref[...]).astype(BF16), wi_ref[...])


def _final_kernel(o_ref, x_ref, wo_ref, g_ref, y_ref):
    x2 = x_ref[...] + _dot(o_ref[...], wo_ref[...])
    y_ref[...] = _rms(x2, g_ref[...])


def _row_tile(m):
    return math.gcd(m, 512)


def _full(shape):
    return pl.BlockSpec(shape, lambda i: (0,) * len(shape))


def _dense_params():
    return pltpu.CompilerParams(dimension_semantics=("arbitrary",), vmem_limit_bytes=VMEM_LIMIT)


def _inproj(x, g, w):
    m, d = x.shape
    n = w.shape[1]
    tm = _row_tile(m)
    return pl.pallas_call(
        _inproj_kernel, grid=(m // tm,),
        in_specs=[pl.BlockSpec((tm, d), lambda i: (i, 0)), _full((1, d)), _full((d, n))],
        out_specs=pl.BlockSpec((tm, n), lambda i: (i, 0)),
        out_shape=jax.ShapeDtypeStruct((m, n), F32),
        compiler_params=_dense_params(), name="inproj")(x, g, w)


def _mid(o, x, wo, g, wi):
    m, d = x.shape
    n = wi.shape[1]
    tm = _row_tile(m)
    return pl.pallas_call(
        _mid_kernel, grid=(m // tm,),
        in_specs=[pl.BlockSpec((tm, d), lambda i: (i, 0)), pl.BlockSpec((tm, d), lambda i: (i, 0)),
                  _full((d, d)), _full((1, d)), _full((d, n))],
        out_specs=[pl.BlockSpec((tm, d), lambda i: (i, 0)), pl.BlockSpec((tm, n), lambda i: (i, 0))],
        out_shape=[jax.ShapeDtypeStruct((m, d), F32), jax.ShapeDtypeStruct((m, n), F32)],
        compiler_params=_dense_params(), name="outproj_inproj")(o, x, wo, g, wi)


def _final(o, x, wo, g):
    m, d = x.shape
    tm = _row_tile(m)
    return pl.pallas_call(
        _final_kernel, grid=(m // tm,),
        in_specs=[pl.BlockSpec((tm, d), lambda i: (i, 0)), pl.BlockSpec((tm, d), lambda i: (i, 0)),
                  _full((d, d)), _full((1, d))],
        out_specs=pl.BlockSpec((tm, d), lambda i: (i, 0)),
        out_shape=jax.ShapeDtypeStruct((m, d), F32),
        compiler_params=_dense_params(), name="outproj_final")(o, x, wo, g)


def _interleave(*staged):
    live = [[gen, 0, max(n, 1)] for gen, n in staged]
    while live:
        item = min(live, key=lambda it: it[1] / it[2])
        try:
            next(item[0])
            item[1] += 1
        except StopIteration:
            live.remove(item)


def _rwkv_factor_stages(chunks, consts, store):
    wide, incl, strict, eye, halves = consts
    L = chunks[0][0].shape[0]
    hl = A_H * L
    n = range(len(chunks))
    tile = functools.partial(_tile_rows, n=A_H)
    for i in n:
        store(i, "r_w", (tile(chunks[i][1]) * wide).astype(BF16))
        store(i, "be_w", (tile(chunks[i][5]) * wide).astype(BF16))
    stack = lambda x: _stack_heads(x, A_H, A_HD)
    g = [_mm(jnp.concatenate([stack(c[0]), stack(c[1])], axis=0),
             jnp.concatenate([stack(c[2]), stack(c[3])], axis=0), "nt", M_G) for c in chunks]
    yield
    a_kk = [jnp.where(strict, x[:hl, :hl], 0.0) for x in g]
    a_kb = [jnp.where(strict, x[:hl, hl:], 0.0) for x in g]
    a_rk = [jnp.where(incl, x[hl:, :hl], 0.0) for x in g]
    for i in n:
        store(i, "a_rb", jnp.where(incl, g[i][hl:, hl:], 0.0).astype(BF16))
    t = [eye - jnp.where(halves[0], x, 0.0) for x in a_kb]
    v_st = [_stack_heads(c[6], A_H, A_HD) for c in chunks]
    av = [_mm(a_kk[i], v_st[i], "nn", M_ST) for i in n]
    for i in n:
        store(i, "o2", _mm(a_rk[i], v_st[i], "nn", M_ST))
        store(i, "kv", _mm(v_st[i], tile(chunks[i][4]) * wide, "tn", M_ST))
    for half in halves[1:]:
        ta = [_mm(t[i], jnp.where(half, a_kb[i], 0.0), "nn", M_TINV) for i in n]
        yield
        t = [t[i] - _mm(ta[i], t[i], "nn", M_TINV) for i in n]
        yield
    for i in n:
        store(i, "m1n", (-_mm(t[i], tile(chunks[i][0]) * wide, "nn", M_ST)).astype(BF16))
        store(i, "m2n", -_mm(t[i], av[i], "nn", M_ST))
    yield


def _rwkv_state_stages(n_chunks, load_factors, store_o, s, out):
    for c in range(n_chunks):
        xm, m2n, o2, a_rb, kv, be_w, ecl = load_factors(c)
        hl = xm.shape[0] // 2
        xs = _mm(xm, s, "nt", M_ST)
        yield
        u = xs[:hl] + m2n
        s = s * ecl + kv + _mm(u, be_w, "tn", M_ST)
        o_st = xs[hl:] + o2 + _mm(a_rb, u, "nn", M_ST)
        yield
        store_o(c, _unstack_heads(o_st, A_H, hl // A_H))
    out["rwkv"] = s


def _gla_stages(n_chunks, load_chunk, store_o, s, consts, n_heads, vd, out, key):
    wide, incl = consts
    tile = functools.partial(_tile_rows, n=n_heads)
    for c in range(n_chunks):
        qd, ki, kend, bl, v = load_chunk(c)
        q_w = tile(qd) * wide
        v_st = _stack_heads(v, n_heads, vd)
        att = jnp.where(incl, _mm(q_w, tile(ki), "nt", M_ATT), 0.0)
        kv = _mm(v_st, tile(kend) * wide, "tn", M_ATT)
        yield
        o_st = _mm(att, v_st, "nn", M_ATT) + _mm(q_w, s, "nt", M_ATT)
        s = s * jnp.exp(bl) + kv
        yield
        store_o(c, _unstack_heads(o_st, n_heads, qd.shape[0]))
    out[key] = s


def _mlstm_stages(n_chunks, load_chunk, store_o, state, consts, out):
    wide, same, incl, ones_bd = consts
    c, nrow, m = state
    tile = functools.partial(_tile_rows, n=D_H)
    for ch in range(n_chunks):
        q, k, v, li, a, a_last = load_chunk(ch)
        L = q.shape[0]
        hl = D_H * L
        to_row = lambda col: jnp.broadcast_to(col, (hl, LANE)).T[:hl]
        q_st = _stack_heads(q, D_H, D_HD)
        k_st = _stack_heads(k, D_H, D_HD)
        v_st = _stack_heads(v, D_H, D_HD)
        qk = _mm(q_st, k_st, "nt", M_ATT)
        dlog = jnp.where(incl, a - to_row(a) + to_row(li), -jnp.inf)
        m_intra = jnp.max(dlog, -1, keepdims=True)
        e_end = a_last - a + li
        m_loc = jnp.max(jnp.where(same, to_row(e_end), -jnp.inf), -1, keepdims=True)
        yield
        s_intra = jnp.exp(dlog - m_intra) * qk
        num_intra = _mm(s_intra, v_st, "nn", M_ATT)
        den_intra = jnp.sum(s_intra, -1, keepdims=True)
        w_end = jnp.exp(e_end - m_loc)
        c_loc = _mm(k_st, tile(v) * wide * w_end, "tn", M_ATT)
        n_loc = _mm(ones_bd, w_end * k_st, "nn", M_ATT)
        yield
        g = a + m
        m_t = jnp.maximum(g, m_intra)
        f_inter = jnp.exp(g - m_t)
        f_intra = jnp.exp(m_intra - m_t)
        qc = _mm(q_st, c, "nn", M_ATT)
        num_inter = jnp.concatenate(
            [qc[h * L:(h + 1) * L, h * D_HD:(h + 1) * D_HD] for h in range(D_H)], axis=0)
        den_inter = jnp.sum(q_st * nrow, -1, keepdims=True)
        num = num_intra * f_intra + num_inter * f_inter
        den = f_intra * den_intra + f_inter * den_inter
        h_st = num / jnp.maximum(jnp.abs(den), jnp.exp(-m_t))
        m_new = jnp.maximum(a_last + m, m_loc)
        fo = jnp.exp(a_last + m - m_new)
        fl = jnp.exp(m_loc - m_new)
        per_value = lambda col: jnp.concatenate(
            [jnp.broadcast_to(col[h * L:h * L + 1], (1, D_HD)) for h in range(D_H)], axis=1)
        c = per_value(fo) * c + per_value(fl) * c_loc
        nrow = fo * nrow + fl * n_loc
        m = m_new
        yield
        store_o(ch, _unstack_heads(h_st, D_H, L))
    out["mlstm"] = (c, nrow, m)


(_E_W0, _E_A0, _E_KK, _E_KA, _E_RK, _E_LNW, _E_LNB, _E_BNORM, _E_LB0, _E_LB1) = range(10)


def _even_kernel(p_ref, shift0_ref, srw0_ref, shg0_ref, mu_ref, w2_ref, a2_ref, par_ref,
                 o_ref, shift_ref, srw_ref, shg_ref,
                 prev_sc, srw_sc, shg_sc, kkt_sc, rt_sc, kh_sc, bh_sc, ke_sc, be_sc, v_sc, epos_sc, bonus_sc, oa_sc,
                 qd_sc, ki_sc, kend_sc, bl_sc, ob_sc, xm_sc, m2_sc, o2_sc, arb_sc, kv_sc, bew_sc, *, nb, tb, la, lb):
    j = pl.program_id(1)
    nj = pl.num_programs(1)
    seqs = range(nb)
    cols = lambda lo, hi: jnp.concatenate([p_ref[b, :, lo:hi] for b in seqs], axis=0)

    @pl.when(j == 0)
    def _():
        for b in seqs:
            prev_sc[b] = jnp.broadcast_to(shift0_ref[b], prev_sc.shape[1:])
            srw_sc[b] = jnp.concatenate([srw0_ref[b, h] for h in range(A_H)], axis=1)
            shg_sc[b] = shg0_ref[b].T

    par = par_ref[...]
    row = lambda i: par[i:i + 1]

    per_seq = tb // la
    group = math.gcd(nb * per_seq, 8)
    chains = nb
    run = group // chains
    ngroups = per_seq // run
    prows = run * la
    assert group % chains == 0 and per_seq % run == 0 and prows % lb == 0
    hper = prows // lb
    late_hgrn = ngroups > 1

    def pieces(g):
        if ngroups == 1:
            return [(0, nb * tb)]
        return [(q * tb + g * prows, q * tb + (g + 1) * prows) for q in seqs]

    def rows_of(lo, hi, r0, r1):
        parts = [p_ref[b, max(r0, b * tb) - b * tb:min(r1, (b + 1) * tb) - b * tb, lo:hi]
                 for b in seqs if max(r0, b * tb) < min(r1, (b + 1) * tb)]
        return parts[0] if len(parts) == 1 else jnp.concatenate(parts, axis=0)

    def rwkv_token_stages(r0, r1):
        grows = r1 - r0
        pa = rows_of(0, A_SHIFT_W, r0, r1)
        rolled = pltpu.roll(pa, 1, axis=0)
        starts = range(r0, r1, tb) if grows % tb == 0 else (r0,)
        span = tb if grows % tb == 0 else grows
        first = _iota(pa.shape, 0) % span == 0
        before = [jnp.broadcast_to(prev_sc[s // tb, 0:1], (span, A_SHIFT_W)) for s in starts]
        shifted = jnp.where(first, before[0] if len(before) == 1 else jnp.concatenate(before, axis=0), rolled)
        for s in starts:
            prev_sc[s // tb] = jnp.broadcast_to(pa[s - r0 + span - 1:s - r0 + span], prev_sc.shape[1:])
        xm = pa + (shifted - pa) * mu_ref[...]
        r = xm[:, :MIX]
        k = xm[:, MIX:2 * MIX]
        v = xm[:, 2 * MIX:3 * MIX]
        lora_in = xm[:, 3 * MIX:]
        yield
        lw = _sigmoid(row(_E_W0) + _dot(jnp.tanh(lora_in).astype(BF16), w2_ref[...])) * (-math.exp(-0.5))
        a = _sigmoid(row(_E_A0) + _dot(lora_in.astype(BF16), a2_ref[...]))
        yield
        kk = k * row(_E_KK)
        kk = kk * lax.rsqrt(jnp.maximum(_seg_sum(kk * kk, A_HD), 1e-12))
        yield
        k = k * (1.0 + (a - 1.0) * row(_E_KA))
        bv = kk * a
        cum_a, ones_a = _chunk_mats(grows, la)
        c = _mm(cum_a, lw, "nn", M_EXACT_R)
        yield
        cl = _mm(ones_a, lw, "nn", M_EXACT_R)
        yield
        e_pos = jnp.exp(c)
        e_neg = jnp.exp(-c)
        e_end = jnp.exp(cl - c)
        kkt_sc[r0:r1, :] = kk * jnp.exp(c - lw)
        rt_sc[r0:r1, :] = r * e_pos
        yield
        kh_sc[r0:r1, :] = k * e_neg
        bh_sc[r0:r1, :] = bv * e_neg
        ke_sc[r0:r1, :] = k * e_end
        be_sc[r0:r1, :] = bv * e_end
        yield
        v_sc[r0:r1, :] = v
        epos_sc[r0:r1, :] = e_pos
        bonus_sc[r0:r1, :] = _seg_sum(r * k * row(_E_RK), A_HD) * v

    wide, _, incl, strict = _head_masks(A_H, la, A_HD)
    hl = A_H * la
    ri, ci_ = _iota((hl, hl), 0), _iota((hl, hl), 1)
    eye = (ri == ci_).astype(F32)
    halves = []
    m = 1
    while m < la:
        halves.append(((ri // (2 * m)) == (ci_ // (2 * m))) & ((ri // m) != (ci_ // m)))
        m *= 2
    consts_a = (wide, incl, strict, eye, halves)

    base = A_SHIFT_W + MIX

    def hgrn_token_stages():
        e0 = row(_E_LB0)
        e1 = row(_E_LB1)
        emax = jnp.maximum(e0, e1)
        e0 = jnp.exp(e0 - emax)
        lower = e0 / (e0 + jnp.exp(e1 - emax))
        g = lower + (1.0 - lower) * jax.nn.sigmoid(cols(base + MIX, base + 2 * MIX))
        yield
        logg = jnp.log(g)
        cum_b, ones_b = _chunk_mats(nb * tb, lb)
        yield
        gb = _mm(cum_b, logg, "nn", M_EXACT_R)
        yield
        gl = _mm(ones_b, logg, "nn", M_EXACT_R)
        yield
        qd_sc[...] = _silu(cols(base, base + MIX)) * (B_HD ** -0.5) * jnp.exp(gb)
        yield
        ki_sc[...] = (1.0 - g) * jnp.exp(-gb)
        yield
        kend_sc[...] = (1.0 - g) * jnp.exp(gl - gb)
        bl_sc[...] = gl

    wide_b, _, incl_b, _ = _head_masks(B_H, lb, B_HD)
    consts_b = (wide_b, incl_b)

    def factor_stages(gi, slot):
        sls = [_rows_at(((q * per_seq + gi * run + u) * la, la), la) for q in seqs for u in range(run)]
        hl_a = A_H * la

        def store(u, name, value):
            if name == "m1n":
                xm_sc[slot + u, :hl_a, :] = value
            elif name == "r_w":
                xm_sc[slot + u, hl_a:, :] = value
            else:
                {"m2n": m2_sc, "o2": o2_sc, "a_rb": arb_sc, "kv": kv_sc, "be_w": bew_sc}[name][slot + u] = value

        yield from _rwkv_factor_stages(
            [(kkt_sc[sl, :], rt_sc[sl, :], kh_sc[sl, :], bh_sc[sl, :], ke_sc[sl, :], be_sc[sl, :], v_sc[sl, :])
             for sl in sls], consts_a, store)

    def state_stages(gi):
        slot = (gi % 2) * group
        stages, finish = [], []
        for q in range(chains):
            seq = q
            first = seq * per_seq + gi * run
            row0 = first * la
            out = {}

            def load_factors(u, q=q, first=first):
                i = slot + q * run + u
                return (xm_sc[i], m2_sc[i], o2_sc[i], arb_sc[i], kv_sc[i], bew_sc[i],
                        epos_sc[pl.ds((first + u) * la + la - 1, 1), :])

            def store_oa(u, o, row0=row0):
                oa_sc[_rows_at((row0 + u * la, la), la), :] = o

            def load_b(u, row0=row0, seq=seq):
                r = row0 + u * lb
                sl = _rows_at((r, lb), lb)
                return (qd_sc[sl, :], ki_sc[sl, :], kend_sc[sl, :], bl_sc[pl.ds(r, 1), :],
                        p_ref[seq, _rows_at((r - seq * tb, lb), lb), base + 2 * MIX:base + 3 * MIX])

            def store_ob(u, o, row0=row0):
                ob_sc[_rows_at((row0 + u * lb, lb), lb), :] = o

            stages.append((_rwkv_state_stages(run, load_factors, store_oa, srw_sc[seq], out), 2 * run + 1))
            if late_hgrn and gi == ngroups - 1:
                load_all = functools.partial(load_b, row0=seq * tb)
                store_all = functools.partial(store_ob, row0=seq * tb)
                stages.append((_gla_stages(hper * ngroups, load_all, store_all, shg_sc[seq], consts_b, B_H, B_HD,
                                           out, "hgrn"), 2 * hper * ngroups + 1))
            elif not late_hgrn:
                stages.append((_gla_stages(hper, load_b, store_ob, shg_sc[seq], consts_b, B_H, B_HD, out, "hgrn"),
                               2 * hper + 1))
            finish.append((seq, out))

        def write_back():
            for seq, out in finish:
                srw_sc[seq] = out["rwkv"]
                if "hgrn" in out:
                    shg_sc[seq] = out["hgrn"]

        return stages, write_back

    def rwkv_output_stages(r0, r1):
        o = oa_sc[r0:r1, :]
        mean = _seg_sum(o, A_HD) * (1.0 / A_HD)
        cen = o - mean
        yield
        var = _seg_sum(cen * cen, A_HD) * (1.0 / A_HD)
        yield
        oa = cen * lax.rsqrt(var + A_GN_EPS) * row(_E_LNW) + row(_E_LNB)
        oa = oa + bonus_sc[r0:r1, :]
        put_o(r0, r1, 0, oa * _silu(rows_of(A_SHIFT_W, A_SHIFT_W + MIX, r0, r1)))

    def put_o(r0, r1, lo, value):
        value = value.astype(o_ref.dtype)
        if len(o_ref.shape) == 2:
            o_ref[r0:r1, lo:lo + MIX] = value
        else:
            for b in seqs:
                a, z = max(r0, b * tb), min(r1, (b + 1) * tb)
                if a < z:
                    o_ref[b, a - b * tb:z - b * tb, lo:lo + MIX] = value[a - r0:z - r0]

    def hgrn_output(r0, r1):
        ob = _head_rms(ob_sc[r0:r1, :], row(_E_BNORM), B_HD) * _silu(rows_of(base + 3 * MIX, base + 4 * MIX, r0, r1))
        put_o(r0, r1, MIX, ob)

    for phase in range(ngroups + 3):
        stages, after = [], []
        if 0 <= phase - 2 < ngroups:
            chain_stages, write_back = state_stages(phase - 2)
            stages += chain_stages
            after.append(write_back)
        if 0 <= phase - 1 < ngroups:
            stages.append((factor_stages(phase - 1, ((phase - 1) % 2) * group), 2 * len(halves) + 1))
        if phase < ngroups:
            stages += [(rwkv_token_stages(r0, r1), 8) for r0, r1 in pieces(phase)]
        if phase == 0:
            stages.append((hgrn_token_stages(), 7))
        if 0 <= phase - 3 < ngroups:
            stages += [(rwkv_output_stages(r0, r1), 3) for r0, r1 in pieces(phase - 3)]
        _interleave(*stages)
        for fn in after:
            fn()
    hgrn_output(0, nb * tb)

    @pl.when(j == nj - 1)
    def _():
        for b in seqs:
            shift_ref[b] = prev_sc[b, 0:1]
            s = srw_sc[b]
            for h in range(A_H):
                srw_ref[b, h] = s[:, h * A_HD:(h + 1) * A_HD]
            shg_ref[b] = shg_sc[b].T


def _mixer_out(bsz, t, nb, tb):
    if nb == 1 or tb == t:
        return (pl.BlockSpec((nb * tb, D_MODEL), lambda b, j: (b * (t // tb) + j, 0)),
                jax.ShapeDtypeStruct((bsz * t, D_MODEL), BF16))
    return (pl.BlockSpec((nb, tb, D_MODEL), lambda b, j: (b, j, 0)),
            jax.ShapeDtypeStruct((bsz, t, D_MODEL), BF16))


def _even_mixer(p, shift0, srw0, shg0, mu, w2p, a2p, par, nb, tb, la, lb):
    bsz, t, _ = p.shape
    kern = functools.partial(_even_kernel, nb=nb, tb=tb, la=la, lb=lb)
    nch, hl = 2 * math.gcd(nb * tb // la, 8), A_H * la
    bmap3 = lambda b, j: (b, 0, 0)
    bmap4 = lambda b, j: (b, 0, 0, 0)
    cmap = lambda b, j: (0, 0)
    blk = lambda: pltpu.VMEM((nb * tb, MIX), F32)
    o_spec, o_shape = _mixer_out(bsz, t, nb, tb)
    return pl.pallas_call(
        kern, grid=(bsz // nb, t // tb),
        in_specs=[pl.BlockSpec((nb, tb, IN_EVEN), lambda b, j: (b, j, 0)),
                  pl.BlockSpec((nb, 1, A_SHIFT_W), bmap3),
                  pl.BlockSpec((nb, A_H, A_HD, A_HD), bmap4),
                  pl.BlockSpec((nb, B_H * B_HD, B_HD), bmap3),
                  pl.BlockSpec((1, A_SHIFT_W), cmap),
                  pl.BlockSpec((2 * A_LORA, MIX), cmap),
                  pl.BlockSpec((2 * A_LORA, MIX), cmap),
                  pl.BlockSpec((16, MIX), cmap)],
        out_specs=[o_spec,
                   pl.BlockSpec((nb, 1, A_SHIFT_W), bmap3),
                   pl.BlockSpec((nb, A_H, A_HD, A_HD), bmap4),
                   pl.BlockSpec((nb, B_H * B_HD, B_HD), bmap3)],
        out_shape=[o_shape,
                   jax.ShapeDtypeStruct((bsz, 1, A_SHIFT_W), F32),
                   jax.ShapeDtypeStruct((bsz, A_H, A_HD, A_HD), F32),
                   jax.ShapeDtypeStruct((bsz, B_H * B_HD, B_HD), F32)],
        scratch_shapes=[pltpu.VMEM((nb, 8, A_SHIFT_W), F32), pltpu.VMEM((nb, A_HD, MIX), F32),
                        pltpu.VMEM((nb, B_HD, B_H * B_HD), F32)] + [blk() for _ in range(15)] + [
                            pltpu.VMEM((nch, 2 * hl, MIX), BF16), pltpu.VMEM((nch, hl, A_HD), F32),
                            pltpu.VMEM((nch, hl, A_HD), F32), pltpu.VMEM((nch, hl, hl), BF16),
                            pltpu.VMEM((nch, A_HD, MIX), F32), pltpu.VMEM((nch, hl, MIX), BF16)],
        compiler_params=pltpu.CompilerParams(dimension_semantics=("arbitrary", "arbitrary"),
                                             vmem_limit_bytes=VMEM_LIMIT),
        name="mixer_rwkv_hgrn")(p, shift0, srw0, shg0, mu, w2p, a2p, par)


_O_CQ = 0
_O_CK = _O_CQ + C_KW
_O_CV = _O_CK + C_KW
_O_GC = _O_CV + MIX
_O_DQK = _O_GC + MIX
_O_DV = _O_DQK + 2 * MIX
_O_GD = _O_DV + MIX
_O_SLAB = _O_GD + MIX
(_P_CNORM, _P_DNORM, _P_G2B) = range(3)


def _odd_kernel(p_ref, sgl0_ref, conv0_ref, mc0_ref, mn0_ref, mm0_ref, g2_ref, cw_ref, cb_ref, sb_ref, par_ref,
                o_ref, sgl_ref, conv_ref, mc_ref, mn_ref, mm_ref,
                prev_sc, sgl_sc, mc_sc, mn_sc, mm_sc, cq_sc, ck_sc, ce_sc, cl_sc, oc_sc,
                dq_sc, dk_sc, li_sc, fa_sc, fl_sc, od_sc, *, nb, tb, lc):
    j = pl.program_id(1)
    nj = pl.num_programs(1)
    hl = D_H * lc
    seqs = range(nb)
    cols = lambda lo, hi: jnp.concatenate([p_ref[b, :, lo:hi] for b in seqs], axis=0)
    head_rows = lambda x: jnp.concatenate(
        [jnp.broadcast_to(x[h:h + 1], (lc, x.shape[1])) for h in range(D_H)], axis=0)

    @pl.when(j == 0)
    def _():
        for b in seqs:
            prev_sc[b] = conv0_ref[b]
            sgl_sc[b] = sgl0_ref[b].T
            mc_sc[b] = mc0_ref[b].T
            mn_sc[b] = head_rows(mn0_ref[b])
            mm_sc[b] = head_rows(mm0_ref[b])[:, :1]

    par = par_ref[...]
    row = lambda i: par[i:i + 1]

    def rows_of(lo, hi, r0, r1):
        parts = [p_ref[b, max(r0, b * tb) - b * tb:min(r1, (b + 1) * tb) - b * tb, lo:hi]
                 for b in seqs if max(r0, b * tb) < min(r1, (b + 1) * tb)]
        return parts[0] if len(parts) == 1 else jnp.concatenate(parts, axis=0)

    def token_stages(r0, r1):
        n = r1 - r0
        span = tb if n % tb == 0 else n
        cum_c, ones_c = _chunk_mats(n, lc)
        slab = rows_of(_O_SLAB, _O_SLAB + SLAB, r0, r1)
        pre = _dot(slab.astype(BF16), g2_ref[...]) + row(_P_G2B)[:, :C_KW]
        logg = _log_sigmoid(pre) * (1.0 / C_GATE_NORM)
        yield
        gb = _mm(cum_c, logg, "nn", M_EXACT_R)
        gl = _mm(ones_c, logg, "nn", M_EXACT_R)
        yield
        ck = rows_of(_O_CK, _O_CK + C_KW, r0, r1)
        cq_sc[r0:r1, :] = rows_of(_O_CQ, _O_CQ + C_KW, r0, r1) * (C_KD ** -0.5) * jnp.exp(gb)
        ck_sc[r0:r1, :] = ck * jnp.exp(-gb)
        ce_sc[r0:r1, :] = ck * jnp.exp(gl - gb)
        cl_sc[r0:r1, :] = gl
        yield
        x = rows_of(_O_DQK, _O_DQK + 2 * MIX, r0, r1)
        cw = cw_ref[...]
        conv = cb_ref[...] + x * cw[D_CONV - 1:D_CONV]
        head_row = _iota((8, 2 * MIX), 0)
        for s in range(1, D_CONV):
            xr = pltpu.roll(x, s, axis=0)
            parts = []
            for a in range(0, n, span):
                kept = pltpu.roll(prev_sc[(r0 + a) // tb], s, axis=0)
                parts.append(jnp.where(head_row < s, kept, xr[a:a + 8]))
                if span > 8:
                    parts.append(xr[a + 8:a + span])
            xs = parts[0] if len(parts) == 1 else jnp.concatenate(parts, axis=0)
            conv = conv + xs * cw[D_CONV - 1 - s:D_CONV - s]
        for a in range(0, n, span):
            prev_sc[(r0 + a) // tb] = x[a + span - 8:a + span]
        yield
        conv = _silu(conv)
        dq_sc[r0:r1, :] = conv[:, :MIX]
        dk_sc[r0:r1, :] = conv[:, MIX:] * (D_HD ** -0.5)
        yield
        gates = slab + sb_ref[...]
        lf = pltpu.roll(_log_sigmoid(gates), SLAB - (SLAB_F - SLAB_I), axis=1)
        li_sc[r0:r1, :] = gates
        fa_sc[r0:r1, :] = _mm(cum_c, lf, "nn", M_EXACT_R)
        fl_sc[r0:r1, :] = _mm(ones_c, lf, "nn", M_EXACT_R)

    wide_c, _, incl_c, _ = _head_masks(C_H, lc, C_KD)
    consts_c = (wide_c, incl_c)
    lane = _iota((hl, SLAB), 1)
    head = _iota((hl, SLAB), 0) // lc
    pick = lambda z: jnp.sum(jnp.where(lane == head + SLAB_I, _tile_rows(z, D_H), 0.0), -1, keepdims=True)

    wide_d, same_d, incl_d, _ = _head_masks(D_H, lc, D_HD)
    consts_d = (wide_d, same_d, incl_d, same_d.astype(BF16))

    per_seq = tb // lc
    if per_seq == 1:
        chains, run = math.gcd(nb, 4), 1
        trips = nb // chains
        first_chunk = lambda gi, q: gi * chains + q
    else:
        chains, run = nb, math.gcd(per_seq, 4)
        trips = per_seq // run
        first_chunk = lambda gi, q: q * per_seq + gi * run
    assert per_seq % run == 0

    def chunk_stages(gi):
        stages, finish = [], []
        for q in range(chains):
            first = first_chunk(gi, q)
            seq = first // per_seq
            out = {}
            rows = lambda u, first=first: _rows_at(((first + u) * lc, lc), lc)
            in_seq = lambda u, first=first, seq=seq: _rows_at(((first + u) * lc - seq * tb, lc), lc)

            def load_c(u, first=first, seq=seq, rows=rows, in_seq=in_seq):
                sl = rows(u)
                return (cq_sc[sl, :], ck_sc[sl, :], ce_sc[sl, :], cl_sc[pl.ds((first + u) * lc, 1), :],
                        p_ref[seq, in_seq(u), _O_CV:_O_CV + MIX])

            def store_c(u, o, rows=rows):
                oc_sc[rows(u), :] = o

            def load_d(u, seq=seq, rows=rows, in_seq=in_seq):
                sl = rows(u)
                return (dq_sc[sl, :], dk_sc[sl, :], p_ref[seq, in_seq(u), _O_DV:_O_DV + MIX],
                        pick(li_sc[sl, :]), pick(fa_sc[sl, :]), pick(fl_sc[sl, :]))

            def store_d(u, o, rows=rows):
                od_sc[rows(u), :] = o

            stages.append((_mlstm_stages(run, load_d, store_d, (mc_sc[seq], mn_sc[seq], mm_sc[seq]), consts_d, out),
                           3 * run + 1))
            stages.append((_gla_stages(run, load_c, store_c, sgl_sc[seq], consts_c, C_H, C_VD, out, "gla"),
                           2 * run + 1))
            finish.append((seq, out))

        def write_back():
            for seq, out in finish:
                sgl_sc[seq] = out["gla"]
                mc_sc[seq], mn_sc[seq], mm_sc[seq] = out["mlstm"]

        return stages, write_back

    def output_stages(r0, r1):
        oc = _head_rms(oc_sc[r0:r1, :], row(_P_CNORM), C_VD) * _silu(rows_of(_O_GC, _O_GC + MIX, r0, r1))
        yield
        od = _head_rms(od_sc[r0:r1, :], row(_P_DNORM), D_HD) * _silu(rows_of(_O_GD, _O_GD + MIX, r0, r1))
        o = jnp.concatenate([oc, od], axis=1).astype(o_ref.dtype)
        if len(o_ref.shape) == 2:
            o_ref[r0:r1, :] = o
        else:
            for b in seqs:
                a, z = max(r0, b * tb), min(r1, (b + 1) * tb)
                if a < z:
                    o_ref[b, a - b * tb:z - b * tb, :] = o[a - r0:z - r0]

    if per_seq == 1 or nb > 1:
        _interleave((token_stages(0, nb * tb), 6))

        def trip(gi, carry):
            stages, write_back = chunk_stages(gi)
            _interleave(*stages)
            write_back()
            return carry

        lax.fori_loop(0, trips, trip, 0)
        _interleave((output_stages(0, nb * tb), 2))
    else:
        grows = run * lc
        for phase in range(trips + 2):
            stages, after = [], []
            if 0 <= phase - 1 < trips:
                chunk, write_back = chunk_stages(phase - 1)
                stages += chunk
                after.append(write_back)
            if phase < trips:
                stages.append((token_stages(phase * grows, (phase + 1) * grows), 6))
            if 0 <= phase - 2 < trips:
                stages.append((output_stages((phase - 2) * grows, (phase - 1) * grows), 2))
            _interleave(*stages)
            for fn in after:
                fn()

    @pl.when(j == nj - 1)
    def _():
        for b in seqs:
            sgl_ref[b] = sgl_sc[b].T
            conv_ref[b] = prev_sc[b]
            mc_ref[b] = mc_sc[b].T
            mn_ref[b] = jnp.concatenate([mn_sc[b, h * lc:h * lc + 1] for h in range(D_H)], axis=0)
            mm_ref[b] = jnp.concatenate(
                [jnp.broadcast_to(mm_sc[b, h * lc:h * lc + 1], (1, LANE)) for h in range(D_H)], axis=0)


def _odd_mixer(p, sgl0, conv0, mc0, mn0, mm0, g2p, cw, cb, sb, par, nb, tb, lc):
    bsz, t, _ = p.shape
    hl = D_H * lc
    kern = functools.partial(_odd_kernel, nb=nb, tb=tb, lc=lc)
    bmap3 = lambda b, j: (b, 0, 0)
    cmap = lambda b, j: (0, 0)
    blk = lambda w: pltpu.VMEM((nb * tb, w), F32)
    o_spec, o_shape = _mixer_out(bsz, t, nb, tb)
    return pl.pallas_call(
        kern, grid=(bsz // nb, t // tb),
        in_specs=[pl.BlockSpec((nb, tb, IN_ODD_PAD), lambda b, j: (b, j, 0)),
                  pl.BlockSpec((nb, C_KW, C_VD), bmap3),
                  pl.BlockSpec((nb, 8, 2 * MIX), bmap3),
                  pl.BlockSpec((nb, MIX, D_HD), bmap3),
                  pl.BlockSpec((nb, D_H, D_HD), bmap3),
                  pl.BlockSpec((nb, D_H, LANE), bmap3),
                  pl.BlockSpec((SLAB, C_KW), cmap),
                  pl.BlockSpec((D_CONV, 2 * MIX), cmap),
                  pl.BlockSpec((1, 2 * MIX), cmap),
                  pl.BlockSpec((1, SLAB), cmap),
                  pl.BlockSpec((8, MIX), cmap)],
        out_specs=[o_spec,
                   pl.BlockSpec((nb, C_KW, C_VD), bmap3),
                   pl.BlockSpec((nb, 8, 2 * MIX), bmap3),
                   pl.BlockSpec((nb, MIX, D_HD), bmap3),
                   pl.BlockSpec((nb, D_H, D_HD), bmap3),
                   pl.BlockSpec((nb, D_H, LANE), bmap3)],
        out_shape=[o_shape,
                   jax.ShapeDtypeStruct((bsz, C_KW, C_VD), F32),
                   jax.ShapeDtypeStruct((bsz, 8, 2 * MIX), F32),
                   jax.ShapeDtypeStruct((bsz, MIX, D_HD), F32),
                   jax.ShapeDtypeStruct((bsz, D_H, D_HD), F32),
                   jax.ShapeDtypeStruct((bsz, D_H, LANE), F32)],
        scratch_shapes=[pltpu.VMEM((nb, 8, 2 * MIX), F32), pltpu.VMEM((nb, C_VD, C_KW), F32),
                        pltpu.VMEM((nb, D_HD, MIX), F32), pltpu.VMEM((nb, hl, D_HD), F32),
                        pltpu.VMEM((nb, hl, 1), F32),
                        blk(C_KW), blk(C_KW), blk(C_KW), blk(C_KW), blk(MIX),
                        blk(MIX), blk(MIX), blk(SLAB), blk(SLAB), blk(SLAB), blk(MIX)],
        compiler_params=pltpu.CompilerParams(dimension_semantics=("arbitrary", "arbitrary"),
                                             vmem_limit_bytes=VMEM_LIMIT),
        name="mixer_gla_mlstm")(p, sgl0, conv0, mc0, mn0, mm0, g2p, cw, cb, sb, par)


MIXER_ROWS = 256
SHORT_SEQS = 8


def _mixer_blocking(bsz, t):
    tb = math.gcd(t, MIXER_ROWS)
    nb = math.gcd(bsz, SHORT_SEQS) if tb == t and t * SHORT_SEQS <= MIXER_ROWS else 1
    return nb, tb


def _paired_blocking(bsz, nb, tb):
    if nb == 1 and bsz % 2 == 0 and (tb // 2) % 32 == 0:
        return 2, tb // 2
    return nb, tb


def _run(x, shift, s_rwkv, s_hgrn, s_gla, conv, mc, mn, mm, wts):
    bsz, t, d = x.shape
    nb, tb = _mixer_blocking(bsz, t)
    la = math.gcd(t, 16)
    lbc = math.gcd(t, 32)
    x2 = x.reshape(bsz * t, d)
    p0 = _inproj(x2, wts["g0"], wts["w_in0"])
    o0, shift_n, srw_n, shg_n = _even_mixer(
        p0.reshape(bsz, t, IN_EVEN), shift[0][:, None], s_rwkv[0], s_hgrn[0].reshape(bsz, B_H * B_HD, B_HD),
        wts["mu"], wts["w2p"], wts["a2p"], wts["par_e"], *_paired_blocking(bsz, nb, tb), la, lbc)
    x1, p1 = _mid(o0.reshape(bsz * t, d), x2, wts["w_out0"], wts["g1"], wts["w_in1"])
    conv8 = jnp.pad(conv[0], ((0, 0), (8 - (D_CONV - 1), 0), (0, 0)))
    mm_l = jnp.broadcast_to(mm[0][:, :, None], (bsz, D_H, LANE))
    o1, sgl_n, conv_n, mc_n, mn_n, mm_n = _odd_mixer(
        p1.reshape(bsz, t, IN_ODD_PAD), s_gla[0].reshape(bsz, C_KW, C_VD), conv8,
        mc[0].reshape(bsz, MIX, D_HD), mn[0], mm_l,
        wts["g2p"], wts["cw"], wts["cb"], wts["sb"], wts["par_o"], nb, tb, lbc)
    y = _final(o1.reshape(bsz * t, d), x1, wts["w_out1"], wts["gf"])
    return (y.reshape(bsz, t, d), shift_n.reshape(1, bsz, A_SHIFT_W), srw_n[None],
            shg_n.reshape(1, bsz, B_H, B_HD, B_HD), sgl_n.reshape(1, bsz, C_H, C_KD, C_VD),
            conv_n[None, :, 8 - (D_CONV - 1):], mc_n.reshape(1, bsz, D_H, D_HD, D_HD), mn_n[None],
            mm_n[None, :, :, 0])


def _odd_in_projection(w):
    sizes = (C_KW, C_KW, MIX, C_LORA, MIX, 2 * MIX, MIX, D_H, D_H, MIX)
    off = np.cumsum((0,) + sizes)
    cq, ck, cv, cg, gc, dqk, dv, di, df, gd = (w[:, off[i]:off[i + 1]] for i in range(len(sizes)))
    pad = jnp.zeros((w.shape[0], SLAB - C_LORA - 2 * D_H), w.dtype)
    return jnp.concatenate([cq, ck, cv, gc, dqk, dv, gd, cg, di, df, pad], axis=1)


def kernel(x_prompt, x_sample, state_shift_a, state_rwkv, state_hgrn, state_gla, state_conv_d, state_mlstm_c,
           state_mlstm_n, state_mlstm_m, norm_g, w_in_even, w_out_even, a_mu, a_w0, a_w2, a_a0, a_a2, a_kk,
           a_ka, a_rk, a_ln_w, a_ln_b, b_lb, b_norm, w_in_odd, w_out_odd, c_g2, c_g2b, c_norm, d_conv_w,
           d_conv_b, d_ib, d_fb, d_norm, final_norm):
    assert w_in_even.shape[0] == 1 and w_in_odd.shape[0] == 1 and b_lb.shape[0] == 2
    zpad = lambda a, rows_before, rows_total: jnp.pad(a, ((rows_before, rows_total - rows_before - a.shape[0]), (0, 0)))
    par_e = jnp.concatenate([a_w0, a_a0, a_kk, a_ka, a_rk, a_ln_w, a_ln_b, b_norm, b_lb[0:1], b_lb[1:2]], axis=0)
    g2b = jnp.pad(c_g2b, ((0, 0), (0, MIX - C_KW)))
    par_o = jnp.concatenate([c_norm, d_norm, g2b], axis=0)
    w_in1 = _odd_in_projection(w_in_odd[0].astype(BF16))
    sb = jnp.pad(jnp.concatenate([d_ib, d_fb], axis=1), ((0, 0), (SLAB_I, SLAB - SLAB_I - 2 * D_H)))
    wts = {
        "g0": norm_g[0:1], "g1": norm_g[1:2], "gf": final_norm[None],
        "w_in0": w_in_even[0].astype(BF16), "w_out0": w_out_even[0].astype(BF16),
        "w_in1": w_in1, "w_out1": w_out_odd[0].astype(BF16),
        "mu": a_mu, "w2p": zpad(a_w2[0], 0, 2 * A_LORA).astype(BF16),
        "a2p": zpad(a_a2[0], A_LORA, 2 * A_LORA).astype(BF16),
        "par_e": zpad(par_e, 0, 16), "par_o": zpad(par_o, 0, 8),
        "g2p": zpad(c_g2[0], 0, SLAB).astype(BF16), "cw": d_conv_w[0], "cb": d_conv_b, "sb": sb,
    }
    bp = x_prompt.shape[0]
    z = lambda *s: jnp.zeros(s, x_prompt.dtype)
    prompt = _run(x_prompt, z(1, bp, A_SHIFT_W), z(1, bp, A_H, A_HD, A_HD), z(1, bp, B_H, B_HD, B_HD),
                  z(1, bp, C_H, C_KD, C_VD), z(1, bp, D_CONV - 1, 2 * MIX), z(1, bp, D_H, D_HD, D_HD),
                  z(1, bp, D_H, D_HD), z(1, bp, D_H), wts)
    sample = _run(x_sample, state_shift_a, state_rwkv, state_hgrn, state_gla, state_conv_d, state_mlstm_c,
                  state_mlstm_n, state_mlstm_m, wts)
    return (prompt[0], sample[0]) + prompt[1:] + sample[1:]
```

```python
import functools
import math

import jax
import jax.numpy as jnp
import numpy as np
from jax import lax
from jax.experimental import pallas as pl
from jax.experimental.pallas import tpu as pltpu

F32 = jnp.float32
BF16 = jnp.bfloat16

D_MODEL = 1024
MIX = D_MODEL // 2
NORM_EPS = 1e-5
A_HD = 64
A_H = MIX // A_HD
A_LORA = 64
A_GN_EPS = 64e-5
A_SHIFT_W = 3 * MIX + 2 * A_LORA
B_HD = 128
B_H = MIX // B_HD
IN_EVEN = A_SHIFT_W + 5 * MIX
C_H = 4
C_VD = MIX // C_H
C_KD = C_VD // 2
C_KW = C_H * C_KD
C_LORA = 16
C_GATE_NORM = 16.0
D_H = 4
D_HD = MIX // D_H
D_CONV = 4
LANE = 128
SLAB = LANE
SLAB_I = C_LORA
SLAB_F = C_LORA + D_H
IN_ODD_PAD = 2 * C_KW + 2 * MIX + 2 * MIX + 2 * MIX + SLAB
VMEM_LIMIT = 48 * 1024 * 1024


_DIMS = {"nn": ((1,), (0,)), "nt": ((1,), (1,)), "tn": ((0,), (0,))}
M_EXACT_R = "rx"
M_EXACT_L = "lx"
M_G = "bf"
M_TINV = "bf"
M_ST = "bf"
M_ATT = "bf"


def _split2(x):
    hi = x.astype(BF16)
    return hi, (x - hi.astype(F32)).astype(BF16)


def _mm(a, b, form, mode):
    dn = (_DIMS[form], ((), ()))
    d = lambda x, y: lax.dot_general(x, y, dn, preferred_element_type=F32)
    if mode == "bf":
        return d(a.astype(BF16), b.astype(BF16))
    if mode == "x3":
        ah, al = _split2(a)
        bh, bl = _split2(b)
        return d(ah, bh) + (d(ah, bl) + d(al, bh))
    if mode == "lx":
        bb = b.astype(BF16)
        h, l = _split2(a)
        return d(h, bb) + d(l, bb)
    assert mode == "rx"
    ab = a.astype(BF16)
    h, l = _split2(b)
    return d(ab, h) + d(ab, l)


def _dot(a, b):
    return jnp.dot(a, b, preferred_element_type=F32)


def _iota(shape, dim):
    return lax.broadcasted_iota(jnp.int32, shape, dim)


def _sigmoid(x):
    return 0.5 + 0.5 * jnp.tanh(0.5 * x)


def _log_sigmoid(x):
    return jnp.minimum(x, 0.0) - jnp.log(1.0 + jnp.exp(-jnp.abs(x)))


def _silu(x):
    return x * _sigmoid(x)


def _rms(x, g):
    return x * lax.rsqrt(jnp.mean(x * x, -1, keepdims=True) + NORM_EPS) * g


def _rows_at(start_and_alignment, size):
    start, alignment = start_and_alignment
    if not isinstance(start, int):
        start = pl.multiple_of(start, alignment)
    return pl.ds(start, size)


def _unroll(trips):
    return 2 if trips % 2 == 0 else 1


def _tile_rows(x, n):
    return jnp.concatenate([x] * n, axis=0)


def _stack_heads(x, n_heads, width):
    return jnp.concatenate([x[:, h * width:(h + 1) * width] for h in range(n_heads)], axis=0)


def _unstack_heads(x, n_heads, rows):
    return jnp.concatenate([x[h * rows:(h + 1) * rows] for h in range(n_heads)], axis=1)


def _head_masks(n_heads, rows, width):
    hl = n_heads * rows
    rh = _iota((hl, n_heads * width), 0) // rows
    lh = _iota((hl, n_heads * width), 1) // width
    wide = (rh == lh).astype(F32)
    ri, ci = _iota((hl, hl), 0), _iota((hl, hl), 1)
    same = (ri // rows) == (ci // rows)
    incl = same & (ci <= ri)
    strict = same & (ci < ri)
    return wide, same, incl, strict


def _chunk_mats(n, chunk):
    ri, ci = _iota((n, n), 0), _iota((n, n), 1)
    same = (ri // chunk) == (ci // chunk)
    return (same & (ci <= ri)).astype(BF16), same.astype(BF16)


def _seg_sum(x, seg):
    bd = ((_iota((LANE, LANE), 0) // seg) == (_iota((LANE, LANE), 1) // seg)).astype(BF16)
    parts = [_mm(x[:, j:j + LANE], bd, "nn", M_EXACT_L) for j in range(0, x.shape[1], LANE)]
    return jnp.concatenate(parts, axis=1)


def _head_rms(x, g, width):
    parts = []
    for j in range(0, x.shape[1], width):
        xs = x[:, j:j + width]
        parts.append(xs * lax.rsqrt(jnp.mean(xs * xs, -1, keepdims=True) + NORM_EPS))
    return jnp.concatenate(parts, axis=1) * g


def _inproj_kernel(x_ref, g_ref, w_ref, p_ref):
    h = _rms(x_ref[...], g_ref[...])
    p_ref[...] = _dot(h.astype(BF16), w_ref[...])


def _mid_kernel(o_ref, x_ref, wo_ref, g_ref, wi_ref, x1_ref, p_ref):
    x1 = x_ref[...] + _dot(o_ref[...], wo_ref[...])
    x1_ref[...] = x1
    p_ref[...] = _dot(_rms(x1, g_ref[...]).astype(BF16), wi_ref[...])


def _row_tile(m):
    return math.gcd(m, 512)


def _full(shape):
    return pl.BlockSpec(shape, lambda i: (0,) * len(shape))


def _dense_params():
    return pltpu.CompilerParams(dimension_semantics=("arbitrary",), vmem_limit_bytes=VMEM_LIMIT)


def _inproj(x, g, w):
    m, d = x.shape
    n = w.shape[1]
    tm = _row_tile(m)
    return pl.pallas_call(
        _inproj_kernel, grid=(m // tm,),
        in_specs=[pl.BlockSpec((tm, d), lambda i: (i, 0)), _full((1, d)), _full((d, n))],
        out_specs=pl.BlockSpec((tm, n), lambda i: (i, 0)),
        out_shape=jax.ShapeDtypeStruct((m, n), F32),
        compiler_params=_dense_params(), name="inproj")(x, g, w)


def _mid(o, x, wo, g, wi):
    m, d = x.shape
    n = wi.shape[1]
    tm = _row_tile(m)
    return pl.pallas_call(
        _mid_kernel, grid=(m // tm,),
        in_specs=[pl.BlockSpec((tm, d), lambda i: (i, 0)), pl.BlockSpec((tm, d), lambda i: (i, 0)),
                  _full((d, d)), _full((1, d)), _full((d, n))],
        out_specs=[pl.BlockSpec((tm, d), lambda i: (i, 0)), pl.BlockSpec((tm, n), lambda i: (i, 0))],
        out_shape=[jax.ShapeDtypeStruct((m, d), F32), jax.ShapeDtypeStruct((m, n), F32)],
        compiler_params=_dense_params(), name="outproj_inproj")(o, x, wo, g, wi)


def _interleave(*staged):
    live = [[gen, 0, max(n, 1)] for gen, n in staged]
    while live:
        item = min(live, key=lambda it: it[1] / it[2])
        try:
            next(item[0])
            item[1] += 1
        except StopIteration:
            live.remove(item)


def _rwkv_factor_stages(chunks, consts, store):
    wide, incl, strict, eye, halves = consts
    L = chunks[0][0].shape[0]
    hl = A_H * L
    n = range(len(chunks))
    tile = functools.partial(_tile_rows, n=A_H)
    for i in n:
        store(i, "r_w", (tile(chunks[i][1]) * wide).astype(BF16))
        store(i, "be_w", (tile(chunks[i][5]) * wide).astype(BF16))
    stack = lambda x: _stack_heads(x, A_H, A_HD)
    g = [_mm(jnp.concatenate([stack(c[0]), stack(c[1])], axis=0),
             jnp.concatenate([stack(c[2]), stack(c[3])], axis=0), "nt", M_G) for c in chunks]
    yield
    a_kk = [jnp.where(strict, x[:hl, :hl], 0.0) for x in g]
    a_kb = [jnp.where(strict, x[:hl, hl:], 0.0) for x in g]
    a_rk = [jnp.where(incl, x[hl:, :hl], 0.0) for x in g]
    for i in n:
        store(i, "a_rb", jnp.where(incl, g[i][hl:, hl:], 0.0).astype(BF16))
    t = [eye - jnp.where(halves[0], x, 0.0) for x in a_kb]
    v_st = [_stack_heads(c[6], A_H, A_HD) for c in chunks]
    av = [_mm(a_kk[i], v_st[i], "nn", M_ST) for i in n]
    for i in n:
        store(i, "o2", _mm(a_rk[i], v_st[i], "nn", M_ST))
        store(i, "kv", _mm(v_st[i], tile(chunks[i][4]) * wide, "tn", M_ST))
    for half in halves[1:]:
        ta = [_mm(t[i], jnp.where(half, a_kb[i], 0.0), "nn", M_TINV) for i in n]
        yield
        t = [t[i] - _mm(ta[i], t[i], "nn", M_TINV) for i in n]
        yield
    for i in n:
        store(i, "m1n", (-_mm(t[i], tile(chunks[i][0]) * wide, "nn", M_ST)).astype(BF16))
        store(i, "m2n", -_mm(t[i], av[i], "nn", M_ST))
    yield


def _rwkv_state_stages(n_chunks, load_factors, store_o, s, out):
    for c in range(n_chunks):
        xm, m2n, o2, a_rb, kv, be_w, ecl = load_factors(c)
        hl = xm.shape[0] // 2
        xs = _mm(xm, s, "nt", M_ST)
        yield
        u = xs[:hl] + m2n
        s = s * ecl + kv + _mm(u, be_w, "tn", M_ST)
        o_st = xs[hl:] + o2 + _mm(a_rb, u, "nn", M_ST)
        yield
        store_o(c, _unstack_heads(o_st, A_H, hl // A_H))
    out["rwkv"] = s


def _gla_stages(n_chunks, load_chunk, store_o, s, consts, n_heads, vd, out, key):
    wide, incl = consts
    tile = functools.partial(_tile_rows, n=n_heads)
    for c in range(n_chunks):
        qd, ki, kend, bl, v = load_chunk(c)
        q_w = tile(qd) * wide
        v_st = _stack_heads(v, n_heads, vd)
        att = jnp.where(incl, _mm(q_w, tile(ki), "nt", M_ATT), 0.0)
        kv = _mm(v_st, tile(kend) * wide, "tn", M_ATT)
        yield
        o_st = _mm(att, v_st, "nn", M_ATT) + _mm(q_w, s, "nt", M_ATT)
        s = s * jnp.exp(bl) + kv
        yield
        store_o(c, _unstack_heads(o_st, n_heads, qd.shape[0]))
    out[key] = s


def _mlstm_stages(n_chunks, load_chunk, store_o, state, consts, out):
    wide, same, incl, ones_bd = consts
    c, nrow, m = state
    tile = functools.partial(_tile_rows, n=D_H)
    for ch in range(n_chunks):
        q, k, v, li, a, a_last = load_chunk(ch)
        L = q.shape[0]
        hl = D_H * L
        to_row = lambda col: jnp.broadcast_to(col, (hl, LANE)).T[:hl]
        q_st = _stack_heads(q, D_H, D_HD)
        k_st = _stack_heads(k, D_H, D_HD)
        v_st = _stack_heads(v, D_H, D_HD)
        qk = _mm(q_st, k_st, "nt", M_ATT)
        dlog = jnp.where(incl, a - to_row(a) + to_row(li), -jnp.inf)
        m_intra = jnp.max(dlog, -1, keepdims=True)
        e_end = a_last - a + li
        m_loc = jnp.max(jnp.where(same, to_row(e_end), -jnp.inf), -1, keepdims=True)
        yield
        s_intra = jnp.exp(dlog - m_intra) * qk
        num_intra = _mm(s_intra, v_st, "nn", M_ATT)
        den_intra = jnp.sum(s_intra, -1, keepdims=True)
        w_end = jnp.exp(e_end - m_loc)
        c_loc = _mm(k_st, tile(v) * wide * w_end, "tn", M_ATT)
        n_loc = _mm(ones_bd, w_end * k_st, "nn", M_ATT)
        yield
        g = a + m
        m_t = jnp.maximum(g, m_intra)
        f_inter = jnp.exp(g - m_t)
        f_intra = jnp.exp(m_intra - m_t)
        qc = _mm(q_st, c, "nn", M_ATT)
        num_inter = jnp.concatenate(
            [qc[h * L:(h + 1) * L, h * D_HD:(h + 1) * D_HD] for h in range(D_H)], axis=0)
        den_inter = jnp.sum(q_st * nrow, -1, keepdims=True)
        num = num_intra * f_intra + num_inter * f_inter
        den = f_intra * den_intra + f_inter * den_inter
        h_st = num / jnp.maximum(jnp.abs(den), jnp.exp(-m_t))
        m_new = jnp.maximum(a_last + m, m_loc)
        fo = jnp.exp(a_last + m - m_new)
        fl = jnp.exp(m_loc - m_new)
        per_value = lambda col: jnp.concatenate(
            [jnp.broadcast_to(col[h * L:h * L + 1], (1, D_HD)) for h in range(D_H)], axis=1)
        c = per_value(fo) * c + per_value(fl) * c_loc
        nrow = fo * nrow + fl * n_loc
        m = m_new
        yield
        store_o(ch, _unstack_heads(h_st, D_H, L))
    out["mlstm"] = (c, nrow, m)


(_E_W0, _E_A0, _E_KK, _E_KA, _E_RK, _E_LNW, _E_LNB, _E_BNORM, _E_LB0, _E_LB1) = range(10)


def _even_kernel(p_ref, shift0_ref, srw0_ref, shg0_ref, mu_ref, w2_ref, a2_ref, par_ref,
                 o_ref, shift_ref, srw_ref, shg_ref,
                 prev_sc, srw_sc, shg_sc, kkt_sc, rt_sc, kh_sc, bh_sc, ke_sc, be_sc, v_sc, epos_sc, bonus_sc, oa_sc,
                 qd_sc, ki_sc, kend_sc, bl_sc, ob_sc, xm_sc, m2_sc, o2_sc, arb_sc, kv_sc, bew_sc, *, nb, tb, la, lb):
    j = pl.program_id(1)
    nj = pl.num_programs(1)
    seqs = range(nb)
    cols = lambda lo, hi: jnp.concatenate([p_ref[b, :, lo:hi] for b in seqs], axis=0)

    @pl.when(j == 0)
    def _():
        for b in seqs:
            prev_sc[b] = jnp.broadcast_to(shift0_ref[b], prev_sc.shape[1:])
            srw_sc[b] = jnp.concatenate([srw0_ref[b, h] for h in range(A_H)], axis=1)
            shg_sc[b] = shg0_ref[b].T

    par = par_ref[...]
    row = lambda i: par[i:i + 1]

    per_seq = tb // la
    group = math.gcd(nb * per_seq, 8)
    chains = nb
    run = group // chains
    ngroups = per_seq // run
    prows = run * la
    assert group % chains == 0 and per_seq % run == 0 and prows % lb == 0
    hper = prows // lb
    late_hgrn = ngroups > 1

    def pieces(g):
        if ngroups == 1:
            return [(0, nb * tb)]
        return [(q * tb + g * prows, q * tb + (g + 1) * prows) for q in seqs]

    def rows_of(lo, hi, r0, r1):
        parts = [p_ref[b, max(r0, b * tb) - b * tb:min(r1, (b + 1) * tb) - b * tb, lo:hi]
                 for b in seqs if max(r0, b * tb) < min(r1, (b + 1) * tb)]
        return parts[0] if len(parts) == 1 else jnp.concatenate(parts, axis=0)

    def rwkv_token_stages(r0, r1):
        grows = r1 - r0
        pa = rows_of(0, A_SHIFT_W, r0, r1)
        rolled = pltpu.roll(pa, 1, axis=0)
        starts = range(r0, r1, tb) if grows % tb == 0 else (r0,)
        span = tb if grows % tb == 0 else grows
        first = _iota(pa.shape, 0) % span == 0
        before = [jnp.broadcast_to(prev_sc[s // tb, 0:1], (span, A_SHIFT_W)) for s in starts]
        shifted = jnp.where(first, before[0] if len(before) == 1 else jnp.concatenate(before, axis=0), rolled)
        for s in starts:
            prev_sc[s // tb] = jnp.broadcast_to(pa[s - r0 + span - 1:s - r0 + span], prev_sc.shape[1:])
        xm = pa + (shifted - pa) * mu_ref[...]
        r = xm[:, :MIX]
        k = xm[:, MIX:2 * MIX]
        v = xm[:, 2 * MIX:3 * MIX]
        lora_in = xm[:, 3 * MIX:]
        yield
        lw = _sigmoid(row(_E_W0) + _dot(jnp.tanh(lora_in).astype(BF16), w2_ref[...])) * (-math.exp(-0.5))
        a = _sigmoid(row(_E_A0) + _dot(lora_in.astype(BF16), a2_ref[...]))
        yield
        kk = k * row(_E_KK)
        kk = kk * lax.rsqrt(jnp.maximum(_seg_sum(kk * kk, A_HD), 1e-12))
        yield
        k = k * (1.0 + (a - 1.0) * row(_E_KA))
        bv = kk * a
        cum_a, ones_a = _chunk_mats(grows, la)
        c = _mm(cum_a, lw, "nn", M_EXACT_R)
        yield
        cl = _mm(ones_a, lw, "nn", M_EXACT_R)
        yield
        e_pos = jnp.exp(c)
        e_neg = jnp.exp(-c)
        e_end = jnp.exp(cl - c)
        kkt_sc[r0:r1, :] = kk * jnp.exp(c - lw)
        rt_sc[r0:r1, :] = r * e_pos
        yield
        kh_sc[r0:r1, :] = k * e_neg
        bh_sc[r0:r1, :] = bv * e_neg
        ke_sc[r0:r1, :] = k * e_end
        be_sc[r0:r1, :] = bv * e_end
        yield
        v_sc[r0:r1, :] = v
        epos_sc[r0:r1, :] = e_pos
        bonus_sc[r0:r1, :] = _seg_sum(r * k * row(_E_RK), A_HD) * v

    wide, _, incl, strict = _head_masks(A_H, la, A_HD)
    hl = A_H * la
    ri, ci_ = _iota((hl, hl), 0), _iota((hl, hl), 1)
    eye = (ri == ci_).astype(F32)
    halves = []
    m = 1
    while m < la:
        halves.append(((ri // (2 * m)) == (ci_ // (2 * m))) & ((ri // m) != (ci_ // m)))
        m *= 2
    consts_a = (wide, incl, strict, eye, halves)

    base = A_SHIFT_W + MIX

    def hgrn_token_stages():
        e0 = row(_E_LB0)
        e1 = row(_E_LB1)
        emax = jnp.maximum(e0, e1)
        e0 = jnp.exp(e0 - emax)
        lower = e0 / (e0 + jnp.exp(e1 - emax))
        g = lower + (1.0 - lower) * jax.nn.sigmoid(cols(base + MIX, base + 2 * MIX))
        yield
        logg = jnp.log(g)
        cum_b, ones_b = _chunk_mats(nb * tb, lb)
        yield
        gb = _mm(cum_b, logg, "nn", M_EXACT_R)
        yield
        gl = _mm(ones_b, logg, "nn", M_EXACT_R)
        yield
        qd_sc[...] = _silu(cols(base, base + MIX)) * (B_HD ** -0.5) * jnp.exp(gb)
        yield
        ki_sc[...] = (1.0 - g) * jnp.exp(-gb)
        yield
        kend_sc[...] = (1.0 - g) * jnp.exp(gl - gb)
        bl_sc[...] = gl

    wide_b, _, incl_b, _ = _head_masks(B_H, lb, B_HD)
    consts_b = (wide_b, incl_b)

    def factor_stages(gi, slot):
        sls = [_rows_at(((q * per_seq + gi * run + u) * la, la), la) for q in seqs for u in range(run)]
        hl_a = A_H * la

        def store(u, name, value):
            if name == "m1n":
                xm_sc[slot + u, :hl_a, :] = value
            elif name == "r_w":
                xm_sc[slot + u, hl_a:, :] = value
            else:
                {"m2n": m2_sc, "o2": o2_sc, "a_rb": arb_sc, "kv": kv_sc, "be_w": bew_sc}[name][slot + u] = value

        yield from _rwkv_factor_stages(
            [(kkt_sc[sl, :], rt_sc[sl, :], kh_sc[sl, :], bh_sc[sl, :], ke_sc[sl, :], be_sc[sl, :], v_sc[sl, :])
             for sl in sls], consts_a, store)

    def state_stages(gi):
        slot = (gi % 2) * group
        stages, finish = [], []
        for q in range(chains):
            seq = q
            first = seq * per_seq + gi * run
            row0 = first * la
            out = {}

            def load_factors(u, q=q, first=first):
                i = slot + q * run + u
                return (xm_sc[i], m2_sc[i], o2_sc[i], arb_sc[i], kv_sc[i], bew_sc[i],
                        epos_sc[pl.ds((first + u) * la + la - 1, 1), :])

            def store_oa(u, o, row0=row0):
                oa_sc[_rows_at((row0 + u * la, la), la), :] = o

            def load_b(u, row0=row0, seq=seq):
                r = row0 + u * lb
                sl = _rows_at((r, lb), lb)
                return (qd_sc[sl, :], ki_sc[sl, :], kend_sc[sl, :], bl_sc[pl.ds(r, 1), :],
                        p_ref[seq, _rows_at((r - seq * tb, lb), lb), base + 2 * MIX:base + 3 * MIX])

            def store_ob(u, o, row0=row0):
                ob_sc[_rows_at((row0 + u * lb, lb), lb), :] = o

            stages.append((_rwkv_state_stages(run, load_factors, store_oa, srw_sc[seq], out), 2 * run + 1))
            if late_hgrn and gi == ngroups - 1:
                load_all = functools.partial(load_b, row0=seq * tb)
                store_all = functools.partial(store_ob, row0=seq * tb)
                stages.append((_gla_stages(hper * ngroups, load_all, store_all, shg_sc[seq], consts_b, B_H, B_HD,
                                           out, "hgrn"), 2 * hper * ngroups + 1))
            elif not late_hgrn:
                stages.append((_gla_stages(hper, load_b, store_ob, shg_sc[seq], consts_b, B_H, B_HD, out, "hgrn"),
                               2 * hper + 1))
            finish.append((seq, out))

        def write_back():
            for seq, out in finish:
                srw_sc[seq] = out["rwkv"]
                if "hgrn" in out:
                    shg_sc[seq] = out["hgrn"]

        return stages, write_back

    def rwkv_output_stages(r0, r1):
        o = oa_sc[r0:r1, :]
        mean = _seg_sum(o, A_HD) * (1.0 / A_HD)
        cen = o - mean
        yield
        var = _seg_sum(cen * cen, A_HD) * (1.0 / A_HD)
        yield
        oa = cen * lax.rsqrt(var + A_GN_EPS) * row(_E_LNW) + row(_E_LNB)
        oa = oa + bonus_sc[r0:r1, :]
        put_o(r0, r1, 0, oa * _silu(rows_of(A_SHIFT_W, A_SHIFT_W + MIX, r0, r1)))

    def put_o(r0, r1, lo, value):
        value = value.astype(o_ref.dtype)
        if len(o_ref.shape) == 2:
            o_ref[r0:r1, lo:lo + MIX] = value
        else:
            for b in seqs:
                a, z = max(r0, b * tb), min(r1, (b + 1) * tb)
                if a < z:
                    o_ref[b, a - b * tb:z - b * tb, lo:lo + MIX] = value[a - r0:z - r0]

    def hgrn_output(r0, r1):
        ob = _head_rms(ob_sc[r0:r1, :], row(_E_BNORM), B_HD) * _silu(rows_of(base + 3 * MIX, base + 4 * MIX, r0, r1))
        put_o(r0, r1, MIX, ob)

    for phase in range(ngroups + 3):
        stages, after = [], []
        if 0 <= phase - 2 < ngroups:
            chain_stages, write_back = state_stages(phase - 2)
            stages += chain_stages
            after.append(write_back)
        if 0 <= phase - 1 < ngroups:
            stages.append((factor_stages(phase - 1, ((phase - 1) % 2) * group), 2 * len(halves) + 1))
        if phase < ngroups:
            stages += [(rwkv_token_stages(r0, r1), 8) for r0, r1 in pieces(phase)]
        if phase == 0:
            stages.append((hgrn_token_stages(), 7))
        if 0 <= phase - 3 < ngroups:
            stages += [(rwkv_output_stages(r0, r1), 3) for r0, r1 in pieces(phase - 3)]
        _interleave(*stages)
        for fn in after:
            fn()
    hgrn_output(0, nb * tb)

    @pl.when(j == nj - 1)
    def _():
        for b in seqs:
            shift_ref[b] = prev_sc[b, 0:1]
            s = srw_sc[b]
            for h in range(A_H):
                srw_ref[b, h] = s[:, h * A_HD:(h + 1) * A_HD]
            shg_ref[b] = shg_sc[b].T


def _mixer_out(bsz, t, nb, tb):
    if nb == 1 or tb == t:
        return (pl.BlockSpec((nb * tb, D_MODEL), lambda b, j: (b * (t // tb) + j, 0)),
                jax.ShapeDtypeStruct((bsz * t, D_MODEL), BF16))
    return (pl.BlockSpec((nb, tb, D_MODEL), lambda b, j: (b, j, 0)),
            jax.ShapeDtypeStruct((bsz, t, D_MODEL), BF16))


def _even_mixer(p, shift0, srw0, shg0, mu, w2p, a2p, par, nb, tb, la, lb):
    bsz, t, _ = p.shape
    kern = functools.partial(_even_kernel, nb=nb, tb=tb, la=la, lb=lb)
    nch, hl = 2 * math.gcd(nb * tb // la, 8), A_H * la
    bmap3 = lambda b, j: (b, 0, 0)
    bmap4 = lambda b, j: (b, 0, 0, 0)
    cmap = lambda b, j: (0, 0)
    blk = lambda: pltpu.VMEM((nb * tb, MIX), F32)
    o_spec, o_shape = _mixer_out(bsz, t, nb, tb)
    return pl.pallas_call(
        kern, grid=(bsz // nb, t // tb),
        in_specs=[pl.BlockSpec((nb, tb, IN_EVEN), lambda b, j: (b, j, 0)),
                  pl.BlockSpec((nb, 1, A_SHIFT_W), bmap3),
                  pl.BlockSpec((nb, A_H, A_HD, A_HD), bmap4),
                  pl.BlockSpec((nb, B_H * B_HD, B_HD), bmap3),
                  pl.BlockSpec((1, A_SHIFT_W), cmap),
                  pl.BlockSpec((2 * A_LORA, MIX), cmap),
                  pl.BlockSpec((2 * A_LORA, MIX), cmap),
                  pl.BlockSpec((16, MIX), cmap)],
        out_specs=[o_spec,
                   pl.BlockSpec((nb, 1, A_SHIFT_W), bmap3),
                   pl.BlockSpec((nb, A_H, A_HD, A_HD), bmap4),
                   pl.BlockSpec((nb, B_H * B_HD, B_HD), bmap3)],
        out_shape=[o_shape,
                   jax.ShapeDtypeStruct((bsz, 1, A_SHIFT_W), F32),
                   jax.ShapeDtypeStruct((bsz, A_H, A_HD, A_HD), F32),
                   jax.ShapeDtypeStruct((bsz, B_H * B_HD, B_HD), F32)],
        scratch_shapes=[pltpu.VMEM((nb, 8, A_SHIFT_W), F32), pltpu.VMEM((nb, A_HD, MIX), F32),
                        pltpu.VMEM((nb, B_HD, B_H * B_HD), F32)] + [blk() for _ in range(15)] + [
                            pltpu.VMEM((nch, 2 * hl, MIX), BF16), pltpu.VMEM((nch, hl, A_HD), F32),
                            pltpu.VMEM((nch, hl, A_HD), F32), pltpu.VMEM((nch, hl, hl), BF16),
                            pltpu.VMEM((nch, A_HD, MIX), F32), pltpu.VMEM((nch, hl, MIX), BF16)],
        compiler_params=pltpu.CompilerParams(dimension_semantics=("arbitrary", "arbitrary"),
                                             vmem_limit_bytes=VMEM_LIMIT),
        name="mixer_rwkv_hgrn")(p, shift0, srw0, shg0, mu, w2p, a2p, par)


_O_CQ = 0
_O_CK = _O_CQ + C_KW
_O_CV = _O_CK + C_KW
_O_GC = _O_CV + MIX
_O_DQK = _O_GC + MIX
_O_DV = _O_DQK + 2 * MIX
_O_GD = _O_DV + MIX
_O_SLAB = _O_GD + MIX
(_P_CNORM, _P_DNORM, _P_G2B) = range(3)


def _odd_kernel(p_ref, sgl0_ref, conv0_ref, mc0_ref, mn0_ref, mm0_ref, g2_ref, cw_ref, cb_ref, sb_ref, par_ref,
                x_ref, wo_ref, gf_ref,
                y_ref, sgl_ref, conv_ref, mc_ref, mn_ref, mm_ref,
                prev_sc, sgl_sc, mc_sc, mn_sc, mm_sc, cq_sc, ck_sc, ce_sc, cl_sc, oc_sc,
                dq_sc, dk_sc, li_sc, fa_sc, fl_sc, od_sc, *, nb, tb, lc):
    j = pl.program_id(1)
    nj = pl.num_programs(1)
    hl = D_H * lc
    seqs = range(nb)
    cols = lambda lo, hi: jnp.concatenate([p_ref[b, :, lo:hi] for b in seqs], axis=0)
    head_rows = lambda x: jnp.concatenate(
        [jnp.broadcast_to(x[h:h + 1], (lc, x.shape[1])) for h in range(D_H)], axis=0)

    @pl.when(j == 0)
    def _():
        for b in seqs:
            prev_sc[b] = conv0_ref[b]
            sgl_sc[b] = sgl0_ref[b].T
            mc_sc[b] = mc0_ref[b].T
            mn_sc[b] = head_rows(mn0_ref[b])
            mm_sc[b] = head_rows(mm0_ref[b])[:, :1]

    par = par_ref[...]
    row = lambda i: par[i:i + 1]

    def rows_of(lo, hi, r0, r1):
        parts = [p_ref[b, max(r0, b * tb) - b * tb:min(r1, (b + 1) * tb) - b * tb, lo:hi]
                 for b in seqs if max(r0, b * tb) < min(r1, (b + 1) * tb)]
        return parts[0] if len(parts) == 1 else jnp.concatenate(parts, axis=0)

    def token_stages(r0, r1):
        n = r1 - r0
        span = tb if n % tb == 0 else n
        cum_c, ones_c = _chunk_mats(n, lc)
        slab = rows_of(_O_SLAB, _O_SLAB + SLAB, r0, r1)
        pre = _dot(slab.astype(BF16), g2_ref[...]) + row(_P_G2B)[:, :C_KW]
        logg = _log_sigmoid(pre) * (1.0 / C_GATE_NORM)
        yield
        gb = _mm(cum_c, logg, "nn", M_EXACT_R)
        gl = _mm(ones_c, logg, "nn", M_EXACT_R)
        yield
        ck = rows_of(_O_CK, _O_CK + C_KW, r0, r1)
        cq_sc[r0:r1, :] = rows_of(_O_CQ, _O_CQ + C_KW, r0, r1) * (C_KD ** -0.5) * jnp.exp(gb)
        ck_sc[r0:r1, :] = ck * jnp.exp(-gb)
        ce_sc[r0:r1, :] = ck * jnp.exp(gl - gb)
        cl_sc[r0:r1, :] = gl
        yield
        x = rows_of(_O_DQK, _O_DQK + 2 * MIX, r0, r1)
        cw = cw_ref[...]
        conv = cb_ref[...] + x * cw[D_CONV - 1:D_CONV]
        head_row = _iota((8, 2 * MIX), 0)
        for s in range(1, D_CONV):
            xr = pltpu.roll(x, s, axis=0)
            parts = []
            for a in range(0, n, span):
                kept = pltpu.roll(prev_sc[(r0 + a) // tb], s, axis=0)
                parts.append(jnp.where(head_row < s, kept, xr[a:a + 8]))
                if span > 8:
                    parts.append(xr[a + 8:a + span])
            xs = parts[0] if len(parts) == 1 else jnp.concatenate(parts, axis=0)
            conv = conv + xs * cw[D_CONV - 1 - s:D_CONV - s]
        for a in range(0, n, span):
            prev_sc[(r0 + a) // tb] = x[a + span - 8:a + span]
        yield
        conv = _silu(conv)
        dq_sc[r0:r1, :] = conv[:, :MIX]
        dk_sc[r0:r1, :] = conv[:, MIX:] * (D_HD ** -0.5)
        yield
        gates = slab + sb_ref[...]
        lf = pltpu.roll(_log_sigmoid(gates), SLAB - (SLAB_F - SLAB_I), axis=1)
        li_sc[r0:r1, :] = gates
        fa_sc[r0:r1, :] = _mm(cum_c, lf, "nn", M_EXACT_R)
        fl_sc[r0:r1, :] = _mm(ones_c, lf, "nn", M_EXACT_R)

    wide_c, _, incl_c, _ = _head_masks(C_H, lc, C_KD)
    consts_c = (wide_c, incl_c)
    lane = _iota((hl, SLAB), 1)
    head = _iota((hl, SLAB), 0) // lc
    pick = lambda z: jnp.sum(jnp.where(lane == head + SLAB_I, _tile_rows(z, D_H), 0.0), -1, keepdims=True)

    wide_d, same_d, incl_d, _ = _head_masks(D_H, lc, D_HD)
    consts_d = (wide_d, same_d, incl_d, same_d.astype(BF16))

    per_seq = tb // lc
    if per_seq == 1:
        chains, run = math.gcd(nb, 4), 1
        trips = nb // chains
        first_chunk = lambda gi, q: gi * chains + q
    else:
        chains, run = nb, math.gcd(per_seq, 4)
        trips = per_seq // run
        first_chunk = lambda gi, q: q * per_seq + gi * run
    assert per_seq % run == 0

    def chunk_stages(gi):
        stages, finish = [], []
        for q in range(chains):
            first = first_chunk(gi, q)
            seq = first // per_seq
            out = {}
            rows = lambda u, first=first: _rows_at(((first + u) * lc, lc), lc)
            in_seq = lambda u, first=first, seq=seq: _rows_at(((first + u) * lc - seq * tb, lc), lc)

            def load_c(u, first=first, seq=seq, rows=rows, in_seq=in_seq):
                sl = rows(u)
                return (cq_sc[sl, :], ck_sc[sl, :], ce_sc[sl, :], cl_sc[pl.ds((first + u) * lc, 1), :],
                        p_ref[seq, in_seq(u), _O_CV:_O_CV + MIX])

            def store_c(u, o, rows=rows):
                oc_sc[rows(u), :] = o

            def load_d(u, seq=seq, rows=rows, in_seq=in_seq):
                sl = rows(u)
                return (dq_sc[sl, :], dk_sc[sl, :], p_ref[seq, in_seq(u), _O_DV:_O_DV + MIX],
                        pick(li_sc[sl, :]), pick(fa_sc[sl, :]), pick(fl_sc[sl, :]))

            def store_d(u, o, rows=rows):
                od_sc[rows(u), :] = o

            stages.append((_mlstm_stages(run, load_d, store_d, (mc_sc[seq], mn_sc[seq], mm_sc[seq]), consts_d, out),
                           3 * run + 1))
            stages.append((_gla_stages(run, load_c, store_c, sgl_sc[seq], consts_c, C_H, C_VD, out, "gla"),
                           2 * run + 1))
            finish.append((seq, out))

        def write_back():
            for seq, out in finish:
                sgl_sc[seq] = out["gla"]
                mc_sc[seq], mn_sc[seq], mm_sc[seq] = out["mlstm"]

        return stages, write_back

    def output_stages(r0, r1):
        oc = _head_rms(oc_sc[r0:r1, :], row(_P_CNORM), C_VD) * _silu(rows_of(_O_GC, _O_GC + MIX, r0, r1))
        yield
        od = _head_rms(od_sc[r0:r1, :], row(_P_DNORM), D_HD) * _silu(rows_of(_O_GD, _O_GD + MIX, r0, r1))
        o = jnp.concatenate([oc, od], axis=1).astype(BF16)
        yield
        x2 = x_ref[r0:r1, :] + _dot(o, wo_ref[...])
        y_ref[r0:r1, :] = _rms(x2, gf_ref[...])

    if per_seq == 1 or nb > 1:
        _interleave((token_stages(0, nb * tb), 6))

        def trip(gi, carry):
            stages, write_back = chunk_stages(gi)
            _interleave(*stages)
            write_back()
            return carry

        lax.fori_loop(0, trips, trip, 0)
        _interleave((output_stages(0, nb * tb), 3))
    else:
        grows = run * lc
        for phase in range(trips + 2):
            stages, after = [], []
            if 0 <= phase - 1 < trips:
                chunk, write_back = chunk_stages(phase - 1)
                stages += chunk
                after.append(write_back)
            if phase < trips:
                stages.append((token_stages(phase * grows, (phase + 1) * grows), 6))
            if 0 <= phase - 2 < trips:
                stages.append((output_stages((phase - 2) * grows, (phase - 1) * grows), 3))
            _interleave(*stages)
            for fn in after:
                fn()

    @pl.when(j == nj - 1)
    def _():
        for b in seqs:
            sgl_ref[b] = sgl_sc[b].T
            conv_ref[b] = prev_sc[b]
            mc_ref[b] = mc_sc[b].T
            mn_ref[b] = jnp.concatenate([mn_sc[b, h * lc:h * lc + 1] for h in range(D_H)], axis=0)
            mm_ref[b] = jnp.concatenate(
                [jnp.broadcast_to(mm_sc[b, h * lc:h * lc + 1], (1, LANE)) for h in range(D_H)], axis=0)


def _odd_mixer(p, sgl0, conv0, mc0, mn0, mm0, g2p, cw, cb, sb, par, x, wo, gf, nb, tb, lc):
    bsz, t, _ = p.shape
    hl = D_H * lc
    kern = functools.partial(_odd_kernel, nb=nb, tb=tb, lc=lc)
    bmap3 = lambda b, j: (b, 0, 0)
    cmap = lambda b, j: (0, 0)
    blk = lambda w: pltpu.VMEM((nb * tb, w), F32)
    assert nb == 1 or tb == t
    rows = pl.BlockSpec((nb * tb, D_MODEL), lambda b, j: (b * (t // tb) + j, 0))
    return pl.pallas_call(
        kern, grid=(bsz // nb, t // tb),
        in_specs=[pl.BlockSpec((nb, tb, IN_ODD_PAD), lambda b, j: (b, j, 0)),
                  pl.BlockSpec((nb, C_KW, C_VD), bmap3),
                  pl.BlockSpec((nb, 8, 2 * MIX), bmap3),
                  pl.BlockSpec((nb, MIX, D_HD), bmap3),
                  pl.BlockSpec((nb, D_H, D_HD), bmap3),
                  pl.BlockSpec((nb, D_H, LANE), bmap3),
                  pl.BlockSpec((SLAB, C_KW), cmap),
                  pl.BlockSpec((D_CONV, 2 * MIX), cmap),
                  pl.BlockSpec((1, 2 * MIX), cmap),
                  pl.BlockSpec((1, SLAB), cmap),
                  pl.BlockSpec((8, MIX), cmap),
                  rows,
                  pl.BlockSpec((D_MODEL, D_MODEL), cmap),
                  pl.BlockSpec((1, D_MODEL), cmap)],
        out_specs=[rows,
                   pl.BlockSpec((nb, C_KW, C_VD), bmap3),
                   pl.BlockSpec((nb, 8, 2 * MIX), bmap3),
                   pl.BlockSpec((nb, MIX, D_HD), bmap3),
                   pl.BlockSpec((nb, D_H, D_HD), bmap3),
                   pl.BlockSpec((nb, D_H, LANE), bmap3)],
        out_shape=[jax.ShapeDtypeStruct((bsz * t, D_MODEL), F32),
                   jax.ShapeDtypeStruct((bsz, C_KW, C_VD), F32),
                   jax.ShapeDtypeStruct((bsz, 8, 2 * MIX), F32),
                   jax.ShapeDtypeStruct((bsz, MIX, D_HD), F32),
                   jax.ShapeDtypeStruct((bsz, D_H, D_HD), F32),
                   jax.ShapeDtypeStruct((bsz, D_H, LANE), F32)],
        scratch_shapes=[pltpu.VMEM((nb, 8, 2 * MIX), F32), pltpu.VMEM((nb, C_VD, C_KW), F32),
                        pltpu.VMEM((nb, D_HD, MIX), F32), pltpu.VMEM((nb, hl, D_HD), F32),
                        pltpu.VMEM((nb, hl, 1), F32),
                        blk(C_KW), blk(C_KW), blk(C_KW), blk(C_KW), blk(MIX),
                        blk(MIX), blk(MIX), blk(SLAB), blk(SLAB), blk(SLAB), blk(MIX)],
        compiler_params=pltpu.CompilerParams(dimension_semantics=("arbitrary", "arbitrary"),
                                             vmem_limit_bytes=VMEM_LIMIT),
        name="mixer_gla_mlstm")(p, sgl0, conv0, mc0, mn0, mm0, g2p, cw, cb, sb, par, x, wo, gf)


MIXER_ROWS = 256
SHORT_SEQS = 8


def _mixer_blocking(bsz, t):
    tb = math.gcd(t, MIXER_ROWS)
    nb = math.gcd(bsz, SHORT_SEQS) if tb == t and t * SHORT_SEQS <= MIXER_ROWS else 1
    return nb, tb


def _paired_blocking(bsz, nb, tb):
    if nb == 1 and bsz % 2 == 0 and (tb // 2) % 32 == 0:
        return 2, tb // 2
    return nb, tb


def _run(x, shift, s_rwkv, s_hgrn, s_gla, conv, mc, mn, mm, wts):
    bsz, t, d = x.shape
    nb, tb = _mixer_blocking(bsz, t)
    la = math.gcd(t, 16)
    lbc = math.gcd(t, 32)
    x2 = x.reshape(bsz * t, d)
    p0 = _inproj(x2, wts["g0"], wts["w_in0"])
    o0, shift_n, srw_n, shg_n = _even_mixer(
        p0.reshape(bsz, t, IN_EVEN), shift[0][:, None], s_rwkv[0], s_hgrn[0].reshape(bsz, B_H * B_HD, B_HD),
        wts["mu"], wts["w2p"], wts["a2p"], wts["par_e"], *_paired_blocking(bsz, nb, tb), la, lbc)
    x1, p1 = _mid(o0.reshape(bsz * t, d), x2, wts["w_out0"], wts["g1"], wts["w_in1"])
    conv8 = jnp.pad(conv[0], ((0, 0), (8 - (D_CONV - 1), 0), (0, 0)))
    mm_l = jnp.broadcast_to(mm[0][:, :, None], (bsz, D_H, LANE))
    y, sgl_n, conv_n, mc_n, mn_n, mm_n = _odd_mixer(
        p1.reshape(bsz, t, IN_ODD_PAD), s_gla[0].reshape(bsz, C_KW, C_VD), conv8,
        mc[0].reshape(bsz, MIX, D_HD), mn[0], mm_l,
        wts["g2p"], wts["cw"], wts["cb"], wts["sb"], wts["par_o"], x1, wts["w_out1"], wts["gf"], nb, tb, lbc)
    return (y.reshape(bsz, t, d), shift_n.reshape(1, bsz, A_SHIFT_W), srw_n[None],
            shg_n.reshape(1, bsz, B_H, B_HD, B_HD), sgl_n.reshape(1, bsz, C_H, C_KD, C_VD),
            conv_n[None, :, 8 - (D_CONV - 1):], mc_n.reshape(1, bsz, D_H, D_HD, D_HD), mn_n[None],
            mm_n[None, :, :, 0])


def _odd_in_projection(w):
    sizes = (C_KW, C_KW, MIX, C_LORA, MIX, 2 * MIX, MIX, D_H, D_H, MIX)
    off = np.cumsum((0,) + sizes)
    cq, ck, cv, cg, gc, dqk, dv, di, df, gd = (w[:, off[i]:off[i + 1]] for i in range(len(sizes)))
    pad = jnp.zeros((w.shape[0], SLAB - C_LORA - 2 * D_H), w.dtype)
    return jnp.concatenate([cq, ck, cv, gc, dqk, dv, gd, cg, di, df, pad], axis=1)


def kernel(x_prompt, x_sample, state_shift_a, state_rwkv, state_hgrn, state_gla, state_conv_d, state_mlstm_c,
           state_mlstm_n, state_mlstm_m, norm_g, w_in_even, w_out_even, a_mu, a_w0, a_w2, a_a0, a_a2, a_kk,
           a_ka, a_rk, a_ln_w, a_ln_b, b_lb, b_norm, w_in_odd, w_out_odd, c_g2, c_g2b, c_norm, d_conv_w,
           d_conv_b, d_ib, d_fb, d_norm, final_norm):
    assert w_in_even.shape[0] == 1 and w_in_odd.shape[0] == 1 and b_lb.shape[0] == 2
    zpad = lambda a, rows_before, rows_total: jnp.pad(a, ((rows_before, rows_total - rows_before - a.shape[0]), (0, 0)))
    par_e = jnp.concatenate([a_w0, a_a0, a_kk, a_ka, a_rk, a_ln_w, a_ln_b, b_norm, b_lb[0:1], b_lb[1:2]], axis=0)
    g2b = jnp.pad(c_g2b, ((0, 0), (0, MIX - C_KW)))
    par_o = jnp.concatenate([c_norm, d_norm, g2b], axis=0)
    w_in1 = _odd_in_projection(w_in_odd[0].astype(BF16))
    sb = jnp.pad(jnp.concatenate([d_ib, d_fb], axis=1), ((0, 0), (SLAB_I, SLAB - SLAB_I - 2 * D_H)))
    wts = {
        "g0": norm_g[0:1], "g1": norm_g[1:2], "gf": final_norm[None],
        "w_in0": w_in_even[0].astype(BF16), "w_out0": w_out_even[0].astype(BF16),
        "w_in1": w_in1, "w_out1": w_out_odd[0].astype(BF16),
        "mu": a_mu, "w2p": zpad(a_w2[0], 0, 2 * A_LORA).astype(BF16),
        "a2p": zpad(a_a2[0], A_LORA, 2 * A_LORA).astype(BF16),
        "par_e": zpad(par_e, 0, 16), "par_o": zpad(par_o, 0, 8),
        "g2p": zpad(c_g2[0], 0, SLAB).astype(BF16), "cw": d_conv_w[0], "cb": d_conv_b, "sb": sb,
    }
    bp = x_prompt.shape[0]
    z = lambda *s: jnp.zeros(s, x_prompt.dtype)
    prompt = _run(x_prompt, z(1, bp, A_SHIFT_W), z(1, bp, A_H, A_HD, A_HD), z(1, bp, B_H, B_HD, B_HD),
                  z(1, bp, C_H, C_KD, C_VD), z(1, bp, D_CONV - 1, 2 * MIX), z(1, bp, D_H, D_HD, D_HD),
                  z(1, bp, D_H, D_HD), z(1, bp, D_H), wts)
    sample = _run(x_sample, state_shift_a, state_rwkv, state_hgrn, state_gla, state_conv_d, state_mlstm_c,
                  state_mlstm_n, state_mlstm_m, wts)
    return (prompt[0], sample[0]) + prompt[1:] + sample[1:]
```

```python
import functools
import math

import jax
import jax.numpy as jnp
import numpy as np
from jax import lax
from jax.experimental import pallas as pl
from jax.experimental.pallas import tpu as pltpu

F32 = jnp.float32
BF16 = jnp.bfloat16

D_MODEL = 1024
MIX = D_MODEL // 2
NORM_EPS = 1e-5
A_HD = 64
A_H = MIX // A_HD
A_LORA = 64
A_GN_EPS = 64e-5
A_SHIFT_W = 3 * MIX + 2 * A_LORA
B_HD = 128
B_H = MIX // B_HD
IN_EVEN = A_SHIFT_W + 5 * MIX
C_H = 4
C_VD = MIX // C_H
C_KD = C_VD // 2
C_KW = C_H * C_KD
C_LORA = 16
C_GATE_NORM = 16.0
D_H = 4
D_HD = MIX // D_H
D_CONV = 4
LANE = 128
SLAB = LANE
SLAB_I = C_LORA
SLAB_F = C_LORA + D_H
IN_ODD_PAD = 2 * C_KW + 2 * MIX + 2 * MIX + 2 * MIX + SLAB
VMEM_LIMIT = 48 * 1024 * 1024


_DIMS = {"nn": ((1,), (0,)), "nt": ((1,), (1,)), "tn": ((0,), (0,))}
M_EXACT_R = "rx"
M_EXACT_L = "lx"
M_G = "bf"
M_TINV = "bf"
M_ST = "bf"
M_ATT = "bf"


def _split2(x):
    hi = x.astype(BF16)
    return hi, (x - hi.astype(F32)).astype(BF16)


def _mm(a, b, form, mode):
    dn = (_DIMS[form], ((), ()))
    d = lambda x, y: lax.dot_general(x, y, dn, preferred_element_type=F32)
    if mode == "bf":
        return d(a.astype(BF16), b.astype(BF16))
    if mode == "x3":
        ah, al = _split2(a)
        bh, bl = _split2(b)
        return d(ah, bh) + (d(ah, bl) + d(al, bh))
    if mode == "lx":
        bb = b.astype(BF16)
        h, l = _split2(a)
        return d(h, bb) + d(l, bb)
    assert mode == "rx"
    ab = a.astype(BF16)
    h, l = _split2(b)
    return d(ab, h) + d(ab, l)


def _dot(a, b):
    return jnp.dot(a, b, preferred_element_type=F32)


def _iota(shape, dim):
    return lax.broadcasted_iota(jnp.int32, shape, dim)


def _sigmoid(x):
    return 0.5 + 0.5 * jnp.tanh(0.5 * x)


def _log_sigmoid(x):
    return jnp.minimum(x, 0.0) - jnp.log(1.0 + jnp.exp(-jnp.abs(x)))


def _silu(x):
    return x * _sigmoid(x)


def _rms(x, g):
    return x * lax.rsqrt(jnp.mean(x * x, -1, keepdims=True) + NORM_EPS) * g


def _rows_at(start_and_alignment, size):
    start, alignment = start_and_alignment
    if not isinstance(start, int):
        start = pl.multiple_of(start, alignment)
    return pl.ds(start, size)


def _unroll(trips):
    return 2 if trips % 2 == 0 else 1


def _tile_rows(x, n):
    return jnp.concatenate([x] * n, axis=0)


def _stack_heads(x, n_heads, width):
    return jnp.concatenate([x[:, h * width:(h + 1) * width] for h in range(n_heads)], axis=0)


def _unstack_heads(x, n_heads, rows):
    return jnp.concatenate([x[h * rows:(h + 1) * rows] for h in range(n_heads)], axis=1)


def _head_masks(n_heads, rows, width):
    hl = n_heads * rows
    rh = _iota((hl, n_heads * width), 0) // rows
    lh = _iota((hl, n_heads * width), 1) // width
    wide = (rh == lh).astype(F32)
    ri, ci = _iota((hl, hl), 0), _iota((hl, hl), 1)
    same = (ri // rows) == (ci // rows)
    incl = same & (ci <= ri)
    strict = same & (ci < ri)
    return wide, same, incl, strict


def _chunk_mats(n, chunk):
    ri, ci = _iota((n, n), 0), _iota((n, n), 1)
    same = (ri // chunk) == (ci // chunk)
    return (same & (ci <= ri)).astype(BF16), same.astype(BF16)


def _seg_sum(x, seg):
    bd = ((_iota((LANE, LANE), 0) // seg) == (_iota((LANE, LANE), 1) // seg)).astype(BF16)
    parts = [_mm(x[:, j:j + LANE], bd, "nn", M_EXACT_L) for j in range(0, x.shape[1], LANE)]
    return jnp.concatenate(parts, axis=1)


def _head_rms(x, g, width):
    parts = []
    for j in range(0, x.shape[1], width):
        xs = x[:, j:j + width]
        parts.append(xs * lax.rsqrt(jnp.mean(xs * xs, -1, keepdims=True) + NORM_EPS))
    return jnp.concatenate(parts, axis=1) * g


def _inproj_kernel(x_ref, g_ref, w_ref, p_ref):
    h = _rms(x_ref[...], g_ref[...])
    p_ref[...] = _dot(h.astype(BF16), w_ref[...])


def _mid_kernel(o_ref, x_ref, wo_ref, g_ref, wi_ref, x1_ref, p_ref):
    x1 = x_ref[...] + _dot(o_ref[...], wo_ref[...])
    x1_ref[...] = x1
    p_ref[...] = _dot(_rms(x1, g_ref[...]).astype(BF16), wi_ref[...])


def _row_tile(m):
    return math.gcd(m, 512)


def _full(shape):
    return pl.BlockSpec(shape, lambda i: (0,) * len(shape))


def _dense_params():
    return pltpu.CompilerParams(dimension_semantics=("arbitrary",), vmem_limit_bytes=VMEM_LIMIT)


def _inproj(x, g, w):
    m, d = x.shape
    n = w.shape[1]
    tm = _row_tile(m)
    return pl.pallas_call(
        _inproj_kernel, grid=(m // tm,),
        in_specs=[pl.BlockSpec((tm, d), lambda i: (i, 0)), _full((1, d)), _full((d, n))],
        out_specs=pl.BlockSpec((tm, n), lambda i: (i, 0)),
        out_shape=jax.ShapeDtypeStruct((m, n), F32),
        compiler_params=_dense_params(), name="inproj")(x, g, w)


def _mid(o, x, wo, g, wi):
    m, d = x.shape
    n = wi.shape[1]
    tm = _row_tile(m)
    return pl.pallas_call(
        _mid_kernel, grid=(m // tm,),
        in_specs=[pl.BlockSpec((tm, d), lambda i: (i, 0)), pl.BlockSpec((tm, d), lambda i: (i, 0)),
                  _full((d, d)), _full((1, d)), _full((d, n))],
        out_specs=[pl.BlockSpec((tm, d), lambda i: (i, 0)), pl.BlockSpec((tm, n), lambda i: (i, 0))],
        out_shape=[jax.ShapeDtypeStruct((m, d), F32), jax.ShapeDtypeStruct((m, n), F32)],
        compiler_params=_dense_params(), name="outproj_inproj")(o, x, wo, g, wi)


def _interleave(*staged):
    live = [[gen, 0, max(n, 1)] for gen, n in staged]
    while live:
        item = min(live, key=lambda it: it[1] / it[2])
        try:
            next(item[0])
            item[1] += 1
        except StopIteration:
            live.remove(item)


def _rwkv_factor_stages(chunks, consts, store):
    wide, incl, strict, eye, halves = consts
    L = chunks[0][0].shape[0]
    hl = A_H * L
    n = range(len(chunks))
    tile = functools.partial(_tile_rows, n=A_H)
    for i in n:
        store(i, "r_w", (tile(chunks[i][1]) * wide).astype(BF16))
        store(i, "be_w", (tile(chunks[i][5]) * wide).astype(BF16))
    stack = lambda x: _stack_heads(x, A_H, A_HD)
    g = [_mm(jnp.concatenate([stack(c[0]), stack(c[1])], axis=0),
             jnp.concatenate([stack(c[2]), stack(c[3])], axis=0), "nt", M_G) for c in chunks]
    yield
    a_kk = [jnp.where(strict, x[:hl, :hl], 0.0) for x in g]
    a_kb = [jnp.where(strict, x[:hl, hl:], 0.0) for x in g]
    a_rk = [jnp.where(incl, x[hl:, :hl], 0.0) for x in g]
    for i in n:
        store(i, "a_rb", jnp.where(incl, g[i][hl:, hl:], 0.0).astype(BF16))
    t = [eye - jnp.where(halves[0], x, 0.0) for x in a_kb]
    v_st = [_stack_heads(c[6], A_H, A_HD) for c in chunks]
    av = [_mm(a_kk[i], v_st[i], "nn", M_ST) for i in n]
    for i in n:
        store(i, "o2", _mm(a_rk[i], v_st[i], "nn", M_ST))
        store(i, "kv", _mm(v_st[i], tile(chunks[i][4]) * wide, "tn", M_ST))
    for half in halves[1:]:
        ta = [_mm(t[i], jnp.where(half, a_kb[i], 0.0), "nn", M_TINV) for i in n]
        yield
        t = [t[i] - _mm(ta[i], t[i], "nn", M_TINV) for i in n]
        yield
    for i in n:
        store(i, "m1n", (-_mm(t[i], tile(chunks[i][0]) * wide, "nn", M_ST)).astype(BF16))
        store(i, "m2n", -_mm(t[i], av[i], "nn", M_ST))
    yield


def _rwkv_state_stages(n_chunks, load_factors, store_o, s, out):
    for c in range(n_chunks):
        xm, m2n, o2, a_rb, kv, be_w, ecl = load_factors(c)
        hl = xm.shape[0] // 2
        xs = _mm(xm, s, "nt", M_ST)
        yield
        u = xs[:hl] + m2n
        s = s * ecl + kv + _mm(u, be_w, "tn", M_ST)
        o_st = xs[hl:] + o2 + _mm(a_rb, u, "nn", M_ST)
        yield
        store_o(c, _unstack_heads(o_st, A_H, hl // A_H))
    out["rwkv"] = s


def _gla_stages(n_chunks, load_chunk, store_o, s, consts, n_heads, vd, out, key):
    wide, incl = consts
    tile = functools.partial(_tile_rows, n=n_heads)
    for c in range(n_chunks):
        qd, ki, kend, bl, v = load_chunk(c)
        q_w = tile(qd) * wide
        v_st = _stack_heads(v, n_heads, vd)
        att = jnp.where(incl, _mm(q_w, tile(ki), "nt", M_ATT), 0.0)
        kv = _mm(v_st, tile(kend) * wide, "tn", M_ATT)
        yield
        o_st = _mm(att, v_st, "nn", M_ATT) + _mm(q_w, s, "nt", M_ATT)
        s = s * jnp.exp(bl) + kv
        yield
        store_o(c, _unstack_heads(o_st, n_heads, qd.shape[0]))
    out[key] = s


def _mlstm_stages(n_chunks, load_chunk, store_o, state, consts, out):
    wide, same, incl, ones_bd = consts
    c, nrow, m = state
    tile = functools.partial(_tile_rows, n=D_H)
    for ch in range(n_chunks):
        q, k, v, li, a, a_last = load_chunk(ch)
        L = q.shape[0]
        hl = D_H * L
        to_row = lambda col: jnp.broadcast_to(col, (hl, LANE)).T[:hl]
        q_st = _stack_heads(q, D_H, D_HD)
        k_st = _stack_heads(k, D_H, D_HD)
        v_st = _stack_heads(v, D_H, D_HD)
        qk = _mm(q_st, k_st, "nt", M_ATT)
        dlog = jnp.where(incl, a - to_row(a) + to_row(li), -jnp.inf)
        m_intra = jnp.max(dlog, -1, keepdims=True)
        e_end = a_last - a + li
        m_loc = jnp.max(jnp.where(same, to_row(e_end), -jnp.inf), -1, keepdims=True)
        yield
        s_intra = jnp.exp(dlog - m_intra) * qk
        num_intra = _mm(s_intra, v_st, "nn", M_ATT)
        den_intra = jnp.sum(s_intra, -1, keepdims=True)
        w_end = jnp.exp(e_end - m_loc)
        c_loc = _mm(k_st, tile(v) * wide * w_end, "tn", M_ATT)
        n_loc = _mm(ones_bd, w_end * k_st, "nn", M_ATT)
        yield
        g = a + m
        m_t = jnp.maximum(g, m_intra)
        f_inter = jnp.exp(g - m_t)
        f_intra = jnp.exp(m_intra - m_t)
        qc = _mm(q_st, c, "nn", M_ATT)
        num_inter = jnp.concatenate(
            [qc[h * L:(h + 1) * L, h * D_HD:(h + 1) * D_HD] for h in range(D_H)], axis=0)
        den_inter = jnp.sum(q_st * nrow, -1, keepdims=True)
        num = num_intra * f_intra + num_inter * f_inter
        den = f_intra * den_intra + f_inter * den_inter
        h_st = num / jnp.maximum(jnp.abs(den), jnp.exp(-m_t))
        m_new = jnp.maximum(a_last + m, m_loc)
        fo = jnp.exp(a_last + m - m_new)
        fl = jnp.exp(m_loc - m_new)
        per_value = lambda col: jnp.concatenate(
            [jnp.broadcast_to(col[h * L:h * L + 1], (1, D_HD)) for h in range(D_H)], axis=1)
        c = per_value(fo) * c + per_value(fl) * c_loc
        nrow = fo * nrow + fl * n_loc
        m = m_new
        yield
        store_o(ch, _unstack_heads(h_st, D_H, L))
    out["mlstm"] = (c, nrow, m)


(_E_W0, _E_A0, _E_KK, _E_KA, _E_RK, _E_LNW, _E_LNB, _E_BNORM, _E_LB0, _E_LB1) = range(10)


def _even_kernel(p_ref, shift0_ref, srw0_ref, shg0_ref, mu_ref, w2_ref, a2_ref, par_ref,
                 o_ref, shift_ref, srw_ref, shg_ref,
                 prev_sc, srw_sc, shg_sc, kkt_sc, rt_sc, kh_sc, bh_sc, ke_sc, be_sc, v_sc, epos_sc, bonus_sc, oa_sc,
                 qd_sc, ki_sc, kend_sc, bl_sc, ob_sc, xm_sc, m2_sc, o2_sc, arb_sc, kv_sc, bew_sc, *, nb, tb, la, lb):
    j = pl.program_id(1)
    nj = pl.num_programs(1)
    seqs = range(nb)
    cols = lambda lo, hi: jnp.concatenate([p_ref[b, :, lo:hi] for b in seqs], axis=0)

    @pl.when(j == 0)
    def _():
        for b in seqs:
            prev_sc[b] = jnp.broadcast_to(shift0_ref[b], prev_sc.shape[1:])
            srw_sc[b] = jnp.concatenate([srw0_ref[b, h] for h in range(A_H)], axis=1)
            shg_sc[b] = shg0_ref[b].T

    par = par_ref[...]
    row = lambda i: par[i:i + 1]

    per_seq = tb // la
    group = math.gcd(nb * per_seq, 8)
    chains = nb
    run = group // chains
    ngroups = per_seq // run
    prows = run * la
    assert group % chains == 0 and per_seq % run == 0 and prows % lb == 0
    hper = prows // lb
    late_hgrn = ngroups > 1

    def pieces(g):
        if ngroups == 1:
            return [(0, nb * tb)]
        return [(q * tb + g * prows, q * tb + (g + 1) * prows) for q in seqs]

    def rows_of(lo, hi, r0, r1):
        parts = [p_ref[b, max(r0, b * tb) - b * tb:min(r1, (b + 1) * tb) - b * tb, lo:hi]
                 for b in seqs if max(r0, b * tb) < min(r1, (b + 1) * tb)]
        return parts[0] if len(parts) == 1 else jnp.concatenate(parts, axis=0)

    def rwkv_token_stages(r0, r1):
        grows = r1 - r0
        pa = rows_of(0, A_SHIFT_W, r0, r1)
        rolled = pltpu.roll(pa, 1, axis=0)
        starts = range(r0, r1, tb) if grows % tb == 0 else (r0,)
        span = tb if grows % tb == 0 else grows
        first = _iota(pa.shape, 0) % span == 0
        before = [jnp.broadcast_to(prev_sc[s // tb, 0:1], (span, A_SHIFT_W)) for s in starts]
        shifted = jnp.where(first, before[0] if len(before) == 1 else jnp.concatenate(before, axis=0), rolled)
        for s in starts:
            prev_sc[s // tb] = jnp.broadcast_to(pa[s - r0 + span - 1:s - r0 + span], prev_sc.shape[1:])
        xm = pa + (shifted - pa) * mu_ref[...]
        r = xm[:, :MIX]
        k = xm[:, MIX:2 * MIX]
        v = xm[:, 2 * MIX:3 * MIX]
        lora_in = xm[:, 3 * MIX:]
        yield
        lw = _sigmoid(row(_E_W0) + _dot(jnp.tanh(lora_in).astype(BF16), w2_ref[...])) * (-math.exp(-0.5))
        a = _sigmoid(row(_E_A0) + _dot(lora_in.astype(BF16), a2_ref[...]))
        yield
        kk = k * row(_E_KK)
        kk = kk * lax.rsqrt(jnp.maximum(_seg_sum(kk * kk, A_HD), 1e-12))
        yield
        k = k * (1.0 + (a - 1.0) * row(_E_KA))
        bv = kk * a
        cum_a, ones_a = _chunk_mats(grows, la)
        c = _mm(cum_a, lw, "nn", M_EXACT_R)
        yield
        cl = _mm(ones_a, lw, "nn", M_EXACT_R)
        yield
        e_pos = jnp.exp(c)
        e_neg = jnp.exp(-c)
        e_end = jnp.exp(cl - c)
        kkt_sc[r0:r1, :] = kk * jnp.exp(c - lw)
        rt_sc[r0:r1, :] = r * e_pos
        yield
        kh_sc[r0:r1, :] = k * e_neg
        bh_sc[r0:r1, :] = bv * e_neg
        ke_sc[r0:r1, :] = k * e_end
        be_sc[r0:r1, :] = bv * e_end
        yield
        v_sc[r0:r1, :] = v
        epos_sc[r0:r1, :] = e_pos
        bonus_sc[r0:r1, :] = _seg_sum(r * k * row(_E_RK), A_HD) * v

    wide, _, incl, strict = _head_masks(A_H, la, A_HD)
    hl = A_H * la
    ri, ci_ = _iota((hl, hl), 0), _iota((hl, hl), 1)
    eye = (ri == ci_).astype(F32)
    halves = []
    m = 1
    while m < la:
        halves.append(((ri // (2 * m)) == (ci_ // (2 * m))) & ((ri // m) != (ci_ // m)))
        m *= 2
    consts_a = (wide, incl, strict, eye, halves)

    base = A_SHIFT_W + MIX

    def hgrn_token_stages():
        e0 = row(_E_LB0)
        e1 = row(_E_LB1)
        emax = jnp.maximum(e0, e1)
        e0 = jnp.exp(e0 - emax)
        lower = e0 / (e0 + jnp.exp(e1 - emax))
        g = lower + (1.0 - lower) * jax.nn.sigmoid(cols(base + MIX, base + 2 * MIX))
        yield
        logg = jnp.log(g)
        cum_b, ones_b = _chunk_mats(nb * tb, lb)
        yield
        gb = _mm(cum_b, logg, "nn", M_EXACT_R)
        yield
        gl = _mm(ones_b, logg, "nn", M_EXACT_R)
        yield
        qd_sc[...] = _silu(cols(base, base + MIX)) * (B_HD ** -0.5) * jnp.exp(gb)
        yield
        ki_sc[...] = (1.0 - g) * jnp.exp(-gb)
        yield
        kend_sc[...] = (1.0 - g) * jnp.exp(gl - gb)
        bl_sc[...] = gl

    wide_b, _, incl_b, _ = _head_masks(B_H, lb, B_HD)
    consts_b = (wide_b, incl_b)

    def factor_stages(gi, slot):
        sls = [_rows_at(((q * per_seq + gi * run + u) * la, la), la) for q in seqs for u in range(run)]
        hl_a = A_H * la

        def store(u, name, value):
            if name == "m1n":
                xm_sc[slot + u, :hl_a, :] = value
            elif name == "r_w":
                xm_sc[slot + u, hl_a:, :] = value
            else:
                {"m2n": m2_sc, "o2": o2_sc, "a_rb": arb_sc, "kv": kv_sc, "be_w": bew_sc}[name][slot + u] = value

        yield from _rwkv_factor_stages(
            [(kkt_sc[sl, :], rt_sc[sl, :], kh_sc[sl, :], bh_sc[sl, :], ke_sc[sl, :], be_sc[sl, :], v_sc[sl, :])
             for sl in sls], consts_a, store)

    def state_stages(gi):
        slot = (gi % 2) * group
        stages, finish = [], []
        for q in range(chains):
            seq = q
            first = seq * per_seq + gi * run
            row0 = first * la
            out = {}

            def load_factors(u, q=q, first=first):
                i = slot + q * run + u
                return (xm_sc[i], m2_sc[i], o2_sc[i], arb_sc[i], kv_sc[i], bew_sc[i],
                        epos_sc[pl.ds((first + u) * la + la - 1, 1), :])

            def store_oa(u, o, row0=row0):
                oa_sc[_rows_at((row0 + u * la, la), la), :] = o

            def load_b(u, row0=row0, seq=seq):
                r = row0 + u * lb
                sl = _rows_at((r, lb), lb)
                return (qd_sc[sl, :], ki_sc[sl, :], kend_sc[sl, :], bl_sc[pl.ds(r, 1), :],
                        p_ref[seq, _rows_at((r - seq * tb, lb), lb), base + 2 * MIX:base + 3 * MIX])

            def store_ob(u, o, row0=row0):
                ob_sc[_rows_at((row0 + u * lb, lb), lb), :] = o

            stages.append((_rwkv_state_stages(run, load_factors, store_oa, srw_sc[seq], out), 2 * run + 1))
            if late_hgrn and gi == ngroups - 1:
                load_all = functools.partial(load_b, row0=seq * tb)
                store_all = functools.partial(store_ob, row0=seq * tb)
                stages.append((_gla_stages(hper * ngroups, load_all, store_all, shg_sc[seq], consts_b, B_H, B_HD,
                                           out, "hgrn"), 2 * hper * ngroups + 1))
            elif not late_hgrn:
                stages.append((_gla_stages(hper, load_b, store_ob, shg_sc[seq], consts_b, B_H, B_HD, out, "hgrn"),
                               2 * hper + 1))
            finish.append((seq, out))

        def write_back():
            for seq, out in finish:
                srw_sc[seq] = out["rwkv"]
                if "hgrn" in out:
                    shg_sc[seq] = out["hgrn"]

        return stages, write_back

    def rwkv_output_stages(r0, r1):
        o = oa_sc[r0:r1, :]
        mean = _seg_sum(o, A_HD) * (1.0 / A_HD)
        cen = o - mean
        yield
        var = _seg_sum(cen * cen, A_HD) * (1.0 / A_HD)
        yield
        oa = cen * lax.rsqrt(var + A_GN_EPS) * row(_E_LNW) + row(_E_LNB)
        oa = oa + bonus_sc[r0:r1, :]
        put_o(r0, r1, 0, oa * _silu(rows_of(A_SHIFT_W, A_SHIFT_W + MIX, r0, r1)))

    def put_o(r0, r1, lo, value):
        value = value.astype(o_ref.dtype)
        if len(o_ref.shape) == 2:
            o_ref[r0:r1, lo:lo + MIX] = value
        else:
            for b in seqs:
                a, z = max(r0, b * tb), min(r1, (b + 1) * tb)
                if a < z:
                    o_ref[b, a - b * tb:z - b * tb, lo:lo + MIX] = value[a - r0:z - r0]

    def hgrn_output(r0, r1):
        ob = _head_rms(ob_sc[r0:r1, :], row(_E_BNORM), B_HD) * _silu(rows_of(base + 3 * MIX, base + 4 * MIX, r0, r1))
        put_o(r0, r1, MIX, ob)

    for phase in range(ngroups + 3):
        stages, after = [], []
        if 0 <= phase - 2 < ngroups:
            chain_stages, write_back = state_stages(phase - 2)
            stages += chain_stages
            after.append(write_back)
        if 0 <= phase - 1 < ngroups:
            stages.append((factor_stages(phase - 1, ((phase - 1) % 2) * group), 2 * len(halves) + 1))
        if phase < ngroups:
            stages += [(rwkv_token_stages(r0, r1), 8) for r0, r1 in pieces(phase)]
        if phase == 0:
            stages.append((hgrn_token_stages(), 7))
        if 0 <= phase - 3 < ngroups:
            stages += [(rwkv_output_stages(r0, r1), 3) for r0, r1 in pieces(phase - 3)]
        _interleave(*stages)
        for fn in after:
            fn()
    hgrn_output(0, nb * tb)

    @pl.when(j == nj - 1)
    def _():
        for b in seqs:
            shift_ref[b] = prev_sc[b, 0:1]
            s = srw_sc[b]
            for h in range(A_H):
                srw_ref[b, h] = s[:, h * A_HD:(h + 1) * A_HD]
            shg_ref[b] = shg_sc[b].T


def _mixer_out(bsz, t, nb, tb):
    if nb == 1 or tb == t:
        return (pl.BlockSpec((nb * tb, D_MODEL), lambda b, j: (b * (t // tb) + j, 0)),
                jax.ShapeDtypeStruct((bsz * t, D_MODEL), BF16))
    return (pl.BlockSpec((nb, tb, D_MODEL), lambda b, j: (b, j, 0)),
            jax.ShapeDtypeStruct((bsz, t, D_MODEL), BF16))


def _even_mixer(p, shift0, srw0, shg0, mu, w2p, a2p, par, nb, tb, la, lb):
    bsz, t, _ = p.shape
    kern = functools.partial(_even_kernel, nb=nb, tb=tb, la=la, lb=lb)
    nch, hl = 2 * math.gcd(nb * tb // la, 8), A_H * la
    bmap3 = lambda b, j: (b, 0, 0)
    bmap4 = lambda b, j: (b, 0, 0, 0)
    cmap = lambda b, j: (0, 0)
    blk = lambda: pltpu.VMEM((nb * tb, MIX), F32)
    o_spec, o_shape = _mixer_out(bsz, t, nb, tb)
    return pl.pallas_call(
        kern, grid=(bsz // nb, t // tb),
        in_specs=[pl.BlockSpec((nb, tb, IN_EVEN), lambda b, j: (b, j, 0)),
                  pl.BlockSpec((nb, 1, A_SHIFT_W), bmap3),
                  pl.BlockSpec((nb, A_H, A_HD, A_HD), bmap4),
                  pl.BlockSpec((nb, B_H * B_HD, B_HD), bmap3),
                  pl.BlockSpec((1, A_SHIFT_W), cmap),
                  pl.BlockSpec((2 * A_LORA, MIX), cmap),
                  pl.BlockSpec((2 * A_LORA, MIX), cmap),
                  pl.BlockSpec((16, MIX), cmap)],
        out_specs=[o_spec,
                   pl.BlockSpec((nb, 1, A_SHIFT_W), bmap3),
                   pl.BlockSpec((nb, A_H, A_HD, A_HD), bmap4),
                   pl.BlockSpec((nb, B_H * B_HD, B_HD), bmap3)],
        out_shape=[o_shape,
                   jax.ShapeDtypeStruct((bsz, 1, A_SHIFT_W), F32),
                   jax.ShapeDtypeStruct((bsz, A_H, A_HD, A_HD), F32),
                   jax.ShapeDtypeStruct((bsz, B_H * B_HD, B_HD), F32)],
        scratch_shapes=[pltpu.VMEM((nb, 8, A_SHIFT_W), F32), pltpu.VMEM((nb, A_HD, MIX), F32),
                        pltpu.VMEM((nb, B_HD, B_H * B_HD), F32)] + [blk() for _ in range(15)] + [
                            pltpu.VMEM((nch, 2 * hl, MIX), BF16), pltpu.VMEM((nch, hl, A_HD), F32),
                            pltpu.VMEM((nch, hl, A_HD), F32), pltpu.VMEM((nch, hl, hl), BF16),
                            pltpu.VMEM((nch, A_HD, MIX), F32), pltpu.VMEM((nch, hl, MIX), BF16)],
        compiler_params=pltpu.CompilerParams(dimension_semantics=("arbitrary", "arbitrary"),
                                             vmem_limit_bytes=VMEM_LIMIT),
        name="mixer_rwkv_hgrn")(p, shift0, srw0, shg0, mu, w2p, a2p, par)


_O_CQ = 0
_O_CK = _O_CQ + C_KW
_O_CV = _O_CK + C_KW
_O_GC = _O_CV + MIX
_O_DQK = _O_GC + MIX
_O_DV = _O_DQK + 2 * MIX
_O_GD = _O_DV + MIX
_O_SLAB = _O_GD + MIX
(_P_CNORM, _P_DNORM, _P_G2B) = range(3)


def _odd_kernel(oin_ref, xin_ref, woin_ref, gin_ref, win_ref,
                sgl0_ref, conv0_ref, mc0_ref, mn0_ref, mm0_ref, g2_ref, cw_ref, cb_ref, sb_ref, par_ref,
                wo_ref, gf_ref,
                y_ref, sgl_ref, conv_ref, mc_ref, mn_ref, mm_ref,
                prev_sc, sgl_sc, mc_sc, mn_sc, mm_sc, cq_sc, ck_sc, ce_sc, cl_sc, oc_sc,
                dq_sc, dk_sc, li_sc, fa_sc, fl_sc, od_sc, x_sc, p_sc, *, nb, tb, lc):
    j = pl.program_id(1)
    nj = pl.num_programs(1)
    hl = D_H * lc
    seqs = range(nb)
    head_rows = lambda x: jnp.concatenate(
        [jnp.broadcast_to(x[h:h + 1], (lc, x.shape[1])) for h in range(D_H)], axis=0)

    @pl.when(j == 0)
    def _():
        for b in seqs:
            prev_sc[b] = conv0_ref[b]
            sgl_sc[b] = sgl0_ref[b].T
            mc_sc[b] = mc0_ref[b].T
            mn_sc[b] = head_rows(mn0_ref[b])
            mm_sc[b] = head_rows(mm0_ref[b])[:, :1]

    par = par_ref[...]
    row = lambda i: par[i:i + 1]

    def rows_of(lo, hi, r0, r1):
        return p_sc[r0:r1, lo:hi]

    proj_slabs = [(c, min(c + 1024, IN_ODD_PAD)) for c in range(0, IN_ODD_PAD, 1024)]

    def projection_stages(r0, r1):
        x1 = xin_ref[r0:r1, :] + _dot(oin_ref[r0:r1, :], woin_ref[...])
        x_sc[r0:r1, :] = x1
        h = _rms(x1, gin_ref[...]).astype(BF16)
        for lo, hi in proj_slabs:
            yield
            p_sc[r0:r1, lo:hi] = _dot(h, win_ref[:, lo:hi])

    def token_stages(r0, r1):
        n = r1 - r0
        span = tb if n % tb == 0 else n
        cum_c, ones_c = _chunk_mats(n, lc)
        slab = rows_of(_O_SLAB, _O_SLAB + SLAB, r0, r1)
        pre = _dot(slab.astype(BF16), g2_ref[...]) + row(_P_G2B)[:, :C_KW]
        logg = _log_sigmoid(pre) * (1.0 / C_GATE_NORM)
        yield
        gb = _mm(cum_c, logg, "nn", M_EXACT_R)
        gl = _mm(ones_c, logg, "nn", M_EXACT_R)
        yield
        ck = rows_of(_O_CK, _O_CK + C_KW, r0, r1)
        cq_sc[r0:r1, :] = rows_of(_O_CQ, _O_CQ + C_KW, r0, r1) * (C_KD ** -0.5) * jnp.exp(gb)
        ck_sc[r0:r1, :] = ck * jnp.exp(-gb)
        ce_sc[r0:r1, :] = ck * jnp.exp(gl - gb)
        cl_sc[r0:r1, :] = gl
        yield
        x = rows_of(_O_DQK, _O_DQK + 2 * MIX, r0, r1)
        cw = cw_ref[...]
        conv = cb_ref[...] + x * cw[D_CONV - 1:D_CONV]
        head_row = _iota((8, 2 * MIX), 0)
        for s in range(1, D_CONV):
            xr = pltpu.roll(x, s, axis=0)
            parts = []
            for a in range(0, n, span):
                kept = pltpu.roll(prev_sc[(r0 + a) // tb], s, axis=0)
                parts.append(jnp.where(head_row < s, kept, xr[a:a + 8]))
                if span > 8:
                    parts.append(xr[a + 8:a + span])
            xs = parts[0] if len(parts) == 1 else jnp.concatenate(parts, axis=0)
            conv = conv + xs * cw[D_CONV - 1 - s:D_CONV - s]
        for a in range(0, n, span):
            prev_sc[(r0 + a) // tb] = x[a + span - 8:a + span]
        yield
        conv = _silu(conv)
        dq_sc[r0:r1, :] = conv[:, :MIX]
        dk_sc[r0:r1, :] = conv[:, MIX:] * (D_HD ** -0.5)
        yield
        gates = slab + sb_ref[...]
        lf = pltpu.roll(_log_sigmoid(gates), SLAB - (SLAB_F - SLAB_I), axis=1)
        li_sc[r0:r1, :] = gates
        fa_sc[r0:r1, :] = _mm(cum_c, lf, "nn", M_EXACT_R)
        fl_sc[r0:r1, :] = _mm(ones_c, lf, "nn", M_EXACT_R)

    wide_c, _, incl_c, _ = _head_masks(C_H, lc, C_KD)
    consts_c = (wide_c, incl_c)
    lane = _iota((hl, SLAB), 1)
    head = _iota((hl, SLAB), 0) // lc
    pick = lambda z: jnp.sum(jnp.where(lane == head + SLAB_I, _tile_rows(z, D_H), 0.0), -1, keepdims=True)

    wide_d, same_d, incl_d, _ = _head_masks(D_H, lc, D_HD)
    consts_d = (wide_d, same_d, incl_d, same_d.astype(BF16))

    per_seq = tb // lc
    if per_seq == 1:
        chains, run = math.gcd(nb, 4), 1
        trips = nb // chains
        first_chunk = lambda gi, q: gi * chains + q
    else:
        chains, run = nb, math.gcd(per_seq, 4)
        trips = per_seq // run
        first_chunk = lambda gi, q: q * per_seq + gi * run
    assert per_seq % run == 0

    def chunk_stages(gi):
        stages, finish = [], []
        for q in range(chains):
            first = first_chunk(gi, q)
            seq = first // per_seq
            out = {}
            rows = lambda u, first=first: _rows_at(((first + u) * lc, lc), lc)
            in_seq = lambda u, first=first, seq=seq: _rows_at(((first + u) * lc - seq * tb, lc), lc)

            def load_c(u, first=first, seq=seq, rows=rows, in_seq=in_seq):
                sl = rows(u)
                return (cq_sc[sl, :], ck_sc[sl, :], ce_sc[sl, :], cl_sc[pl.ds((first + u) * lc, 1), :],
                        p_sc[sl, _O_CV:_O_CV + MIX])

            def store_c(u, o, rows=rows):
                oc_sc[rows(u), :] = o

            def load_d(u, seq=seq, rows=rows, in_seq=in_seq):
                sl = rows(u)
                return (dq_sc[sl, :], dk_sc[sl, :], p_sc[sl, _O_DV:_O_DV + MIX],
                        pick(li_sc[sl, :]), pick(fa_sc[sl, :]), pick(fl_sc[sl, :]))

            def store_d(u, o, rows=rows):
                od_sc[rows(u), :] = o

            stages.append((_mlstm_stages(run, load_d, store_d, (mc_sc[seq], mn_sc[seq], mm_sc[seq]), consts_d, out),
                           3 * run + 1))
            stages.append((_gla_stages(run, load_c, store_c, sgl_sc[seq], consts_c, C_H, C_VD, out, "gla"),
                           2 * run + 1))
            finish.append((seq, out))

        def write_back():
            for seq, out in finish:
                sgl_sc[seq] = out["gla"]
                mc_sc[seq], mn_sc[seq], mm_sc[seq] = out["mlstm"]

        return stages, write_back

    def output_stages(r0, r1):
        oc = _head_rms(oc_sc[r0:r1, :], row(_P_CNORM), C_VD) * _silu(rows_of(_O_GC, _O_GC + MIX, r0, r1))
        yield
        od = _head_rms(od_sc[r0:r1, :], row(_P_DNORM), D_HD) * _silu(rows_of(_O_GD, _O_GD + MIX, r0, r1))
        o = jnp.concatenate([oc, od], axis=1).astype(BF16)
        yield
        x2 = x_sc[r0:r1, :] + _dot(o, wo_ref[...])
        y_ref[r0:r1, :] = _rms(x2, gf_ref[...])

    if per_seq == 1 or nb > 1:
        _interleave((projection_stages(0, nb * tb), len(proj_slabs) + 1))
        _interleave((token_stages(0, nb * tb), 6))

        def trip(gi, carry):
            stages, write_back = chunk_stages(gi)
            _interleave(*stages)
            write_back()
            return carry

        lax.fori_loop(0, trips, trip, 0)
        _interleave((output_stages(0, nb * tb), 3))
    else:
        grows = run * lc
        for phase in range(trips + 3):
            stages, after = [], []
            if 0 <= phase - 2 < trips:
                chunk, write_back = chunk_stages(phase - 2)
                stages += chunk
                after.append(write_back)
            if phase < trips:
                stages.append((projection_stages(phase * grows, (phase + 1) * grows), len(proj_slabs) + 1))
            if 0 <= phase - 1 < trips:
                stages.append((token_stages((phase - 1) * grows, phase * grows), 6))
            if 0 <= phase - 3 < trips:
                stages.append((output_stages((phase - 3) * grows, (phase - 2) * grows), 3))
            _interleave(*stages)
            for fn in after:
                fn()

    @pl.when(j == nj - 1)
    def _():
        for b in seqs:
            sgl_ref[b] = sgl_sc[b].T
            conv_ref[b] = prev_sc[b]
            mc_ref[b] = mc_sc[b].T
            mn_ref[b] = jnp.concatenate([mn_sc[b, h * lc:h * lc + 1] for h in range(D_H)], axis=0)
            mm_ref[b] = jnp.concatenate(
                [jnp.broadcast_to(mm_sc[b, h * lc:h * lc + 1], (1, LANE)) for h in range(D_H)], axis=0)


def _odd_layer(o_in, x, wo_in, g_in, w_in, sgl0, conv0, mc0, mn0, mm0, g2p, cw, cb, sb, par, wo, gf, bsz, t, nb, tb,
               lc):
    hl = D_H * lc
    once = lambda shape: pl.BlockSpec(shape, lambda b, j: (0, 0), pipeline_mode=pl.Buffered(1))
    kern = functools.partial(_odd_kernel, nb=nb, tb=tb, lc=lc)
    bmap3 = lambda b, j: (b, 0, 0)
    cmap = lambda b, j: (0, 0)
    blk = lambda w: pltpu.VMEM((nb * tb, w), F32)
    assert nb == 1 or tb == t
    rows = pl.BlockSpec((nb * tb, D_MODEL), lambda b, j: (b * (t // tb) + j, 0))
    return pl.pallas_call(
        kern, grid=(bsz // nb, t // tb),
        in_specs=[rows, rows, once((D_MODEL, D_MODEL)), pl.BlockSpec((1, D_MODEL), cmap),
                  once((D_MODEL, IN_ODD_PAD)),
                  pl.BlockSpec((nb, C_KW, C_VD), bmap3),
                  pl.BlockSpec((nb, 8, 2 * MIX), bmap3),
                  pl.BlockSpec((nb, MIX, D_HD), bmap3),
                  pl.BlockSpec((nb, D_H, D_HD), bmap3),
                  pl.BlockSpec((nb, D_H, LANE), bmap3),
                  pl.BlockSpec((SLAB, C_KW), cmap),
                  pl.BlockSpec((D_CONV, 2 * MIX), cmap),
                  pl.BlockSpec((1, 2 * MIX), cmap),
                  pl.BlockSpec((1, SLAB), cmap),
                  pl.BlockSpec((8, MIX), cmap),
                  once((D_MODEL, D_MODEL)),
                  pl.BlockSpec((1, D_MODEL), cmap)],
        out_specs=[rows,
                   pl.BlockSpec((nb, C_KW, C_VD), bmap3),
                   pl.BlockSpec((nb, 8, 2 * MIX), bmap3),
                   pl.BlockSpec((nb, MIX, D_HD), bmap3),
                   pl.BlockSpec((nb, D_H, D_HD), bmap3),
                   pl.BlockSpec((nb, D_H, LANE), bmap3)],
        out_shape=[jax.ShapeDtypeStruct((bsz * t, D_MODEL), F32),
                   jax.ShapeDtypeStruct((bsz, C_KW, C_VD), F32),
                   jax.ShapeDtypeStruct((bsz, 8, 2 * MIX), F32),
                   jax.ShapeDtypeStruct((bsz, MIX, D_HD), F32),
                   jax.ShapeDtypeStruct((bsz, D_H, D_HD), F32),
                   jax.ShapeDtypeStruct((bsz, D_H, LANE), F32)],
        scratch_shapes=[pltpu.VMEM((nb, 8, 2 * MIX), F32), pltpu.VMEM((nb, C_VD, C_KW), F32),
                        pltpu.VMEM((nb, D_HD, MIX), F32), pltpu.VMEM((nb, hl, D_HD), F32),
                        pltpu.VMEM((nb, hl, 1), F32),
                        blk(C_KW), blk(C_KW), blk(C_KW), blk(C_KW), blk(MIX),
                        blk(MIX), blk(MIX), blk(SLAB), blk(SLAB), blk(SLAB), blk(MIX),
                        blk(D_MODEL), blk(IN_ODD_PAD)],
        compiler_params=pltpu.CompilerParams(dimension_semantics=("arbitrary", "arbitrary"),
                                             vmem_limit_bytes=VMEM_LIMIT),
        name="layer_gla_mlstm")(o_in, x, wo_in, g_in, w_in, sgl0, conv0, mc0, mn0, mm0, g2p, cw, cb, sb, par, wo, gf)


MIXER_ROWS = 256
SHORT_SEQS = 8


def _mixer_blocking(bsz, t):
    tb = math.gcd(t, MIXER_ROWS)
    nb = math.gcd(bsz, SHORT_SEQS) if tb == t and t * SHORT_SEQS <= MIXER_ROWS else 1
    return nb, tb


def _paired_blocking(bsz, nb, tb):
    if nb == 1 and bsz % 2 == 0 and (tb // 2) % 32 == 0:
        return 2, tb // 2
    return nb, tb


def _run(x, shift, s_rwkv, s_hgrn, s_gla, conv, mc, mn, mm, wts):
    bsz, t, d = x.shape
    nb, tb = _mixer_blocking(bsz, t)
    la = math.gcd(t, 16)
    lbc = math.gcd(t, 32)
    x2 = x.reshape(bsz * t, d)
    p0 = _inproj(x2, wts["g0"], wts["w_in0"])
    o0, shift_n, srw_n, shg_n = _even_mixer(
        p0.reshape(bsz, t, IN_EVEN), shift[0][:, None], s_rwkv[0], s_hgrn[0].reshape(bsz, B_H * B_HD, B_HD),
        wts["mu"], wts["w2p"], wts["a2p"], wts["par_e"], *_paired_blocking(bsz, nb, tb), la, lbc)
    conv8 = jnp.pad(conv[0], ((0, 0), (8 - (D_CONV - 1), 0), (0, 0)))
    mm_l = jnp.broadcast_to(mm[0][:, :, None], (bsz, D_H, LANE))
    y, sgl_n, conv_n, mc_n, mn_n, mm_n = _odd_layer(
        o0.reshape(bsz * t, d), x2, wts["w_out0"], wts["g1"], wts["w_in1"],
        s_gla[0].reshape(bsz, C_KW, C_VD), conv8, mc[0].reshape(bsz, MIX, D_HD), mn[0], mm_l,
        wts["g2p"], wts["cw"], wts["cb"], wts["sb"], wts["par_o"], wts["w_out1"], wts["gf"], bsz, t, nb, tb, lbc)
    return (y.reshape(bsz, t, d), shift_n.reshape(1, bsz, A_SHIFT_W), srw_n[None],
            shg_n.reshape(1, bsz, B_H, B_HD, B_HD), sgl_n.reshape(1, bsz, C_H, C_KD, C_VD),
            conv_n[None, :, 8 - (D_CONV - 1):], mc_n.reshape(1, bsz, D_H, D_HD, D_HD), mn_n[None],
            mm_n[None, :, :, 0])


def _odd_in_projection(w):
    sizes = (C_KW, C_KW, MIX, C_LORA, MIX, 2 * MIX, MIX, D_H, D_H, MIX)
    off = np.cumsum((0,) + sizes)
    cq, ck, cv, cg, gc, dqk, dv, di, df, gd = (w[:, off[i]:off[i + 1]] for i in range(len(sizes)))
    pad = jnp.zeros((w.shape[0], SLAB - C_LORA - 2 * D_H), w.dtype)
    return jnp.concatenate([cq, ck, cv, gc, dqk, dv, gd, cg, di, df, pad], axis=1)


def kernel(x_prompt, x_sample, state_shift_a, state_rwkv, state_hgrn, state_gla, state_conv_d, state_mlstm_c,
           state_mlstm_n, state_mlstm_m, norm_g, w_in_even, w_out_even, a_mu, a_w0, a_w2, a_a0, a_a2, a_kk,
           a_ka, a_rk, a_ln_w, a_ln_b, b_lb, b_norm, w_in_odd, w_out_odd, c_g2, c_g2b, c_norm, d_conv_w,
           d_conv_b, d_ib, d_fb, d_norm, final_norm):
    assert w_in_even.shape[0] == 1 and w_in_odd.shape[0] == 1 and b_lb.shape[0] == 2
    zpad = lambda a, rows_before, rows_total: jnp.pad(a, ((rows_before, rows_total - rows_before - a.shape[0]), (0, 0)))
    par_e = jnp.concatenate([a_w0, a_a0, a_kk, a_ka, a_rk, a_ln_w, a_ln_b, b_norm, b_lb[0:1], b_lb[1:2]], axis=0)
    g2b = jnp.pad(c_g2b, ((0, 0), (0, MIX - C_KW)))
    par_o = jnp.concatenate([c_norm, d_norm, g2b], axis=0)
    w_in1 = _odd_in_projection(w_in_odd[0].astype(BF16))
    sb = jnp.pad(jnp.concatenate([d_ib, d_fb], axis=1), ((0, 0), (SLAB_I, SLAB - SLAB_I - 2 * D_H)))
    wts = {
        "g0": norm_g[0:1], "g1": norm_g[1:2], "gf": final_norm[None],
        "w_in0": w_in_even[0].astype(BF16), "w_out0": w_out_even[0].astype(BF16),
        "w_in1": w_in1, "w_out1": w_out_odd[0].astype(BF16),
        "mu": a_mu, "w2p": zpad(a_w2[0], 0, 2 * A_LORA).astype(BF16),
        "a2p": zpad(a_a2[0], A_LORA, 2 * A_LORA).astype(BF16),
        "par_e": zpad(par_e, 0, 16), "par_o": zpad(par_o, 0, 8),
        "g2p": zpad(c_g2[0], 0, SLAB).astype(BF16), "cw": d_conv_w[0], "cb": d_conv_b, "sb": sb,
    }
    bp = x_prompt.shape[0]
    z = lambda *s: jnp.zeros(s, x_prompt.dtype)
    prompt = _run(x_prompt, z(1, bp, A_SHIFT_W), z(1, bp, A_H, A_HD, A_HD), z(1, bp, B_H, B_HD, B_HD),
                  z(1, bp, C_H, C_KD, C_VD), z(1, bp, D_CONV - 1, 2 * MIX), z(1, bp, D_H, D_HD, D_HD),
                  z(1, bp, D_H, D_HD), z(1, bp, D_H), wts)
    sample = _run(x_sample, state_shift_a, state_rwkv, state_hgrn, state_gla, state_conv_d, state_mlstm_c,
                  state_mlstm_n, state_mlstm_m, wts)
    return (prompt[0], sample[0]) + prompt[1:] + sample[1:]
```

```python
import functools
import math

import jax
import jax.numpy as jnp
import numpy as np
from jax import lax
from jax.experimental import pallas as pl
from jax.experimental.pallas import tpu as pltpu

F32 = jnp.float32
BF16 = jnp.bfloat16

D_MODEL = 1024
MIX = D_MODEL // 2
NORM_EPS = 1e-5
A_HD = 64
A_H = MIX // A_HD
A_LORA = 64
A_GN_EPS = 64e-5
A_SHIFT_W = 3 * MIX + 2 * A_LORA
B_HD = 128
B_H = MIX // B_HD
IN_EVEN = A_SHIFT_W + 5 * MIX
C_H = 4
C_VD = MIX // C_H
C_KD = C_VD // 2
C_KW = C_H * C_KD
C_LORA = 16
C_GATE_NORM = 16.0
D_H = 4
D_HD = MIX // D_H
D_CONV = 4
LANE = 128
SLAB = LANE
SLAB_I = C_LORA
SLAB_F = C_LORA + D_H
IN_ODD_PAD = 2 * C_KW + 2 * MIX + 2 * MIX + 2 * MIX + SLAB
VMEM_LIMIT = 48 * 1024 * 1024


_DIMS = {"nn": ((1,), (0,)), "nt": ((1,), (1,)), "tn": ((0,), (0,))}
M_EXACT_R = "rx"
M_EXACT_L = "lx"
M_G = "bf"
M_TINV = "bf"
M_ST = "bf"
M_ATT = "bf"


def _split2(x):
    hi = x.astype(BF16)
    return hi, (x - hi.astype(F32)).astype(BF16)


def _mm(a, b, form, mode):
    dn = (_DIMS[form], ((), ()))
    d = lambda x, y: lax.dot_general(x, y, dn, preferred_element_type=F32)
    if mode == "bf":
        return d(a.astype(BF16), b.astype(BF16))
    if mode == "x3":
        ah, al = _split2(a)
        bh, bl = _split2(b)
        return d(ah, bh) + (d(ah, bl) + d(al, bh))
    if mode == "lx":
        bb = b.astype(BF16)
        h, l = _split2(a)
        return d(h, bb) + d(l, bb)
    assert mode == "rx"
    ab = a.astype(BF16)
    h, l = _split2(b)
    return d(ab, h) + d(ab, l)


def _dot(a, b):
    return jnp.dot(a, b, preferred_element_type=F32)


def _iota(shape, dim):
    return lax.broadcasted_iota(jnp.int32, shape, dim)


def _sigmoid(x):
    return 0.5 + 0.5 * jnp.tanh(0.5 * x)


def _log_sigmoid(x):
    return jnp.minimum(x, 0.0) - jnp.log(1.0 + jnp.exp(-jnp.abs(x)))


def _silu(x):
    return x * _sigmoid(x)


def _rms(x, g):
    return x * lax.rsqrt(jnp.mean(x * x, -1, keepdims=True) + NORM_EPS) * g


def _rows_at(start_and_alignment, size):
    start, alignment = start_and_alignment
    if not isinstance(start, int):
        start = pl.multiple_of(start, alignment)
    return pl.ds(start, size)


def _unroll(trips):
    return 2 if trips % 2 == 0 else 1


def _tile_rows(x, n):
    return jnp.concatenate([x] * n, axis=0)


def _stack_heads(x, n_heads, width):
    return jnp.concatenate([x[:, h * width:(h + 1) * width] for h in range(n_heads)], axis=0)


def _unstack_heads(x, n_heads, rows):
    return jnp.concatenate([x[h * rows:(h + 1) * rows] for h in range(n_heads)], axis=1)


def _head_masks(n_heads, rows, width):
    hl = n_heads * rows
    rh = _iota((hl, n_heads * width), 0) // rows
    lh = _iota((hl, n_heads * width), 1) // width
    wide = (rh == lh).astype(F32)
    ri, ci = _iota((hl, hl), 0), _iota((hl, hl), 1)
    same = (ri // rows) == (ci // rows)
    incl = same & (ci <= ri)
    strict = same & (ci < ri)
    return wide, same, incl, strict


def _chunk_mats(n, chunk):
    ri, ci = _iota((n, n), 0), _iota((n, n), 1)
    same = (ri // chunk) == (ci // chunk)
    return (same & (ci <= ri)).astype(BF16), same.astype(BF16)


def _seg_sum(x, seg):
    bd = ((_iota((LANE, LANE), 0) // seg) == (_iota((LANE, LANE), 1) // seg)).astype(BF16)
    parts = [_mm(x[:, j:j + LANE], bd, "nn", M_EXACT_L) for j in range(0, x.shape[1], LANE)]
    return jnp.concatenate(parts, axis=1)


def _head_rms(x, g, width):
    parts = []
    for j in range(0, x.shape[1], width):
        xs = x[:, j:j + width]
        parts.append(xs * lax.rsqrt(jnp.mean(xs * xs, -1, keepdims=True) + NORM_EPS))
    return jnp.concatenate(parts, axis=1) * g


def _inproj_kernel(x_ref, g_ref, w_ref, p_ref):
    h = _rms(x_ref[...], g_ref[...])
    p_ref[...] = _dot(h.astype(BF16), w_ref[...])


def _mid_kernel(o_ref, x_ref, wo_ref, g_ref, wi_ref, x1_ref, p_ref):
    x1 = x_ref[...] + _dot(o_ref[...], wo_ref[...])
    x1_ref[...] = x1
    p_ref[...] = _dot(_rms(x1, g_ref[...]).astype(BF16), wi_ref[...])


def _row_tile(m):
    return math.gcd(m, 512)


def _full(shape):
    return pl.BlockSpec(shape, lambda i: (0,) * len(shape))


def _dense_params():
    return pltpu.CompilerParams(dimension_semantics=("arbitrary",), vmem_limit_bytes=VMEM_LIMIT)


def _inproj(x, g, w):
    m, d = x.shape
    n = w.shape[1]
    tm = _row_tile(m)
    return pl.pallas_call(
        _inproj_kernel, grid=(m // tm,),
        in_specs=[pl.BlockSpec((tm, d), lambda i: (i, 0)), _full((1, d)), _full((d, n))],
        out_specs=pl.BlockSpec((tm, n), lambda i: (i, 0)),
        out_shape=jax.ShapeDtypeStruct((m, n), F32),
        compiler_params=_dense_params(), name="inproj")(x, g, w)


def _mid(o, x, wo, g, wi):
    m, d = x.shape
    n = wi.shape[1]
    tm = _row_tile(m)
    return pl.pallas_call(
        _mid_kernel, grid=(m // tm,),
        in_specs=[pl.BlockSpec((tm, d), lambda i: (i, 0)), pl.BlockSpec((tm, d), lambda i: (i, 0)),
                  _full((d, d)), _full((1, d)), _full((d, n))],
        out_specs=[pl.BlockSpec((tm, d), lambda i: (i, 0)), pl.BlockSpec((tm, n), lambda i: (i, 0))],
        out_shape=[jax.ShapeDtypeStruct((m, d), F32), jax.ShapeDtypeStruct((m, n), F32)],
        compiler_params=_dense_params(), name="outproj_inproj")(o, x, wo, g, wi)


def _interleave(*staged):
    live = [[gen, 0, max(n, 1)] for gen, n in staged]
    while live:
        item = min(live, key=lambda it: it[1] / it[2])
        try:
            next(item[0])
            item[1] += 1
        except StopIteration:
            live.remove(item)


def _rwkv_factor_stages(chunks, consts, store):
    wide, incl, strict, eye, halves = consts
    L = chunks[0][0].shape[0]
    hl = A_H * L
    n = range(len(chunks))
    tile = functools.partial(_tile_rows, n=A_H)
    for i in n:
        store(i, "r_w", (tile(chunks[i][1]) * wide).astype(BF16))
        store(i, "be_w", (tile(chunks[i][5]) * wide).astype(BF16))
    stack = lambda x: _stack_heads(x, A_H, A_HD)
    g = [_mm(jnp.concatenate([stack(c[0]), stack(c[1])], axis=0),
             jnp.concatenate([stack(c[2]), stack(c[3])], axis=0), "nt", M_G) for c in chunks]
    yield
    a_kk = [jnp.where(strict, x[:hl, :hl], 0.0) for x in g]
    a_kb = [jnp.where(strict, x[:hl, hl:], 0.0) for x in g]
    a_rk = [jnp.where(incl, x[hl:, :hl], 0.0) for x in g]
    for i in n:
        store(i, "a_rb", jnp.where(incl, g[i][hl:, hl:], 0.0).astype(BF16))
    t = [eye - jnp.where(halves[0], x, 0.0) for x in a_kb]
    v_st = [_stack_heads(c[6], A_H, A_HD) for c in chunks]
    av = [_mm(a_kk[i], v_st[i], "nn", M_ST) for i in n]
    for i in n:
        store(i, "o2", _mm(a_rk[i], v_st[i], "nn", M_ST))
        store(i, "kv", _mm(v_st[i], tile(chunks[i][4]) * wide, "tn", M_ST))
    for half in halves[1:]:
        ta = [_mm(t[i], jnp.where(half, a_kb[i], 0.0), "nn", M_TINV) for i in n]
        yield
        t = [t[i] - _mm(ta[i], t[i], "nn", M_TINV) for i in n]
        yield
    for i in n:
        store(i, "m1n", (-_mm(t[i], tile(chunks[i][0]) * wide, "nn", M_ST)).astype(BF16))
        store(i, "m2n", -_mm(t[i], av[i], "nn", M_ST))
    yield


def _rwkv_state_stages(n_chunks, load_factors, store_o, s, out):
    for c in range(n_chunks):
        xm, m2n, o2, a_rb, kv, be_w, ecl = load_factors(c)
        hl = xm.shape[0] // 2
        xs = _mm(xm, s, "nt", M_ST)
        yield
        u = xs[:hl] + m2n
        s = s * ecl + kv + _mm(u, be_w, "tn", M_ST)
        o_st = xs[hl:] + o2 + _mm(a_rb, u, "nn", M_ST)
        yield
        store_o(c, _unstack_heads(o_st, A_H, hl // A_H))
    out["rwkv"] = s


def _gla_stages(n_chunks, load_chunk, store_o, s, consts, n_heads, vd, out, key):
    wide, incl = consts
    tile = functools.partial(_tile_rows, n=n_heads)
    for c in range(n_chunks):
        qd, ki, kend, bl, v = load_chunk(c)
        q_w = tile(qd) * wide
        v_st = _stack_heads(v, n_heads, vd)
        att = jnp.where(incl, _mm(q_w, tile(ki), "nt", M_ATT), 0.0)
        kv = _mm(v_st, tile(kend) * wide, "tn", M_ATT)
        yield
        o_st = _mm(att, v_st, "nn", M_ATT) + _mm(q_w, s, "nt", M_ATT)
        s = s * jnp.exp(bl) + kv
        yield
        store_o(c, _unstack_heads(o_st, n_heads, qd.shape[0]))
    out[key] = s


def _mlstm_stages(n_chunks, load_chunk, store_o, state, consts, out):
    wide, same, incl, ones_bd = consts
    c, nrow, m = state
    tile = functools.partial(_tile_rows, n=D_H)
    for ch in range(n_chunks):
        q, k, v, li, a, a_last = load_chunk(ch)
        L = q.shape[0]
        hl = D_H * L
        to_row = lambda col: jnp.broadcast_to(col, (hl, LANE)).T[:hl]
        q_st = _stack_heads(q, D_H, D_HD)
        k_st = _stack_heads(k, D_H, D_HD)
        v_st = _stack_heads(v, D_H, D_HD)
        qk = _mm(q_st, k_st, "nt", M_ATT)
        dlog = jnp.where(incl, a - to_row(a) + to_row(li), -jnp.inf)
        m_intra = jnp.max(dlog, -1, keepdims=True)
        e_end = a_last - a + li
        m_loc = jnp.max(jnp.where(same, to_row(e_end), -jnp.inf), -1, keepdims=True)
        yield
        s_intra = jnp.exp(dlog - m_intra) * qk
        num_intra = _mm(s_intra, v_st, "nn", M_ATT)
        den_intra = jnp.sum(s_intra, -1, keepdims=True)
        w_end = jnp.exp(e_end - m_loc)
        c_loc = _mm(k_st, tile(v) * wide * w_end, "tn", M_ATT)
        n_loc = _mm(ones_bd, w_end * k_st, "nn", M_ATT)
        yield
        g = a + m
        m_t = jnp.maximum(g, m_intra)
        f_inter = jnp.exp(g - m_t)
        f_intra = jnp.exp(m_intra - m_t)
        qc = _mm(q_st, c, "nn", M_ATT)
        num_inter = jnp.concatenate(
            [qc[h * L:(h + 1) * L, h * D_HD:(h + 1) * D_HD] for h in range(D_H)], axis=0)
        den_inter = jnp.sum(q_st * nrow, -1, keepdims=True)
        num = num_intra * f_intra + num_inter * f_inter
        den = f_intra * den_intra + f_inter * den_inter
        h_st = num / jnp.maximum(jnp.abs(den), jnp.exp(-m_t))
        m_new = jnp.maximum(a_last + m, m_loc)
        fo = jnp.exp(a_last + m - m_new)
        fl = jnp.exp(m_loc - m_new)
        per_value = lambda col: jnp.concatenate(
            [jnp.broadcast_to(col[h * L:h * L + 1], (1, D_HD)) for h in range(D_H)], axis=1)
        c = per_value(fo) * c + per_value(fl) * c_loc
        nrow = fo * nrow + fl * n_loc
        m = m_new
        yield
        store_o(ch, _unstack_heads(h_st, D_H, L))
    out["mlstm"] = (c, nrow, m)


(_E_W0, _E_A0, _E_KK, _E_KA, _E_RK, _E_LNW, _E_LNB, _E_BNORM, _E_LB0, _E_LB1) = range(10)


def _even_kernel(p_ref, shift0_ref, srw0_ref, shg0_ref, mu_ref, w2_ref, a2_ref, par_ref,
                 o_ref, shift_ref, srw_ref, shg_ref,
                 prev_sc, srw_sc, shg_sc, kkt_sc, rt_sc, kh_sc, bh_sc, ke_sc, be_sc, v_sc, epos_sc, bonus_sc, oa_sc,
                 qd_sc, ki_sc, kend_sc, bl_sc, ob_sc, xm_sc, m2_sc, o2_sc, arb_sc, kv_sc, bew_sc, *, nb, tb, la, lb):
    j = pl.program_id(1)
    nj = pl.num_programs(1)
    seqs = range(nb)
    cols = lambda lo, hi: jnp.concatenate([p_ref[b, :, lo:hi] for b in seqs], axis=0)

    @pl.when(j == 0)
    def _():
        for b in seqs:
            prev_sc[b] = jnp.broadcast_to(shift0_ref[b], prev_sc.shape[1:])
            srw_sc[b] = jnp.concatenate([srw0_ref[b, h] for h in range(A_H)], axis=1)
            shg_sc[b] = shg0_ref[b].T

    par = par_ref[...]
    row = lambda i: par[i:i + 1]

    per_seq = tb // la
    group = math.gcd(nb * per_seq, 8)
    chains = nb
    run = group // chains
    ngroups = per_seq // run
    prows = run * la
    assert group % chains == 0 and per_seq % run == 0 and prows % lb == 0
    hper = prows // lb
    late_hgrn = ngroups > 1

    def pieces(g):
        if ngroups == 1:
            return [(0, nb * tb)]
        return [(q * tb + g * prows, q * tb + (g + 1) * prows) for q in seqs]

    def rows_of(lo, hi, r0, r1):
        parts = [p_ref[b, max(r0, b * tb) - b * tb:min(r1, (b + 1) * tb) - b * tb, lo:hi]
                 for b in seqs if max(r0, b * tb) < min(r1, (b + 1) * tb)]
        return parts[0] if len(parts) == 1 else jnp.concatenate(parts, axis=0)

    def rwkv_token_stages(r0, r1):
        grows = r1 - r0
        pa = rows_of(0, A_SHIFT_W, r0, r1)
        rolled = pltpu.roll(pa, 1, axis=0)
        starts = range(r0, r1, tb) if grows % tb == 0 else (r0,)
        span = tb if grows % tb == 0 else grows
        first = _iota(pa.shape, 0) % span == 0
        before = [jnp.broadcast_to(prev_sc[s // tb, 0:1], (span, A_SHIFT_W)) for s in starts]
        shifted = jnp.where(first, before[0] if len(before) == 1 else jnp.concatenate(before, axis=0), rolled)
        for s in starts:
            prev_sc[s // tb] = jnp.broadcast_to(pa[s - r0 + span - 1:s - r0 + span], prev_sc.shape[1:])
        xm = pa + (shifted - pa) * mu_ref[...]
        r = xm[:, :MIX]
        k = xm[:, MIX:2 * MIX]
        v = xm[:, 2 * MIX:3 * MIX]
        lora_in = xm[:, 3 * MIX:]
        yield
        lw = _sigmoid(row(_E_W0) + _dot(jnp.tanh(lora_in).astype(BF16), w2_ref[...])) * (-math.exp(-0.5))
        a = _sigmoid(row(_E_A0) + _dot(lora_in.astype(BF16), a2_ref[...]))
        yield
        kk = k * row(_E_KK)
        kk = kk * lax.rsqrt(jnp.maximum(_seg_sum(kk * kk, A_HD), 1e-12))
        yield
        k = k * (1.0 + (a - 1.0) * row(_E_KA))
        bv = kk * a
        cum_a, ones_a = _chunk_mats(grows, la)
        c = _mm(cum_a, lw, "nn", M_EXACT_R)
        yield
        cl = _mm(ones_a, lw, "nn", M_EXACT_R)
        yield
        e_pos = jnp.exp(c)
        e_neg = jnp.exp(-c)
        e_end = jnp.exp(cl - c)
        kkt_sc[r0:r1, :] = kk * jnp.exp(c - lw)
        rt_sc[r0:r1, :] = r * e_pos
        yield
        kh_sc[r0:r1, :] = k * e_neg
        bh_sc[r0:r1, :] = bv * e_neg
        ke_sc[r0:r1, :] = k * e_end
        be_sc[r0:r1, :] = bv * e_end
        yield
        v_sc[r0:r1, :] = v
        epos_sc[r0:r1, :] = e_pos
        bonus_sc[r0:r1, :] = _seg_sum(r * k * row(_E_RK), A_HD) * v

    wide, _, incl, strict = _head_masks(A_H, la, A_HD)
    hl = A_H * la
    ri, ci_ = _iota((hl, hl), 0), _iota((hl, hl), 1)
    eye = (ri == ci_).astype(F32)
    halves = []
    m = 1
    while m < la:
        halves.append(((ri // (2 * m)) == (ci_ // (2 * m))) & ((ri // m) != (ci_ // m)))
        m *= 2
    consts_a = (wide, incl, strict, eye, halves)

    base = A_SHIFT_W + MIX

    def hgrn_token_stages():
        e0 = row(_E_LB0)
        e1 = row(_E_LB1)
        emax = jnp.maximum(e0, e1)
        e0 = jnp.exp(e0 - emax)
        lower = e0 / (e0 + jnp.exp(e1 - emax))
        g = lower + (1.0 - lower) * jax.nn.sigmoid(cols(base + MIX, base + 2 * MIX))
        yield
        logg = jnp.log(g)
        cum_b, ones_b = _chunk_mats(nb * tb, lb)
        yield
        gb = _mm(cum_b, logg, "nn", M_EXACT_R)
        yield
        gl = _mm(ones_b, logg, "nn", M_EXACT_R)
        yield
        qd_sc[...] = _silu(cols(base, base + MIX)) * (B_HD ** -0.5) * jnp.exp(gb)
        yield
        ki_sc[...] = (1.0 - g) * jnp.exp(-gb)
        yield
        kend_sc[...] = (1.0 - g) * jnp.exp(gl - gb)
        bl_sc[...] = gl

    wide_b, _, incl_b, _ = _head_masks(B_H, lb, B_HD)
    consts_b = (wide_b, incl_b)

    def factor_stages(gi, slot):
        sls = [_rows_at(((q * per_seq + gi * run + u) * la, la), la) for q in seqs for u in range(run)]
        hl_a = A_H * la

        def store(u, name, value):
            if name == "m1n":
                xm_sc[slot + u, :hl_a, :] = value
            elif name == "r_w":
                xm_sc[slot + u, hl_a:, :] = value
            else:
                {"m2n": m2_sc, "o2": o2_sc, "a_rb": arb_sc, "kv": kv_sc, "be_w": bew_sc}[name][slot + u] = value

        yield from _rwkv_factor_stages(
            [(kkt_sc[sl, :], rt_sc[sl, :], kh_sc[sl, :], bh_sc[sl, :], ke_sc[sl, :], be_sc[sl, :], v_sc[sl, :])
             for sl in sls], consts_a, store)

    def state_stages(gi):
        slot = (gi % 2) * group
        stages, finish = [], []
        for q in range(chains):
            seq = q
            first = seq * per_seq + gi * run
            row0 = first * la
            out = {}

            def load_factors(u, q=q, first=first):
                i = slot + q * run + u
                return (xm_sc[i], m2_sc[i], o2_sc[i], arb_sc[i], kv_sc[i], bew_sc[i],
                        epos_sc[pl.ds((first + u) * la + la - 1, 1), :])

            def store_oa(u, o, row0=row0):
                oa_sc[_rows_at((row0 + u * la, la), la), :] = o

            def load_b(u, row0=row0, seq=seq):
                r = row0 + u * lb
                sl = _rows_at((r, lb), lb)
                return (qd_sc[sl, :], ki_sc[sl, :], kend_sc[sl, :], bl_sc[pl.ds(r, 1), :],
                        p_ref[seq, _rows_at((r - seq * tb, lb), lb), base + 2 * MIX:base + 3 * MIX])

            def store_ob(u, o, row0=row0):
                ob_sc[_rows_at((row0 + u * lb, lb), lb), :] = o

            stages.append((_rwkv_state_stages(run, load_factors, store_oa, srw_sc[seq], out), 2 * run + 1))
            if late_hgrn and gi == ngroups - 1:
                load_all = functools.partial(load_b, row0=seq * tb)
                store_all = functools.partial(store_ob, row0=seq * tb)
                stages.append((_gla_stages(hper * ngroups, load_all, store_all, shg_sc[seq], consts_b, B_H, B_HD,
                                           out, "hgrn"), 2 * hper * ngroups + 1))
            elif not late_hgrn:
                stages.append((_gla_stages(hper, load_b, store_ob, shg_sc[seq], consts_b, B_H, B_HD, out, "hgrn"),
                               2 * hper + 1))
            finish.append((seq, out))

        def write_back():
            for seq, out in finish:
                srw_sc[seq] = out["rwkv"]
                if "hgrn" in out:
                    shg_sc[seq] = out["hgrn"]

        return stages, write_back

    def rwkv_output_stages(r0, r1):
        o = oa_sc[r0:r1, :]
        mean = _seg_sum(o, A_HD) * (1.0 / A_HD)
        cen = o - mean
        yield
        var = _seg_sum(cen * cen, A_HD) * (1.0 / A_HD)
        yield
        oa = cen * lax.rsqrt(var + A_GN_EPS) * row(_E_LNW) + row(_E_LNB)
        oa = oa + bonus_sc[r0:r1, :]
        put_o(r0, r1, 0, oa * _silu(rows_of(A_SHIFT_W, A_SHIFT_W + MIX, r0, r1)))

    def put_o(r0, r1, lo, value):
        value = value.astype(o_ref.dtype)
        if len(o_ref.shape) == 2:
            o_ref[r0:r1, lo:lo + MIX] = value
        else:
            for b in seqs:
                a, z = max(r0, b * tb), min(r1, (b + 1) * tb)
                if a < z:
                    o_ref[b, a - b * tb:z - b * tb, lo:lo + MIX] = value[a - r0:z - r0]

    def hgrn_output(r0, r1):
        ob = _head_rms(ob_sc[r0:r1, :], row(_E_BNORM), B_HD) * _silu(rows_of(base + 3 * MIX, base + 4 * MIX, r0, r1))
        put_o(r0, r1, MIX, ob)

    for phase in range(ngroups + 3):
        stages, after = [], []
        if 0 <= phase - 2 < ngroups:
            chain_stages, write_back = state_stages(phase - 2)
            stages += chain_stages
            after.append(write_back)
        if 0 <= phase - 1 < ngroups:
            stages.append((factor_stages(phase - 1, ((phase - 1) % 2) * group), 2 * len(halves) + 1))
        if phase < ngroups:
            stages += [(rwkv_token_stages(r0, r1), 8) for r0, r1 in pieces(phase)]
        if phase == 0:
            stages.append((hgrn_token_stages(), 7))
        if 0 <= phase - 3 < ngroups:
            stages += [(rwkv_output_stages(r0, r1), 3) for r0, r1 in pieces(phase - 3)]
        _interleave(*stages)
        for fn in after:
            fn()
    hgrn_output(0, nb * tb)

    @pl.when(j == nj - 1)
    def _():
        for b in seqs:
            shift_ref[b] = prev_sc[b, 0:1]
            s = srw_sc[b]
            for h in range(A_H):
                srw_ref[b, h] = s[:, h * A_HD:(h + 1) * A_HD]
            shg_ref[b] = shg_sc[b].T


def _mixer_out(bsz, t, nb, tb):
    if nb == 1 or tb == t:
        return (pl.BlockSpec((nb * tb, D_MODEL), lambda b, j: (b * (t // tb) + j, 0)),
                jax.ShapeDtypeStruct((bsz * t, D_MODEL), BF16))
    return (pl.BlockSpec((nb, tb, D_MODEL), lambda b, j: (b, j, 0)),
            jax.ShapeDtypeStruct((bsz, t, D_MODEL), BF16))


def _even_mixer(p, shift0, srw0, shg0, mu, w2p, a2p, par, nb, tb, la, lb):
    bsz, t, _ = p.shape
    kern = functools.partial(_even_kernel, nb=nb, tb=tb, la=la, lb=lb)
    nch, hl = 2 * math.gcd(nb * tb // la, 8), A_H * la
    bmap3 = lambda b, j: (b, 0, 0)
    bmap4 = lambda b, j: (b, 0, 0, 0)
    cmap = lambda b, j: (0, 0)
    blk = lambda: pltpu.VMEM((nb * tb, MIX), F32)
    o_spec, o_shape = _mixer_out(bsz, t, nb, tb)
    return pl.pallas_call(
        kern, grid=(bsz // nb, t // tb),
        in_specs=[pl.BlockSpec((nb, tb, IN_EVEN), lambda b, j: (b, j, 0)),
                  pl.BlockSpec((nb, 1, A_SHIFT_W), bmap3),
                  pl.BlockSpec((nb, A_H, A_HD, A_HD), bmap4),
                  pl.BlockSpec((nb, B_H * B_HD, B_HD), bmap3),
                  pl.BlockSpec((1, A_SHIFT_W), cmap),
                  pl.BlockSpec((2 * A_LORA, MIX), cmap),
                  pl.BlockSpec((2 * A_LORA, MIX), cmap),
                  pl.BlockSpec((16, MIX), cmap)],
        out_specs=[o_spec,
                   pl.BlockSpec((nb, 1, A_SHIFT_W), bmap3),
                   pl.BlockSpec((nb, A_H, A_HD, A_HD), bmap4),
                   pl.BlockSpec((nb, B_H * B_HD, B_HD), bmap3)],
        out_shape=[o_shape,
                   jax.ShapeDtypeStruct((bsz, 1, A_SHIFT_W), F32),
                   jax.ShapeDtypeStruct((bsz, A_H, A_HD, A_HD), F32),
                   jax.ShapeDtypeStruct((bsz, B_H * B_HD, B_HD), F32)],
        scratch_shapes=[pltpu.VMEM((nb, 8, A_SHIFT_W), F32), pltpu.VMEM((nb, A_HD, MIX), F32),
                        pltpu.VMEM((nb, B_HD, B_H * B_HD), F32)] + [blk() for _ in range(15)] + [
                            pltpu.VMEM((nch, 2 * hl, MIX), BF16), pltpu.VMEM((nch, hl, A_HD), F32),
                            pltpu.VMEM((nch, hl, A_HD), F32), pltpu.VMEM((nch, hl, hl), BF16),
                            pltpu.VMEM((nch, A_HD, MIX), F32), pltpu.VMEM((nch, hl, MIX), BF16)],
        compiler_params=pltpu.CompilerParams(dimension_semantics=("arbitrary", "arbitrary"),
                                             vmem_limit_bytes=VMEM_LIMIT),
        name="mixer_rwkv_hgrn")(p, shift0, srw0, shg0, mu, w2p, a2p, par)


_O_CQ = 0
_O_CK = _O_CQ + C_KW
_O_CV = _O_CK + C_KW
_O_GC = _O_CV + MIX
_O_DQK = _O_GC + MIX
_O_DV = _O_DQK + 2 * MIX
_O_GD = _O_DV + MIX
_O_SLAB = _O_GD + MIX
(_P_CNORM, _P_DNORM, _P_G2B) = range(3)


def _odd_kernel(oin_ref, xin_ref, woin_ref, gin_ref, win_ref,
                sgl0_ref, conv0_ref, mc0_ref, mn0_ref, mm0_ref, g2_ref, cw_ref, cb_ref, sb_ref, par_ref,
                wo_ref, gf_ref,
                y_ref, sgl_ref, conv_ref, mc_ref, mn_ref, mm_ref,
                prev_sc, sgl_sc, mc_sc, mn_sc, mm_sc, cq_sc, ck_sc, ce_sc, cl_sc, oc_sc,
                dq_sc, dk_sc, li_sc, fa_sc, fl_sc, od_sc, x_sc, p_sc, *, nb, tb, lc):
    j = pl.program_id(1)
    nj = pl.num_programs(1)
    hl = D_H * lc
    seqs = range(nb)
    head_rows = lambda x: jnp.concatenate(
        [jnp.broadcast_to(x[h:h + 1], (lc, x.shape[1])) for h in range(D_H)], axis=0)

    @pl.when(j == 0)
    def _():
        for b in seqs:
            prev_sc[b] = conv0_ref[b]
            sgl_sc[b] = sgl0_ref[b].T
            mc_sc[b] = mc0_ref[b].T
            mn_sc[b] = head_rows(mn0_ref[b])
            mm_sc[b] = head_rows(mm0_ref[b])[:, :1]

    par = par_ref[...]
    row = lambda i: par[i:i + 1]

    def rows_of(lo, hi, r0, r1):
        return p_sc[r0:r1, lo:hi]

    proj_slabs = [(c, min(c + 256, IN_ODD_PAD)) for c in range(0, IN_ODD_PAD, 256)]

    def projection_stages(r0, r1):
        x1 = xin_ref[r0:r1, :] + _dot(oin_ref[r0:r1, :], woin_ref[...])
        x_sc[r0:r1, :] = x1
        h = _rms(x1, gin_ref[...]).astype(BF16)
        for lo, hi in proj_slabs:
            yield
            p_sc[r0:r1, lo:hi] = _dot(h, win_ref[:, lo:hi])

    def token_stages(r0, r1):
        n = r1 - r0
        span = tb if n % tb == 0 else n
        cum_c, ones_c = _chunk_mats(n, lc)
        slab = rows_of(_O_SLAB, _O_SLAB + SLAB, r0, r1)
        pre = _dot(slab.astype(BF16), g2_ref[...]) + row(_P_G2B)[:, :C_KW]
        logg = _log_sigmoid(pre) * (1.0 / C_GATE_NORM)
        yield
        gb = _mm(cum_c, logg, "nn", M_EXACT_R)
        gl = _mm(ones_c, logg, "nn", M_EXACT_R)
        yield
        ck = rows_of(_O_CK, _O_CK + C_KW, r0, r1)
        cq_sc[r0:r1, :] = rows_of(_O_CQ, _O_CQ + C_KW, r0, r1) * (C_KD ** -0.5) * jnp.exp(gb)
        ck_sc[r0:r1, :] = ck * jnp.exp(-gb)
        ce_sc[r0:r1, :] = ck * jnp.exp(gl - gb)
        cl_sc[r0:r1, :] = gl
        yield
        x = rows_of(_O_DQK, _O_DQK + 2 * MIX, r0, r1)
        cw = cw_ref[...]
        conv = cb_ref[...] + x * cw[D_CONV - 1:D_CONV]
        head_row = _iota((8, 2 * MIX), 0)
        for s in range(1, D_CONV):
            xr = pltpu.roll(x, s, axis=0)
            parts = []
            for a in range(0, n, span):
                kept = pltpu.roll(prev_sc[(r0 + a) // tb], s, axis=0)
                parts.append(jnp.where(head_row < s, kept, xr[a:a + 8]))
                if span > 8:
                    parts.append(xr[a + 8:a + span])
            xs = parts[0] if len(parts) == 1 else jnp.concatenate(parts, axis=0)
            conv = conv + xs * cw[D_CONV - 1 - s:D_CONV - s]
        for a in range(0, n, span):
            prev_sc[(r0 + a) // tb] = x[a + span - 8:a + span]
        yield
        conv = _silu(conv)
        dq_sc[r0:r1, :] = conv[:, :MIX]
        dk_sc[r0:r1, :] = conv[:, MIX:] * (D_HD ** -0.5)
        yield
        gates = slab + sb_ref[...]
        lf = pltpu.roll(_log_sigmoid(gates), SLAB - (SLAB_F - SLAB_I), axis=1)
        li_sc[r0:r1, :] = gates
        fa_sc[r0:r1, :] = _mm(cum_c, lf, "nn", M_EXACT_R)
        fl_sc[r0:r1, :] = _mm(ones_c, lf, "nn", M_EXACT_R)

    wide_c, _, incl_c, _ = _head_masks(C_H, lc, C_KD)
    consts_c = (wide_c, incl_c)
    lane = _iota((hl, SLAB), 1)
    head = _iota((hl, SLAB), 0) // lc
    pick = lambda z: jnp.sum(jnp.where(lane == head + SLAB_I, _tile_rows(z, D_H), 0.0), -1, keepdims=True)

    wide_d, same_d, incl_d, _ = _head_masks(D_H, lc, D_HD)
    consts_d = (wide_d, same_d, incl_d, same_d.astype(BF16))

    per_seq = tb // lc
    if per_seq == 1:
        chains, run = math.gcd(nb, 4), 1
        trips = nb // chains
        first_chunk = lambda gi, q: gi * chains + q
    else:
        chains, run = nb, math.gcd(per_seq, 8)
        trips = per_seq // run
        first_chunk = lambda gi, q: q * per_seq + gi * run
    assert per_seq % run == 0

    def chunk_stages(gi):
        stages, finish = [], []
        for q in range(chains):
            first = first_chunk(gi, q)
            seq = first // per_seq
            out = {}
            rows = lambda u, first=first: _rows_at(((first + u) * lc, lc), lc)
            in_seq = lambda u, first=first, seq=seq: _rows_at(((first + u) * lc - seq * tb, lc), lc)

            def load_c(u, first=first, seq=seq, rows=rows, in_seq=in_seq):
                sl = rows(u)
                return (cq_sc[sl, :], ck_sc[sl, :], ce_sc[sl, :], cl_sc[pl.ds((first + u) * lc, 1), :],
                        p_sc[sl, _O_CV:_O_CV + MIX])

            def store_c(u, o, rows=rows):
                oc_sc[rows(u), :] = o

            def load_d(u, seq=seq, rows=rows, in_seq=in_seq):
                sl = rows(u)
                return (dq_sc[sl, :], dk_sc[sl, :], p_sc[sl, _O_DV:_O_DV + MIX],
                        pick(li_sc[sl, :]), pick(fa_sc[sl, :]), pick(fl_sc[sl, :]))

            def store_d(u, o, rows=rows):
                od_sc[rows(u), :] = o

            stages.append((_mlstm_stages(run, load_d, store_d, (mc_sc[seq], mn_sc[seq], mm_sc[seq]), consts_d, out),
                           3 * run + 1))
            stages.append((_gla_stages(run, load_c, store_c, sgl_sc[seq], consts_c, C_H, C_VD, out, "gla"),
                           2 * run + 1))
            finish.append((seq, out))

        def write_back():
            for seq, out in finish:
                sgl_sc[seq] = out["gla"]
                mc_sc[seq], mn_sc[seq], mm_sc[seq] = out["mlstm"]

        return stages, write_back

    def output_stages(r0, r1):
        oc = _head_rms(oc_sc[r0:r1, :], row(_P_CNORM), C_VD) * _silu(rows_of(_O_GC, _O_GC + MIX, r0, r1))
        yield
        od = _head_rms(od_sc[r0:r1, :], row(_P_DNORM), D_HD) * _silu(rows_of(_O_GD, _O_GD + MIX, r0, r1))
        o = jnp.concatenate([oc, od], axis=1).astype(BF16)
        yield
        x2 = x_sc[r0:r1, :] + _dot(o, wo_ref[...])
        y_ref[r0:r1, :] = _rms(x2, gf_ref[...])

    if per_seq == 1 or nb > 1:
        _interleave((projection_stages(0, nb * tb), len(proj_slabs) + 1))
        _interleave((token_stages(0, nb * tb), 6))

        def trip(gi, carry):
            stages, write_back = chunk_stages(gi)
            _interleave(*stages)
            write_back()
            return carry

        lax.fori_loop(0, trips, trip, 0)
        _interleave((output_stages(0, nb * tb), 3))
    else:
        grows = run * lc
        for phase in range(trips + 3):
            stages, after = [], []
            if 0 <= phase - 2 < trips:
                chunk, write_back = chunk_stages(phase - 2)
                stages += chunk
                after.append(write_back)
            if phase < trips:
                stages.append((projection_stages(phase * grows, (phase + 1) * grows), len(proj_slabs) + 1))
            if 0 <= phase - 1 < trips:
                stages.append((token_stages((phase - 1) * grows, phase * grows), 6))
            if 0 <= phase - 3 < trips:
                stages.append((output_stages((phase - 3) * grows, (phase - 2) * grows), 3))
            _interleave(*stages)
            for fn in after:
                fn()

    @pl.when(j == nj - 1)
    def _():
        for b in seqs:
            sgl_ref[b] = sgl_sc[b].T
            conv_ref[b] = prev_sc[b]
            mc_ref[b] = mc_sc[b].T
            mn_ref[b] = jnp.concatenate([mn_sc[b, h * lc:h * lc + 1] for h in range(D_H)], axis=0)
            mm_ref[b] = jnp.concatenate(
                [jnp.broadcast_to(mm_sc[b, h * lc:h * lc + 1], (1, LANE)) for h in range(D_H)], axis=0)


def _odd_layer(o_in, x, wo_in, g_in, w_in, sgl0, conv0, mc0, mn0, mm0, g2p, cw, cb, sb, par, wo, gf, bsz, t, nb, tb,
               lc):
    hl = D_H * lc
    once = lambda shape: pl.BlockSpec(shape, lambda b, j: (0, 0), pipeline_mode=pl.Buffered(1))
    kern = functools.partial(_odd_kernel, nb=nb, tb=tb, lc=lc)
    bmap3 = lambda b, j: (b, 0, 0)
    cmap = lambda b, j: (0, 0)
    blk = lambda w: pltpu.VMEM((nb * tb, w), F32)
    assert nb == 1 or tb == t
    rows = pl.BlockSpec((nb * tb, D_MODEL), lambda b, j: (b * (t // tb) + j, 0))
    return pl.pallas_call(
        kern, grid=(bsz // nb, t // tb),
        in_specs=[rows, rows, once((D_MODEL, D_MODEL)), pl.BlockSpec((1, D_MODEL), cmap),
                  once((D_MODEL, IN_ODD_PAD)),
                  pl.BlockSpec((nb, C_KW, C_VD), bmap3),
                  pl.BlockSpec((nb, 8, 2 * MIX), bmap3),
                  pl.BlockSpec((nb, MIX, D_HD), bmap3),
                  pl.BlockSpec((nb, D_H, D_HD), bmap3),
                  pl.BlockSpec((nb, D_H, LANE), bmap3),
                  pl.BlockSpec((SLAB, C_KW), cmap),
                  pl.BlockSpec((D_CONV, 2 * MIX), cmap),
                  pl.BlockSpec((1, 2 * MIX), cmap),
                  pl.BlockSpec((1, SLAB), cmap),
                  pl.BlockSpec((8, MIX), cmap),
                  once((D_MODEL, D_MODEL)),
                  pl.BlockSpec((1, D_MODEL), cmap)],
        out_specs=[rows,
                   pl.BlockSpec((nb, C_KW, C_VD), bmap3),
                   pl.BlockSpec((nb, 8, 2 * MIX), bmap3),
                   pl.BlockSpec((nb, MIX, D_HD), bmap3),
                   pl.BlockSpec((nb, D_H, D_HD), bmap3),
                   pl.BlockSpec((nb, D_H, LANE), bmap3)],
        out_shape=[jax.ShapeDtypeStruct((bsz * t, D_MODEL), F32),
                   jax.ShapeDtypeStruct((bsz, C_KW, C_VD), F32),
                   jax.ShapeDtypeStruct((bsz, 8, 2 * MIX), F32),
                   jax.ShapeDtypeStruct((bsz, MIX, D_HD), F32),
                   jax.ShapeDtypeStruct((bsz, D_H, D_HD), F32),
                   jax.ShapeDtypeStruct((bsz, D_H, LANE), F32)],
        scratch_shapes=[pltpu.VMEM((nb, 8, 2 * MIX), F32), pltpu.VMEM((nb, C_VD, C_KW), F32),
                        pltpu.VMEM((nb, D_HD, MIX), F32), pltpu.VMEM((nb, hl, D_HD), F32),
                        pltpu.VMEM((nb, hl, 1), F32),
                        blk(C_KW), blk(C_KW), blk(C_KW), blk(C_KW), blk(MIX),
                        blk(MIX), blk(MIX), blk(SLAB), blk(SLAB), blk(SLAB), blk(MIX),
                        blk(D_MODEL), blk(IN_ODD_PAD)],
        compiler_params=pltpu.CompilerParams(dimension_semantics=("arbitrary", "arbitrary"),
                                             vmem_limit_bytes=VMEM_LIMIT),
        name="layer_gla_mlstm")(o_in, x, wo_in, g_in, w_in, sgl0, conv0, mc0, mn0, mm0, g2p, cw, cb, sb, par, wo, gf)


MIXER_ROWS = 256
SHORT_SEQS = 8


def _mixer_blocking(bsz, t):
    tb = math.gcd(t, MIXER_ROWS)
    nb = math.gcd(bsz, SHORT_SEQS) if tb == t and t * SHORT_SEQS <= MIXER_ROWS else 1
    return nb, tb


def _paired_blocking(bsz, nb, tb):
    if nb == 1 and bsz % 2 == 0 and (tb // 2) % 32 == 0:
        return 2, tb // 2
    return nb, tb


def _run(x, shift, s_rwkv, s_hgrn, s_gla, conv, mc, mn, mm, wts):
    bsz, t, d = x.shape
    nb, tb = _mixer_blocking(bsz, t)
    la = math.gcd(t, 16)
    lbc = math.gcd(t, 32)
    x2 = x.reshape(bsz * t, d)
    p0 = _inproj(x2, wts["g0"], wts["w_in0"])
    o0, shift_n, srw_n, shg_n = _even_mixer(
        p0.reshape(bsz, t, IN_EVEN), shift[0][:, None], s_rwkv[0], s_hgrn[0].reshape(bsz, B_H * B_HD, B_HD),
        wts["mu"], wts["w2p"], wts["a2p"], wts["par_e"], *_paired_blocking(bsz, nb, tb), la, lbc)
    conv8 = jnp.pad(conv[0], ((0, 0), (8 - (D_CONV - 1), 0), (0, 0)))
    mm_l = jnp.broadcast_to(mm[0][:, :, None], (bsz, D_H, LANE))
    y, sgl_n, conv_n, mc_n, mn_n, mm_n = _odd_layer(
        o0.reshape(bsz * t, d), x2, wts["w_out0"], wts["g1"], wts["w_in1"],
        s_gla[0].reshape(bsz, C_KW, C_VD), conv8, mc[0].reshape(bsz, MIX, D_HD), mn[0], mm_l,
        wts["g2p"], wts["cw"], wts["cb"], wts["sb"], wts["par_o"], wts["w_out1"], wts["gf"], bsz, t, nb,
        math.gcd(t, 2 * tb) if nb == 1 else tb, lbc)
    return (y.reshape(bsz, t, d), shift_n.reshape(1, bsz, A_SHIFT_W), srw_n[None],
            shg_n.reshape(1, bsz, B_H, B_HD, B_HD), sgl_n.reshape(1, bsz, C_H, C_KD, C_VD),
            conv_n[None, :, 8 - (D_CONV - 1):], mc_n.reshape(1, bsz, D_H, D_HD, D_HD), mn_n[None],
            mm_n[None, :, :, 0])


def _odd_in_projection(w):
    sizes = (C_KW, C_KW, MIX, C_LORA, MIX, 2 * MIX, MIX, D_H, D_H, MIX)
    off = np.cumsum((0,) + sizes)
    cq, ck, cv, cg, gc, dqk, dv, di, df, gd = (w[:, off[i]:off[i + 1]] for i in range(len(sizes)))
    pad = jnp.zeros((w.shape[0], SLAB - C_LORA - 2 * D_H), w.dtype)
    return jnp.concatenate([cq, ck, cv, gc, dqk, dv, gd, cg, di, df, pad], axis=1)


def kernel(x_prompt, x_sample, state_shift_a, state_rwkv, state_hgrn, state_gla, state_conv_d, state_mlstm_c,
           state_mlstm_n, state_mlstm_m, norm_g, w_in_even, w_out_even, a_mu, a_w0, a_w2, a_a0, a_a2, a_kk,
           a_ka, a_rk, a_ln_w, a_ln_b, b_lb, b_norm, w_in_odd, w_out_odd, c_g2, c_g2b, c_norm, d_conv_w,
           d_conv_b, d_ib, d_fb, d_norm, final_norm):
    assert w_in_even.shape[0] == 1 and w_in_odd.shape[0] == 1 and b_lb.shape[0] == 2
    zpad = lambda a, rows_before, rows_total: jnp.pad(a, ((rows_before, rows_total - rows_before - a.shape[0]), (0, 0)))
    par_e = jnp.concatenate([a_w0, a_a0, a_kk, a_ka, a_rk, a_ln_w, a_ln_b, b_norm, b_lb[0:1], b_lb[1:2]], axis=0)
    g2b = jnp.pad(c_g2b, ((0, 0), (0, MIX - C_KW)))
    par_o = jnp.concatenate([c_norm, d_norm, g2b], axis=0)
    w_in1 = _odd_in_projection(w_in_odd[0].astype(BF16))
    sb = jnp.pad(jnp.concatenate([d_ib, d_fb], axis=1), ((0, 0), (SLAB_I, SLAB - SLAB_I - 2 * D_H)))
    wts = {
        "g0": norm_g[0:1], "g1": norm_g[1:2], "gf": final_norm[None],
        "w_in0": w_in_even[0].astype(BF16), "w_out0": w_out_even[0].astype(BF16),
        "w_in1": w_in1, "w_out1": w_out_odd[0].astype(BF16),
        "mu": a_mu, "w2p": zpad(a_w2[0], 0, 2 * A_LORA).astype(BF16),
        "a2p": zpad(a_a2[0], A_LORA, 2 * A_LORA).astype(BF16),
        "par_e": zpad(par_e, 0, 16), "par_o": zpad(par_o, 0, 8),
        "g2p": zpad(c_g2[0], 0, SLAB).astype(BF16), "cw": d_conv_w[0], "cb": d_conv_b, "sb": sb,
    }
    bp = x_prompt.shape[0]
    z = lambda *s: jnp.zeros(s, x_prompt.dtype)
    prompt = _run(x_prompt, z(1, bp, A_SHIFT_W), z(1, bp, A_H, A_HD, A_HD), z(1, bp, B_H, B_HD, B_HD),
                  z(1, bp, C_H, C_KD, C_VD), z(1, bp, D_CONV - 1, 2 * MIX), z(1, bp, D_H, D_HD, D_HD),
                  z(1, bp, D_H, D_HD), z(1, bp, D_H), wts)
    sample = _run(x_sample, state_shift_a, state_rwkv, state_hgrn, state_gla, state_conv_d, state_mlstm_c,
                  state_mlstm_n, state_mlstm_m, wts)
    return (prompt[0], sample[0]) + prompt[1:] + sample[1:]
```

```python
import functools
import math

import jax
import jax.numpy as jnp
import numpy as np
from jax import lax
from jax.experimental import pallas as pl
from jax.experimental.pallas import tpu as pltpu

F32 = jnp.float32
BF16 = jnp.bfloat16

D_MODEL = 1024
MIX = D_MODEL // 2
NORM_EPS = 1e-5
A_HD = 64
A_H = MIX // A_HD
A_LORA = 64
A_GN_EPS = 64e-5
A_SHIFT_W = 3 * MIX + 2 * A_LORA
B_HD = 128
B_H = MIX // B_HD
IN_EVEN = A_SHIFT_W + 5 * MIX
C_H = 4
C_VD = MIX // C_H
C_KD = C_VD // 2
C_KW = C_H * C_KD
C_LORA = 16
C_GATE_NORM = 16.0
D_H = 4
D_HD = MIX // D_H
D_CONV = 4
LANE = 128
SLAB = LANE
SLAB_I = C_LORA
SLAB_F = C_LORA + D_H
IN_ODD_PAD = 2 * C_KW + 2 * MIX + 2 * MIX + 2 * MIX + SLAB
VMEM_LIMIT = 48 * 1024 * 1024


_DIMS = {"nn": ((1,), (0,)), "nt": ((1,), (1,)), "tn": ((0,), (0,))}
M_EXACT_R = "rx"
M_EXACT_L = "lx"
M_G = "bf"
M_TINV = "bf"
M_ST = "bf"
M_ATT = "bf"


def _split2(x):
    hi = x.astype(BF16)
    return hi, (x - hi.astype(F32)).astype(BF16)


def _mm(a, b, form, mode):
    dn = (_DIMS[form], ((), ()))
    d = lambda x, y: lax.dot_general(x, y, dn, preferred_element_type=F32)
    if mode == "bf":
        return d(a.astype(BF16), b.astype(BF16))
    if mode == "lx":
        bb = b.astype(BF16)
        h, l = _split2(a)
        return d(h, bb) + d(l, bb)
    assert mode == "rx"
    ab = a.astype(BF16)
    h, l = _split2(b)
    return d(ab, h) + d(ab, l)


def _dot(a, b):
    return jnp.dot(a, b, preferred_element_type=F32)


def _iota(shape, dim):
    return lax.broadcasted_iota(jnp.int32, shape, dim)


def _sigmoid(x):
    return 0.5 + 0.5 * jnp.tanh(0.5 * x)


def _log_sigmoid(x):
    return jnp.minimum(x, 0.0) - jnp.log(1.0 + jnp.exp(-jnp.abs(x)))


def _silu(x):
    return x * _sigmoid(x)


def _rms(x, g):
    return x * lax.rsqrt(jnp.mean(x * x, -1, keepdims=True) + NORM_EPS) * g


def _rows_at(start_and_alignment, size):
    start, alignment = start_and_alignment
    if not isinstance(start, int):
        start = pl.multiple_of(start, alignment)
    return pl.ds(start, size)


def _tile_rows(x, n):
    return jnp.concatenate([x] * n, axis=0)


def _stack_heads(x, n_heads, width):
    return jnp.concatenate([x[:, h * width:(h + 1) * width] for h in range(n_heads)], axis=0)


def _unstack_heads(x, n_heads, rows):
    return jnp.concatenate([x[h * rows:(h + 1) * rows] for h in range(n_heads)], axis=1)


def _head_masks(n_heads, rows, width):
    hl = n_heads * rows
    rh = _iota((hl, n_heads * width), 0) // rows
    lh = _iota((hl, n_heads * width), 1) // width
    wide = (rh == lh).astype(F32)
    ri, ci = _iota((hl, hl), 0), _iota((hl, hl), 1)
    same = (ri // rows) == (ci // rows)
    incl = same & (ci <= ri)
    strict = same & (ci < ri)
    return wide, same, incl, strict


def _chunk_mats(n, chunk):
    ri, ci = _iota((n, n), 0), _iota((n, n), 1)
    same = (ri // chunk) == (ci // chunk)
    return (same & (ci <= ri)).astype(BF16), same.astype(BF16)


def _seg_sum(x, seg):
    bd = ((_iota((LANE, LANE), 0) // seg) == (_iota((LANE, LANE), 1) // seg)).astype(BF16)
    parts = [_mm(x[:, j:j + LANE], bd, "nn", M_EXACT_L) for j in range(0, x.shape[1], LANE)]
    return jnp.concatenate(parts, axis=1)


def _head_rms(x, g, width):
    parts = []
    for j in range(0, x.shape[1], width):
        xs = x[:, j:j + width]
        parts.append(xs * lax.rsqrt(jnp.mean(xs * xs, -1, keepdims=True) + NORM_EPS))
    return jnp.concatenate(parts, axis=1) * g


def _inproj_kernel(x_ref, g_ref, w_ref, p_ref):
    h = _rms(x_ref[...], g_ref[...])
    p_ref[...] = _dot(h.astype(BF16), w_ref[...])


def _row_tile(m):
    return math.gcd(m, 512)


def _full(shape):
    return pl.BlockSpec(shape, lambda i: (0,) * len(shape))


def _dense_params():
    return pltpu.CompilerParams(dimension_semantics=("arbitrary",), vmem_limit_bytes=VMEM_LIMIT)


def _inproj(x, g, w):
    m, d = x.shape
    n = w.shape[1]
    tm = _row_tile(m)
    return pl.pallas_call(
        _inproj_kernel, grid=(m // tm,),
        in_specs=[pl.BlockSpec((tm, d), lambda i: (i, 0)), _full((1, d)), _full((d, n))],
        out_specs=pl.BlockSpec((tm, n), lambda i: (i, 0)),
        out_shape=jax.ShapeDtypeStruct((m, n), F32),
        compiler_params=_dense_params(), name="inproj")(x, g, w)


def _interleave(*staged):
    live = [[gen, 0, max(n, 1)] for gen, n in staged]
    while live:
        item = min(live, key=lambda it: it[1] / it[2])
        try:
            next(item[0])
            item[1] += 1
        except StopIteration:
            live.remove(item)


def _rwkv_factor_stages(chunks, consts, store):
    wide, incl, strict, eye, halves = consts
    L = chunks[0][0].shape[0]
    hl = A_H * L
    n = range(len(chunks))
    tile = functools.partial(_tile_rows, n=A_H)
    for i in n:
        store(i, "r_w", (tile(chunks[i][1]) * wide).astype(BF16))
        store(i, "be_w", (tile(chunks[i][5]) * wide).astype(BF16))
    stack = lambda x: _stack_heads(x, A_H, A_HD)
    g = [_mm(jnp.concatenate([stack(c[0]), stack(c[1])], axis=0),
             jnp.concatenate([stack(c[2]), stack(c[3])], axis=0), "nt", M_G) for c in chunks]
    yield
    a_kk = [jnp.where(strict, x[:hl, :hl], 0.0) for x in g]
    a_kb = [jnp.where(strict, x[:hl, hl:], 0.0) for x in g]
    a_rk = [jnp.where(incl, x[hl:, :hl], 0.0) for x in g]
    for i in n:
        store(i, "a_rb", jnp.where(incl, g[i][hl:, hl:], 0.0).astype(BF16))
    t = [eye - jnp.where(halves[0], x, 0.0) for x in a_kb]
    v_st = [_stack_heads(c[6], A_H, A_HD) for c in chunks]
    av = [_mm(a_kk[i], v_st[i], "nn", M_ST) for i in n]
    for i in n:
        store(i, "o2", _mm(a_rk[i], v_st[i], "nn", M_ST))
        store(i, "kv", _mm(v_st[i], tile(chunks[i][4]) * wide, "tn", M_ST))
    for half in halves[1:]:
        ta = [_mm(t[i], jnp.where(half, a_kb[i], 0.0), "nn", M_TINV) for i in n]
        yield
        t = [t[i] - _mm(ta[i], t[i], "nn", M_TINV) for i in n]
        yield
    for i in n:
        store(i, "m1n", (-_mm(t[i], tile(chunks[i][0]) * wide, "nn", M_ST)).astype(BF16))
        store(i, "m2n", -_mm(t[i], av[i], "nn", M_ST))
    yield


def _rwkv_state_stages(n_chunks, load_factors, store_o, s, out):
    for c in range(n_chunks):
        xm, m2n, o2, a_rb, kv, be_w, ecl = load_factors(c)
        hl = xm.shape[0] // 2
        xs = _mm(xm, s, "nt", M_ST)
        yield
        u = xs[:hl] + m2n
        s = s * ecl + kv + _mm(u, be_w, "tn", M_ST)
        o_st = xs[hl:] + o2 + _mm(a_rb, u, "nn", M_ST)
        yield
        store_o(c, _unstack_heads(o_st, A_H, hl // A_H))
    out["rwkv"] = s


def _gla_stages(n_chunks, load_chunk, store_o, s, consts, n_heads, vd, out, key):
    wide, incl = consts
    tile = functools.partial(_tile_rows, n=n_heads)
    for c in range(n_chunks):
        qd, ki, kend, bl, v = load_chunk(c)
        q_w = tile(qd) * wide
        v_st = _stack_heads(v, n_heads, vd)
        att = jnp.where(incl, _mm(q_w, tile(ki), "nt", M_ATT), 0.0)
        kv = _mm(v_st, tile(kend) * wide, "tn", M_ATT)
        yield
        o_st = _mm(att, v_st, "nn", M_ATT) + _mm(q_w, s, "nt", M_ATT)
        s = s * jnp.exp(bl) + kv
        yield
        store_o(c, _unstack_heads(o_st, n_heads, qd.shape[0]))
    out[key] = s


def _mlstm_stages(n_chunks, load_chunk, store_o, state, consts, out):
    wide, same, incl, ones_bd = consts
    c, nrow, m = state
    tile = functools.partial(_tile_rows, n=D_H)
    for ch in range(n_chunks):
        q, k, v, li, a, a_last = load_chunk(ch)
        L = q.shape[0]
        hl = D_H * L
        to_row = lambda col: jnp.broadcast_to(col, (hl, LANE)).T[:hl]
        q_st = _stack_heads(q, D_H, D_HD)
        k_st = _stack_heads(k, D_H, D_HD)
        v_st = _stack_heads(v, D_H, D_HD)
        qk = _mm(q_st, k_st, "nt", M_ATT)
        dlog = jnp.where(incl, a - to_row(a) + to_row(li), -jnp.inf)
        m_intra = jnp.max(dlog, -1, keepdims=True)
        e_end = a_last - a + li
        m_loc = jnp.max(jnp.where(same, to_row(e_end), -jnp.inf), -1, keepdims=True)
        yield
        s_intra = jnp.exp(dlog - m_intra) * qk
        num_intra = _mm(s_intra, v_st, "nn", M_ATT)
        den_intra = jnp.sum(s_intra, -1, keepdims=True)
        w_end = jnp.exp(e_end - m_loc)
        c_loc = _mm(k_st, tile(v) * wide * w_end, "tn", M_ATT)
        n_loc = _mm(ones_bd, w_end * k_st, "nn", M_ATT)
        yield
        g = a + m
        m_t = jnp.maximum(g, m_intra)
        f_inter = jnp.exp(g - m_t)
        f_intra = jnp.exp(m_intra - m_t)
        qc = _mm(q_st, c, "nn", M_ATT)
        num_inter = jnp.concatenate(
            [qc[h * L:(h + 1) * L, h * D_HD:(h + 1) * D_HD] for h in range(D_H)], axis=0)
        den_inter = jnp.sum(q_st * nrow, -1, keepdims=True)
        num = num_intra * f_intra + num_inter * f_inter
        den = f_intra * den_intra + f_inter * den_inter
        h_st = num / jnp.maximum(jnp.abs(den), jnp.exp(-m_t))
        m_new = jnp.maximum(a_last + m, m_loc)
        fo = jnp.exp(a_last + m - m_new)
        fl = jnp.exp(m_loc - m_new)
        per_value = lambda col: jnp.concatenate(
            [jnp.broadcast_to(col[h * L:h * L + 1], (1, D_HD)) for h in range(D_H)], axis=1)
        c = per_value(fo) * c + per_value(fl) * c_loc
        nrow = fo * nrow + fl * n_loc
        m = m_new
        yield
        store_o(ch, _unstack_heads(h_st, D_H, L))
    out["mlstm"] = (c, nrow, m)


(_E_W0, _E_A0, _E_KK, _E_KA, _E_RK, _E_LNW, _E_LNB, _E_BNORM, _E_LB0, _E_LB1) = range(10)


def _even_kernel(p_ref, shift0_ref, srw0_ref, shg0_ref, mu_ref, w2_ref, a2_ref, par_ref,
                 o_ref, shift_ref, srw_ref, shg_ref,
                 prev_sc, srw_sc, shg_sc, kkt_sc, rt_sc, kh_sc, bh_sc, ke_sc, be_sc, v_sc, epos_sc, bonus_sc, oa_sc,
                 qd_sc, ki_sc, kend_sc, bl_sc, ob_sc, xm_sc, m2_sc, o2_sc, arb_sc, kv_sc, bew_sc, *, nb, tb, la, lb):
    j = pl.program_id(1)
    nj = pl.num_programs(1)
    seqs = range(nb)
    cols = lambda lo, hi: jnp.concatenate([p_ref[b, :, lo:hi] for b in seqs], axis=0)

    @pl.when(j == 0)
    def _():
        for b in seqs:
            prev_sc[b] = jnp.broadcast_to(shift0_ref[b], prev_sc.shape[1:])
            srw_sc[b] = jnp.concatenate([srw0_ref[b, h] for h in range(A_H)], axis=1)
            shg_sc[b] = shg0_ref[b].T

    par = par_ref[...]
    row = lambda i: par[i:i + 1]

    per_seq = tb // la
    group = math.gcd(nb * per_seq, 8)
    chains = nb
    run = group // chains
    ngroups = per_seq // run
    prows = run * la
    assert group % chains == 0 and per_seq % run == 0 and prows % lb == 0
    hper = prows // lb
    late_hgrn = ngroups > 1

    def pieces(g):
        if ngroups == 1:
            return [(0, nb * tb)]
        return [(q * tb + g * prows, q * tb + (g + 1) * prows) for q in seqs]

    def rows_of(lo, hi, r0, r1):
        parts = [p_ref[b, max(r0, b * tb) - b * tb:min(r1, (b + 1) * tb) - b * tb, lo:hi]
                 for b in seqs if max(r0, b * tb) < min(r1, (b + 1) * tb)]
        return parts[0] if len(parts) == 1 else jnp.concatenate(parts, axis=0)

    def rwkv_token_stages(r0, r1):
        grows = r1 - r0
        pa = rows_of(0, A_SHIFT_W, r0, r1)
        rolled = pltpu.roll(pa, 1, axis=0)
        starts = range(r0, r1, tb) if grows % tb == 0 else (r0,)
        span = tb if grows % tb == 0 else grows
        first = _iota(pa.shape, 0) % span == 0
        before = [jnp.broadcast_to(prev_sc[s // tb, 0:1], (span, A_SHIFT_W)) for s in starts]
        shifted = jnp.where(first, before[0] if len(before) == 1 else jnp.concatenate(before, axis=0), rolled)
        for s in starts:
            prev_sc[s // tb] = jnp.broadcast_to(pa[s - r0 + span - 1:s - r0 + span], prev_sc.shape[1:])
        xm = pa + (shifted - pa) * mu_ref[...]
        r = xm[:, :MIX]
        k = xm[:, MIX:2 * MIX]
        v = xm[:, 2 * MIX:3 * MIX]
        lora_in = xm[:, 3 * MIX:]
        yield
        lw = _sigmoid(row(_E_W0) + _dot(jnp.tanh(lora_in).astype(BF16), w2_ref[...])) * (-math.exp(-0.5))
        a = _sigmoid(row(_E_A0) + _dot(lora_in.astype(BF16), a2_ref[...]))
        yield
        kk = k * row(_E_KK)
        kk = kk * lax.rsqrt(jnp.maximum(_seg_sum(kk * kk, A_HD), 1e-12))
        yield
        k = k * (1.0 + (a - 1.0) * row(_E_KA))
        bv = kk * a
        cum_a, ones_a = _chunk_mats(grows, la)
        c = _mm(cum_a, lw, "nn", M_EXACT_R)
        yield
        cl = _mm(ones_a, lw, "nn", M_EXACT_R)
        yield
        e_pos = jnp.exp(c)
        e_neg = jnp.exp(-c)
        e_end = jnp.exp(cl - c)
        kkt_sc[r0:r1, :] = kk * jnp.exp(c - lw)
        rt_sc[r0:r1, :] = r * e_pos
        yield
        kh_sc[r0:r1, :] = k * e_neg
        bh_sc[r0:r1, :] = bv * e_neg
        ke_sc[r0:r1, :] = k * e_end
        be_sc[r0:r1, :] = bv * e_end
        yield
        v_sc[r0:r1, :] = v
        epos_sc[r0:r1, :] = e_pos
        bonus_sc[r0:r1, :] = _seg_sum(r * k * row(_E_RK), A_HD) * v

    wide, _, incl, strict = _head_masks(A_H, la, A_HD)
    hl = A_H * la
    ri, ci_ = _iota((hl, hl), 0), _iota((hl, hl), 1)
    eye = (ri == ci_).astype(F32)
    halves = []
    m = 1
    while m < la:
        halves.append(((ri // (2 * m)) == (ci_ // (2 * m))) & ((ri // m) != (ci_ // m)))
        m *= 2
    consts_a = (wide, incl, strict, eye, halves)

    base = A_SHIFT_W + MIX

    def hgrn_token_stages():
        e0 = row(_E_LB0)
        e1 = row(_E_LB1)
        emax = jnp.maximum(e0, e1)
        e0 = jnp.exp(e0 - emax)
        lower = e0 / (e0 + jnp.exp(e1 - emax))
        g = lower + (1.0 - lower) * jax.nn.sigmoid(cols(base + MIX, base + 2 * MIX))
        yield
        logg = jnp.log(g)
        cum_b, ones_b = _chunk_mats(nb * tb, lb)
        yield
        gb = _mm(cum_b, logg, "nn", M_EXACT_R)
        yield
        gl = _mm(ones_b, logg, "nn", M_EXACT_R)
        yield
        qd_sc[...] = _silu(cols(base, base + MIX)) * (B_HD ** -0.5) * jnp.exp(gb)
        yield
        ki_sc[...] = (1.0 - g) * jnp.exp(-gb)
        yield
        kend_sc[...] = (1.0 - g) * jnp.exp(gl - gb)
        bl_sc[...] = gl

    wide_b, _, incl_b, _ = _head_masks(B_H, lb, B_HD)
    consts_b = (wide_b, incl_b)

    def factor_stages(gi, slot):
        sls = [_rows_at(((q * per_seq + gi * run + u) * la, la), la) for q in seqs for u in range(run)]
        hl_a = A_H * la

        def store(u, name, value):
            if name == "m1n":
                xm_sc[slot + u, :hl_a, :] = value
            elif name == "r_w":
                xm_sc[slot + u, hl_a:, :] = value
            else:
                {"m2n": m2_sc, "o2": o2_sc, "a_rb": arb_sc, "kv": kv_sc, "be_w": bew_sc}[name][slot + u] = value

        yield from _rwkv_factor_stages(
            [(kkt_sc[sl, :], rt_sc[sl, :], kh_sc[sl, :], bh_sc[sl, :], ke_sc[sl, :], be_sc[sl, :], v_sc[sl, :])
             for sl in sls], consts_a, store)

    def state_stages(gi):
        slot = (gi % 2) * group
        stages, finish = [], []
        for q in range(chains):
            seq = q
            first = seq * per_seq + gi * run
            row0 = first * la
            out = {}

            def load_factors(u, q=q, first=first):
                i = slot + q * run + u
                return (xm_sc[i], m2_sc[i], o2_sc[i], arb_sc[i], kv_sc[i], bew_sc[i],
                        epos_sc[pl.ds((first + u) * la + la - 1, 1), :])

            def store_oa(u, o, row0=row0):
                oa_sc[_rows_at((row0 + u * la, la), la), :] = o

            def load_b(u, row0=row0, seq=seq):
                r = row0 + u * lb
                sl = _rows_at((r, lb), lb)
                return (qd_sc[sl, :], ki_sc[sl, :], kend_sc[sl, :], bl_sc[pl.ds(r, 1), :],
                        p_ref[seq, _rows_at((r - seq * tb, lb), lb), base + 2 * MIX:base + 3 * MIX])

            def store_ob(u, o, row0=row0):
                ob_sc[_rows_at((row0 + u * lb, lb), lb), :] = o

            stages.append((_rwkv_state_stages(run, load_factors, store_oa, srw_sc[seq], out), 2 * run + 1))
            if late_hgrn and gi == ngroups - 1:
                load_all = functools.partial(load_b, row0=seq * tb)
                store_all = functools.partial(store_ob, row0=seq * tb)
                stages.append((_gla_stages(hper * ngroups, load_all, store_all, shg_sc[seq], consts_b, B_H, B_HD,
                                           out, "hgrn"), 2 * hper * ngroups + 1))
            elif not late_hgrn:
                stages.append((_gla_stages(hper, load_b, store_ob, shg_sc[seq], consts_b, B_H, B_HD, out, "hgrn"),
                               2 * hper + 1))
            finish.append((seq, out))

        def write_back():
            for seq, out in finish:
                srw_sc[seq] = out["rwkv"]
                if "hgrn" in out:
                    shg_sc[seq] = out["hgrn"]

        return stages, write_back

    def rwkv_output_stages(r0, r1):
        o = oa_sc[r0:r1, :]
        mean = _seg_sum(o, A_HD) * (1.0 / A_HD)
        cen = o - mean
        yield
        var = _seg_sum(cen * cen, A_HD) * (1.0 / A_HD)
        yield
        oa = cen * lax.rsqrt(var + A_GN_EPS) * row(_E_LNW) + row(_E_LNB)
        oa = oa + bonus_sc[r0:r1, :]
        put_o(r0, r1, 0, oa * _silu(rows_of(A_SHIFT_W, A_SHIFT_W + MIX, r0, r1)))

    def put_o(r0, r1, lo, value):
        value = value.astype(o_ref.dtype)
        if len(o_ref.shape) == 2:
            o_ref[r0:r1, lo:lo + MIX] = value
        else:
            for b in seqs:
                a, z = max(r0, b * tb), min(r1, (b + 1) * tb)
                if a < z:
                    o_ref[b, a - b * tb:z - b * tb, lo:lo + MIX] = value[a - r0:z - r0]

    def hgrn_output(r0, r1):
        ob = _head_rms(ob_sc[r0:r1, :], row(_E_BNORM), B_HD) * _silu(rows_of(base + 3 * MIX, base + 4 * MIX, r0, r1))
        put_o(r0, r1, MIX, ob)

    for phase in range(ngroups + 3):
        stages, after = [], []
        if 0 <= phase - 2 < ngroups:
            chain_stages, write_back = state_stages(phase - 2)
            stages += chain_stages
            after.append(write_back)
        if 0 <= phase - 1 < ngroups:
            stages.append((factor_stages(phase - 1, ((phase - 1) % 2) * group), 2 * len(halves) + 1))
        if phase < ngroups:
            stages += [(rwkv_token_stages(r0, r1), 8) for r0, r1 in pieces(phase)]
        if phase == 0:
            stages.append((hgrn_token_stages(), 7))
        if 0 <= phase - 3 < ngroups:
            stages += [(rwkv_output_stages(r0, r1), 3) for r0, r1 in pieces(phase - 3)]
        _interleave(*stages)
        for fn in after:
            fn()
    hgrn_output(0, nb * tb)

    @pl.when(j == nj - 1)
    def _():
        for b in seqs:
            shift_ref[b] = prev_sc[b, 0:1]
            s = srw_sc[b]
            for h in range(A_H):
                srw_ref[b, h] = s[:, h * A_HD:(h + 1) * A_HD]
            shg_ref[b] = shg_sc[b].T


def _mixer_out(bsz, t, nb, tb):
    if nb == 1 or tb == t:
        return (pl.BlockSpec((nb * tb, D_MODEL), lambda b, j: (b * (t // tb) + j, 0)),
                jax.ShapeDtypeStruct((bsz * t, D_MODEL), BF16))
    return (pl.BlockSpec((nb, tb, D_MODEL), lambda b, j: (b, j, 0)),
            jax.ShapeDtypeStruct((bsz, t, D_MODEL), BF16))


def _even_mixer(p, shift0, srw0, shg0, mu, w2p, a2p, par, nb, tb, la, lb):
    bsz, t, _ = p.shape
    kern = functools.partial(_even_kernel, nb=nb, tb=tb, la=la, lb=lb)
    nch, hl = 2 * math.gcd(nb * tb // la, 8), A_H * la
    bmap3 = lambda b, j: (b, 0, 0)
    bmap4 = lambda b, j: (b, 0, 0, 0)
    cmap = lambda b, j: (0, 0)
    blk = lambda: pltpu.VMEM((nb * tb, MIX), F32)
    o_spec, o_shape = _mixer_out(bsz, t, nb, tb)
    return pl.pallas_call(
        kern, grid=(bsz // nb, t // tb),
        in_specs=[pl.BlockSpec((nb, tb, IN_EVEN), lambda b, j: (b, j, 0)),
                  pl.BlockSpec((nb, 1, A_SHIFT_W), bmap3),
                  pl.BlockSpec((nb, A_H, A_HD, A_HD), bmap4),
                  pl.BlockSpec((nb, B_H * B_HD, B_HD), bmap3),
                  pl.BlockSpec((1, A_SHIFT_W), cmap),
                  pl.BlockSpec((2 * A_LORA, MIX), cmap),
                  pl.BlockSpec((2 * A_LORA, MIX), cmap),
                  pl.BlockSpec((16, MIX), cmap)],
        out_specs=[o_spec,
                   pl.BlockSpec((nb, 1, A_SHIFT_W), bmap3),
                   pl.BlockSpec((nb, A_H, A_HD, A_HD), bmap4),
                   pl.BlockSpec((nb, B_H * B_HD, B_HD), bmap3)],
        out_shape=[o_shape,
                   jax.ShapeDtypeStruct((bsz, 1, A_SHIFT_W), F32),
                   jax.ShapeDtypeStruct((bsz, A_H, A_HD, A_HD), F32),
                   jax.ShapeDtypeStruct((bsz, B_H * B_HD, B_HD), F32)],
        scratch_shapes=[pltpu.VMEM((nb, 8, A_SHIFT_W), F32), pltpu.VMEM((nb, A_HD, MIX), F32),
                        pltpu.VMEM((nb, B_HD, B_H * B_HD), F32)] + [blk() for _ in range(15)] + [
                            pltpu.VMEM((nch, 2 * hl, MIX), BF16), pltpu.VMEM((nch, hl, A_HD), F32),
                            pltpu.VMEM((nch, hl, A_HD), F32), pltpu.VMEM((nch, hl, hl), BF16),
                            pltpu.VMEM((nch, A_HD, MIX), F32), pltpu.VMEM((nch, hl, MIX), BF16)],
        compiler_params=pltpu.CompilerParams(dimension_semantics=("arbitrary", "arbitrary"),
                                             vmem_limit_bytes=VMEM_LIMIT),
        name="mixer_rwkv_hgrn")(p, shift0, srw0, shg0, mu, w2p, a2p, par)


_O_CQ = 0
_O_CK = _O_CQ + C_KW
_O_CV = _O_CK + C_KW
_O_GC = _O_CV + MIX
_O_DQK = _O_GC + MIX
_O_DV = _O_DQK + 2 * MIX
_O_GD = _O_DV + MIX
_O_SLAB = _O_GD + MIX
(_P_CNORM, _P_DNORM, _P_G2B) = range(3)


def _odd_kernel(oin_ref, xin_ref, woin_ref, gin_ref, win_ref,
                sgl0_ref, conv0_ref, mc0_ref, mn0_ref, mm0_ref, g2_ref, cw_ref, cb_ref, sb_ref, par_ref,
                wo_ref, gf_ref,
                y_ref, sgl_ref, conv_ref, mc_ref, mn_ref, mm_ref,
                prev_sc, sgl_sc, mc_sc, mn_sc, mm_sc, cq_sc, ck_sc, ce_sc, cl_sc, oc_sc,
                dq_sc, dk_sc, li_sc, fa_sc, fl_sc, od_sc, x_sc, p_sc, *, nb, tb, lc):
    j = pl.program_id(1)
    nj = pl.num_programs(1)
    hl = D_H * lc
    seqs = range(nb)
    head_rows = lambda x: jnp.concatenate(
        [jnp.broadcast_to(x[h:h + 1], (lc, x.shape[1])) for h in range(D_H)], axis=0)

    @pl.when(j == 0)
    def _():
        for b in seqs:
            prev_sc[b] = conv0_ref[b]
            sgl_sc[b] = sgl0_ref[b].T
            mc_sc[b] = mc0_ref[b].T
            mn_sc[b] = head_rows(mn0_ref[b])
            mm_sc[b] = head_rows(mm0_ref[b])[:, :1]

    par = par_ref[...]
    row = lambda i: par[i:i + 1]

    def rows_of(lo, hi, r0, r1):
        return p_sc[r0:r1, lo:hi]

    proj_slabs = [(c, min(c + 256, IN_ODD_PAD)) for c in range(0, IN_ODD_PAD, 256)]

    def projection_stages(r0, r1):
        x1 = xin_ref[r0:r1, :] + _dot(oin_ref[r0:r1, :], woin_ref[...])
        x_sc[r0:r1, :] = x1
        h = _rms(x1, gin_ref[...]).astype(BF16)
        for lo, hi in proj_slabs:
            yield
            p_sc[r0:r1, lo:hi] = _dot(h, win_ref[:, lo:hi])

    def token_stages(r0, r1):
        n = r1 - r0
        span = tb if n % tb == 0 else n
        cum_c, ones_c = _chunk_mats(n, lc)
        slab = rows_of(_O_SLAB, _O_SLAB + SLAB, r0, r1)
        pre = _dot(slab.astype(BF16), g2_ref[...]) + row(_P_G2B)[:, :C_KW]
        logg = _log_sigmoid(pre) * (1.0 / C_GATE_NORM)
        yield
        gb = _mm(cum_c, logg, "nn", M_EXACT_R)
        gl = _mm(ones_c, logg, "nn", M_EXACT_R)
        yield
        ck = rows_of(_O_CK, _O_CK + C_KW, r0, r1)
        cq_sc[r0:r1, :] = rows_of(_O_CQ, _O_CQ + C_KW, r0, r1) * (C_KD ** -0.5) * jnp.exp(gb)
        ck_sc[r0:r1, :] = ck * jnp.exp(-gb)
        ce_sc[r0:r1, :] = ck * jnp.exp(gl - gb)
        cl_sc[r0:r1, :] = gl
        yield
        x = rows_of(_O_DQK, _O_DQK + 2 * MIX, r0, r1)
        cw = cw_ref[...]
        conv = cb_ref[...] + x * cw[D_CONV - 1:D_CONV]
        head_row = _iota((8, 2 * MIX), 0)
        for s in range(1, D_CONV):
            xr = pltpu.roll(x, s, axis=0)
            parts = []
            for a in range(0, n, span):
                kept = pltpu.roll(prev_sc[(r0 + a) // tb], s, axis=0)
                parts.append(jnp.where(head_row < s, kept, xr[a:a + 8]))
                if span > 8:
                    parts.append(xr[a + 8:a + span])
            xs = parts[0] if len(parts) == 1 else jnp.concatenate(parts, axis=0)
            conv = conv + xs * cw[D_CONV - 1 - s:D_CONV - s]
        for a in range(0, n, span):
            prev_sc[(r0 + a) // tb] = x[a + span - 8:a + span]
        yield
        conv = _silu(conv)
        dq_sc[r0:r1, :] = conv[:, :MIX]
        dk_sc[r0:r1, :] = conv[:, MIX:] * (D_HD ** -0.5)
        yield
        gates = slab + sb_ref[...]
        lf = pltpu.roll(_log_sigmoid(gates), SLAB - (SLAB_F - SLAB_I), axis=1)
        li_sc[r0:r1, :] = gates
        fa_sc[r0:r1, :] = _mm(cum_c, lf, "nn", M_EXACT_R)
        fl_sc[r0:r1, :] = _mm(ones_c, lf, "nn", M_EXACT_R)

    wide_c, _, incl_c, _ = _head_masks(C_H, lc, C_KD)
    consts_c = (wide_c, incl_c)
    lane = _iota((hl, SLAB), 1)
    head = _iota((hl, SLAB), 0) // lc
    pick = lambda z: jnp.sum(jnp.where(lane == head + SLAB_I, _tile_rows(z, D_H), 0.0), -1, keepdims=True)

    wide_d, same_d, incl_d, _ = _head_masks(D_H, lc, D_HD)
    consts_d = (wide_d, same_d, incl_d, same_d.astype(BF16))

    per_seq = tb // lc
    if per_seq == 1:
        chains, run = math.gcd(nb, 4), 1
        trips = nb // chains
        first_chunk = lambda gi, q: gi * chains + q
    else:
        chains, run = nb, math.gcd(per_seq, 8)
        trips = per_seq // run
        first_chunk = lambda gi, q: q * per_seq + gi * run
    assert per_seq % run == 0

    def chunk_stages(gi):
        stages, finish = [], []
        for q in range(chains):
            first = first_chunk(gi, q)
            seq = first // per_seq
            out = {}
            rows = lambda u, first=first: _rows_at(((first + u) * lc, lc), lc)
            in_seq = lambda u, first=first, seq=seq: _rows_at(((first + u) * lc - seq * tb, lc), lc)

            def load_c(u, first=first, seq=seq, rows=rows, in_seq=in_seq):
                sl = rows(u)
                return (cq_sc[sl, :], ck_sc[sl, :], ce_sc[sl, :], cl_sc[pl.ds((first + u) * lc, 1), :],
                        p_sc[sl, _O_CV:_O_CV + MIX])

            def store_c(u, o, rows=rows):
                oc_sc[rows(u), :] = o

            def load_d(u, seq=seq, rows=rows, in_seq=in_seq):
                sl = rows(u)
                return (dq_sc[sl, :], dk_sc[sl, :], p_sc[sl, _O_DV:_O_DV + MIX],
                        pick(li_sc[sl, :]), pick(fa_sc[sl, :]), pick(fl_sc[sl, :]))

            def store_d(u, o, rows=rows):
                od_sc[rows(u), :] = o

            stages.append((_mlstm_stages(run, load_d, store_d, (mc_sc[seq], mn_sc[seq], mm_sc[seq]), consts_d, out),
                           3 * run + 1))
            stages.append((_gla_stages(run, load_c, store_c, sgl_sc[seq], consts_c, C_H, C_VD, out, "gla"),
                           2 * run + 1))
            finish.append((seq, out))

        def write_back():
            for seq, out in finish:
                sgl_sc[seq] = out["gla"]
                mc_sc[seq], mn_sc[seq], mm_sc[seq] = out["mlstm"]

        return stages, write_back

    def output_stages(r0, r1):
        oc = _head_rms(oc_sc[r0:r1, :], row(_P_CNORM), C_VD) * _silu(rows_of(_O_GC, _O_GC + MIX, r0, r1))
        yield
        od = _head_rms(od_sc[r0:r1, :], row(_P_DNORM), D_HD) * _silu(rows_of(_O_GD, _O_GD + MIX, r0, r1))
        o = jnp.concatenate([oc, od], axis=1).astype(BF16)
        yield
        x2 = x_sc[r0:r1, :] + _dot(o, wo_ref[...])
        y_ref[r0:r1, :] = _rms(x2, gf_ref[...])

    if per_seq == 1 or nb > 1:
        _interleave((projection_stages(0, nb * tb), len(proj_slabs) + 1))
        _interleave((token_stages(0, nb * tb), 6))

        def trip(gi, carry):
            stages, write_back = chunk_stages(gi)
            _interleave(*stages)
            write_back()
            return carry

        lax.fori_loop(0, trips, trip, 0)
        _interleave((output_stages(0, nb * tb), 3))
    else:
        grows = run * lc
        for phase in range(trips + 3):
            stages, after = [], []
            if 0 <= phase - 2 < trips:
                chunk, write_back = chunk_stages(phase - 2)
                stages += chunk
                after.append(write_back)
            if phase < trips:
                stages.append((projection_stages(phase * grows, (phase + 1) * grows), len(proj_slabs) + 1))
            if 0 <= phase - 1 < trips:
                stages.append((token_stages((phase - 1) * grows, phase * grows), 6))
            if 0 <= phase - 3 < trips:
                stages.append((output_stages((phase - 3) * grows, (phase - 2) * grows), 3))
            _interleave(*stages)
            for fn in after:
                fn()

    @pl.when(j == nj - 1)
    def _():
        for b in seqs:
            sgl_ref[b] = sgl_sc[b].T
            conv_ref[b] = prev_sc[b]
            mc_ref[b] = mc_sc[b].T
            mn_ref[b] = jnp.concatenate([mn_sc[b, h * lc:h * lc + 1] for h in range(D_H)], axis=0)
            mm_ref[b] = jnp.concatenate(
                [jnp.broadcast_to(mm_sc[b, h * lc:h * lc + 1], (1, LANE)) for h in range(D_H)], axis=0)


def _odd_layer(o_in, x, wo_in, g_in, w_in, sgl0, conv0, mc0, mn0, mm0, g2p, cw, cb, sb, par, wo, gf, bsz, t, nb, tb,
               lc):
    hl = D_H * lc
    once = lambda shape: pl.BlockSpec(shape, lambda b, j: (0, 0), pipeline_mode=pl.Buffered(1))
    kern = functools.partial(_odd_kernel, nb=nb, tb=tb, lc=lc)
    bmap3 = lambda b, j: (b, 0, 0)
    cmap = lambda b, j: (0, 0)
    blk = lambda w: pltpu.VMEM((nb * tb, w), F32)
    assert nb == 1 or tb == t
    rows = pl.BlockSpec((nb * tb, D_MODEL), lambda b, j: (b * (t // tb) + j, 0))
    return pl.pallas_call(
        kern, grid=(bsz // nb, t // tb),
        in_specs=[rows, rows, once((D_MODEL, D_MODEL)), pl.BlockSpec((1, D_MODEL), cmap),
                  once((D_MODEL, IN_ODD_PAD)),
                  pl.BlockSpec((nb, C_KW, C_VD), bmap3),
                  pl.BlockSpec((nb, 8, 2 * MIX), bmap3),
                  pl.BlockSpec((nb, MIX, D_HD), bmap3),
                  pl.BlockSpec((nb, D_H, D_HD), bmap3),
                  pl.BlockSpec((nb, D_H, LANE), bmap3),
                  pl.BlockSpec((SLAB, C_KW), cmap),
                  pl.BlockSpec((D_CONV, 2 * MIX), cmap),
                  pl.BlockSpec((1, 2 * MIX), cmap),
                  pl.BlockSpec((1, SLAB), cmap),
                  pl.BlockSpec((8, MIX), cmap),
                  once((D_MODEL, D_MODEL)),
                  pl.BlockSpec((1, D_MODEL), cmap)],
        out_specs=[rows,
                   pl.BlockSpec((nb, C_KW, C_VD), bmap3),
                   pl.BlockSpec((nb, 8, 2 * MIX), bmap3),
                   pl.BlockSpec((nb, MIX, D_HD), bmap3),
                   pl.BlockSpec((nb, D_H, D_HD), bmap3),
                   pl.BlockSpec((nb, D_H, LANE), bmap3)],
        out_shape=[jax.ShapeDtypeStruct((bsz * t, D_MODEL), F32),
                   jax.ShapeDtypeStruct((bsz, C_KW, C_VD), F32),
                   jax.ShapeDtypeStruct((bsz, 8, 2 * MIX), F32),
                   jax.ShapeDtypeStruct((bsz, MIX, D_HD), F32),
                   jax.ShapeDtypeStruct((bsz, D_H, D_HD), F32),
                   jax.ShapeDtypeStruct((bsz, D_H, LANE), F32)],
        scratch_shapes=[pltpu.VMEM((nb, 8, 2 * MIX), F32), pltpu.VMEM((nb, C_VD, C_KW), F32),
                        pltpu.VMEM((nb, D_HD, MIX), F32), pltpu.VMEM((nb, hl, D_HD), F32),
                        pltpu.VMEM((nb, hl, 1), F32),
                        blk(C_KW), blk(C_KW), blk(C_KW), blk(C_KW), blk(MIX),
                        blk(MIX), blk(MIX), blk(SLAB), blk(SLAB), blk(SLAB), blk(MIX),
                        blk(D_MODEL), blk(IN_ODD_PAD)],
        compiler_params=pltpu.CompilerParams(dimension_semantics=("arbitrary", "arbitrary"),
                                             vmem_limit_bytes=VMEM_LIMIT),
        name="layer_gla_mlstm")(o_in, x, wo_in, g_in, w_in, sgl0, conv0, mc0, mn0, mm0, g2p, cw, cb, sb, par, wo, gf)


MIXER_ROWS = 256
SHORT_SEQS = 8


def _mixer_blocking(bsz, t):
    tb = math.gcd(t, MIXER_ROWS)
    nb = math.gcd(bsz, SHORT_SEQS) if tb == t and t * SHORT_SEQS <= MIXER_ROWS else 1
    return nb, tb


def _paired_blocking(bsz, nb, tb):
    if nb == 1 and bsz % 2 == 0 and (tb // 2) % 32 == 0:
        return 2, tb // 2
    return nb, tb


def _run(x, shift, s_rwkv, s_hgrn, s_gla, conv, mc, mn, mm, wts):
    bsz, t, d = x.shape
    nb, tb = _mixer_blocking(bsz, t)
    la = math.gcd(t, 16)
    lbc = math.gcd(t, 32)
    x2 = x.reshape(bsz * t, d)
    p0 = _inproj(x2, wts["g0"], wts["w_in0"])
    o0, shift_n, srw_n, shg_n = _even_mixer(
        p0.reshape(bsz, t, IN_EVEN), shift[0][:, None], s_rwkv[0], s_hgrn[0].reshape(bsz, B_H * B_HD, B_HD),
        wts["mu"], wts["w2p"], wts["a2p"], wts["par_e"], *_paired_blocking(bsz, nb, tb), la, lbc)
    conv8 = jnp.pad(conv[0], ((0, 0), (8 - (D_CONV - 1), 0), (0, 0)))
    mm_l = jnp.broadcast_to(mm[0][:, :, None], (bsz, D_H, LANE))
    y, sgl_n, conv_n, mc_n, mn_n, mm_n = _odd_layer(
        o0.reshape(bsz * t, d), x2, wts["w_out0"], wts["g1"], wts["w_in1"],
        s_gla[0].reshape(bsz, C_KW, C_VD), conv8, mc[0].reshape(bsz, MIX, D_HD), mn[0], mm_l,
        wts["g2p"], wts["cw"], wts["cb"], wts["sb"], wts["par_o"], wts["w_out1"], wts["gf"], bsz, t, nb,
        math.gcd(t, 2 * tb) if nb == 1 else tb, lbc)
    return (y.reshape(bsz, t, d), shift_n.reshape(1, bsz, A_SHIFT_W), srw_n[None],
            shg_n.reshape(1, bsz, B_H, B_HD, B_HD), sgl_n.reshape(1, bsz, C_H, C_KD, C_VD),
            conv_n[None, :, 8 - (D_CONV - 1):], mc_n.reshape(1, bsz, D_H, D_HD, D_HD), mn_n[None],
            mm_n[None, :, :, 0])


def _odd_in_projection(w):
    sizes = (C_KW, C_KW, MIX, C_LORA, MIX, 2 * MIX, MIX, D_H, D_H, MIX)
    off = np.cumsum((0,) + sizes)
    cq, ck, cv, cg, gc, dqk, dv, di, df, gd = (w[:, off[i]:off[i + 1]] for i in range(len(sizes)))
    pad = jnp.zeros((w.shape[0], SLAB - C_LORA - 2 * D_H), w.dtype)
    return jnp.concatenate([cq, ck, cv, gc, dqk, dv, gd, cg, di, df, pad], axis=1)


def kernel(x_prompt, x_sample, state_shift_a, state_rwkv, state_hgrn, state_gla, state_conv_d, state_mlstm_c,
           state_mlstm_n, state_mlstm_m, norm_g, w_in_even, w_out_even, a_mu, a_w0, a_w2, a_a0, a_a2, a_kk,
           a_ka, a_rk, a_ln_w, a_ln_b, b_lb, b_norm, w_in_odd, w_out_odd, c_g2, c_g2b, c_norm, d_conv_w,
           d_conv_b, d_ib, d_fb, d_norm, final_norm):
    assert w_in_even.shape[0] == 1 and w_in_odd.shape[0] == 1 and b_lb.shape[0] == 2
    zpad = lambda a, rows_before, rows_total: jnp.pad(a, ((rows_before, rows_total - rows_before - a.shape[0]), (0, 0)))
    par_e = jnp.concatenate([a_w0, a_a0, a_kk, a_ka, a_rk, a_ln_w, a_ln_b, b_norm, b_lb[0:1], b_lb[1:2]], axis=0)
    g2b = jnp.pad(c_g2b, ((0, 0), (0, MIX - C_KW)))
    par_o = jnp.concatenate([c_norm, d_norm, g2b], axis=0)
    w_in1 = _odd_in_projection(w_in_odd[0].astype(BF16))
    sb = jnp.pad(jnp.concatenate([d_ib, d_fb], axis=1), ((0, 0), (SLAB_I, SLAB - SLAB_I - 2 * D_H)))
    wts = {
        "g0": norm_g[0:1], "g1": norm_g[1:2], "gf": final_norm[None],
        "w_in0": w_in_even[0].astype(BF16), "w_out0": w_out_even[0].astype(BF16),
        "w_in1": w_in1, "w_out1": w_out_odd[0].astype(BF16),
        "mu": a_mu, "w2p": zpad(a_w2[0], 0, 2 * A_LORA).astype(BF16),
        "a2p": zpad(a_a2[0], A_LORA, 2 * A_LORA).astype(BF16),
        "par_e": zpad(par_e, 0, 16), "par_o": zpad(par_o, 0, 8),
        "g2p": zpad(c_g2[0], 0, SLAB).astype(BF16), "cw": d_conv_w[0], "cb": d_conv_b, "sb": sb,
    }
    bp = x_prompt.shape[0]
    z = lambda *s: jnp.zeros(s, x_prompt.dtype)
    prompt = _run(x_prompt, z(1, bp, A_SHIFT_W), z(1, bp, A_H, A_HD, A_HD), z(1, bp, B_H, B_HD, B_HD),
                  z(1, bp, C_H, C_KD, C_VD), z(1, bp, D_CONV - 1, 2 * MIX), z(1, bp, D_H, D_HD, D_HD),
                  z(1, bp, D_H, D_HD), z(1, bp, D_H), wts)
    sample = _run(x_sample, state_shift_a, state_rwkv, state_hgrn, state_gla, state_conv_d, state_mlstm_c,
                  state_mlstm_n, state_mlstm_m, wts)
    return (prompt[0], sample[0]) + prompt[1:] + sample[1:]
```

```python
import functools
import math

import jax
import jax.numpy as jnp
import numpy as np
from jax import lax
from jax.experimental import pallas as pl
from jax.experimental.pallas import tpu as pltpu

F32 = jnp.float32
BF16 = jnp.bfloat16

D_MODEL = 1024
MIX = D_MODEL // 2
NORM_EPS = 1e-5
A_HD = 64
A_H = MIX // A_HD
A_LORA = 64
A_GN_EPS = 64e-5
A_SHIFT_W = 3 * MIX + 2 * A_LORA
B_HD = 128
B_H = MIX // B_HD
IN_EVEN = A_SHIFT_W + 5 * MIX
C_H = 4
C_VD = MIX // C_H
C_KD = C_VD // 2
C_KW = C_H * C_KD
C_LORA = 16
C_GATE_NORM = 16.0
D_H = 4
D_HD = MIX // D_H
D_CONV = 4
LANE = 128
SLAB = LANE
SLAB_I = C_LORA
SLAB_F = C_LORA + D_H
IN_ODD_PAD = 2 * C_KW + 2 * MIX + 2 * MIX + 2 * MIX + SLAB
VMEM_LIMIT = 48 * 1024 * 1024


_DIMS = {"nn": ((1,), (0,)), "nt": ((1,), (1,)), "tn": ((0,), (0,))}
M_EXACT_R = "rx"
M_EXACT_L = "lx"
M_G = "bf"
M_TINV = "bf"
M_ST = "bf"
M_ATT = "bf"


def _split2(x):
    hi = x.astype(BF16)
    return hi, (x - hi.astype(F32)).astype(BF16)


def _mm(a, b, form, mode):
    dn = (_DIMS[form], ((), ()))
    d = lambda x, y: lax.dot_general(x, y, dn, preferred_element_type=F32)
    if mode == "bf":
        return d(a.astype(BF16), b.astype(BF16))
    if mode == "lx":
        bb = b.astype(BF16)
        h, l = _split2(a)
        return d(h, bb) + d(l, bb)
    assert mode == "rx"
    ab = a.astype(BF16)
    h, l = _split2(b)
    return d(ab, h) + d(ab, l)


def _dot(a, b):
    return jnp.dot(a, b, preferred_element_type=F32)


def _iota(shape, dim):
    return lax.broadcasted_iota(jnp.int32, shape, dim)


def _sigmoid(x):
    return 0.5 + 0.5 * jnp.tanh(0.5 * x)


def _log_sigmoid(x):
    return jnp.minimum(x, 0.0) - jnp.log(1.0 + jnp.exp(-jnp.abs(x)))


def _silu(x):
    return x * _sigmoid(x)


def _rms(x, g):
    return x * lax.rsqrt(jnp.mean(x * x, -1, keepdims=True) + NORM_EPS) * g


def _rows_at(start_and_alignment, size):
    start, alignment = start_and_alignment
    if not isinstance(start, int):
        start = pl.multiple_of(start, alignment)
    return pl.ds(start, size)


def _tile_rows(x, n):
    return jnp.concatenate([x] * n, axis=0)


def _stack_heads(x, n_heads, width):
    return jnp.concatenate([x[:, h * width:(h + 1) * width] for h in range(n_heads)], axis=0)


def _unstack_heads(x, n_heads, rows):
    return jnp.concatenate([x[h * rows:(h + 1) * rows] for h in range(n_heads)], axis=1)


def _head_masks(n_heads, rows, width):
    hl = n_heads * rows
    rh = _iota((hl, n_heads * width), 0) // rows
    lh = _iota((hl, n_heads * width), 1) // width
    wide = (rh == lh).astype(F32)
    ri, ci = _iota((hl, hl), 0), _iota((hl, hl), 1)
    same = (ri // rows) == (ci // rows)
    incl = same & (ci <= ri)
    strict = same & (ci < ri)
    return wide, same, incl, strict


def _chunk_mats(n, chunk):
    ri, ci = _iota((n, n), 0), _iota((n, n), 1)
    same = (ri // chunk) == (ci // chunk)
    return (same & (ci <= ri)).astype(BF16), same.astype(BF16)


def _seg_sum(x, seg):
    bd = ((_iota((LANE, LANE), 0) // seg) == (_iota((LANE, LANE), 1) // seg)).astype(BF16)
    parts = [_mm(x[:, j:j + LANE], bd, "nn", M_EXACT_L) for j in range(0, x.shape[1], LANE)]
    return jnp.concatenate(parts, axis=1)


def _head_rms(x, g, width):
    parts = []
    for j in range(0, x.shape[1], width):
        xs = x[:, j:j + width]
        parts.append(xs * lax.rsqrt(jnp.mean(xs * xs, -1, keepdims=True) + NORM_EPS))
    return jnp.concatenate(parts, axis=1) * g


def _inproj_kernel(x_ref, g_ref, w_ref, p_ref):
    h = _rms(x_ref[...], g_ref[...])
    p_ref[...] = _dot(h.astype(BF16), w_ref[...])


def _row_tile(m):
    return math.gcd(m, 512)


def _full(shape):
    return pl.BlockSpec(shape, lambda i: (0,) * len(shape))


def _dense_params():
    return pltpu.CompilerParams(dimension_semantics=("arbitrary",), vmem_limit_bytes=VMEM_LIMIT)


def _inproj(x, g, w):
    m, d = x.shape
    n = w.shape[1]
    tm = _row_tile(m)
    return pl.pallas_call(
        _inproj_kernel, grid=(m // tm,),
        in_specs=[pl.BlockSpec((tm, d), lambda i: (i, 0)), _full((1, d)), _full((d, n))],
        out_specs=pl.BlockSpec((tm, n), lambda i: (i, 0)),
        out_shape=jax.ShapeDtypeStruct((m, n), F32),
        compiler_params=_dense_params(), name="inproj")(x, g, w)


def _interleave(*staged):
    live = [[gen, 0, max(n, 1)] for gen, n in staged]
    while live:
        item = min(live, key=lambda it: it[1] / it[2])
        try:
            next(item[0])
            item[1] += 1
        except StopIteration:
            live.remove(item)


def _rwkv_factor_stages(chunks, consts, store):
    wide, incl, strict, eye, halves = consts
    L = chunks[0][0].shape[0]
    hl = A_H * L
    n = range(len(chunks))
    tile = functools.partial(_tile_rows, n=A_H)
    for i in n:
        store(i, "r_w", (tile(chunks[i][1]) * wide).astype(BF16))
        store(i, "be_w", (tile(chunks[i][5]) * wide).astype(BF16))
    stack = lambda x: _stack_heads(x, A_H, A_HD)
    g = [_mm(jnp.concatenate([stack(c[0]), stack(c[1])], axis=0),
             jnp.concatenate([stack(c[2]), stack(c[3])], axis=0), "nt", M_G) for c in chunks]
    yield
    a_kk = [jnp.where(strict, x[:hl, :hl], 0.0) for x in g]
    a_kb = [jnp.where(strict, x[:hl, hl:], 0.0) for x in g]
    a_rk = [jnp.where(incl, x[hl:, :hl], 0.0) for x in g]
    for i in n:
        store(i, "a_rb", jnp.where(incl, g[i][hl:, hl:], 0.0).astype(BF16))
    t = [eye - jnp.where(halves[0], x, 0.0) for x in a_kb]
    v_st = [_stack_heads(c[6], A_H, A_HD) for c in chunks]
    av = [_mm(a_kk[i], v_st[i], "nn", M_ST) for i in n]
    for i in n:
        store(i, "o2", _mm(a_rk[i], v_st[i], "nn", M_ST))
        store(i, "kv", _mm(v_st[i], tile(chunks[i][4]) * wide, "tn", M_ST))
    for half in halves[1:]:
        ta = [_mm(t[i], jnp.where(half, a_kb[i], 0.0), "nn", M_TINV) for i in n]
        yield
        t = [t[i] - _mm(ta[i], t[i], "nn", M_TINV) for i in n]
        yield
    for i in n:
        store(i, "m1n", (-_mm(t[i], tile(chunks[i][0]) * wide, "nn", M_ST)).astype(BF16))
        store(i, "m2n", -_mm(t[i], av[i], "nn", M_ST))
    yield


def _rwkv_state_stages(n_chunks, load_factors, store_o, s, out):
    for c in range(n_chunks):
        xm, m2n, o2, a_rb, kv, be_w, ecl = load_factors(c)
        hl = xm.shape[0] // 2
        xs = _mm(xm, s, "nt", M_ST)
        yield
        u = xs[:hl] + m2n
        s = s * ecl + kv + _mm(u, be_w, "tn", M_ST)
        o_st = xs[hl:] + o2 + _mm(a_rb, u, "nn", M_ST)
        yield
        store_o(c, _unstack_heads(o_st, A_H, hl // A_H))
    out["rwkv"] = s


def _gla_stages(n_chunks, load_chunk, store_o, s, consts, n_heads, vd, out, key):
    wide, incl = consts
    tile = functools.partial(_tile_rows, n=n_heads)
    for c in range(n_chunks):
        qd, ki, kend, bl, v = load_chunk(c)
        q_w = tile(qd) * wide
        v_st = _stack_heads(v, n_heads, vd)
        att = jnp.where(incl, _mm(q_w, tile(ki), "nt", M_ATT), 0.0)
        kv = _mm(v_st, tile(kend) * wide, "tn", M_ATT)
        yield
        o_st = _mm(att, v_st, "nn", M_ATT) + _mm(q_w, s, "nt", M_ATT)
        s = s * jnp.exp(bl) + kv
        yield
        store_o(c, _unstack_heads(o_st, n_heads, qd.shape[0]))
    out[key] = s


def _mlstm_stages(n_chunks, load_chunk, store_o, state, consts, out):
    wide, same, incl, ones_bd = consts
    c, nrow, m = state
    tile = functools.partial(_tile_rows, n=D_H)
    for ch in range(n_chunks):
        q, k, v, li, a, a_last = load_chunk(ch)
        L = q.shape[0]
        hl = D_H * L
        to_row = lambda col: jnp.broadcast_to(col, (hl, LANE)).T[:hl]
        q_st = _stack_heads(q, D_H, D_HD)
        k_st = _stack_heads(k, D_H, D_HD)
        v_st = _stack_heads(v, D_H, D_HD)
        qk = _mm(q_st, k_st, "nt", M_ATT)
        dlog = jnp.where(incl, a - to_row(a) + to_row(li), -jnp.inf)
        m_intra = jnp.max(dlog, -1, keepdims=True)
        e_end = a_last - a + li
        m_loc = jnp.max(jnp.where(same, to_row(e_end), -jnp.inf), -1, keepdims=True)
        yield
        s_intra = jnp.exp(dlog - m_intra) * qk
        num_intra = _mm(s_intra, v_st, "nn", M_ATT)
        den_intra = jnp.sum(s_intra, -1, keepdims=True)
        w_end = jnp.exp(e_end - m_loc)
        c_loc = _mm(k_st, tile(v) * wide * w_end, "tn", M_ATT)
        n_loc = _mm(ones_bd, w_end * k_st, "nn", M_ATT)
        yield
        g = a + m
        m_t = jnp.maximum(g, m_intra)
        f_inter = jnp.exp(g - m_t)
        f_intra = jnp.exp(m_intra - m_t)
        qc = _mm(q_st, c, "nn", M_ATT)
        num_inter = jnp.concatenate(
            [qc[h * L:(h + 1) * L, h * D_HD:(h + 1) * D_HD] for h in range(D_H)], axis=0)
        den_inter = jnp.sum(q_st * nrow, -1, keepdims=True)
        num = num_intra * f_intra + num_inter * f_inter
        den = f_intra * den_intra + f_inter * den_inter
        h_st = num / jnp.maximum(jnp.abs(den), jnp.exp(-m_t))
        m_new = jnp.maximum(a_last + m, m_loc)
        fo = jnp.exp(a_last + m - m_new)
        fl = jnp.exp(m_loc - m_new)
        per_value = lambda col: jnp.concatenate(
            [jnp.broadcast_to(col[h * L:h * L + 1], (1, D_HD)) for h in range(D_H)], axis=1)
        c = per_value(fo) * c + per_value(fl) * c_loc
        nrow = fo * nrow + fl * n_loc
        m = m_new
        yield
        store_o(ch, _unstack_heads(h_st, D_H, L))
    out["mlstm"] = (c, nrow, m)


(_E_W0, _E_A0, _E_KK, _E_KA, _E_RK, _E_LNW, _E_LNB, _E_BNORM, _E_LB0, _E_LB1) = range(10)


def _even_kernel(p_ref, shift0_ref, srw0_ref, shg0_ref, mu_ref, w2_ref, a2_ref, par_ref,
                 o_ref, shift_ref, srw_ref, shg_ref,
                 prev_sc, srw_sc, shg_sc, kkt_sc, rt_sc, kh_sc, bh_sc, ke_sc, be_sc, v_sc, epos_sc, bonus_sc, oa_sc,
                 qd_sc, ki_sc, kend_sc, bl_sc, ob_sc, xm_sc, m2_sc, o2_sc, arb_sc, kv_sc, bew_sc, *, nb, tb, la, lb):
    j = pl.program_id(1)
    nj = pl.num_programs(1)
    seqs = range(nb)
    cols = lambda lo, hi: jnp.concatenate([p_ref[b, :, lo:hi] for b in seqs], axis=0)

    @pl.when(j == 0)
    def _():
        for b in seqs:
            prev_sc[b] = jnp.broadcast_to(shift0_ref[b], prev_sc.shape[1:])
            srw_sc[b] = jnp.concatenate([srw0_ref[b, h] for h in range(A_H)], axis=1)
            shg_sc[b] = shg0_ref[b].T

    par = par_ref[...]
    row = lambda i: par[i:i + 1]

    per_seq = tb // la
    group = math.gcd(nb * per_seq, 8)
    chains = nb
    run = group // chains
    ngroups = per_seq // run
    prows = run * la
    assert group % chains == 0 and per_seq % run == 0 and prows % lb == 0
    hper = prows // lb
    late_hgrn = ngroups > 1

    def pieces(g):
        if ngroups == 1:
            return [(0, nb * tb)]
        return [(q * tb + g * prows, q * tb + (g + 1) * prows) for q in seqs]

    def rows_of(lo, hi, r0, r1):
        parts = [p_ref[b, max(r0, b * tb) - b * tb:min(r1, (b + 1) * tb) - b * tb, lo:hi]
                 for b in seqs if max(r0, b * tb) < min(r1, (b + 1) * tb)]
        return parts[0] if len(parts) == 1 else jnp.concatenate(parts, axis=0)

    def rwkv_token_stages(r0, r1):
        grows = r1 - r0
        pa = rows_of(0, A_SHIFT_W, r0, r1)
        rolled = pltpu.roll(pa, 1, axis=0)
        starts = range(r0, r1, tb) if grows % tb == 0 else (r0,)
        span = tb if grows % tb == 0 else grows
        first = _iota(pa.shape, 0) % span == 0
        before = [jnp.broadcast_to(prev_sc[s // tb, 0:1], (span, A_SHIFT_W)) for s in starts]
        shifted = jnp.where(first, before[0] if len(before) == 1 else jnp.concatenate(before, axis=0), rolled)
        for s in starts:
            prev_sc[s // tb] = jnp.broadcast_to(pa[s - r0 + span - 1:s - r0 + span], prev_sc.shape[1:])
        xm = pa + (shifted - pa) * mu_ref[...]
        r = xm[:, :MIX]
        k = xm[:, MIX:2 * MIX]
        v = xm[:, 2 * MIX:3 * MIX]
        lora_in = xm[:, 3 * MIX:]
        yield
        lw = _sigmoid(row(_E_W0) + _dot(jnp.tanh(lora_in).astype(BF16), w2_ref[...])) * (-math.exp(-0.5))
        a = _sigmoid(row(_E_A0) + _dot(lora_in.astype(BF16), a2_ref[...]))
        yield
        kk = k * row(_E_KK)
        kk = kk * lax.rsqrt(jnp.maximum(_seg_sum(kk * kk, A_HD), 1e-12))
        yield
        k = k * (1.0 + (a - 1.0) * row(_E_KA))
        bv = kk * a
        cum_a, ones_a = _chunk_mats(grows, la)
        c = _mm(cum_a, lw, "nn", M_EXACT_R)
        yield
        cl = _mm(ones_a, lw, "nn", M_EXACT_R)
        yield
        e_pos = jnp.exp(c)
        e_neg = jnp.exp(-c)
        e_end = jnp.exp(cl - c)
        kkt_sc[r0:r1, :] = kk * jnp.exp(c - lw)
        rt_sc[r0:r1, :] = r * e_pos
        yield
        kh_sc[r0:r1, :] = k * e_neg
        bh_sc[r0:r1, :] = bv * e_neg
        ke_sc[r0:r1, :] = k * e_end
        be_sc[r0:r1, :] = bv * e_end
        yield
        v_sc[r0:r1, :] = v
        epos_sc[r0:r1, :] = e_pos
        bonus_sc[r0:r1, :] = _seg_sum(r * k * row(_E_RK), A_HD) * v

    wide, _, incl, strict = _head_masks(A_H, la, A_HD)
    hl = A_H * la
    ri, ci_ = _iota((hl, hl), 0), _iota((hl, hl), 1)
    eye = (ri == ci_).astype(F32)
    halves = []
    m = 1
    while m < la:
        halves.append(((ri // (2 * m)) == (ci_ // (2 * m))) & ((ri // m) != (ci_ // m)))
        m *= 2
    consts_a = (wide, incl, strict, eye, halves)

    base = A_SHIFT_W + MIX

    def hgrn_token_stages():
        e0 = row(_E_LB0)
        e1 = row(_E_LB1)
        emax = jnp.maximum(e0, e1)
        e0 = jnp.exp(e0 - emax)
        lower = e0 / (e0 + jnp.exp(e1 - emax))
        g = lower + (1.0 - lower) * jax.nn.sigmoid(cols(base + MIX, base + 2 * MIX))
        yield
        logg = jnp.log(g)
        cum_b, ones_b = _chunk_mats(nb * tb, lb)
        yield
        gb = _mm(cum_b, logg, "nn", M_EXACT_R)
        yield
        gl = _mm(ones_b, logg, "nn", M_EXACT_R)
        yield
        qd_sc[...] = _silu(cols(base, base + MIX)) * (B_HD ** -0.5) * jnp.exp(gb)
        yield
        ki_sc[...] = (1.0 - g) * jnp.exp(-gb)
        yield
        kend_sc[...] = (1.0 - g) * jnp.exp(gl - gb)
        bl_sc[...] = gl

    wide_b, _, incl_b, _ = _head_masks(B_H, lb, B_HD)
    consts_b = (wide_b, incl_b)

    def factor_stages(gi, slot):
        sls = [_rows_at(((q * per_seq + gi * run + u) * la, la), la) for q in seqs for u in range(run)]
        hl_a = A_H * la

        def store(u, name, value):
            if name == "m1n":
                xm_sc[slot + u, :hl_a, :] = value
            elif name == "r_w":
                xm_sc[slot + u, hl_a:, :] = value
            else:
                {"m2n": m2_sc, "o2": o2_sc, "a_rb": arb_sc, "kv": kv_sc, "be_w": bew_sc}[name][slot + u] = value

        yield from _rwkv_factor_stages(
            [(kkt_sc[sl, :], rt_sc[sl, :], kh_sc[sl, :], bh_sc[sl, :], ke_sc[sl, :], be_sc[sl, :], v_sc[sl, :])
             for sl in sls], consts_a, store)

    def state_stages(gi):
        slot = (gi % 2) * group
        stages, finish = [], []
        for q in range(chains):
            seq = q
            first = seq * per_seq + gi * run
            row0 = first * la
            out = {}

            def load_factors(u, q=q, first=first):
                i = slot + q * run + u
                return (xm_sc[i], m2_sc[i], o2_sc[i], arb_sc[i], kv_sc[i], bew_sc[i],
                        epos_sc[pl.ds((first + u) * la + la - 1, 1), :])

            def store_oa(u, o, row0=row0):
                oa_sc[_rows_at((row0 + u * la, la), la), :] = o

            def load_b(u, row0=row0, seq=seq):
                r = row0 + u * lb
                sl = _rows_at((r, lb), lb)
                return (qd_sc[sl, :], ki_sc[sl, :], kend_sc[sl, :], bl_sc[pl.ds(r, 1), :],
                        p_ref[seq, _rows_at((r - seq * tb, lb), lb), base + 2 * MIX:base + 3 * MIX])

            def store_ob(u, o, row0=row0):
                ob_sc[_rows_at((row0 + u * lb, lb), lb), :] = o

            stages.append((_rwkv_state_stages(run, load_factors, store_oa, srw_sc[seq], out), 2 * run + 1))
            if late_hgrn and gi == ngroups - 1:
                load_all = functools.partial(load_b, row0=seq * tb)
                store_all = functools.partial(store_ob, row0=seq * tb)
                stages.append((_gla_stages(hper * ngroups, load_all, store_all, shg_sc[seq], consts_b, B_H, B_HD,
                                           out, "hgrn"), 2 * hper * ngroups + 1))
            elif not late_hgrn:
                stages.append((_gla_stages(hper, load_b, store_ob, shg_sc[seq], consts_b, B_H, B_HD, out, "hgrn"),
                               2 * hper + 1))
            finish.append((seq, out))

        def write_back():
            for seq, out in finish:
                srw_sc[seq] = out["rwkv"]
                if "hgrn" in out:
                    shg_sc[seq] = out["hgrn"]

        return stages, write_back

    def rwkv_output_stages(r0, r1):
        o = oa_sc[r0:r1, :]
        mean = _seg_sum(o, A_HD) * (1.0 / A_HD)
        cen = o - mean
        yield
        var = _seg_sum(cen * cen, A_HD) * (1.0 / A_HD)
        yield
        oa = cen * lax.rsqrt(var + A_GN_EPS) * row(_E_LNW) + row(_E_LNB)
        oa = oa + bonus_sc[r0:r1, :]
        put_o(r0, r1, 0, oa * _silu(rows_of(A_SHIFT_W, A_SHIFT_W + MIX, r0, r1)))

    def put_o(r0, r1, lo, value):
        value = value.astype(o_ref.dtype)
        if len(o_ref.shape) == 2:
            o_ref[r0:r1, lo:lo + MIX] = value
        else:
            for b in seqs:
                a, z = max(r0, b * tb), min(r1, (b + 1) * tb)
                if a < z:
                    o_ref[b, a - b * tb:z - b * tb, lo:lo + MIX] = value[a - r0:z - r0]

    def hgrn_output(r0, r1):
        ob = _head_rms(ob_sc[r0:r1, :], row(_E_BNORM), B_HD) * _silu(rows_of(base + 3 * MIX, base + 4 * MIX, r0, r1))
        put_o(r0, r1, MIX, ob)

    for phase in range(ngroups + 3):
        stages, after = [], []
        if 0 <= phase - 2 < ngroups:
            chain_stages, write_back = state_stages(phase - 2)
            stages += chain_stages
            after.append(write_back)
        if 0 <= phase - 1 < ngroups:
            stages.append((factor_stages(phase - 1, ((phase - 1) % 2) * group), 2 * len(halves) + 1))
        if phase < ngroups:
            stages += [(rwkv_token_stages(r0, r1), 8) for r0, r1 in pieces(phase)]
        if phase == 0:
            stages.append((hgrn_token_stages(), 7))
        if 0 <= phase - 3 < ngroups:
            stages += [(rwkv_output_stages(r0, r1), 3) for r0, r1 in pieces(phase - 3)]
        _interleave(*stages)
        for fn in after:
            fn()
    hgrn_output(0, nb * tb)

    @pl.when(j == nj - 1)
    def _():
        for b in seqs:
            shift_ref[b] = prev_sc[b, 0:1]
            s = srw_sc[b]
            for h in range(A_H):
                srw_ref[b, h] = s[:, h * A_HD:(h + 1) * A_HD]
            shg_ref[b] = shg_sc[b].T


def _mixer_out(bsz, t, nb, tb):
    if nb == 1 or tb == t:
        return (pl.BlockSpec((nb * tb, D_MODEL), lambda b, j: (b * (t // tb) + j, 0)),
                jax.ShapeDtypeStruct((bsz * t, D_MODEL), BF16))
    return (pl.BlockSpec((nb, tb, D_MODEL), lambda b, j: (b, j, 0)),
            jax.ShapeDtypeStruct((bsz, t, D_MODEL), BF16))


def _even_mixer(p, shift0, srw0, shg0, mu, w2p, a2p, par, nb, tb, la, lb):
    bsz, t, _ = p.shape
    kern = functools.partial(_even_kernel, nb=nb, tb=tb, la=la, lb=lb)
    nch, hl = 2 * math.gcd(nb * tb // la, 8), A_H * la
    bmap3 = lambda b, j: (b, 0, 0)
    bmap4 = lambda b, j: (b, 0, 0, 0)
    cmap = lambda b, j: (0, 0)
    blk = lambda: pltpu.VMEM((nb * tb, MIX), F32)
    o_spec, o_shape = _mixer_out(bsz, t, nb, tb)
    return pl.pallas_call(
        kern, grid=(bsz // nb, t // tb),
        in_specs=[pl.BlockSpec((nb, tb, IN_EVEN), lambda b, j: (b, j, 0)),
                  pl.BlockSpec((nb, 1, A_SHIFT_W), bmap3),
                  pl.BlockSpec((nb, A_H, A_HD, A_HD), bmap4),
                  pl.BlockSpec((nb, B_H * B_HD, B_HD), bmap3),
                  pl.BlockSpec((1, A_SHIFT_W), cmap),
                  pl.BlockSpec((2 * A_LORA, MIX), cmap),
                  pl.BlockSpec((2 * A_LORA, MIX), cmap),
                  pl.BlockSpec((16, MIX), cmap)],
        out_specs=[o_spec,
                   pl.BlockSpec((nb, 1, A_SHIFT_W), bmap3),
                   pl.BlockSpec((nb, A_H, A_HD, A_HD), bmap4),
                   pl.BlockSpec((nb, B_H * B_HD, B_HD), bmap3)],
        out_shape=[o_shape,
                   jax.ShapeDtypeStruct((bsz, 1, A_SHIFT_W), F32),
                   jax.ShapeDtypeStruct((bsz, A_H, A_HD, A_HD), F32),
                   jax.ShapeDtypeStruct((bsz, B_H * B_HD, B_HD), F32)],
        scratch_shapes=[pltpu.VMEM((nb, 8, A_SHIFT_W), F32), pltpu.VMEM((nb, A_HD, MIX), F32),
                        pltpu.VMEM((nb, B_HD, B_H * B_HD), F32)] + [blk() for _ in range(15)] + [
                            pltpu.VMEM((nch, 2 * hl, MIX), BF16), pltpu.VMEM((nch, hl, A_HD), F32),
                            pltpu.VMEM((nch, hl, A_HD), F32), pltpu.VMEM((nch, hl, hl), BF16),
                            pltpu.VMEM((nch, A_HD, MIX), F32), pltpu.VMEM((nch, hl, MIX), BF16)],
        compiler_params=pltpu.CompilerParams(dimension_semantics=("arbitrary", "arbitrary"),
                                             vmem_limit_bytes=VMEM_LIMIT),
        name="mixer_rwkv_hgrn")(p, shift0, srw0, shg0, mu, w2p, a2p, par)


_O_CQ = 0
_O_CK = _O_CQ + C_KW
_O_CV = _O_CK + C_KW
_O_GC = _O_CV + MIX
_O_DQK = _O_GC + MIX
_O_DV = _O_DQK + 2 * MIX
_O_GD = _O_DV + MIX
_O_SLAB = _O_GD + MIX
(_P_CNORM, _P_DNORM, _P_G2B) = range(3)


def _odd_kernel(oin_ref, xin_ref, woin_ref, gin_ref, win_ref,
                sgl0_ref, conv0_ref, mc0_ref, mn0_ref, mm0_ref, g2_ref, cw_ref, cb_ref, sb_ref, par_ref,
                wo_ref, gf_ref,
                y_ref, sgl_ref, conv_ref, mc_ref, mn_ref, mm_ref,
                prev_sc, sgl_sc, mc_sc, mn_sc, mm_sc, cq_sc, ck_sc, ce_sc, cl_sc, oc_sc,
                dq_sc, dk_sc, li_sc, fa_sc, fl_sc, od_sc, x_sc, p_sc, *, nb, tb, lc):
    j = pl.program_id(1)
    nj = pl.num_programs(1)
    hl = D_H * lc
    seqs = range(nb)
    head_rows = lambda x: jnp.concatenate(
        [jnp.broadcast_to(x[h:h + 1], (lc, x.shape[1])) for h in range(D_H)], axis=0)

    @pl.when(j == 0)
    def _():
        for b in seqs:
            prev_sc[b] = conv0_ref[b]
            sgl_sc[b] = sgl0_ref[b].T
            mc_sc[b] = mc0_ref[b].T
            mn_sc[b] = head_rows(mn0_ref[b])
            mm_sc[b] = head_rows(mm0_ref[b])[:, :1]

    par = par_ref[...]
    row = lambda i: par[i:i + 1]

    def rows_of(lo, hi, r0, r1):
        return p_sc[r0:r1, lo:hi]

    proj_slabs = [(c, min(c + 256, IN_ODD_PAD)) for c in range(0, IN_ODD_PAD, 256)]

    def projection_stages(r0, r1):
        x1 = xin_ref[r0:r1, :] + _dot(oin_ref[r0:r1, :], woin_ref[...])
        x_sc[r0:r1, :] = x1
        h = _rms(x1, gin_ref[...]).astype(BF16)
        for lo, hi in proj_slabs:
            yield
            p_sc[r0:r1, lo:hi] = _dot(h, win_ref[:, lo:hi])

    def token_stages(r0, r1):
        n = r1 - r0
        span = tb if n % tb == 0 else n
        cum_c, ones_c = _chunk_mats(n, lc)
        slab = rows_of(_O_SLAB, _O_SLAB + SLAB, r0, r1)
        pre = _dot(slab.astype(BF16), g2_ref[...]) + row(_P_G2B)[:, :C_KW]
        logg = _log_sigmoid(pre) * (1.0 / C_GATE_NORM)
        yield
        gb = _mm(cum_c, logg, "nn", M_EXACT_R)
        gl = _mm(ones_c, logg, "nn", M_EXACT_R)
        yield
        ck = rows_of(_O_CK, _O_CK + C_KW, r0, r1)
        cq_sc[r0:r1, :] = rows_of(_O_CQ, _O_CQ + C_KW, r0, r1) * (C_KD ** -0.5) * jnp.exp(gb)
        ck_sc[r0:r1, :] = ck * jnp.exp(-gb)
        ce_sc[r0:r1, :] = ck * jnp.exp(gl - gb)
        cl_sc[r0:r1, :] = gl
        yield
        x = rows_of(_O_DQK, _O_DQK + 2 * MIX, r0, r1)
        cw = cw_ref[...]
        conv = cb_ref[...] + x * cw[D_CONV - 1:D_CONV]
        head_row = _iota((8, 2 * MIX), 0)
        for s in range(1, D_CONV):
            xr = pltpu.roll(x, s, axis=0)
            parts = []
            for a in range(0, n, span):
                kept = pltpu.roll(prev_sc[(r0 + a) // tb], s, axis=0)
                parts.append(jnp.where(head_row < s, kept, xr[a:a + 8]))
                if span > 8:
                    parts.append(xr[a + 8:a + span])
            xs = parts[0] if len(parts) == 1 else jnp.concatenate(parts, axis=0)
            conv = conv + xs * cw[D_CONV - 1 - s:D_CONV - s]
        for a in range(0, n, span):
            prev_sc[(r0 + a) // tb] = x[a + span - 8:a + span]
        yield
        conv = _silu(conv)
        dq_sc[r0:r1, :] = conv[:, :MIX]
        dk_sc[r0:r1, :] = conv[:, MIX:] * (D_HD ** -0.5)
        yield
        gates = slab + sb_ref[...]
        lf = pltpu.roll(_log_sigmoid(gates), SLAB - (SLAB_F - SLAB_I), axis=1)
        li_sc[r0:r1, :] = gates
        fa_sc[r0:r1, :] = _mm(cum_c, lf, "nn", M_EXACT_R)
        fl_sc[r0:r1, :] = _mm(ones_c, lf, "nn", M_EXACT_R)

    wide_c, _, incl_c, _ = _head_masks(C_H, lc, C_KD)
    consts_c = (wide_c, incl_c)
    lane = _iota((hl, SLAB), 1)
    head = _iota((hl, SLAB), 0) // lc
    pick = lambda z: jnp.sum(jnp.where(lane == head + SLAB_I, _tile_rows(z, D_H), 0.0), -1, keepdims=True)

    wide_d, same_d, incl_d, _ = _head_masks(D_H, lc, D_HD)
    consts_d = (wide_d, same_d, incl_d, same_d.astype(BF16))

    per_seq = tb // lc
    if per_seq == 1:
        chains, run = math.gcd(nb, 4), 1
        trips = nb // chains
        first_chunk = lambda gi, q: gi * chains + q
    else:
        chains, run = nb, math.gcd(per_seq, 8)
        trips = per_seq // run
        first_chunk = lambda gi, q: q * per_seq + gi * run
    assert per_seq % run == 0

    def chunk_stages(gi):
        stages, finish = [], []
        for q in range(chains):
            first = first_chunk(gi, q)
            seq = first // per_seq
            out = {}
            rows = lambda u, first=first: _rows_at(((first + u) * lc, lc), lc)
            in_seq = lambda u, first=first, seq=seq: _rows_at(((first + u) * lc - seq * tb, lc), lc)

            def load_c(u, first=first, seq=seq, rows=rows, in_seq=in_seq):
                sl = rows(u)
                return (cq_sc[sl, :], ck_sc[sl, :], ce_sc[sl, :], cl_sc[pl.ds((first + u) * lc, 1), :],
                        p_sc[sl, _O_CV:_O_CV + MIX])

            def store_c(u, o, rows=rows):
                oc_sc[rows(u), :] = o

            def load_d(u, seq=seq, rows=rows, in_seq=in_seq):
                sl = rows(u)
                return (dq_sc[sl, :], dk_sc[sl, :], p_sc[sl, _O_DV:_O_DV + MIX],
                        pick(li_sc[sl, :]), pick(fa_sc[sl, :]), pick(fl_sc[sl, :]))

            def store_d(u, o, rows=rows):
                od_sc[rows(u), :] = o

            stages.append((_mlstm_stages(run, load_d, store_d, (mc_sc[seq], mn_sc[seq], mm_sc[seq]), consts_d, out),
                           3 * run + 1))
            stages.append((_gla_stages(run, load_c, store_c, sgl_sc[seq], consts_c, C_H, C_VD, out, "gla"),
                           2 * run + 1))
            finish.append((seq, out))

        def write_back():
            for seq, out in finish:
                sgl_sc[seq] = out["gla"]
                mc_sc[seq], mn_sc[seq], mm_sc[seq] = out["mlstm"]

        return stages, write_back

    def output_stages(r0, r1):
        oc = _head_rms(oc_sc[r0:r1, :], row(_P_CNORM), C_VD) * _silu(rows_of(_O_GC, _O_GC + MIX, r0, r1))
        yield
        od = _head_rms(od_sc[r0:r1, :], row(_P_DNORM), D_HD) * _silu(rows_of(_O_GD, _O_GD + MIX, r0, r1))
        o = jnp.concatenate([oc, od], axis=1).astype(BF16)
        yield
        x2 = x_sc[r0:r1, :] + _dot(o, wo_ref[...])
        y_ref[r0:r1, :] = _rms(x2, gf_ref[...])

    if per_seq == 1 or nb > 1:
        _interleave((projection_stages(0, nb * tb), len(proj_slabs) + 1))
        _interleave((token_stages(0, nb * tb), 6))

        def trip(gi, carry):
            stages, write_back = chunk_stages(gi)
            _interleave(*stages)
            write_back()
            return carry

        lax.fori_loop(0, trips, trip, 0)
        _interleave((output_stages(0, nb * tb), 3))
    else:
        grows = run * lc
        for phase in range(trips + 3):
            stages, after = [], []
            if 0 <= phase - 2 < trips:
                chunk, write_back = chunk_stages(phase - 2)
                stages += chunk
                after.append(write_back)
            if phase < trips:
                stages.append((projection_stages(phase * grows, (phase + 1) * grows), len(proj_slabs) + 1))
            if 0 <= phase - 1 < trips:
                stages.append((token_stages((phase - 1) * grows, phase * grows), 6))
            if 0 <= phase - 3 < trips:
                stages.append((output_stages((phase - 3) * grows, (phase - 2) * grows), 3))
            _interleave(*stages)
            for fn in after:
                fn()

    @pl.when(j == nj - 1)
    def _():
        for b in seqs:
            sgl_ref[b] = sgl_sc[b].T
            conv_ref[b] = prev_sc[b]
            mc_ref[b] = mc_sc[b].T
            mn_ref[b] = jnp.concatenate([mn_sc[b, h * lc:h * lc + 1] for h in range(D_H)], axis=0)
            mm_ref[b] = jnp.concatenate(
                [jnp.broadcast_to(mm_sc[b, h * lc:h * lc + 1], (1, LANE)) for h in range(D_H)], axis=0)


def _odd_layer(o_in, x, wo_in, g_in, w_in, sgl0, conv0, mc0, mn0, mm0, g2p, cw, cb, sb, par, wo, gf, bsz, t, nb, tb,
               lc):
    hl = D_H * lc
    once = lambda shape: pl.BlockSpec(shape, lambda b, j: (0, 0), pipeline_mode=pl.Buffered(1))
    kern = functools.partial(_odd_kernel, nb=nb, tb=tb, lc=lc)
    bmap3 = lambda b, j: (b, 0, 0)
    cmap = lambda b, j: (0, 0)
    blk = lambda w: pltpu.VMEM((nb * tb, w), F32)
    assert nb == 1 or tb == t
    rows = pl.BlockSpec((nb * tb, D_MODEL), lambda b, j: (b * (t // tb) + j, 0))
    return pl.pallas_call(
        kern, grid=(bsz // nb, t // tb),
        in_specs=[rows, rows, once((D_MODEL, D_MODEL)), pl.BlockSpec((1, D_MODEL), cmap),
                  once((D_MODEL, IN_ODD_PAD)),
                  pl.BlockSpec((nb, C_KW, C_VD), bmap3),
                  pl.BlockSpec((nb, 8, 2 * MIX), bmap3),
                  pl.BlockSpec((nb, MIX, D_HD), bmap3),
                  pl.BlockSpec((nb, D_H, D_HD), bmap3),
                  pl.BlockSpec((nb, D_H, LANE), bmap3),
                  pl.BlockSpec((SLAB, C_KW), cmap),
                  pl.BlockSpec((D_CONV, 2 * MIX), cmap),
                  pl.BlockSpec((1, 2 * MIX), cmap),
                  pl.BlockSpec((1, SLAB), cmap),
                  pl.BlockSpec((8, MIX), cmap),
                  once((D_MODEL, D_MODEL)),
                  pl.BlockSpec((1, D_MODEL), cmap)],
        out_specs=[rows,
                   pl.BlockSpec((nb, C_KW, C_VD), bmap3),
                   pl.BlockSpec((nb, 8, 2 * MIX), bmap3),
                   pl.BlockSpec((nb, MIX, D_HD), bmap3),
                   pl.BlockSpec((nb, D_H, D_HD), bmap3),
                   pl.BlockSpec((nb, D_H, LANE), bmap3)],
        out_shape=[jax.ShapeDtypeStruct((bsz * t, D_MODEL), F32),
                   jax.ShapeDtypeStruct((bsz, C_KW, C_VD), F32),
                   jax.ShapeDtypeStruct((bsz, 8, 2 * MIX), F32),
                   jax.ShapeDtypeStruct((bsz, MIX, D_HD), F32),
                   jax.ShapeDtypeStruct((bsz, D_H, D_HD), F32),
                   jax.ShapeDtypeStruct((bsz, D_H, LANE), F32)],
        scratch_shapes=[pltpu.VMEM((nb, 8, 2 * MIX), F32), pltpu.VMEM((nb, C_VD, C_KW), F32),
                        pltpu.VMEM((nb, D_HD, MIX), F32), pltpu.VMEM((nb, hl, D_HD), F32),
                        pltpu.VMEM((nb, hl, 1), F32),
                        blk(C_KW), blk(C_KW), blk(C_KW), blk(C_KW), blk(MIX),
                        blk(MIX), blk(MIX), blk(SLAB), blk(SLAB), blk(SLAB), blk(MIX),
                        blk(D_MODEL), blk(IN_ODD_PAD)],
        compiler_params=pltpu.CompilerParams(dimension_semantics=("arbitrary", "arbitrary"),
                                             vmem_limit_bytes=VMEM_LIMIT),
        name="layer_gla_mlstm")(o_in, x, wo_in, g_in, w_in, sgl0, conv0, mc0, mn0, mm0, g2p, cw, cb, sb, par, wo, gf)


MIXER_ROWS = 256
SHORT_SEQS = 8


def _mixer_blocking(bsz, t):
    tb = math.gcd(t, MIXER_ROWS)
    nb = math.gcd(bsz, SHORT_SEQS) if tb == t and t * SHORT_SEQS <= MIXER_ROWS else 1
    return nb, tb


def _paired_blocking(bsz, nb, tb):
    if nb == 1 and bsz % 2 == 0 and (tb // 2) % 32 == 0:
        return 2, tb
    return nb, tb


def _run(x, shift, s_rwkv, s_hgrn, s_gla, conv, mc, mn, mm, wts):
    bsz, t, d = x.shape
    nb, tb = _mixer_blocking(bsz, t)
    la = math.gcd(t, 16)
    lbc = math.gcd(t, 32)
    x2 = x.reshape(bsz * t, d)
    p0 = _inproj(x2, wts["g0"], wts["w_in0"])
    o0, shift_n, srw_n, shg_n = _even_mixer(
        p0.reshape(bsz, t, IN_EVEN), shift[0][:, None], s_rwkv[0], s_hgrn[0].reshape(bsz, B_H * B_HD, B_HD),
        wts["mu"], wts["w2p"], wts["a2p"], wts["par_e"], *_paired_blocking(bsz, nb, tb), la, lbc)
    conv8 = jnp.pad(conv[0], ((0, 0), (8 - (D_CONV - 1), 0), (0, 0)))
    mm_l = jnp.broadcast_to(mm[0][:, :, None], (bsz, D_H, LANE))
    y, sgl_n, conv_n, mc_n, mn_n, mm_n = _odd_layer(
        o0.reshape(bsz * t, d), x2, wts["w_out0"], wts["g1"], wts["w_in1"],
        s_gla[0].reshape(bsz, C_KW, C_VD), conv8, mc[0].reshape(bsz, MIX, D_HD), mn[0], mm_l,
        wts["g2p"], wts["cw"], wts["cb"], wts["sb"], wts["par_o"], wts["w_out1"], wts["gf"], bsz, t, nb,
        math.gcd(t, 2 * tb) if nb == 1 else tb, lbc)
    return (y.reshape(bsz, t, d), shift_n.reshape(1, bsz, A_SHIFT_W), srw_n[None],
            shg_n.reshape(1, bsz, B_H, B_HD, B_HD), sgl_n.reshape(1, bsz, C_H, C_KD, C_VD),
            conv_n[None, :, 8 - (D_CONV - 1):], mc_n.reshape(1, bsz, D_H, D_HD, D_HD), mn_n[None],
            mm_n[None, :, :, 0])


def _odd_in_projection(w):
    sizes = (C_KW, C_KW, MIX, C_LORA, MIX, 2 * MIX, MIX, D_H, D_H, MIX)
    off = np.cumsum((0,) + sizes)
    cq, ck, cv, cg, gc, dqk, dv, di, df, gd = (w[:, off[i]:off[i + 1]] for i in range(len(sizes)))
    pad = jnp.zeros((w.shape[0], SLAB - C_LORA - 2 * D_H), w.dtype)
    return jnp.concatenate([cq, ck, cv, gc, dqk, dv, gd, cg, di, df, pad], axis=1)


def kernel(x_prompt, x_sample, state_shift_a, state_rwkv, state_hgrn, state_gla, state_conv_d, state_mlstm_c,
           state_mlstm_n, state_mlstm_m, norm_g, w_in_even, w_out_even, a_mu, a_w0, a_w2, a_a0, a_a2, a_kk,
           a_ka, a_rk, a_ln_w, a_ln_b, b_lb, b_norm, w_in_odd, w_out_odd, c_g2, c_g2b, c_norm, d_conv_w,
           d_conv_b, d_ib, d_fb, d_norm, final_norm):
    assert w_in_even.shape[0] == 1 and w_in_odd.shape[0] == 1 and b_lb.shape[0] == 2
    zpad = lambda a, rows_before, rows_total: jnp.pad(a, ((rows_before, rows_total - rows_before - a.shape[0]), (0, 0)))
    par_e = jnp.concatenate([a_w0, a_a0, a_kk, a_ka, a_rk, a_ln_w, a_ln_b, b_norm, b_lb[0:1], b_lb[1:2]], axis=0)
    g2b = jnp.pad(c_g2b, ((0, 0), (0, MIX - C_KW)))
    par_o = jnp.concatenate([c_norm, d_norm, g2b], axis=0)
    w_in1 = _odd_in_projection(w_in_odd[0].astype(BF16))
    sb = jnp.pad(jnp.concatenate([d_ib, d_fb], axis=1), ((0, 0), (SLAB_I, SLAB - SLAB_I - 2 * D_H)))
    wts = {
        "g0": norm_g[0:1], "g1": norm_g[1:2], "gf": final_norm[None],
        "w_in0": w_in_even[0].astype(BF16), "w_out0": w_out_even[0].astype(BF16),
        "w_in1": w_in1, "w_out1": w_out_odd[0].astype(BF16),
        "mu": a_mu, "w2p": zpad(a_w2[0], 0, 2 * A_LORA).astype(BF16),
        "a2p": zpad(a_a2[0], A_LORA, 2 * A_LORA).astype(BF16),
        "par_e": zpad(par_e, 0, 16), "par_o": zpad(par_o, 0, 8),
        "g2p": zpad(c_g2[0], 0, SLAB).astype(BF16), "cw": d_conv_w[0], "cb": d_conv_b, "sb": sb,
    }
    bp = x_prompt.shape[0]
    z = lambda *s: jnp.zeros(s, x_prompt.dtype)
    prompt = _run(x_prompt, z(1, bp, A_SHIFT_W), z(1, bp, A_H, A_HD, A_HD), z(1, bp, B_H, B_HD, B_HD),
                  z(1, bp, C_H, C_KD, C_VD), z(1, bp, D_CONV - 1, 2 * MIX), z(1, bp, D_H, D_HD, D_HD),
                  z(1, bp, D_H, D_HD), z(1, bp, D_H), wts)
    sample = _run(x_sample, state_shift_a, state_rwkv, state_hgrn, state_gla, state_conv_d, state_mlstm_c,
                  state_mlstm_n, state_mlstm_m, wts)
    return (prompt[0], sample[0]) + prompt[1:] + sample[1:]
```

```python
import functools
import math

import jax
import jax.numpy as jnp
import numpy as np
from jax import lax
from jax.experimental import pallas as pl
from jax.experimental.pallas import tpu as pltpu

F32 = jnp.float32
BF16 = jnp.bfloat16

D_MODEL = 1024
MIX = D_MODEL // 2
NORM_EPS = 1e-5
A_HD = 64
A_H = MIX // A_HD
A_LORA = 64
A_GN_EPS = 64e-5
A_SHIFT_W = 3 * MIX + 2 * A_LORA
B_HD = 128
B_H = MIX // B_HD
IN_EVEN = A_SHIFT_W + 5 * MIX
C_H = 4
C_VD = MIX // C_H
C_KD = C_VD // 2
C_KW = C_H * C_KD
C_LORA = 16
C_GATE_NORM = 16.0
D_H = 4
D_HD = MIX // D_H
D_CONV = 4
LANE = 128
SLAB = LANE
SLAB_I = C_LORA
SLAB_F = C_LORA + D_H
IN_ODD_PAD = 2 * C_KW + 2 * MIX + 2 * MIX + 2 * MIX + SLAB
VMEM_LIMIT = 48 * 1024 * 1024


_DIMS = {"nn": ((1,), (0,)), "nt": ((1,), (1,)), "tn": ((0,), (0,))}
M_EXACT_R = "rx"
M_EXACT_L = "lx"
M_G = "bf"
M_TINV = "bf"
M_ST = "bf"
M_ATT = "bf"


def _split2(x):
    hi = x.astype(BF16)
    return hi, (x - hi.astype(F32)).astype(BF16)


def _mm(a, b, form, mode):
    dn = (_DIMS[form], ((), ()))
    d = lambda x, y: lax.dot_general(x, y, dn, preferred_element_type=F32)
    if mode == "bf":
        return d(a.astype(BF16), b.astype(BF16))
    if mode == "lx":
        bb = b.astype(BF16)
        h, l = _split2(a)
        return d(h, bb) + d(l, bb)
    assert mode == "rx"
    ab = a.astype(BF16)
    h, l = _split2(b)
    return d(ab, h) + d(ab, l)


def _dot(a, b):
    return jnp.dot(a, b, preferred_element_type=F32)


def _iota(shape, dim):
    return lax.broadcasted_iota(jnp.int32, shape, dim)


def _sigmoid(x):
    return 0.5 + 0.5 * jnp.tanh(0.5 * x)


def _log_sigmoid(x):
    return jnp.minimum(x, 0.0) - jnp.log(1.0 + jnp.exp(-jnp.abs(x)))


def _silu(x):
    return x * _sigmoid(x)


def _rms(x, g):
    return x * lax.rsqrt(jnp.mean(x * x, -1, keepdims=True) + NORM_EPS) * g


def _rows_at(start_and_alignment, size):
    start, alignment = start_and_alignment
    if not isinstance(start, int):
        start = pl.multiple_of(start, alignment)
    return pl.ds(start, size)


def _tile_rows(x, n):
    return jnp.concatenate([x] * n, axis=0)


def _stack_heads(x, n_heads, width):
    return jnp.concatenate([x[:, h * width:(h + 1) * width] for h in range(n_heads)], axis=0)


def _unstack_heads(x, n_heads, rows):
    return jnp.concatenate([x[h * rows:(h + 1) * rows] for h in range(n_heads)], axis=1)


def _head_masks(n_heads, rows, width):
    hl = n_heads * rows
    rh = _iota((hl, n_heads * width), 0) // rows
    lh = _iota((hl, n_heads * width), 1) // width
    wide = (rh == lh).astype(F32)
    ri, ci = _iota((hl, hl), 0), _iota((hl, hl), 1)
    same = (ri // rows) == (ci // rows)
    incl = same & (ci <= ri)
    strict = same & (ci < ri)
    return wide, same, incl, strict


def _chunk_mats(n, chunk):
    ri, ci = _iota((n, n), 0), _iota((n, n), 1)
    same = (ri // chunk) == (ci // chunk)
    return (same & (ci <= ri)).astype(BF16), same.astype(BF16)


def _seg_sum(x, seg):
    bd = ((_iota((LANE, LANE), 0) // seg) == (_iota((LANE, LANE), 1) // seg)).astype(BF16)
    parts = [_mm(x[:, j:j + LANE], bd, "nn", M_EXACT_L) for j in range(0, x.shape[1], LANE)]
    return jnp.concatenate(parts, axis=1)


def _head_rms(x, g, width):
    parts = []
    for j in range(0, x.shape[1], width):
        xs = x[:, j:j + width]
        parts.append(xs * lax.rsqrt(jnp.mean(xs * xs, -1, keepdims=True) + NORM_EPS))
    return jnp.concatenate(parts, axis=1) * g


def _inproj_kernel(x_ref, g_ref, w_ref, p_ref):
    h = _rms(x_ref[...], g_ref[...])
    p_ref[...] = _dot(h.astype(BF16), w_ref[...])


def _row_tile(m):
    return math.gcd(m, 512)


def _full(shape):
    return pl.BlockSpec(shape, lambda i: (0,) * len(shape))


def _dense_params():
    return pltpu.CompilerParams(dimension_semantics=("arbitrary",), vmem_limit_bytes=VMEM_LIMIT)


def _inproj(x, g, w):
    m, d = x.shape
    n = w.shape[1]
    tm = _row_tile(m)
    return pl.pallas_call(
        _inproj_kernel, grid=(m // tm,),
        in_specs=[pl.BlockSpec((tm, d), lambda i: (i, 0)), _full((1, d)), _full((d, n))],
        out_specs=pl.BlockSpec((tm, n), lambda i: (i, 0)),
        out_shape=jax.ShapeDtypeStruct((m, n), F32),
        compiler_params=_dense_params(), name="inproj")(x, g, w)


def _interleave(*staged):
    live = [[gen, 0, max(n, 1)] for gen, n in staged]
    while live:
        item = min(live, key=lambda it: it[1] / it[2])
        try:
            next(item[0])
            item[1] += 1
        except StopIteration:
            live.remove(item)


def _rwkv_factor_stages(chunks, consts, store):
    wide, incl, strict, eye, halves = consts
    L = chunks[0][0].shape[0]
    hl = A_H * L
    n = range(len(chunks))
    tile = functools.partial(_tile_rows, n=A_H)
    for i in n:
        store(i, "r_w", (tile(chunks[i][1]) * wide).astype(BF16))
        store(i, "be_w", (tile(chunks[i][5]) * wide).astype(BF16))
    stack = lambda x: _stack_heads(x, A_H, A_HD)
    g = [_mm(jnp.concatenate([stack(c[0]), stack(c[1])], axis=0),
             jnp.concatenate([stack(c[2]), stack(c[3])], axis=0), "nt", M_G) for c in chunks]
    yield
    a_kk = [jnp.where(strict, x[:hl, :hl], 0.0) for x in g]
    a_kb = [jnp.where(strict, x[:hl, hl:], 0.0) for x in g]
    a_rk = [jnp.where(incl, x[hl:, :hl], 0.0) for x in g]
    for i in n:
        store(i, "a_rb", jnp.where(incl, g[i][hl:, hl:], 0.0).astype(BF16))
    t = [eye - jnp.where(halves[0], x, 0.0) for x in a_kb]
    v_st = [_stack_heads(c[6], A_H, A_HD) for c in chunks]
    av = [_mm(a_kk[i], v_st[i], "nn", M_ST) for i in n]
    for i in n:
        store(i, "o2", _mm(a_rk[i], v_st[i], "nn", M_ST))
        store(i, "kv", _mm(v_st[i], tile(chunks[i][4]) * wide, "tn", M_ST))
    for half in halves[1:]:
        ta = [_mm(t[i], jnp.where(half, a_kb[i], 0.0), "nn", M_TINV) for i in n]
        yield
        t = [t[i] - _mm(ta[i], t[i], "nn", M_TINV) for i in n]
        yield
    for i in n:
        store(i, "m1n", (-_mm(t[i], tile(chunks[i][0]) * wide, "nn", M_ST)).astype(BF16))
        store(i, "m2n", -_mm(t[i], av[i], "nn", M_ST))
    yield


def _rwkv_state_stages(n_chunks, load_factors, store_o, s, out):
    for c in range(n_chunks):
        xm, m2n, o2, a_rb, kv, be_w, ecl = load_factors(c)
        hl = xm.shape[0] // 2
        xs = _mm(xm, s, "nt", M_ST)
        yield
        u = xs[:hl] + m2n
        s = s * ecl + kv + _mm(u, be_w, "tn", M_ST)
        o_st = xs[hl:] + o2 + _mm(a_rb, u, "nn", M_ST)
        yield
        store_o(c, _unstack_heads(o_st, A_H, hl // A_H))
    out["rwkv"] = s


def _gla_stages(n_chunks, load_chunk, store_o, s, consts, n_heads, vd, out, key):
    wide, incl = consts
    tile = functools.partial(_tile_rows, n=n_heads)
    for c in range(n_chunks):
        qd, ki, kend, bl, v = load_chunk(c)
        q_w = tile(qd) * wide
        v_st = _stack_heads(v, n_heads, vd)
        att = jnp.where(incl, _mm(q_w, tile(ki), "nt", M_ATT), 0.0)
        kv = _mm(v_st, tile(kend) * wide, "tn", M_ATT)
        yield
        o_st = _mm(att, v_st, "nn", M_ATT) + _mm(q_w, s, "nt", M_ATT)
        s = s * jnp.exp(bl) + kv
        yield
        store_o(c, _unstack_heads(o_st, n_heads, qd.shape[0]))
    out[key] = s


def _mlstm_stages(n_chunks, load_chunk, store_o, state, consts, out):
    wide, same, incl, ones_bd = consts
    c, nrow, m = state
    tile = functools.partial(_tile_rows, n=D_H)
    for ch in range(n_chunks):
        q, k, v, li, a, a_last = load_chunk(ch)
        L = q.shape[0]
        hl = D_H * L
        to_row = lambda col: jnp.broadcast_to(col, (hl, LANE)).T[:hl]
        q_st = _stack_heads(q, D_H, D_HD)
        k_st = _stack_heads(k, D_H, D_HD)
        v_st = _stack_heads(v, D_H, D_HD)
        qk = _mm(q_st, k_st, "nt", M_ATT)
        dlog = jnp.where(incl, a - to_row(a) + to_row(li), -jnp.inf)
        m_intra = jnp.max(dlog, -1, keepdims=True)
        e_end = a_last - a + li
        m_loc = jnp.max(jnp.where(same, to_row(e_end), -jnp.inf), -1, keepdims=True)
        yield
        s_intra = jnp.exp(dlog - m_intra) * qk
        num_intra = _mm(s_intra, v_st, "nn", M_ATT)
        den_intra = jnp.sum(s_intra, -1, keepdims=True)
        w_end = jnp.exp(e_end - m_loc)
        c_loc = _mm(k_st, tile(v) * wide * w_end, "tn", M_ATT)
        n_loc = _mm(ones_bd, w_end * k_st, "nn", M_ATT)
        yield
        g = a + m
        m_t = jnp.maximum(g, m_intra)
        f_inter = jnp.exp(g - m_t)
        f_intra = jnp.exp(m_intra - m_t)
        qc = _mm(q_st, c, "nn", M_ATT)
        num_inter = jnp.concatenate(
            [qc[h * L:(h + 1) * L, h * D_HD:(h + 1) * D_HD] for h in range(D_H)], axis=0)
        den_inter = jnp.sum(q_st * nrow, -1, keepdims=True)
        num = num_intra * f_intra + num_inter * f_inter
        den = f_intra * den_intra + f_inter * den_inter
        h_st = num / jnp.maximum(jnp.abs(den), jnp.exp(-m_t))
        m_new = jnp.maximum(a_last + m, m_loc)
        fo = jnp.exp(a_last + m - m_new)
        fl = jnp.exp(m_loc - m_new)
        per_value = lambda col: jnp.concatenate(
            [jnp.broadcast_to(col[h * L:h * L + 1], (1, D_HD)) for h in range(D_H)], axis=1)
        c = per_value(fo) * c + per_value(fl) * c_loc
        nrow = fo * nrow + fl * n_loc
        m = m_new
        yield
        store_o(ch, _unstack_heads(h_st, D_H, L))
    out["mlstm"] = (c, nrow, m)


(_E_W0, _E_A0, _E_KK, _E_KA, _E_RK, _E_LNW, _E_LNB, _E_BNORM, _E_LB0, _E_LB1) = range(10)


def _even_kernel(p_ref, shift0_ref, srw0_ref, shg0_ref, mu_ref, w2_ref, a2_ref, par_ref,
                 o_ref, shift_ref, srw_ref, shg_ref,
                 prev_sc, srw_sc, shg_sc, kkt_sc, rt_sc, kh_sc, bh_sc, ke_sc, be_sc, v_sc, epos_sc, bonus_sc, oa_sc,
                 qd_sc, ki_sc, kend_sc, bl_sc, ob_sc, xm_sc, m2_sc, o2_sc, arb_sc, kv_sc, bew_sc, *, nb, tb, la, lb):
    j = pl.program_id(1)
    nj = pl.num_programs(1)
    seqs = range(nb)
    cols = lambda lo, hi: jnp.concatenate([p_ref[b, :, lo:hi] for b in seqs], axis=0)

    @pl.when(j == 0)
    def _():
        for b in seqs:
            prev_sc[b] = jnp.broadcast_to(shift0_ref[b], prev_sc.shape[1:])
            srw_sc[b] = jnp.concatenate([srw0_ref[b, h] for h in range(A_H)], axis=1)
            shg_sc[b] = shg0_ref[b].T

    par = par_ref[...]
    row = lambda i: par[i:i + 1]

    per_seq = tb // la
    group = math.gcd(nb * per_seq, 8)
    chains = nb
    run = group // chains
    ngroups = per_seq // run
    prows = run * la
    assert group % chains == 0 and per_seq % run == 0 and prows % lb == 0
    hper = prows // lb
    late_hgrn = ngroups > 1

    def pieces(g):
        if ngroups == 1:
            return [(0, nb * tb)]
        return [(q * tb + g * prows, q * tb + (g + 1) * prows) for q in seqs]

    def rows_of(lo, hi, r0, r1):
        parts = [p_ref[b, max(r0, b * tb) - b * tb:min(r1, (b + 1) * tb) - b * tb, lo:hi]
                 for b in seqs if max(r0, b * tb) < min(r1, (b + 1) * tb)]
        return parts[0] if len(parts) == 1 else jnp.concatenate(parts, axis=0)

    def rwkv_token_stages(r0, r1):
        grows = r1 - r0
        pa = rows_of(0, A_SHIFT_W, r0, r1)
        rolled = pltpu.roll(pa, 1, axis=0)
        starts = range(r0, r1, tb) if grows % tb == 0 else (r0,)
        span = tb if grows % tb == 0 else grows
        first = _iota(pa.shape, 0) % span == 0
        before = [jnp.broadcast_to(prev_sc[s // tb, 0:1], (span, A_SHIFT_W)) for s in starts]
        shifted = jnp.where(first, before[0] if len(before) == 1 else jnp.concatenate(before, axis=0), rolled)
        for s in starts:
            prev_sc[s // tb] = jnp.broadcast_to(pa[s - r0 + span - 1:s - r0 + span], prev_sc.shape[1:])
        xm = pa + (shifted - pa) * mu_ref[...]
        r = xm[:, :MIX]
        k = xm[:, MIX:2 * MIX]
        v = xm[:, 2 * MIX:3 * MIX]
        lora_in = xm[:, 3 * MIX:]
        yield
        lw = _sigmoid(row(_E_W0) + _dot(jnp.tanh(lora_in).astype(BF16), w2_ref[...])) * (-math.exp(-0.5))
        a = _sigmoid(row(_E_A0) + _dot(lora_in.astype(BF16), a2_ref[...]))
        yield
        kk = k * row(_E_KK)
        kk = kk * lax.rsqrt(jnp.maximum(_seg_sum(kk * kk, A_HD), 1e-12))
        yield
        k = k * (1.0 + (a - 1.0) * row(_E_KA))
        bv = kk * a
        cum_a, ones_a = _chunk_mats(grows, la)
        c = _mm(cum_a, lw, "nn", M_EXACT_R)
        yield
        cl = _mm(ones_a, lw, "nn", M_EXACT_R)
        yield
        e_pos = jnp.exp(c)
        e_neg = jnp.exp(-c)
        e_end = jnp.exp(cl - c)
        kkt_sc[r0:r1, :] = kk * jnp.exp(c - lw)
        rt_sc[r0:r1, :] = r * e_pos
        yield
        kh_sc[r0:r1, :] = k * e_neg
        bh_sc[r0:r1, :] = bv * e_neg
        ke_sc[r0:r1, :] = k * e_end
        be_sc[r0:r1, :] = bv * e_end
        yield
        v_sc[r0:r1, :] = v
        epos_sc[r0:r1, :] = e_pos
        bonus_sc[r0:r1, :] = _seg_sum(r * k * row(_E_RK), A_HD) * v

    wide, _, incl, strict = _head_masks(A_H, la, A_HD)
    hl = A_H * la
    ri, ci_ = _iota((hl, hl), 0), _iota((hl, hl), 1)
    eye = (ri == ci_).astype(F32)
    halves = []
    m = 1
    while m < la:
        halves.append(((ri // (2 * m)) == (ci_ // (2 * m))) & ((ri // m) != (ci_ // m)))
        m *= 2
    consts_a = (wide, incl, strict, eye, halves)

    base = A_SHIFT_W + MIX

    def hgrn_token_stages():
        e0 = row(_E_LB0)
        e1 = row(_E_LB1)
        emax = jnp.maximum(e0, e1)
        e0 = jnp.exp(e0 - emax)
        lower = e0 / (e0 + jnp.exp(e1 - emax))
        g = lower + (1.0 - lower) * jax.nn.sigmoid(cols(base + MIX, base + 2 * MIX))
        yield
        logg = jnp.log(g)
        cum_b, ones_b = _chunk_mats(nb * tb, lb)
        yield
        gb = _mm(cum_b, logg, "nn", M_EXACT_R)
        yield
        gl = _mm(ones_b, logg, "nn", M_EXACT_R)
        yield
        qd_sc[...] = _silu(cols(base, base + MIX)) * (B_HD ** -0.5) * jnp.exp(gb)
        yield
        ki_sc[...] = (1.0 - g) * jnp.exp(-gb)
        yield
        kend_sc[...] = (1.0 - g) * jnp.exp(gl - gb)
        bl_sc[...] = gl

    wide_b, _, incl_b, _ = _head_masks(B_H, lb, B_HD)
    consts_b = (wide_b, incl_b)

    def factor_stages(gi, slot):
        sls = [_rows_at(((q * per_seq + gi * run + u) * la, la), la) for q in seqs for u in range(run)]
        hl_a = A_H * la

        def store(u, name, value):
            if name == "m1n":
                xm_sc[slot + u, :hl_a, :] = value
            elif name == "r_w":
                xm_sc[slot + u, hl_a:, :] = value
            else:
                {"m2n": m2_sc, "o2": o2_sc, "a_rb": arb_sc, "kv": kv_sc, "be_w": bew_sc}[name][slot + u] = value

        yield from _rwkv_factor_stages(
            [(kkt_sc[sl, :], rt_sc[sl, :], kh_sc[sl, :], bh_sc[sl, :], ke_sc[sl, :], be_sc[sl, :], v_sc[sl, :])
             for sl in sls], consts_a, store)

    def state_stages(gi):
        slot = (gi % 2) * group
        stages, finish = [], []
        for q in range(chains):
            seq = q
            first = seq * per_seq + gi * run
            row0 = first * la
            out = {}

            def load_factors(u, q=q, first=first):
                i = slot + q * run + u
                return (xm_sc[i], m2_sc[i], o2_sc[i], arb_sc[i], kv_sc[i], bew_sc[i],
                        epos_sc[pl.ds((first + u) * la + la - 1, 1), :])

            def store_oa(u, o, row0=row0):
                oa_sc[_rows_at((row0 + u * la, la), la), :] = o

            def load_b(u, row0=row0, seq=seq):
                r = row0 + u * lb
                sl = _rows_at((r, lb), lb)
                return (qd_sc[sl, :], ki_sc[sl, :], kend_sc[sl, :], bl_sc[pl.ds(r, 1), :],
                        p_ref[seq, _rows_at((r - seq * tb, lb), lb), base + 2 * MIX:base + 3 * MIX])

            def store_ob(u, o, row0=row0):
                ob_sc[_rows_at((row0 + u * lb, lb), lb), :] = o

            stages.append((_rwkv_state_stages(run, load_factors, store_oa, srw_sc[seq], out), 2 * run + 1))
            if late_hgrn and gi == ngroups - 1:
                load_all = functools.partial(load_b, row0=seq * tb)
                store_all = functools.partial(store_ob, row0=seq * tb)
                stages.append((_gla_stages(hper * ngroups, load_all, store_all, shg_sc[seq], consts_b, B_H, B_HD,
                                           out, "hgrn"), 2 * hper * ngroups + 1))
            elif not late_hgrn:
                stages.append((_gla_stages(hper, load_b, store_ob, shg_sc[seq], consts_b, B_H, B_HD, out, "hgrn"),
                               2 * hper + 1))
            finish.append((seq, out))

        def write_back():
            for seq, out in finish:
                srw_sc[seq] = out["rwkv"]
                if "hgrn" in out:
                    shg_sc[seq] = out["hgrn"]

        return stages, write_back

    def rwkv_output_stages(r0, r1):
        o = oa_sc[r0:r1, :]
        mean = _seg_sum(o, A_HD) * (1.0 / A_HD)
        cen = o - mean
        yield
        var = _seg_sum(cen * cen, A_HD) * (1.0 / A_HD)
        yield
        oa = cen * lax.rsqrt(var + A_GN_EPS) * row(_E_LNW) + row(_E_LNB)
        oa = oa + bonus_sc[r0:r1, :]
        put_o(r0, r1, 0, oa * _silu(rows_of(A_SHIFT_W, A_SHIFT_W + MIX, r0, r1)))

    def put_o(r0, r1, lo, value):
        value = value.astype(o_ref.dtype)
        if len(o_ref.shape) == 2:
            o_ref[r0:r1, lo:lo + MIX] = value
        else:
            for b in seqs:
                a, z = max(r0, b * tb), min(r1, (b + 1) * tb)
                if a < z:
                    o_ref[b, a - b * tb:z - b * tb, lo:lo + MIX] = value[a - r0:z - r0]

    def hgrn_output(r0, r1):
        ob = _head_rms(ob_sc[r0:r1, :], row(_E_BNORM), B_HD) * _silu(rows_of(base + 3 * MIX, base + 4 * MIX, r0, r1))
        put_o(r0, r1, MIX, ob)

    for phase in range(ngroups + 3):
        stages, after = [], []
        if 0 <= phase - 2 < ngroups:
            chain_stages, write_back = state_stages(phase - 2)
            stages += chain_stages
            after.append(write_back)
        if 0 <= phase - 1 < ngroups:
            stages.append((factor_stages(phase - 1, ((phase - 1) % 2) * group), 2 * len(halves) + 1))
        if phase < ngroups:
            stages += [(rwkv_token_stages(r0, r1), 8) for r0, r1 in pieces(phase)]
        if phase == 0:
            stages.append((hgrn_token_stages(), 7))
        if 0 <= phase - 3 < ngroups:
            stages += [(rwkv_output_stages(r0, r1), 3) for r0, r1 in pieces(phase - 3)]
        _interleave(*stages)
        for fn in after:
            fn()
    hgrn_output(0, nb * tb)

    @pl.when(j == nj - 1)
    def _():
        for b in seqs:
            shift_ref[b] = prev_sc[b, 0:1]
            s = srw_sc[b]
            for h in range(A_H):
                srw_ref[b, h] = s[:, h * A_HD:(h + 1) * A_HD]
            shg_ref[b] = shg_sc[b].T


def _mixer_out(bsz, t, nb, tb):
    if nb == 1 or tb == t:
        return (pl.BlockSpec((nb * tb, D_MODEL), lambda b, j: (b * (t // tb) + j, 0)),
                jax.ShapeDtypeStruct((bsz * t, D_MODEL), BF16))
    return (pl.BlockSpec((nb, tb, D_MODEL), lambda b, j: (b, j, 0)),
            jax.ShapeDtypeStruct((bsz, t, D_MODEL), BF16))


def _even_mixer(p, shift0, srw0, shg0, mu, w2p, a2p, par, nb, tb, la, lb):
    bsz, t, _ = p.shape
    kern = functools.partial(_even_kernel, nb=nb, tb=tb, la=la, lb=lb)
    nch, hl = 2 * math.gcd(nb * tb // la, 8), A_H * la
    bmap3 = lambda b, j: (b, 0, 0)
    bmap4 = lambda b, j: (b, 0, 0, 0)
    cmap = lambda b, j: (0, 0)
    blk = lambda: pltpu.VMEM((nb * tb, MIX), F32)
    o_spec, o_shape = _mixer_out(bsz, t, nb, tb)
    return pl.pallas_call(
        kern, grid=(bsz // nb, t // tb),
        in_specs=[pl.BlockSpec((nb, tb, IN_EVEN), lambda b, j: (b, j, 0)),
                  pl.BlockSpec((nb, 1, A_SHIFT_W), bmap3),
                  pl.BlockSpec((nb, A_H, A_HD, A_HD), bmap4),
                  pl.BlockSpec((nb, B_H * B_HD, B_HD), bmap3),
                  pl.BlockSpec((1, A_SHIFT_W), cmap),
                  pl.BlockSpec((2 * A_LORA, MIX), cmap),
                  pl.BlockSpec((2 * A_LORA, MIX), cmap),
                  pl.BlockSpec((16, MIX), cmap)],
        out_specs=[o_spec,
                   pl.BlockSpec((nb, 1, A_SHIFT_W), bmap3),
                   pl.BlockSpec((nb, A_H, A_HD, A_HD), bmap4),
                   pl.BlockSpec((nb, B_H * B_HD, B_HD), bmap3)],
        out_shape=[o_shape,
                   jax.ShapeDtypeStruct((bsz, 1, A_SHIFT_W), F32),
                   jax.ShapeDtypeStruct((bsz, A_H, A_HD, A_HD), F32),
                   jax.ShapeDtypeStruct((bsz, B_H * B_HD, B_HD), F32)],
        scratch_shapes=[pltpu.VMEM((nb, 8, A_SHIFT_W), F32), pltpu.VMEM((nb, A_HD, MIX), F32),
                        pltpu.VMEM((nb, B_HD, B_H * B_HD), F32)] + [blk() for _ in range(15)] + [
                            pltpu.VMEM((nch, 2 * hl, MIX), BF16), pltpu.VMEM((nch, hl, A_HD), F32),
                            pltpu.VMEM((nch, hl, A_HD), F32), pltpu.VMEM((nch, hl, hl), BF16),
                            pltpu.VMEM((nch, A_HD, MIX), F32), pltpu.VMEM((nch, hl, MIX), BF16)],
        compiler_params=pltpu.CompilerParams(dimension_semantics=("arbitrary", "arbitrary"),
                                             vmem_limit_bytes=VMEM_LIMIT),
        name="mixer_rwkv_hgrn")(p, shift0, srw0, shg0, mu, w2p, a2p, par)


_O_CQ = 0
_O_CK = _O_CQ + C_KW
_O_CV = _O_CK + C_KW
_O_GC = _O_CV + MIX
_O_DQK = _O_GC + MIX
_O_DV = _O_DQK + 2 * MIX
_O_GD = _O_DV + MIX
_O_SLAB = _O_GD + MIX
(_P_CNORM, _P_DNORM, _P_G2B) = range(3)


def _odd_kernel(oin_ref, xin_ref, woin_ref, gin_ref, win_ref,
                sgl0_ref, conv0_ref, mc0_ref, mn0_ref, mm0_ref, g2_ref, cw_ref, cb_ref, sb_ref, par_ref,
                wo_ref, gf_ref,
                y_ref, sgl_ref, conv_ref, mc_ref, mn_ref, mm_ref,
                prev_sc, sgl_sc, mc_sc, mn_sc, mm_sc, cq_sc, ck_sc, ce_sc, cl_sc, oc_sc,
                dq_sc, dk_sc, li_sc, fa_sc, fl_sc, od_sc, x_sc, p_sc, *, nb, tb, lc):
    j = pl.program_id(1)
    nj = pl.num_programs(1)
    hl = D_H * lc
    seqs = range(nb)
    head_rows = lambda x: jnp.concatenate(
        [jnp.broadcast_to(x[h:h + 1], (lc, x.shape[1])) for h in range(D_H)], axis=0)

    @pl.when(j == 0)
    def _():
        for b in seqs:
            prev_sc[b] = conv0_ref[b]
            sgl_sc[b] = sgl0_ref[b].T
            mc_sc[b] = mc0_ref[b].T
            mn_sc[b] = head_rows(mn0_ref[b])
            mm_sc[b] = head_rows(mm0_ref[b])[:, :1]

    par = par_ref[...]
    row = lambda i: par[i:i + 1]

    def rows_of(lo, hi, r0, r1):
        return p_sc[r0:r1, lo:hi]

    proj_slabs = [(c, min(c + 256, IN_ODD_PAD)) for c in range(0, IN_ODD_PAD, 256)]

    def projection_stages(r0, r1):
        x1 = xin_ref[r0:r1, :] + _dot(oin_ref[r0:r1, :], woin_ref[...])
        x_sc[r0:r1, :] = x1
        h = _rms(x1, gin_ref[...]).astype(BF16)
        for lo, hi in proj_slabs:
            yield
            p_sc[r0:r1, lo:hi] = _dot(h, win_ref[:, lo:hi])

    def token_stages(r0, r1):
        n = r1 - r0
        span = tb if n % tb == 0 else n
        cum_c, ones_c = _chunk_mats(n, lc)
        slab = rows_of(_O_SLAB, _O_SLAB + SLAB, r0, r1)
        pre = _dot(slab.astype(BF16), g2_ref[...]) + row(_P_G2B)[:, :C_KW]
        logg = _log_sigmoid(pre) * (1.0 / C_GATE_NORM)
        yield
        gb = _mm(cum_c, logg, "nn", M_EXACT_R)
        gl = _mm(ones_c, logg, "nn", M_EXACT_R)
        yield
        ck = rows_of(_O_CK, _O_CK + C_KW, r0, r1)
        cq_sc[r0:r1, :] = rows_of(_O_CQ, _O_CQ + C_KW, r0, r1) * (C_KD ** -0.5) * jnp.exp(gb)
        ck_sc[r0:r1, :] = ck * jnp.exp(-gb)
        ce_sc[r0:r1, :] = ck * jnp.exp(gl - gb)
        cl_sc[r0:r1, :] = gl
        yield
        x = rows_of(_O_DQK, _O_DQK + 2 * MIX, r0, r1)
        cw = cw_ref[...]
        conv = cb_ref[...] + x * cw[D_CONV - 1:D_CONV]
        head_row = _iota((8, 2 * MIX), 0)
        for s in range(1, D_CONV):
            xr = pltpu.roll(x, s, axis=0)
            parts = []
            for a in range(0, n, span):
                kept = pltpu.roll(prev_sc[(r0 + a) // tb], s, axis=0)
                parts.append(jnp.where(head_row < s, kept, xr[a:a + 8]))
                if span > 8:
                    parts.append(xr[a + 8:a + span])
            xs = parts[0] if len(parts) == 1 else jnp.concatenate(parts, axis=0)
            conv = conv + xs * cw[D_CONV - 1 - s:D_CONV - s]
        for a in range(0, n, span):
            prev_sc[(r0 + a) // tb] = x[a + span - 8:a + span]
        yield
        conv = _silu(conv)
        dq_sc[r0:r1, :] = conv[:, :MIX]
        dk_sc[r0:r1, :] = conv[:, MIX:] * (D_HD ** -0.5)
        yield
        gates = slab + sb_ref[...]
        lf = pltpu.roll(_log_sigmoid(gates), SLAB - (SLAB_F - SLAB_I), axis=1)
        li_sc[r0:r1, :] = gates
        fa_sc[r0:r1, :] = _mm(cum_c, lf, "nn", M_EXACT_R)
        fl_sc[r0:r1, :] = _mm(ones_c, lf, "nn", M_EXACT_R)

    wide_c, _, incl_c, _ = _head_masks(C_H, lc, C_KD)
    consts_c = (wide_c, incl_c)
    lane = _iota((hl, SLAB), 1)
    head = _iota((hl, SLAB), 0) // lc
    pick = lambda z: jnp.sum(jnp.where(lane == head + SLAB_I, _tile_rows(z, D_H), 0.0), -1, keepdims=True)

    wide_d, same_d, incl_d, _ = _head_masks(D_H, lc, D_HD)
    consts_d = (wide_d, same_d, incl_d, same_d.astype(BF16))

    per_seq = tb // lc
    if per_seq == 1:
        chains, run = math.gcd(nb, 4), 1
        trips = nb // chains
        first_chunk = lambda gi, q: gi * chains + q
    else:
        chains, run = nb, math.gcd(per_seq, 8)
        trips = per_seq // run
        first_chunk = lambda gi, q: q * per_seq + gi * run
    assert per_seq % run == 0

    def chunk_stages(gi):
        stages, finish = [], []
        for q in range(chains):
            first = first_chunk(gi, q)
            seq = first // per_seq
            out = {}
            rows = lambda u, first=first: _rows_at(((first + u) * lc, lc), lc)
            in_seq = lambda u, first=first, seq=seq: _rows_at(((first + u) * lc - seq * tb, lc), lc)

            def load_c(u, first=first, seq=seq, rows=rows, in_seq=in_seq):
                sl = rows(u)
                return (cq_sc[sl, :], ck_sc[sl, :], ce_sc[sl, :], cl_sc[pl.ds((first + u) * lc, 1), :],
                        p_sc[sl, _O_CV:_O_CV + MIX])

            def store_c(u, o, rows=rows):
                oc_sc[rows(u), :] = o

            def load_d(u, seq=seq, rows=rows, in_seq=in_seq):
                sl = rows(u)
                return (dq_sc[sl, :], dk_sc[sl, :], p_sc[sl, _O_DV:_O_DV + MIX],
                        pick(li_sc[sl, :]), pick(fa_sc[sl, :]), pick(fl_sc[sl, :]))

            def store_d(u, o, rows=rows):
                od_sc[rows(u), :] = o

            stages.append((_mlstm_stages(run, load_d, store_d, (mc_sc[seq], mn_sc[seq], mm_sc[seq]), consts_d, out),
                           3 * run + 1))
            stages.append((_gla_stages(run, load_c, store_c, sgl_sc[seq], consts_c, C_H, C_VD, out, "gla"),
                           2 * run + 1))
            finish.append((seq, out))

        def write_back():
            for seq, out in finish:
                sgl_sc[seq] = out["gla"]
                mc_sc[seq], mn_sc[seq], mm_sc[seq] = out["mlstm"]

        return stages, write_back

    def output_stages(r0, r1):
        oc = _head_rms(oc_sc[r0:r1, :], row(_P_CNORM), C_VD) * _silu(rows_of(_O_GC, _O_GC + MIX, r0, r1))
        yield
        od = _head_rms(od_sc[r0:r1, :], row(_P_DNORM), D_HD) * _silu(rows_of(_O_GD, _O_GD + MIX, r0, r1))
        o = jnp.concatenate([oc, od], axis=1).astype(BF16)
        yield
        x2 = x_sc[r0:r1, :] + _dot(o, wo_ref[...])
        y_ref[r0:r1, :] = _rms(x2, gf_ref[...])

    if per_seq == 1:
        _interleave((projection_stages(0, nb * tb), len(proj_slabs) + 1))
        grows = chains * tb
        for phase in range(trips + 2):
            stages, after = [], []
            if 0 <= phase - 1 < trips:
                chunk, write_back = chunk_stages(phase - 1)
                stages += chunk
                after.append(write_back)
            if phase < trips:
                stages.append((token_stages(phase * grows, (phase + 1) * grows), 6))
            if 0 <= phase - 2 < trips:
                stages.append((output_stages((phase - 2) * grows, (phase - 1) * grows), 3))
            _interleave(*stages)
            for fn in after:
                fn()
    elif nb > 1:
        _interleave((projection_stages(0, nb * tb), len(proj_slabs) + 1))
        _interleave((token_stages(0, nb * tb), 6))

        def trip(gi, carry):
            stages, write_back = chunk_stages(gi)
            _interleave(*stages)
            write_back()
            return carry

        lax.fori_loop(0, trips, trip, 0)
        _interleave((output_stages(0, nb * tb), 3))
    else:
        grows = run * lc
        for phase in range(trips + 3):
            stages, after = [], []
            if 0 <= phase - 2 < trips:
                chunk, write_back = chunk_stages(phase - 2)
                stages += chunk
                after.append(write_back)
            if phase < trips:
                stages.append((projection_stages(phase * grows, (phase + 1) * grows), len(proj_slabs) + 1))
            if 0 <= phase - 1 < trips:
                stages.append((token_stages((phase - 1) * grows, phase * grows), 6))
            if 0 <= phase - 3 < trips:
                stages.append((output_stages((phase - 3) * grows, (phase - 2) * grows), 3))
            _interleave(*stages)
            for fn in after:
                fn()

    @pl.when(j == nj - 1)
    def _():
        for b in seqs:
            sgl_ref[b] = sgl_sc[b].T
            conv_ref[b] = prev_sc[b]
            mc_ref[b] = mc_sc[b].T
            mn_ref[b] = jnp.concatenate([mn_sc[b, h * lc:h * lc + 1] for h in range(D_H)], axis=0)
            mm_ref[b] = jnp.concatenate(
                [jnp.broadcast_to(mm_sc[b, h * lc:h * lc + 1], (1, LANE)) for h in range(D_H)], axis=0)


def _odd_layer(o_in, x, wo_in, g_in, w_in, sgl0, conv0, mc0, mn0, mm0, g2p, cw, cb, sb, par, wo, gf, bsz, t, nb, tb,
               lc):
    hl = D_H * lc
    once = lambda shape: pl.BlockSpec(shape, lambda b, j: (0, 0), pipeline_mode=pl.Buffered(1))
    kern = functools.partial(_odd_kernel, nb=nb, tb=tb, lc=lc)
    bmap3 = lambda b, j: (b, 0, 0)
    cmap = lambda b, j: (0, 0)
    blk = lambda w: pltpu.VMEM((nb * tb, w), F32)
    assert nb == 1 or tb == t
    rows = pl.BlockSpec((nb * tb, D_MODEL), lambda b, j: (b * (t // tb) + j, 0))
    return pl.pallas_call(
        kern, grid=(bsz // nb, t // tb),
        in_specs=[rows, rows, once((D_MODEL, D_MODEL)), pl.BlockSpec((1, D_MODEL), cmap),
                  once((D_MODEL, IN_ODD_PAD)),
                  pl.BlockSpec((nb, C_KW, C_VD), bmap3),
                  pl.BlockSpec((nb, 8, 2 * MIX), bmap3),
                  pl.BlockSpec((nb, MIX, D_HD), bmap3),
                  pl.BlockSpec((nb, D_H, D_HD), bmap3),
                  pl.BlockSpec((nb, D_H, LANE), bmap3),
                  pl.BlockSpec((SLAB, C_KW), cmap),
                  pl.BlockSpec((D_CONV, 2 * MIX), cmap),
                  pl.BlockSpec((1, 2 * MIX), cmap),
                  pl.BlockSpec((1, SLAB), cmap),
                  pl.BlockSpec((8, MIX), cmap),
                  once((D_MODEL, D_MODEL)),
                  pl.BlockSpec((1, D_MODEL), cmap)],
        out_specs=[rows,
                   pl.BlockSpec((nb, C_KW, C_VD), bmap3),
                   pl.BlockSpec((nb, 8, 2 * MIX), bmap3),
                   pl.BlockSpec((nb, MIX, D_HD), bmap3),
                   pl.BlockSpec((nb, D_H, D_HD), bmap3),
                   pl.BlockSpec((nb, D_H, LANE), bmap3)],
        out_shape=[jax.ShapeDtypeStruct((bsz * t, D_MODEL), F32),
                   jax.ShapeDtypeStruct((bsz, C_KW, C_VD), F32),
                   jax.ShapeDtypeStruct((bsz, 8, 2 * MIX), F32),
                   jax.ShapeDtypeStruct((bsz, MIX, D_HD), F32),
                   jax.ShapeDtypeStruct((bsz, D_H, D_HD), F32),
                   jax.ShapeDtypeStruct((bsz, D_H, LANE), F32)],
        scratch_shapes=[pltpu.VMEM((nb, 8, 2 * MIX), F32), pltpu.VMEM((nb, C_VD, C_KW), F32),
                        pltpu.VMEM((nb, D_HD, MIX), F32), pltpu.VMEM((nb, hl, D_HD), F32),
                        pltpu.VMEM((nb, hl, 1), F32),
                        blk(C_KW), blk(C_KW), blk(C_KW), blk(C_KW), blk(MIX),
                        blk(MIX), blk(MIX), blk(SLAB), blk(SLAB), blk(SLAB), blk(MIX),
                        blk(D_MODEL), blk(IN_ODD_PAD)],
        compiler_params=pltpu.CompilerParams(dimension_semantics=("arbitrary", "arbitrary"),
                                             vmem_limit_bytes=VMEM_LIMIT),
        name="layer_gla_mlstm")(o_in, x, wo_in, g_in, w_in, sgl0, conv0, mc0, mn0, mm0, g2p, cw, cb, sb, par, wo, gf)


MIXER_ROWS = 256
SHORT_SEQS = 8


def _mixer_blocking(bsz, t):
    tb = math.gcd(t, MIXER_ROWS)
    nb = math.gcd(bsz, SHORT_SEQS) if tb == t and t * SHORT_SEQS <= MIXER_ROWS else 1
    return nb, tb


def _paired_blocking(bsz, nb, tb):
    if nb == 1 and bsz % 2 == 0 and (tb // 2) % 32 == 0:
        return 2, tb // 2
    return nb, tb


def _run(x, shift, s_rwkv, s_hgrn, s_gla, conv, mc, mn, mm, wts):
    bsz, t, d = x.shape
    nb, tb = _mixer_blocking(bsz, t)
    la = math.gcd(t, 16)
    lbc = math.gcd(t, 32)
    x2 = x.reshape(bsz * t, d)
    p0 = _inproj(x2, wts["g0"], wts["w_in0"])
    o0, shift_n, srw_n, shg_n = _even_mixer(
        p0.reshape(bsz, t, IN_EVEN), shift[0][:, None], s_rwkv[0], s_hgrn[0].reshape(bsz, B_H * B_HD, B_HD),
        wts["mu"], wts["w2p"], wts["a2p"], wts["par_e"], *_paired_blocking(bsz, nb, tb), la, lbc)
    conv8 = jnp.pad(conv[0], ((0, 0), (8 - (D_CONV - 1), 0), (0, 0)))
    mm_l = jnp.broadcast_to(mm[0][:, :, None], (bsz, D_H, LANE))
    y, sgl_n, conv_n, mc_n, mn_n, mm_n = _odd_layer(
        o0.reshape(bsz * t, d), x2, wts["w_out0"], wts["g1"], wts["w_in1"],
        s_gla[0].reshape(bsz, C_KW, C_VD), conv8, mc[0].reshape(bsz, MIX, D_HD), mn[0], mm_l,
        wts["g2p"], wts["cw"], wts["cb"], wts["sb"], wts["par_o"], wts["w_out1"], wts["gf"], bsz, t, nb,
        math.gcd(t, 2 * tb) if nb == 1 else tb, lbc)
    return (y.reshape(bsz, t, d), shift_n.reshape(1, bsz, A_SHIFT_W), srw_n[None],
            shg_n.reshape(1, bsz, B_H, B_HD, B_HD), sgl_n.reshape(1, bsz, C_H, C_KD, C_VD),
            conv_n[None, :, 8 - (D_CONV - 1):], mc_n.reshape(1, bsz, D_H, D_HD, D_HD), mn_n[None],
            mm_n[None, :, :, 0])


def _odd_in_projection(w):
    sizes = (C_KW, C_KW, MIX, C_LORA, MIX, 2 * MIX, MIX, D_H, D_H, MIX)
    off = np.cumsum((0,) + sizes)
    cq, ck, cv, cg, gc, dqk, dv, di, df, gd = (w[:, off[i]:off[i + 1]] for i in range(len(sizes)))
    pad = jnp.zeros((w.shape[0], SLAB - C_LORA - 2 * D_H), w.dtype)
    return jnp.concatenate([cq, ck, cv, gc, dqk, dv, gd, cg, di, df, pad], axis=1)


def kernel(x_prompt, x_sample, state_shift_a, state_rwkv, state_hgrn, state_gla, state_conv_d, state_mlstm_c,
           state_mlstm_n, state_mlstm_m, norm_g, w_in_even, w_out_even, a_mu, a_w0, a_w2, a_a0, a_a2, a_kk,
           a_ka, a_rk, a_ln_w, a_ln_b, b_lb, b_norm, w_in_odd, w_out_odd, c_g2, c_g2b, c_norm, d_conv_w,
           d_conv_b, d_ib, d_fb, d_norm, final_norm):
    assert w_in_even.shape[0] == 1 and w_in_odd.shape[0] == 1 and b_lb.shape[0] == 2
    zpad = lambda a, rows_before, rows_total: jnp.pad(a, ((rows_before, rows_total - rows_before - a.shape[0]), (0, 0)))
    par_e = jnp.concatenate([a_w0, a_a0, a_kk, a_ka, a_rk, a_ln_w, a_ln_b, b_norm, b_lb[0:1], b_lb[1:2]], axis=0)
    g2b = jnp.pad(c_g2b, ((0, 0), (0, MIX - C_KW)))
    par_o = jnp.concatenate([c_norm, d_norm, g2b], axis=0)
    w_in1 = _odd_in_projection(w_in_odd[0].astype(BF16))
    sb = jnp.pad(jnp.concatenate([d_ib, d_fb], axis=1), ((0, 0), (SLAB_I, SLAB - SLAB_I - 2 * D_H)))
    wts = {
        "g0": norm_g[0:1], "g1": norm_g[1:2], "gf": final_norm[None],
        "w_in0": w_in_even[0].astype(BF16), "w_out0": w_out_even[0].astype(BF16),
        "w_in1": w_in1, "w_out1": w_out_odd[0].astype(BF16),
        "mu": a_mu, "w2p": zpad(a_w2[0], 0, 2 * A_LORA).astype(BF16),
        "a2p": zpad(a_a2[0], A_LORA, 2 * A_LORA).astype(BF16),
        "par_e": zpad(par_e, 0, 16), "par_o": zpad(par_o, 0, 8),
        "g2p": zpad(c_g2[0], 0, SLAB).astype(BF16), "cw": d_conv_w[0], "cb": d_conv_b, "sb": sb,
    }
    bp = x_prompt.shape[0]
    z = lambda *s: jnp.zeros(s, x_prompt.dtype)
    prompt = _run(x_prompt, z(1, bp, A_SHIFT_W), z(1, bp, A_H, A_HD, A_HD), z(1, bp, B_H, B_HD, B_HD),
                  z(1, bp, C_H, C_KD, C_VD), z(1, bp, D_CONV - 1, 2 * MIX), z(1, bp, D_H, D_HD, D_HD),
                  z(1, bp, D_H, D_HD), z(1, bp, D_H), wts)
    sample = _run(x_sample, state_shift_a, state_rwkv, state_hgrn, state_gla, state_conv_d, state_mlstm_c,
                  state_mlstm_n, state_mlstm_m, wts)
    return (prompt[0], sample[0]) + prompt[1:] + sample[1:]
```
